```python
import math
import jax, jax.numpy as jnp
from jax import lax
import numpy as np

D_MODEL = 2048
BATCH = 8
SEQ = 4096
DEPTH = 4

SB_HEADS = 8
SB_HEAD_DIM = 128
SB_WIDTH = SB_HEADS * SB_HEAD_DIM
QUERY_BLOCK = 128
SSM_WIDTH = D_MODEL // 4
SSM_GROUP = 16
SSM_GROUPS = SSM_WIDTH // SSM_GROUP
SSM_STATE = 64
SCAN_CHUNK = 128
DT_MIN = 1e-3
DT_MAX = 1e-1
MEM_TOKENS = 256
MEM_HEADS = 4
MEM_HEAD_DIM = 128
MEM_WIDTH = MEM_HEADS * MEM_HEAD_DIM
N_BRANCHES = 3
IN_WIDTH = 3 * SB_WIDTH + SSM_WIDTH + MEM_WIDTH + N_BRANCHES * D_MODEL
D_FF = ((8 * D_MODEL + 3 * 256 - 1) // (3 * 256)) * 256
DN_ALPHA = (2 * DEPTH) ** 0.25
DN_BETA = (8 * DEPTH) ** -0.25
LN_EPS = 1e-5

kernel_name = "hybrid_sb_s5_mem_deepnorm"


def _layer_norm(x, g, b):
    xf = x.astype(jnp.float32)
    mu = jnp.mean(xf, axis=-1, keepdims=True)
    var = jnp.mean(jnp.square(xf - mu), axis=-1, keepdims=True)
    return ((xf - mu) * lax.rsqrt(var + LN_EPS) * g + b).astype(x.dtype)


def _stick_breaking_attention(q, k, v):
    bsz, seq, heads, dh = q.shape
    scale = dh ** -0.5
    outs = []
    for i in range(seq // QUERY_BLOCK):
        q0 = i * QUERY_BLOCK
        k_end = q0 + QUERY_BLOCK
        z = jnp.einsum('bqhd,bkhd->bhqk', q[:, q0:k_end], k[:, :k_end]).astype(jnp.float32) * scale
        t_idx = q0 + jnp.arange(QUERY_BLOCK)
        s_idx = jnp.arange(k_end)
        causal = s_idx[None, :] < t_idx[:, None]
        log1m = jnp.where(causal, jax.nn.log_sigmoid(-z), 0.0)
        after = lax.cumsum(log1m, axis=3, reverse=True) - log1m
        logw = jnp.where(causal, jax.nn.log_sigmoid(z) + after, -jnp.inf)
        w = jnp.exp(logw).astype(v.dtype)
        outs.append(jnp.einsum('bhqk,bkhd->bqhd', w, v[:, :k_end]))
    return jnp.concatenate(outs, axis=1)


def _ssm_combine(left, right):
    a1, b1 = left
    a2, b2 = right
    return a1 * a2, a2 * b1 + b2


def _s5(u, lam_re, lam_im, log_dt, b_re, b_im, c_re, c_im, d_skip):
    f32 = jnp.float32
    bsz, seq, _ = u.shape
    uf = u.astype(f32)
    lam = lax.complex(lam_re.astype(f32), lam_im.astype(f32))
    dt = jnp.exp(log_dt.astype(f32))[:, None]
    lam_bar = jnp.exp(lam * dt)
    b_bar = ((lam_bar - 1.0) / lam)[..., None] * lax.complex(b_re.astype(f32), b_im.astype(f32))
    c = lax.complex(c_re.astype(f32), c_im.astype(f32))
    n_chunks = seq // SCAN_CHUNK
    u_chunks = uf.reshape(bsz, n_chunks, SCAN_CHUNK, SSM_GROUPS, SSM_GROUP).transpose(1, 0, 2, 3, 4)

    def chunk_step(h_prev, uc):
        bu = jnp.einsum('gpc,btgc->btgp', b_bar, uc.astype(jnp.complex64))
        a = jnp.broadcast_to(lam_bar, bu.shape)
        a_cum, h_loc = lax.associative_scan(_ssm_combine, (a, bu), axis=1)
        h_all = h_loc + a_cum * h_prev[:, None]
        y = jnp.real(jnp.einsum('gcp,btgp->btgc', c, h_all))
        return h_all[:, -1], y

    h0 = jnp.zeros((bsz, SSM_GROUPS, SSM_STATE), jnp.complex64)
    _, ys = lax.scan(chunk_step, h0, u_chunks)
    y = ys.transpose(1, 0, 2, 3, 4).reshape(bsz, seq, SSM_WIDTH)
    return y + d_skip.astype(f32) * uf


def _memory_attention(q, mem, w_kv):
    bsz, seq, _ = q.shape
    kv = jnp.einsum('bmd,de->bme', mem, w_kv)
    k, v = jnp.split(kv, 2, axis=-1)
    qh = q.reshape(bsz, seq, MEM_HEADS, MEM_HEAD_DIM)
    kh = k.reshape(bsz, MEM_TOKENS, MEM_HEADS, MEM_HEAD_DIM)
    vh = v.reshape(bsz, MEM_TOKENS, MEM_HEADS, MEM_HEAD_DIM)
    s = jnp.einsum('blhd,bmhd->bhlm', qh, kh).astype(jnp.float32) * (MEM_HEAD_DIM ** -0.5)
    p = jax.nn.softmax(s, axis=-1).astype(v.dtype)
    return jnp.einsum('bhlm,bmhd->blhd', p, vh).reshape(bsz, seq, MEM_WIDTH)


def _fwd_setup_inputs(seed: int = 0) -> dict:
    key = jax.random.key(seed)
    ks = jax.random.split(key, 24)
    f32 = jnp.float32

    def nrm(k, shape, scale):
        return jax.random.normal(k, shape, f32) * scale

    n_idx = jnp.arange(SSM_STATE, dtype=f32)
    lam_im_base = jnp.broadcast_to(math.pi * n_idx, (DEPTH, SSM_GROUPS, SSM_STATE))
    return {
        "x": nrm(ks[0], (BATCH, SEQ, D_MODEL), 1.0),
        "mem": nrm(ks[1], (BATCH, MEM_TOKENS, D_MODEL), 1.0),
        "w_in": nrm(ks[2], (DEPTH, D_MODEL, IN_WIDTH), D_MODEL ** -0.5),
        "b_in": nrm(ks[3], (DEPTH, IN_WIDTH), 0.02),
        "sb_w_out": nrm(ks[4], (DEPTH, SB_WIDTH, D_MODEL), SB_WIDTH ** -0.5),
        "ssm_lambda_re": -0.5 + nrm(ks[5], (DEPTH, SSM_GROUPS, SSM_STATE), 0.01),
        "ssm_lambda_im": lam_im_base + nrm(ks[6], (DEPTH, SSM_GROUPS, SSM_STATE), 0.01),
        "ssm_log_dt": jax.random.uniform(ks[7], (DEPTH, SSM_GROUPS), f32, math.log(DT_MIN), math.log(DT_MAX)),
        "ssm_b_re": nrm(ks[8], (DEPTH, SSM_GROUPS, SSM_STATE, SSM_GROUP), (2 * SSM_GROUP) ** -0.5),
        "ssm_b_im": nrm(ks[9], (DEPTH, SSM_GROUPS, SSM_STATE, SSM_GROUP), (2 * SSM_GROUP) ** -0.5),
        "ssm_c_re": nrm(ks[10], (DEPTH, SSM_GROUPS, SSM_GROUP, SSM_STATE), (2 * SSM_STATE) ** -0.5),
        "ssm_c_im": nrm(ks[11], (DEPTH, SSM_GROUPS, SSM_GROUP, SSM_STATE), (2 * SSM_STATE) ** -0.5),
        "ssm_d": nrm(ks[12], (DEPTH, SSM_WIDTH), 1.0),
        "ssm_w_glu": nrm(ks[13], (DEPTH, SSM_WIDTH, 2 * SSM_WIDTH), SSM_WIDTH ** -0.5),
        "ssm_w_out": nrm(ks[14], (DEPTH, SSM_WIDTH, D_MODEL), SSM_WIDTH ** -0.5),
        "mem_w_kv": nrm(ks[15], (DEPTH, D_MODEL, 2 * MEM_WIDTH), D_MODEL ** -0.5),
        "mem_w_out": nrm(ks[16], (DEPTH, MEM_WIDTH, D_MODEL), MEM_WIDTH ** -0.5),
        "w_o": nrm(ks[17], (DEPTH, D_MODEL, D_MODEL), DN_BETA * D_MODEL ** -0.5),
        "ln1_g": 1.0 + nrm(ks[18], (DEPTH, D_MODEL), 0.02),
        "ln1_b": nrm(ks[19], (DEPTH, D_MODEL), 0.02),
        "ffn_w_gate_up": nrm(ks[20], (DEPTH, D_MODEL, 2 * D_FF), D_MODEL ** -0.5),
        "ffn_w_down": nrm(ks[21], (DEPTH, D_FF, D_MODEL), DN_BETA * D_FF ** -0.5),
        "ln2_g": 1.0 + nrm(ks[22], (DEPTH, D_MODEL), 0.02),
        "ln2_b": nrm(ks[23], (DEPTH, D_MODEL), 0.02),
    }


def _fwd_reference(x, mem, w_in, b_in, sb_w_out, ssm_lambda_re, ssm_lambda_im, ssm_log_dt,
              ssm_b_re, ssm_b_im, ssm_c_re, ssm_c_im, ssm_d, ssm_w_glu, ssm_w_out,
              mem_w_kv, mem_w_out, w_o, ln1_g, ln1_b, ffn_w_gate_up, ffn_w_down,
              ln2_g, ln2_b):
    bsz, seq, _ = x.shape
    splits = [SB_WIDTH, 2 * SB_WIDTH, 3 * SB_WIDTH,
              3 * SB_WIDTH + SSM_WIDTH, 3 * SB_WIDTH + SSM_WIDTH + MEM_WIDTH]
    for l in range(DEPTH):
        proj = jnp.einsum('bld,de->ble', x, w_in[l]) + b_in[l]
        q_sb, k_sb, v_sb, u_ssm, q_mem, gate_logits = jnp.split(proj, splits, axis=-1)

        hs = (bsz, seq, SB_HEADS, SB_HEAD_DIM)
        sb = _stick_breaking_attention(q_sb.reshape(hs), k_sb.reshape(hs), v_sb.reshape(hs))
        p_sb = jnp.einsum('ble,ed->bld', sb.reshape(bsz, seq, SB_WIDTH), sb_w_out[l])

        y = _s5(u_ssm, ssm_lambda_re[l], ssm_lambda_im[l], ssm_log_dt[l], ssm_b_re[l], ssm_b_im[l],
                ssm_c_re[l], ssm_c_im[l], ssm_d[l])
        g = jax.nn.gelu(y).astype(x.dtype)
        glu_a, glu_b = jnp.split(jnp.einsum('ble,ef->blf', g, ssm_w_glu[l]), 2, axis=-1)
        p_ssm = jnp.einsum('ble,ed->bld', glu_a * jax.nn.sigmoid(glu_b), ssm_w_out[l])

        mm = _memory_attention(q_mem, mem, mem_w_kv[l])
        p_mem = jnp.einsum('ble,ed->bld', mm, mem_w_out[l])

        gates = jax.nn.sigmoid(gate_logits.astype(jnp.float32)).astype(x.dtype)
        gates = gates.reshape(bsz, seq, N_BRANCHES, D_MODEL)
        merged = gates[:, :, 0] * p_sb + gates[:, :, 1] * p_ssm + gates[:, :, 2] * p_mem
        mix_out = jnp.einsum('bld,de->ble', merged, w_o[l])
        x = _layer_norm(DN_ALPHA * x + mix_out, ln1_g[l], ln1_b[l])

        gate_up = jnp.einsum('bld,df->blf', x, ffn_w_gate_up[l])
        f_gate, f_up = jnp.split(gate_up, 2, axis=-1)
        ffn_out = jnp.einsum('blf,fd->bld', jax.nn.silu(f_gate) * f_up, ffn_w_down[l])
        x = _layer_norm(DN_ALPHA * x + ffn_out, ln2_g[l], ln2_b[l])
    return x


import jax as _jax
import jax.numpy as _jnp

TWIN_FORMAT = 'train_step'
FWD_PARAMS = ['x', 'mem', 'w_in', 'b_in', 'sb_w_out', 'ssm_lambda_re', 'ssm_lambda_im', 'ssm_log_dt', 'ssm_b_re', 'ssm_b_im', 'ssm_c_re', 'ssm_c_im', 'ssm_d', 'ssm_w_glu', 'ssm_w_out', 'mem_w_kv', 'mem_w_out', 'w_o', 'ln1_g', 'ln1_b', 'ffn_w_gate_up', 'ffn_w_down', 'ln2_g', 'ln2_b']
TWIN_WEIGHTS = ['w_in', 'b_in', 'sb_w_out', 'ssm_lambda_re', 'ssm_lambda_im', 'ssm_log_dt', 'ssm_b_re', 'ssm_b_im', 'ssm_c_re', 'ssm_c_im', 'ssm_d', 'ssm_w_glu', 'ssm_w_out', 'mem_w_kv', 'mem_w_out', 'w_o', 'ln1_g', 'ln1_b', 'ffn_w_gate_up', 'ffn_w_down', 'ln2_g', 'ln2_b']
TWIN_DIFF_INPUT = 'x'
TWIN_INPUTS = ['x', 'mem', 'w_in', 'b_in', 'sb_w_out', 'ssm_lambda_re', 'ssm_lambda_im', 'ssm_log_dt', 'ssm_b_re', 'ssm_b_im', 'ssm_c_re', 'ssm_c_im', 'ssm_d', 'ssm_w_glu', 'ssm_w_out', 'mem_w_kv', 'mem_w_out', 'w_o', 'ln1_g', 'ln1_b', 'ffn_w_gate_up', 'ffn_w_down', 'ln2_g', 'ln2_b', 'loss_target', 'm_w_in', 'm_b_in', 'm_sb_w_out', 'm_ssm_lambda_re', 'm_ssm_lambda_im', 'm_ssm_log_dt', 'm_ssm_b_re', 'm_ssm_b_im', 'm_ssm_c_re', 'm_ssm_c_im', 'm_ssm_d', 'm_ssm_w_glu', 'm_ssm_w_out', 'm_mem_w_kv', 'm_mem_w_out', 'm_w_o', 'm_ln1_g', 'm_ln1_b', 'm_ffn_w_gate_up', 'm_ffn_w_down', 'm_ln2_g', 'm_ln2_b', 'v_w_in', 'v_b_in', 'v_sb_w_out', 'v_ssm_lambda_re', 'v_ssm_lambda_im', 'v_ssm_log_dt', 'v_ssm_b_re', 'v_ssm_b_im', 'v_ssm_c_re', 'v_ssm_c_im', 'v_ssm_d', 'v_ssm_w_glu', 'v_ssm_w_out', 'v_mem_w_kv', 'v_mem_w_out', 'v_w_o', 'v_ln1_g', 'v_ln1_b', 'v_ffn_w_gate_up', 'v_ffn_w_down', 'v_ln2_g', 'v_ln2_b']
TWIN_OUTPUTS = ['loss', 'grad_x', 'grad_w_in', 'grad_b_in', 'grad_sb_w_out', 'grad_ssm_lambda_re', 'grad_ssm_lambda_im', 'grad_ssm_log_dt', 'grad_ssm_b_re', 'grad_ssm_b_im', 'grad_ssm_c_re', 'grad_ssm_c_im', 'grad_ssm_d', 'grad_ssm_w_glu', 'grad_ssm_w_out', 'grad_mem_w_kv', 'grad_mem_w_out', 'grad_w_o', 'grad_ln1_g', 'grad_ln1_b', 'grad_ffn_w_gate_up', 'grad_ffn_w_down', 'grad_ln2_g', 'grad_ln2_b', 'delta_w_in', 'delta_b_in', 'delta_sb_w_out', 'delta_ssm_lambda_re', 'delta_ssm_lambda_im', 'delta_ssm_log_dt', 'delta_ssm_b_re', 'delta_ssm_b_im', 'delta_ssm_c_re', 'delta_ssm_c_im', 'delta_ssm_d', 'delta_ssm_w_glu', 'delta_ssm_w_out', 'delta_mem_w_kv', 'delta_mem_w_out', 'delta_w_o', 'delta_ln1_g', 'delta_ln1_b', 'delta_ffn_w_gate_up', 'delta_ffn_w_down', 'delta_ln2_g', 'delta_ln2_b', 'new_m_w_in', 'new_m_b_in', 'new_m_sb_w_out', 'new_m_ssm_lambda_re', 'new_m_ssm_lambda_im', 'new_m_ssm_log_dt', 'new_m_ssm_b_re', 'new_m_ssm_b_im', 'new_m_ssm_c_re', 'new_m_ssm_c_im', 'new_m_ssm_d', 'new_m_ssm_w_glu', 'new_m_ssm_w_out', 'new_m_mem_w_kv', 'new_m_mem_w_out', 'new_m_w_o', 'new_m_ln1_g', 'new_m_ln1_b', 'new_m_ffn_w_gate_up', 'new_m_ffn_w_down', 'new_m_ln2_g', 'new_m_ln2_b', 'new_v_w_in', 'new_v_b_in', 'new_v_sb_w_out', 'new_v_ssm_lambda_re', 'new_v_ssm_lambda_im', 'new_v_ssm_log_dt', 'new_v_ssm_b_re', 'new_v_ssm_b_im', 'new_v_ssm_c_re', 'new_v_ssm_c_im', 'new_v_ssm_d', 'new_v_ssm_w_glu', 'new_v_ssm_w_out', 'new_v_mem_w_kv', 'new_v_mem_w_out', 'new_v_w_o', 'new_v_ln1_g', 'new_v_ln1_b', 'new_v_ffn_w_gate_up', 'new_v_ffn_w_down', 'new_v_ln2_g', 'new_v_ln2_b']
TWIN_LEAF_KINDS = {'loss': 'loss', 'grad_x': 'grad_x', 'grad_w_in': 'grad_w', 'grad_b_in': 'grad_w', 'grad_sb_w_out': 'grad_w', 'grad_ssm_lambda_re': 'grad_w', 'grad_ssm_lambda_im': 'grad_w', 'grad_ssm_log_dt': 'grad_w', 'grad_ssm_b_re': 'grad_w', 'grad_ssm_b_im': 'grad_w', 'grad_ssm_c_re': 'grad_w', 'grad_ssm_c_im': 'grad_w', 'grad_ssm_d': 'grad_w', 'grad_ssm_w_glu': 'grad_w', 'grad_ssm_w_out': 'grad_w', 'grad_mem_w_kv': 'grad_w', 'grad_mem_w_out': 'grad_w', 'grad_w_o': 'grad_w', 'grad_ln1_g': 'grad_w', 'grad_ln1_b': 'grad_w', 'grad_ffn_w_gate_up': 'grad_w', 'grad_ffn_w_down': 'grad_w', 'grad_ln2_g': 'grad_w', 'grad_ln2_b': 'grad_w', 'delta_w_in': 'delta_w', 'delta_b_in': 'delta_w', 'delta_sb_w_out': 'delta_w', 'delta_ssm_lambda_re': 'delta_w', 'delta_ssm_lambda_im': 'delta_w', 'delta_ssm_log_dt': 'delta_w', 'delta_ssm_b_re': 'delta_w', 'delta_ssm_b_im': 'delta_w', 'delta_ssm_c_re': 'delta_w', 'delta_ssm_c_im': 'delta_w', 'delta_ssm_d': 'delta_w', 'delta_ssm_w_glu': 'delta_w', 'delta_ssm_w_out': 'delta_w', 'delta_mem_w_kv': 'delta_w', 'delta_mem_w_out': 'delta_w', 'delta_w_o': 'delta_w', 'delta_ln1_g': 'delta_w', 'delta_ln1_b': 'delta_w', 'delta_ffn_w_gate_up': 'delta_w', 'delta_ffn_w_down': 'delta_w', 'delta_ln2_g': 'delta_w', 'delta_ln2_b': 'delta_w', 'new_m_w_in': 'new_m', 'new_m_b_in': 'new_m', 'new_m_sb_w_out': 'new_m', 'new_m_ssm_lambda_re': 'new_m', 'new_m_ssm_lambda_im': 'new_m', 'new_m_ssm_log_dt': 'new_m', 'new_m_ssm_b_re': 'new_m', 'new_m_ssm_b_im': 'new_m', 'new_m_ssm_c_re': 'new_m', 'new_m_ssm_c_im': 'new_m', 'new_m_ssm_d': 'new_m', 'new_m_ssm_w_glu': 'new_m', 'new_m_ssm_w_out': 'new_m', 'new_m_mem_w_kv': 'new_m', 'new_m_mem_w_out': 'new_m', 'new_m_w_o': 'new_m', 'new_m_ln1_g': 'new_m', 'new_m_ln1_b': 'new_m', 'new_m_ffn_w_gate_up': 'new_m', 'new_m_ffn_w_down': 'new_m', 'new_m_ln2_g': 'new_m', 'new_m_ln2_b': 'new_m', 'new_v_w_in': 'new_v', 'new_v_b_in': 'new_v', 'new_v_sb_w_out': 'new_v', 'new_v_ssm_lambda_re': 'new_v', 'new_v_ssm_lambda_im': 'new_v', 'new_v_ssm_log_dt': 'new_v', 'new_v_ssm_b_re': 'new_v', 'new_v_ssm_b_im': 'new_v', 'new_v_ssm_c_re': 'new_v', 'new_v_ssm_c_im': 'new_v', 'new_v_ssm_d': 'new_v', 'new_v_ssm_w_glu': 'new_v', 'new_v_ssm_w_out': 'new_v', 'new_v_mem_w_kv': 'new_v', 'new_v_mem_w_out': 'new_v', 'new_v_w_o': 'new_v', 'new_v_ln1_g': 'new_v', 'new_v_ln1_b': 'new_v', 'new_v_ffn_w_gate_up': 'new_v', 'new_v_ffn_w_down': 'new_v', 'new_v_ln2_g': 'new_v', 'new_v_ln2_b': 'new_v'}


def _forward(args):
    return _fwd_reference(*[args[k] for k in FWD_PARAMS])


def _output_shape():
    out = _jax.eval_shape(lambda: _forward(_fwd_setup_inputs(0)))
    return out.shape, out.dtype

N_MICROBATCH = 1
ADAM_LR = 0.001
ADAM_B1 = 0.9
ADAM_B2 = 0.999
ADAM_EPS = 1e-08
ADAM_WD = 0.01
ADAM_STEP = 10
PER_EXAMPLE_BATCH_AXIS = {'x': 0, 'mem': 0, 'loss_target': 0}
SHARED_INPUTS = []
_WEIGHT_DTYPES = {'w_in': _jnp.float32, 'b_in': _jnp.float32, 'sb_w_out': _jnp.float32, 'ssm_lambda_re': _jnp.float32, 'ssm_lambda_im': _jnp.float32, 'ssm_log_dt': _jnp.float32, 'ssm_b_re': _jnp.float32, 'ssm_b_im': _jnp.float32, 'ssm_c_re': _jnp.float32, 'ssm_c_im': _jnp.float32, 'ssm_d': _jnp.float32, 'ssm_w_glu': _jnp.float32, 'ssm_w_out': _jnp.float32, 'mem_w_kv': _jnp.float32, 'mem_w_out': _jnp.float32, 'w_o': _jnp.float32, 'ln1_g': _jnp.float32, 'ln1_b': _jnp.float32, 'ffn_w_gate_up': _jnp.float32, 'ffn_w_down': _jnp.float32, 'ln2_g': _jnp.float32, 'ln2_b': _jnp.float32}
MOMENT_SCALE = {'w_in': 5.049507e-03, 'b_in': 1.465064e-02, 'sb_w_out': 8.064967e-03, 'ssm_lambda_re': 5.134131e-04, 'ssm_lambda_im': 5.124561e-04, 'ssm_log_dt': 2.333912e-01, 'ssm_b_re': 2.943452e-04, 'ssm_b_im': 2.991911e-04, 'ssm_c_re': 5.892255e-04, 'ssm_c_im': 6.038824e-04, 'ssm_d': 1.296717e-02, 'ssm_w_glu': 8.458907e-03, 'ssm_w_out': 5.756532e-03, 'mem_w_kv': 2.748643e-03, 'mem_w_out': 1.485932e-03, 'w_o': 2.351036e-02, 'ln1_g': 4.994629e-01, 'ln1_b': 2.576110e-01, 'ffn_w_gate_up': 8.182438e-03, 'ffn_w_down': 3.173186e-02, 'ln2_g': 8.048177e+00, 'ln2_b': 5.149662e-01}


def _to_microbatches(a, axis):
    t = _jnp.moveaxis(a, axis, 0)
    t = t.reshape((N_MICROBATCH, t.shape[0] // N_MICROBATCH) + t.shape[1:])
    return _jnp.moveaxis(t, 1, axis + 1)


def setup_inputs(seed: int = 0) -> dict:
    inp = _fwd_setup_inputs(seed)
    key = _jax.random.fold_in(_jax.random.key(seed), 7919)
    shape, _ = _output_shape()
    out = dict(inp)
    out["loss_target"] = _jax.random.normal(_jax.random.fold_in(key, 0), shape, _jnp.float32)
    for i, name in enumerate(TWIN_WEIGHTS):
        w = inp[name].astype(_jnp.float32)
        if MOMENT_SCALE is None:
            s = _jnp.sqrt(_jnp.mean(_jnp.square(w)) + 1e-30)
        else:
            s = MOMENT_SCALE[name]
        km, kv = _jax.random.split(_jax.random.fold_in(key, i + 1))
        out[name] = w
        out["m_" + name] = s * _jax.random.normal(km, w.shape, _jnp.float32)
        out["v_" + name] = (s * s) * _jax.random.uniform(kv, w.shape, _jnp.float32, 0.5, 1.5)
    if N_MICROBATCH > 1:
        for name, axis in PER_EXAMPLE_BATCH_AXIS.items():
            out[name] = _to_microbatches(out[name], axis)
    return {'x': out['x'], 'mem': out['mem'], 'w_in': out['w_in'], 'b_in': out['b_in'], 'sb_w_out': out['sb_w_out'], 'ssm_lambda_re': out['ssm_lambda_re'], 'ssm_lambda_im': out['ssm_lambda_im'], 'ssm_log_dt': out['ssm_log_dt'], 'ssm_b_re': out['ssm_b_re'], 'ssm_b_im': out['ssm_b_im'], 'ssm_c_re': out['ssm_c_re'], 'ssm_c_im': out['ssm_c_im'], 'ssm_d': out['ssm_d'], 'ssm_w_glu': out['ssm_w_glu'], 'ssm_w_out': out['ssm_w_out'], 'mem_w_kv': out['mem_w_kv'], 'mem_w_out': out['mem_w_out'], 'w_o': out['w_o'], 'ln1_g': out['ln1_g'], 'ln1_b': out['ln1_b'], 'ffn_w_gate_up': out['ffn_w_gate_up'], 'ffn_w_down': out['ffn_w_down'], 'ln2_g': out['ln2_g'], 'ln2_b': out['ln2_b'], 'loss_target': out['loss_target'], 'm_w_in': out['m_w_in'], 'm_b_in': out['m_b_in'], 'm_sb_w_out': out['m_sb_w_out'], 'm_ssm_lambda_re': out['m_ssm_lambda_re'], 'm_ssm_lambda_im': out['m_ssm_lambda_im'], 'm_ssm_log_dt': out['m_ssm_log_dt'], 'm_ssm_b_re': out['m_ssm_b_re'], 'm_ssm_b_im': out['m_ssm_b_im'], 'm_ssm_c_re': out['m_ssm_c_re'], 'm_ssm_c_im': out['m_ssm_c_im'], 'm_ssm_d': out['m_ssm_d'], 'm_ssm_w_glu': out['m_ssm_w_glu'], 'm_ssm_w_out': out['m_ssm_w_out'], 'm_mem_w_kv': out['m_mem_w_kv'], 'm_mem_w_out': out['m_mem_w_out'], 'm_w_o': out['m_w_o'], 'm_ln1_g': out['m_ln1_g'], 'm_ln1_b': out['m_ln1_b'], 'm_ffn_w_gate_up': out['m_ffn_w_gate_up'], 'm_ffn_w_down': out['m_ffn_w_down'], 'm_ln2_g': out['m_ln2_g'], 'm_ln2_b': out['m_ln2_b'], 'v_w_in': out['v_w_in'], 'v_b_in': out['v_b_in'], 'v_sb_w_out': out['v_sb_w_out'], 'v_ssm_lambda_re': out['v_ssm_lambda_re'], 'v_ssm_lambda_im': out['v_ssm_lambda_im'], 'v_ssm_log_dt': out['v_ssm_log_dt'], 'v_ssm_b_re': out['v_ssm_b_re'], 'v_ssm_b_im': out['v_ssm_b_im'], 'v_ssm_c_re': out['v_ssm_c_re'], 'v_ssm_c_im': out['v_ssm_c_im'], 'v_ssm_d': out['v_ssm_d'], 'v_ssm_w_glu': out['v_ssm_w_glu'], 'v_ssm_w_out': out['v_ssm_w_out'], 'v_mem_w_kv': out['v_mem_w_kv'], 'v_mem_w_out': out['v_mem_w_out'], 'v_w_o': out['v_w_o'], 'v_ln1_g': out['v_ln1_g'], 'v_ln1_b': out['v_ln1_b'], 'v_ffn_w_gate_up': out['v_ffn_w_gate_up'], 'v_ffn_w_down': out['v_ffn_w_down'], 'v_ln2_g': out['v_ln2_g'], 'v_ln2_b': out['v_ln2_b']}


def _loss(weights, diff, rest, loss_target):
    with _jax.named_scope("forward"):
        args = {**rest, TWIN_DIFF_INPUT: diff, **{k: w.astype(_WEIGHT_DTYPES[k]) for k, w in weights.items()}}
        y = _forward(args)
    with _jax.named_scope("loss_head"):
        err = _jnp.square(y.astype(_jnp.float32) - loss_target)
        return 0.5 * _jnp.sum(_jnp.mean(err, axis=-1)) if err.ndim else 0.5 * err


def _adamw(w, g, m, v):
    m = ADAM_B1 * m + (1.0 - ADAM_B1) * g
    v = ADAM_B2 * v + (1.0 - ADAM_B2) * _jnp.square(g)
    m_hat = m / (1.0 - ADAM_B1 ** ADAM_STEP)
    v_hat = v / (1.0 - ADAM_B2 ** ADAM_STEP)
    delta = -ADAM_LR * (m_hat / (_jnp.sqrt(v_hat) + ADAM_EPS) + ADAM_WD * w)
    return delta, m, v


def reference(x, mem, w_in, b_in, sb_w_out, ssm_lambda_re, ssm_lambda_im, ssm_log_dt, ssm_b_re, ssm_b_im, ssm_c_re, ssm_c_im, ssm_d, ssm_w_glu, ssm_w_out, mem_w_kv, mem_w_out, w_o, ln1_g, ln1_b, ffn_w_gate_up, ffn_w_down, ln2_g, ln2_b, loss_target, m_w_in, m_b_in, m_sb_w_out, m_ssm_lambda_re, m_ssm_lambda_im, m_ssm_log_dt, m_ssm_b_re, m_ssm_b_im, m_ssm_c_re, m_ssm_c_im, m_ssm_d, m_ssm_w_glu, m_ssm_w_out, m_mem_w_kv, m_mem_w_out, m_w_o, m_ln1_g, m_ln1_b, m_ffn_w_gate_up, m_ffn_w_down, m_ln2_g, m_ln2_b, v_w_in, v_b_in, v_sb_w_out, v_ssm_lambda_re, v_ssm_lambda_im, v_ssm_log_dt, v_ssm_b_re, v_ssm_b_im, v_ssm_c_re, v_ssm_c_im, v_ssm_d, v_ssm_w_glu, v_ssm_w_out, v_mem_w_kv, v_mem_w_out, v_w_o, v_ln1_g, v_ln1_b, v_ffn_w_gate_up, v_ffn_w_down, v_ln2_g, v_ln2_b):
    given = dict(x=x, mem=mem, w_in=w_in, b_in=b_in, sb_w_out=sb_w_out, ssm_lambda_re=ssm_lambda_re, ssm_lambda_im=ssm_lambda_im, ssm_log_dt=ssm_log_dt, ssm_b_re=ssm_b_re, ssm_b_im=ssm_b_im, ssm_c_re=ssm_c_re, ssm_c_im=ssm_c_im, ssm_d=ssm_d, ssm_w_glu=ssm_w_glu, ssm_w_out=ssm_w_out, mem_w_kv=mem_w_kv, mem_w_out=mem_w_out, w_o=w_o, ln1_g=ln1_g, ln1_b=ln1_b, ffn_w_gate_up=ffn_w_gate_up, ffn_w_down=ffn_w_down, ln2_g=ln2_g, ln2_b=ln2_b, loss_target=loss_target, m_w_in=m_w_in, m_b_in=m_b_in, m_sb_w_out=m_sb_w_out, m_ssm_lambda_re=m_ssm_lambda_re, m_ssm_lambda_im=m_ssm_lambda_im, m_ssm_log_dt=m_ssm_log_dt, m_ssm_b_re=m_ssm_b_re, m_ssm_b_im=m_ssm_b_im, m_ssm_c_re=m_ssm_c_re, m_ssm_c_im=m_ssm_c_im, m_ssm_d=m_ssm_d, m_ssm_w_glu=m_ssm_w_glu, m_ssm_w_out=m_ssm_w_out, m_mem_w_kv=m_mem_w_kv, m_mem_w_out=m_mem_w_out, m_w_o=m_w_o, m_ln1_g=m_ln1_g, m_ln1_b=m_ln1_b, m_ffn_w_gate_up=m_ffn_w_gate_up, m_ffn_w_down=m_ffn_w_down, m_ln2_g=m_ln2_g, m_ln2_b=m_ln2_b, v_w_in=v_w_in, v_b_in=v_b_in, v_sb_w_out=v_sb_w_out, v_ssm_lambda_re=v_ssm_lambda_re, v_ssm_lambda_im=v_ssm_lambda_im, v_ssm_log_dt=v_ssm_log_dt, v_ssm_b_re=v_ssm_b_re, v_ssm_b_im=v_ssm_b_im, v_ssm_c_re=v_ssm_c_re, v_ssm_c_im=v_ssm_c_im, v_ssm_d=v_ssm_d, v_ssm_w_glu=v_ssm_w_glu, v_ssm_w_out=v_ssm_w_out, v_mem_w_kv=v_mem_w_kv, v_mem_w_out=v_mem_w_out, v_w_o=v_w_o, v_ln1_g=v_ln1_g, v_ln1_b=v_ln1_b, v_ffn_w_gate_up=v_ffn_w_gate_up, v_ffn_w_down=v_ffn_w_down, v_ln2_g=v_ln2_g, v_ln2_b=v_ln2_b)
    weights = {n: given[n] for n in TWIN_WEIGHTS}
    shared = {n: given[n] for n in SHARED_INPUTS}
    per_example = {n: given[n] for n in ['x', 'mem']}
    grad_fn = _jax.value_and_grad(_loss, argnums=(0, 1))

    def one_microbatch(ex, loss_target):
        ex = dict(ex)
        diff = ex.pop(TWIN_DIFF_INPUT)
        return grad_fn(weights, diff, {**shared, **ex}, loss_target)

    if N_MICROBATCH == 1:
        loss, (grad_w, grad_x) = one_microbatch(per_example, given["loss_target"])
    else:
        def body(carry, xs):
            loss_sum, grad_sum = carry
            l_k, (gw_k, gx_k) = one_microbatch(xs[0], xs[1])
            with _jax.named_scope("update"):
                return (loss_sum + l_k, _jax.tree.map(_jnp.add, grad_sum, gw_k)), gx_k

        init = (_jnp.zeros((), _jnp.float32), _jax.tree.map(_jnp.zeros_like, weights))
        (loss, grad_w), grad_x = _jax.lax.scan(body, init, (per_example, given["loss_target"]))
    with _jax.named_scope("update"):
        delta_w, new_m, new_v = {}, {}, {}
        for n in TWIN_WEIGHTS:
            delta_w[n], new_m[n], new_v[n] = _adamw(weights[n], grad_w[n], given["m_" + n], given["v_" + n])
    return (loss, grad_x, *[grad_w[n] for n in TWIN_WEIGHTS], *[delta_w[n] for n in TWIN_WEIGHTS],
            *[new_m[n] for n in TWIN_WEIGHTS], *[new_v[n] for n in TWIN_WEIGHTS])
```

```python
import functools
import math

import jax
import jax.numpy as jnp
from jax import lax
from jax.experimental import pallas as pl
from jax.experimental.pallas import tpu as pltpu

F32 = jnp.float32
BF16 = jnp.bfloat16
MESH = pl.DeviceIdType.MESH
ANY = pl.BlockSpec(memory_space=pl.ANY)
VMEM = pl.BlockSpec(memory_space=pltpu.VMEM)

HEAD_DIM = 128
SSM_GROUP = 16
N_CHIPS = 4
N_DEV = 8
LN_EPS = 1e-5
ADAM_LR = 0.001
ADAM_B1 = 0.9
ADAM_B2 = 0.999
ADAM_EPS = 1e-08
ADAM_WD = 0.01
ADAM_STEP = 10
LANES = 128
SUBLANES = 8
VMEM_LIMIT_MB = 56


def _cparams(sem, mb=VMEM_LIMIT_MB):
    return pltpu.CompilerParams(dimension_semantics=sem, vmem_limit_bytes=mb * 1024 * 1024)


def _tile(n, pref, mult=LANES):
    best = None
    t = mult
    while t <= min(n, pref):
        if n % t == 0:
            best = t
        t += mult
    return n if best is None else best


def _dot(a, b, dims=(((1,), (0,)), ((), ()))):
    return lax.dot_general(a, b, dims, preferred_element_type=F32)


NT = (((1,), (1,)), ((), ()))
TN = (((0,), (0,)), ((), ()))


def _split2(x):
    hi = x.astype(BF16)
    lo = (x - hi.astype(F32)).astype(BF16)
    return hi, lo


def _dot_hp(a, b, dims=(((1,), (0,)), ((), ()))):
    ah, al = _split2(a)
    bh, bl = _split2(b)
    return _dot(ah, bh, dims) + (_dot(ah, bl, dims) + _dot(al, bh, dims))


def _dot_mask(x, u):
    hi = x.astype(BF16)
    r = x - hi.astype(F32)
    mid = r.astype(BF16)
    lo = (r - mid.astype(F32)).astype(BF16)
    return _dot(hi, u) + (_dot(mid, u) + _dot(lo, u))


def _mm_call(a, b, *, dims, grid, a_spec, b_spec, out_spec, out_shape, name,
             bias=None, bias_spec=None, add=None, add_spec=None, add_scale=1.0):
    nk = grid[2]
    has_bias = bias is not None
    has_add = add is not None

    def body(*refs):
        a_ref, b_ref = refs[0], refs[1]
        pos = 2
        bias_ref = refs[pos] if has_bias else None
        pos += int(has_bias)
        add_ref = refs[pos] if has_add else None
        pos += int(has_add)
        o_ref = refs[pos]
        acc_ref = refs[pos + 1] if nk > 1 else None
        p = _dot(a_ref[...].astype(BF16), b_ref[...].astype(BF16), dims)

        def finish(acc):
            if has_bias:
                acc = acc + bias_ref[...]
            if has_add:
                acc = acc + add_scale * add_ref[...].astype(F32)
            o_ref[...] = acc.astype(o_ref.dtype)

        if nk == 1:
            finish(p)
        else:
            k = pl.program_id(2)

            @pl.when(k == 0)
            def _():
                acc_ref[...] = p

            @pl.when(k > 0)
            def _():
                acc_ref[...] += p

            @pl.when(k == nk - 1)
            def _():
                finish(acc_ref[...])

    ins = [a, b]
    in_specs = [a_spec, b_spec]
    if has_bias:
        ins.append(bias)
        in_specs.append(bias_spec)
    if has_add:
        ins.append(add)
        in_specs.append(add_spec)
    scratch = []
    if nk > 1:
        blk = [d for d in out_spec.block_shape if d is not None]
        scratch.append(pltpu.VMEM(tuple(blk), F32))
    return pl.pallas_call(
        body, out_shape=out_shape, grid=grid, in_specs=in_specs, out_specs=out_spec,
        scratch_shapes=scratch, name=name,
        compiler_params=_cparams(("parallel", "parallel", "arbitrary")),
    )(*ins)


def _mm_nn(a, w, kind, *, bias=None, out_dtype=F32, name):
    M, K = a.shape
    tm = _tile(M, 512, SUBLANES)
    tk = K if K <= 2048 else _tile(K, 512)
    if kind == "col":
        Nc = w.shape[2]
        N = N_CHIPS * Nc
        tn = _tile(Nc, 1408)
        npc = Nc // tn
        b_spec = pl.BlockSpec((None, tk, tn), lambda i, j, k: (j // npc, k, j % npc))
    else:
        w = w.reshape(-1, w.shape[-1])
        N = w.shape[1]
        tn = _tile(N, 1024)
        b_spec = pl.BlockSpec((tk, tn), lambda i, j, k: (k, j))
    grid = (M // tm, N // tn, K // tk)
    bias_spec = pl.BlockSpec((1, tn), lambda i, j, k: (0, j)) if bias is not None else None
    return _mm_call(
        a, w, dims=(((1,), (0,)), ((), ())), grid=grid,
        a_spec=pl.BlockSpec((tm, tk), lambda i, j, k: (i, k)), b_spec=b_spec,
        out_spec=pl.BlockSpec((tm, tn), lambda i, j, k: (i, j)),
        out_shape=jax.ShapeDtypeStruct((M, N), out_dtype), name=name,
        bias=bias, bias_spec=bias_spec)


def _mm_nt(dy, w, kind, *, add=None, add_scale=1.0, out_dtype=F32, name):
    M, N = dy.shape
    tm = _tile(M, 512, SUBLANES)
    if kind == "col":
        K, Nc = w.shape[1], w.shape[2]
        tn = _tile(Nc, 1408)
        npc = Nc // tn
        tko = _tile(K, 1024)
        b_spec = pl.BlockSpec((None, tko, tn), lambda i, j, r: (r // npc, j, r % npc))
    else:
        w = w.reshape(-1, w.shape[-1])
        K = w.shape[0]
        tn = _tile(N, 1024)
        tko = _tile(K, 1024)
        b_spec = pl.BlockSpec((tko, tn), lambda i, j, r: (j, r))
    grid = (M // tm, K // tko, N // tn)
    add_spec = pl.BlockSpec((tm, tko), lambda i, j, r: (i, j)) if add is not None else None
    return _mm_call(
        dy, w, dims=NT, grid=grid,
        a_spec=pl.BlockSpec((tm, tn), lambda i, j, r: (i, r)), b_spec=b_spec,
        out_spec=pl.BlockSpec((tm, tko), lambda i, j, r: (i, j)),
        out_shape=jax.ShapeDtypeStruct((M, K), out_dtype), name=name,
        add=add, add_spec=add_spec, add_scale=add_scale)


def _mm_tn(a, dy, kind, *, name):
    M, K = a.shape
    N = dy.shape[1]
    tm = _tile(M, 1024, SUBLANES)
    tkw = _tile(K, 1024)
    if kind == "col":
        Nc = N // N_CHIPS
        tn = _tile(Nc, 1408)
        npc = Nc // tn
        out_spec = pl.BlockSpec((None, tkw, tn), lambda i, j, m: (j // npc, i, j % npc))
        out_shape = jax.ShapeDtypeStruct((N_CHIPS, K, Nc), F32)
    else:
        tn = _tile(N, 1024)
        out_spec = pl.BlockSpec((tkw, tn), lambda i, j, m: (i, j))
        out_shape = jax.ShapeDtypeStruct((K, N), F32)
    grid = (K // tkw, N // tn, M // tm)
    out = _mm_call(
        a, dy, dims=TN, grid=grid,
        a_spec=pl.BlockSpec((tm, tkw), lambda i, j, m: (m, i)),
        b_spec=pl.BlockSpec((tm, tn), lambda i, j, m: (m, j)),
        out_spec=out_spec, out_shape=out_shape, name=name)
    if kind == "row":
        out = out.reshape(N_CHIPS, K // N_CHIPS, N)
    return out


def _gelu_grad(x):
    k = math.sqrt(2.0 / math.pi)
    inner = k * (x + 0.044715 * x * x * x)
    t = jnp.tanh(inner)
    return 0.5 * (1.0 + t) + 0.5 * x * (1.0 - t * t) * k * (1.0 + 3.0 * 0.044715 * x * x)


def _ln_fwd(xin, delta, g, b, alpha):
    S, D = xin.shape
    ts = _tile(S, 256, SUBLANES)

    def body(x_ref, d_ref, g_ref, b_ref, y_ref, xh_ref, rs_ref):
        r = alpha * x_ref[...] + d_ref[...]
        mu = jnp.mean(r, axis=-1, keepdims=True)
        rc = r - mu
        var = jnp.mean(rc * rc, axis=-1, keepdims=True)
        rstd = lax.rsqrt(var + LN_EPS)
        xh = rc * rstd
        y_ref[...] = xh * g_ref[...] + b_ref[...]
        xh_ref[...] = xh
        rs_ref[...] = rstd

    row = pl.BlockSpec((ts, D), lambda i: (i, 0))
    vec = pl.BlockSpec((1, D), lambda i: (0, 0))
    return pl.pallas_call(
        body, grid=(S // ts,), in_specs=[row, row, vec, vec],
        out_specs=[row, row, pl.BlockSpec((ts, 1), lambda i: (i, 0))],
        out_shape=[jax.ShapeDtypeStruct((S, D), F32), jax.ShapeDtypeStruct((S, D), F32),
                   jax.ShapeDtypeStruct((S, 1), F32)],
        name="ln_fwd", compiler_params=_cparams(("parallel",)),
    )(xin, delta, g, b)


def _ln_bwd(dy, xh, rstd, g):
    S, D = dy.shape
    ts = _tile(S, 256, SUBLANES)

    def body(dy_ref, xh_ref, rs_ref, g_ref, dr_ref, dg_ref, db_ref):
        @pl.when(pl.program_id(0) == 0)
        def _():
            dg_ref[...] = jnp.zeros_like(dg_ref)
            db_ref[...] = jnp.zeros_like(db_ref)

        dyv = dy_ref[...]
        xhv = xh_ref[...]
        dyg = dyv * g_ref[...]
        m1 = jnp.mean(dyg, axis=-1, keepdims=True)
        m2 = jnp.mean(dyg * xhv, axis=-1, keepdims=True)
        dr_ref[...] = rs_ref[...] * (dyg - m1 - xhv * m2)
        dg_ref[...] += jnp.sum(dyv * xhv, axis=0, keepdims=True)
        db_ref[...] += jnp.sum(dyv, axis=0, keepdims=True)

    row = pl.BlockSpec((ts, D), lambda i: (i, 0))
    vec = pl.BlockSpec((1, D), lambda i: (0, 0))
    return pl.pallas_call(
        body, grid=(S // ts,),
        in_specs=[row, row, pl.BlockSpec((ts, 1), lambda i: (i, 0)), vec],
        out_specs=[row, vec, vec],
        out_shape=[jax.ShapeDtypeStruct((S, D), F32), jax.ShapeDtypeStruct((1, D), F32),
                   jax.ShapeDtypeStruct((1, D), F32)],
        name="ln_bwd", compiler_params=_cparams(("arbitrary",)),
    )(dy, xh, rstd, g)


def _merge_fwd(proj, gate_off, p_sb, p_ssm, p_mem):
    S, D = p_sb.shape
    ts = _tile(S, 256, SUBLANES)
    gb = gate_off // D

    def body(g0, g1, g2, a0, a1, a2, o_ref):
        o_ref[...] = (jax.nn.sigmoid(g0[...]) * a0[...] + jax.nn.sigmoid(g1[...]) * a1[...]
                      + jax.nn.sigmoid(g2[...]) * a2[...])

    row = pl.BlockSpec((ts, D), lambda i: (i, 0))
    gates = [pl.BlockSpec((ts, D), functools.partial(lambda i, n: (i, gb + n), n=n)) for n in range(3)]
    return pl.pallas_call(
        body, grid=(S // ts,), in_specs=gates + [row, row, row], out_specs=row,
        out_shape=jax.ShapeDtypeStruct((S, D), F32), name="merge_fwd",
        compiler_params=_cparams(("parallel",)),
    )(proj, proj, proj, p_sb, p_ssm, p_mem)


def _merge_bwd(dmerged, proj, gate_off, p_sb, p_ssm, p_mem):
    S, D = p_sb.shape
    ts = _tile(S, 256, SUBLANES)
    gb = gate_off // D

    def body(dm_ref, g0, g1, g2, a0, a1, a2, d0, d1, d2, l0, l1, l2):
        dm = dm_ref[...]
        for g_ref, a_ref, d_ref, l_ref in ((g0, a0, d0, l0), (g1, a1, d1, l1), (g2, a2, d2, l2)):
            s = jax.nn.sigmoid(g_ref[...])
            d_ref[...] = (dm * s).astype(d_ref.dtype)
            l_ref[...] = dm * a_ref[...] * s * (1.0 - s)

    row = pl.BlockSpec((ts, D), lambda i: (i, 0))
    gates = [pl.BlockSpec((ts, D), functools.partial(lambda i, n: (i, gb + n), n=n)) for n in range(3)]
    sd = jax.ShapeDtypeStruct((S, D), F32)
    return pl.pallas_call(
        body, grid=(S // ts,), in_specs=[row] + gates + [row, row, row], out_specs=[row] * 6,
        out_shape=[sd] * 6, name="merge_bwd", compiler_params=_cparams(("parallel",)),
    )(dmerged, proj, proj, proj, p_sb, p_ssm, p_mem)


def _glu_fwd(glu):
    S, W2 = glu.shape
    W = W2 // 2
    ts = _tile(S, 512, SUBLANES)

    def body(x_ref, o_ref):
        o_ref[...] = x_ref[:, :W] * jax.nn.sigmoid(x_ref[:, W:])

    return pl.pallas_call(
        body, grid=(S // ts,), in_specs=[pl.BlockSpec((ts, W2), lambda i: (i, 0))],
        out_specs=pl.BlockSpec((ts, W), lambda i: (i, 0)),
        out_shape=jax.ShapeDtypeStruct((S, W), F32), name="glu_fwd",
        compiler_params=_cparams(("parallel",)),
    )(glu)


def _glu_bwd(dzz, glu):
    S, W2 = glu.shape
    W = W2 // 2
    ts = _tile(S, 512, SUBLANES)

    def body(d_ref, x_ref, o_ref):
        d = d_ref[...]
        a = x_ref[:, :W]
        s = jax.nn.sigmoid(x_ref[:, W:])
        o_ref[:, :W] = d * s
        o_ref[:, W:] = d * a * s * (1.0 - s)

    return pl.pallas_call(
        body, grid=(S // ts,),
        in_specs=[pl.BlockSpec((ts, W), lambda i: (i, 0)), pl.BlockSpec((ts, W2), lambda i: (i, 0))],
        out_specs=pl.BlockSpec((ts, W2), lambda i: (i, 0)),
        out_shape=jax.ShapeDtypeStruct((S, W2), F32), name="glu_bwd",
        compiler_params=_cparams(("parallel",)),
    )(dzz, glu)


def _swiglu_fwd(gu):
    S, F2 = gu.shape
    Fh = F2 // 2
    ts = _tile(S, 128, SUBLANES)

    def body(x_ref, o_ref):
        fg = x_ref[:, :Fh]
        o_ref[...] = (fg * jax.nn.sigmoid(fg) * x_ref[:, Fh:]).astype(o_ref.dtype)

    return pl.pallas_call(
        body, grid=(S // ts,), in_specs=[pl.BlockSpec((ts, F2), lambda i: (i, 0))],
        out_specs=pl.BlockSpec((ts, Fh), lambda i: (i, 0)),
        out_shape=jax.ShapeDtypeStruct((S, Fh), BF16), name="swiglu_fwd",
        compiler_params=_cparams(("parallel",)),
    )(gu)


def _swiglu_bwd(dhid, gu):
    S, F2 = gu.shape
    Fh = F2 // 2
    ts = _tile(S, 128, SUBLANES)

    def body(d_ref, x_ref, o_ref):
        d = d_ref[...]
        fg = x_ref[:, :Fh]
        fu = x_ref[:, Fh:]
        s = jax.nn.sigmoid(fg)
        o_ref[:, :Fh] = (d * fu * s * (1.0 + fg * (1.0 - s))).astype(o_ref.dtype)
        o_ref[:, Fh:] = (d * fg * s).astype(o_ref.dtype)

    return pl.pallas_call(
        body, grid=(S // ts,),
        in_specs=[pl.BlockSpec((ts, Fh), lambda i: (i, 0)), pl.BlockSpec((ts, F2), lambda i: (i, 0))],
        out_specs=pl.BlockSpec((ts, F2), lambda i: (i, 0)),
        out_shape=jax.ShapeDtypeStruct((S, F2), BF16), name="swiglu_bwd",
        compiler_params=_cparams(("parallel",)),
    )(dhid, gu)


def _assemble_dproj(pieces):
    S = pieces[0].shape[0]
    widths = [p.shape[1] for p in pieces]
    total = sum(widths)
    ts = _tile(S, 128, SUBLANES)
    n = len(pieces)

    def body(*refs):
        o_ref, b_ref = refs[n], refs[n + 1]

        @pl.when(pl.program_id(0) == 0)
        def _():
            b_ref[...] = jnp.zeros_like(b_ref)

        off = 0
        for r, w in zip(refs[:n], widths):
            v = r[...].astype(F32)
            o_ref[:, off:off + w] = v
            b_ref[:, off:off + w] += jnp.sum(v, axis=0, keepdims=True)
            off += w

    return pl.pallas_call(
        body, grid=(S // ts,),
        in_specs=[pl.BlockSpec((ts, w), lambda i: (i, 0)) for w in widths],
        out_specs=[pl.BlockSpec((ts, total), lambda i: (i, 0)), pl.BlockSpec((1, total), lambda i: (0, 0))],
        out_shape=[jax.ShapeDtypeStruct((S, total), F32), jax.ShapeDtypeStruct((1, total), F32)],
        name="assemble_dproj", compiler_params=_cparams(("arbitrary",)),
    )(*pieces)


def _loss_head(y, target):
    S, D = y.shape
    ts = _tile(S, 256, SUBLANES)

    def body(y_ref, t_ref, dy_ref, l_ref):
        @pl.when(pl.program_id(0) == 0)
        def _():
            l_ref[...] = jnp.zeros_like(l_ref)

        e = y_ref[...] - t_ref[...]
        dy_ref[...] = e * (1.0 / D)
        part = jnp.sum(jnp.sum(e * e, axis=1, keepdims=True), axis=0, keepdims=True) * (0.5 / D)
        l_ref[...] += jnp.broadcast_to(part, l_ref.shape)

    row = pl.BlockSpec((ts, D), lambda i: (i, 0))
    return pl.pallas_call(
        body, grid=(S // ts,), in_specs=[row, row],
        out_specs=[row, pl.BlockSpec((1, LANES), lambda i: (0, 0))],
        out_shape=[jax.ShapeDtypeStruct((S, D), F32), jax.ShapeDtypeStruct((1, LANES), F32)],
        name="loss_head", compiler_params=_cparams(("arbitrary",)),
    )(y, target)


SB_TQ = 256
SB_TK = 128


def _sb_tile_terms(q, kb, scale, t_idx, s_idx):
    z = _dot(q, kb, NT) * scale
    causal = s_idx < t_idx
    soft = jnp.log(1.0 + jnp.exp(-jnp.abs(z)))
    ls = jnp.minimum(z, 0.0) - soft
    l1m = jnp.where(causal, jnp.minimum(-z, 0.0) - soft, 0.0)
    return causal, ls, l1m


def _sb_fwd(proj, q_off, k_off, v_off, heads):
    S = proj.shape[0]
    Dh = HEAD_DIM
    TQ = min(SB_TQ, S)
    TK = SB_TK
    scale = Dh ** -0.5
    qb, kb0, vb0 = q_off // Dh, k_off // Dh, v_off // Dh

    def body(q_ref, k_ref, v_ref, o_ref, c_ref):
        qi = pl.program_id(1)
        q = q_ref[...].astype(BF16)
        ri = lax.broadcasted_iota(jnp.int32, (TK, TK), 0)
        ci = lax.broadcasted_iota(jnp.int32, (TK, TK), 1)
        upper = (ri > ci).astype(BF16)
        t_idx = qi * TQ + lax.broadcasted_iota(jnp.int32, (TQ, TK), 0)
        s_loc = lax.broadcasted_iota(jnp.int32, (TQ, TK), 1)
        nkb = (qi * TQ + TQ) // TK

        def step(jj, carry):
            c, acc = carry
            kj = nkb - 1 - jj
            off = pl.multiple_of(kj * TK, TK)
            kblk = k_ref[pl.ds(off, TK), :].astype(BF16)
            vblk = v_ref[pl.ds(off, TK), :].astype(BF16)
            causal, ls, l1m = _sb_tile_terms(q, kblk, scale, t_idx, s_loc + kj * TK)
            suf = _dot_mask(l1m, upper)
            w = jnp.where(causal, jnp.exp(ls + suf + c), 0.0)
            acc = acc + _dot(w.astype(BF16), vblk)
            c = c + jnp.sum(l1m, axis=1, keepdims=True)
            return c, acc

        c, acc = lax.fori_loop(0, nkb, step, (jnp.zeros((TQ, 1), F32), jnp.zeros((TQ, Dh), F32)))
        o_ref[...] = acc
        c_ref[...] = c

    return pl.pallas_call(
        body, grid=(heads, S // TQ),
        in_specs=[pl.BlockSpec((TQ, Dh), lambda h, i: (i, qb + h)),
                  pl.BlockSpec((S, Dh), lambda h, i: (0, kb0 + h)),
                  pl.BlockSpec((S, Dh), lambda h, i: (0, vb0 + h))],
        out_specs=[pl.BlockSpec((TQ, Dh), lambda h, i: (i, h)), pl.BlockSpec((None, TQ, 1), lambda h, i: (h, i, 0))],
        out_shape=[jax.ShapeDtypeStruct((S, heads * Dh), F32), jax.ShapeDtypeStruct((heads, S, 1), F32)],
        name="sb_fwd", compiler_params=_cparams(("parallel", "arbitrary")),
    )(proj, proj, proj)


def _sb_bwd(proj, q_off, k_off, v_off, heads, dout, ctot):
    S = proj.shape[0]
    Dh = HEAD_DIM
    TQ = min(SB_TQ, S)
    TK = SB_TK
    scale = Dh ** -0.5
    qb, kb0, vb0 = q_off // Dh, k_off // Dh, v_off // Dh

    def body(q_ref, k_ref, v_ref, do_ref, c_ref, dq_ref, dk_ref, dv_ref):
        qi = pl.program_id(1)

        @pl.when(qi == 0)
        def _():
            dk_ref[...] = jnp.zeros_like(dk_ref)
            dv_ref[...] = jnp.zeros_like(dv_ref)

        q = q_ref[...].astype(BF16)
        do = do_ref[...].astype(BF16)
        ctot = c_ref[...]
        ri = lax.broadcasted_iota(jnp.int32, (TK, TK), 0)
        ci = lax.broadcasted_iota(jnp.int32, (TK, TK), 1)
        lower_incl = (ri <= ci).astype(BF16)
        lower = (ri < ci).astype(BF16)
        t_idx = qi * TQ + lax.broadcasted_iota(jnp.int32, (TQ, TK), 0)
        s_loc = lax.broadcasted_iota(jnp.int32, (TQ, TK), 1)
        nkb = (qi * TQ + TQ) // TK

        def step(kj, carry):
            cl, ce, dq = carry
            off = pl.multiple_of(kj * TK, TK)
            kblk = k_ref[pl.ds(off, TK), :].astype(BF16)
            vblk = v_ref[pl.ds(off, TK), :].astype(BF16)
            causal, ls, l1m = _sb_tile_terms(q, kblk, scale, t_idx, s_loc + kj * TK)
            after = ctot - cl - _dot_mask(l1m, lower_incl)
            w = jnp.where(causal, jnp.exp(ls + after), 0.0)
            dw = _dot(do, vblk, NT)
            e = w * dw
            before = ce + _dot_mask(e, lower)
            beta = jnp.exp(ls)
            dz = jnp.where(causal, e * (1.0 - beta) - beta * before, 0.0) * scale
            dzb = dz.astype(BF16)
            dq = dq + _dot(dzb, kblk)
            dk_ref[pl.ds(off, TK), :] += _dot(dzb, q, TN)
            dv_ref[pl.ds(off, TK), :] += _dot(w.astype(BF16), do, TN)
            cl = cl + jnp.sum(l1m, axis=1, keepdims=True)
            ce = ce + jnp.sum(e, axis=1, keepdims=True)
            return cl, ce, dq

        zero = jnp.zeros((TQ, 1), F32)
        _, _, dq = lax.fori_loop(0, nkb, step, (zero, zero, jnp.zeros((TQ, Dh), F32)))
        dq_ref[...] = dq

    blk = pl.BlockSpec((TQ, Dh), lambda h, i: (i, h))
    col = pl.BlockSpec((S, Dh), lambda h, i: (0, h))
    sd = jax.ShapeDtypeStruct((S, heads * Dh), F32)
    return pl.pallas_call(
        body, grid=(heads, S // TQ),
        in_specs=[pl.BlockSpec((TQ, Dh), lambda h, i: (i, qb + h)),
                  pl.BlockSpec((S, Dh), lambda h, i: (0, kb0 + h)),
                  pl.BlockSpec((S, Dh), lambda h, i: (0, vb0 + h)), blk,
                  pl.BlockSpec((None, TQ, 1), lambda h, i: (h, i, 0))],
        out_specs=[blk, col, col], out_shape=[sd, sd, sd], name="sb_bwd",
        compiler_params=_cparams(("parallel", "arbitrary")),
    )(proj, proj, proj, dout, ctot)


def _mem_probs(qh, kh, scale):
    s = _dot(qh, kh, NT) * scale
    m = jnp.max(s, axis=-1, keepdims=True)
    p = jnp.exp(s - m)
    return p / jnp.sum(p, axis=-1, keepdims=True)


def _mem_fwd(proj, q_off, width, kv):
    S = proj.shape[0]
    Dh = HEAD_DIM
    heads = width // Dh
    ts = _tile(S, 512, SUBLANES)
    scale = Dh ** -0.5
    M = kv.shape[0]

    def body(q_ref, kv_ref, o_ref):
        for h in range(heads):
            qh = q_ref[:, h * Dh:(h + 1) * Dh].astype(BF16)
            kh = kv_ref[:, h * Dh:(h + 1) * Dh].astype(BF16)
            vh = kv_ref[:, width + h * Dh:width + (h + 1) * Dh].astype(BF16)
            p = _mem_probs(qh, kh, scale)
            o_ref[:, h * Dh:(h + 1) * Dh] = _dot(p.astype(BF16), vh)

    return pl.pallas_call(
        body, grid=(S // ts,),
        in_specs=[pl.BlockSpec((ts, width), lambda i: (i, q_off // width)),
                  pl.BlockSpec((M, 2 * width), lambda i: (0, 0))],
        out_specs=pl.BlockSpec((ts, width), lambda i: (i, 0)),
        out_shape=jax.ShapeDtypeStruct((S, width), F32), name="mem_fwd",
        compiler_params=_cparams(("parallel",)),
    )(proj, kv)


def _mem_bwd(proj, q_off, width, kv, dmm):
    S = proj.shape[0]
    Dh = HEAD_DIM
    heads = width // Dh
    ts = _tile(S, 512, SUBLANES)
    scale = Dh ** -0.5
    M = kv.shape[0]

    def body(q_ref, kv_ref, d_ref, dq_ref, dkv_ref):
        @pl.when(pl.program_id(0) == 0)
        def _():
            dkv_ref[...] = jnp.zeros_like(dkv_ref)

        for h in range(heads):
            qh = q_ref[:, h * Dh:(h + 1) * Dh].astype(BF16)
            kh = kv_ref[:, h * Dh:(h + 1) * Dh].astype(BF16)
            vh = kv_ref[:, width + h * Dh:width + (h + 1) * Dh].astype(BF16)
            dh = d_ref[:, h * Dh:(h + 1) * Dh].astype(BF16)
            p = _mem_probs(qh, kh, scale)
            dp = _dot(dh, vh, NT)
            ds = p * (dp - jnp.sum(dp * p, axis=-1, keepdims=True)) * scale
            dsb = ds.astype(BF16)
            dq_ref[:, h * Dh:(h + 1) * Dh] = _dot(dsb, kh)
            dkv_ref[:, h * Dh:(h + 1) * Dh] += _dot(dsb, qh, TN)
            dkv_ref[:, width + h * Dh:width + (h + 1) * Dh] += _dot(p.astype(BF16), dh, TN)

    row = pl.BlockSpec((ts, width), lambda i: (i, 0))
    full = pl.BlockSpec((M, 2 * width), lambda i: (0, 0))
    return pl.pallas_call(
        body, grid=(S // ts,),
        in_specs=[pl.BlockSpec((ts, width), lambda i: (i, q_off // width)), full, row],
        out_specs=[row, full],
        out_shape=[jax.ShapeDtypeStruct((S, width), F32), jax.ShapeDtypeStruct((M, 2 * width), F32)],
        name="mem_bwd", compiler_params=_cparams(("arbitrary",)),
    )(proj, kv, dmm)


def _disc_math(lre, lim, logdt, bre_t, bim_t):
    dt = jnp.exp(logdt)
    mag = jnp.exp(lre * dt)
    ang = lim * dt
    a = mag * jnp.cos(ang)
    b = mag * jnp.sin(ang)
    den = lre * lre + lim * lim
    nr = a - 1.0
    fre = (nr * lre + b * lim) / den
    fim = (b * lre - nr * lim) / den
    bbre = fre * bre_t - fim * bim_t
    bbim = fre * bim_t + fim * bre_t
    return a, b, bbre, bbim


def _s5_disc(lre, lim, logdt, bre_t, bim_t):
    G, _, P = lre.shape
    C = bre_t.shape[1]

    def body(lre_ref, lim_ref, dt_ref, br_ref, bi_ref, a_ref, b_ref, bbre_ref, bbim_ref):
        a, b, bbre, bbim = _disc_math(lre_ref[...], lim_ref[...], dt_ref[...], br_ref[...], bi_ref[...])
        a_ref[...] = a
        b_ref[...] = b
        bbre_ref[...] = bbre
        bbim_ref[...] = bbim

    gp = jax.ShapeDtypeStruct((G, 1, P), F32)
    gcp = jax.ShapeDtypeStruct((G, C, P), F32)
    return pl.pallas_call(
        body, in_specs=[VMEM] * 5, out_specs=[VMEM] * 4, out_shape=[gp, gp, gcp, gcp], name="s5_disc",
    )(lre, lim, logdt, bre_t, bim_t)


def _s5_disc_bwd(lre, lim, logdt, bre_t, bim_t, da, db, dbbre, dbbim):
    G, _, P = lre.shape
    C = bre_t.shape[1]

    def body(lre_ref, lim_ref, dt_ref, br_ref, bi_ref, da_ref, db_ref, dbr_ref, dbi_ref,
             o_lre, o_lim, o_dt, o_br, o_bi):
        _, vjp = jax.vjp(_disc_math, lre_ref[...], lim_ref[...], dt_ref[...], br_ref[...], bi_ref[...])
        g = vjp((da_ref[...], db_ref[...], dbr_ref[...], dbi_ref[...]))
        o_lre[...] = g[0]
        o_lim[...] = g[1]
        o_dt[...] = g[2]
        o_br[...] = g[3]
        o_bi[...] = g[4]

    gp = jax.ShapeDtypeStruct((G, 1, P), F32)
    gcp = jax.ShapeDtypeStruct((G, C, P), F32)
    return pl.pallas_call(
        body, in_specs=[VMEM] * 9, out_specs=[VMEM] * 5,
        out_shape=[gp, gp, jax.ShapeDtypeStruct((G, 1, 1), F32), gcp, gcp], name="s5_disc_bwd",
    )(lre, lim, logdt, bre_t, bim_t, da, db, dbbre, dbbim)


S5_CHUNK = 256


def _load_once(pairs):
    @pl.when(pl.program_id(0) == 0)
    def _():
        for src, dst in pairs:
            pltpu.sync_copy(src, dst)


def _s5_fwd(proj, u_off, width, a_row, b_row, bmre, bmim, cmre, cmimn, d_row):
    S = proj.shape[0]
    GP = a_row.shape[1]
    T = min(S5_CHUNK, S)

    def body(u_ref, a_ref, b_ref, d_ref, bre_hbm, bim_hbm, cre_hbm, cim_hbm,
             uo_ref, y_ref, gy_ref, hre_ref, him_ref, st_ref, bure_s, buim_s, bre_ref, bim_ref, cre_ref, cim_ref):
        @pl.when(pl.program_id(0) == 0)
        def _():
            st_ref[...] = jnp.zeros_like(st_ref)

        _load_once([(bre_hbm, bre_ref), (bim_hbm, bim_ref), (cre_hbm, cre_ref), (cim_hbm, cim_ref)])
        u = u_ref[...]
        uo_ref[...] = u
        bure_s[...] = _dot_hp(u, bre_ref[...])
        buim_s[...] = _dot_hp(u, bim_ref[...])
        a = a_ref[...]
        b = b_ref[...]

        def step(ii, carry):
            hre, him = carry
            base = pl.multiple_of(ii * SUBLANES, SUBLANES)
            br = bure_s[pl.ds(base, SUBLANES), :]
            bi = buim_s[pl.ds(base, SUBLANES), :]
            rows_re, rows_im = [], []
            for j in range(SUBLANES):
                nre = a * hre - b * him + br[j:j + 1, :]
                nim = a * him + b * hre + bi[j:j + 1, :]
                hre, him = nre, nim
                rows_re.append(nre)
                rows_im.append(nim)
            hre_ref[pl.ds(base, SUBLANES), :] = jnp.concatenate(rows_re, axis=0)
            him_ref[pl.ds(base, SUBLANES), :] = jnp.concatenate(rows_im, axis=0)
            return hre, him

        hre, him = lax.fori_loop(0, T // SUBLANES, step, (st_ref[0:1, :], st_ref[1:2, :]))
        st_ref[0:1, :] = hre
        st_ref[1:2, :] = him
        y = _dot_hp(hre_ref[...], cre_ref[...]) + _dot_hp(him_ref[...], cim_ref[...]) + d_ref[...] * u
        y_ref[...] = y
        gy_ref[...] = jax.nn.gelu(y)

    c0 = lambda i: (0, 0)
    urow = pl.BlockSpec((T, width), lambda i: (i, u_off // width))
    row = pl.BlockSpec((T, width), lambda i: (i, 0))
    hrow = pl.BlockSpec((T, GP), lambda i: (i, 0))
    sw = jax.ShapeDtypeStruct((S, width), F32)
    sg = jax.ShapeDtypeStruct((S, GP), F32)
    return pl.pallas_call(
        body, grid=(S // T,),
        in_specs=[urow, pl.BlockSpec((1, GP), c0), pl.BlockSpec((1, GP), c0), pl.BlockSpec((1, width), c0),
                  ANY, ANY, ANY, ANY],
        out_specs=[row, row, row, hrow, hrow], out_shape=[sw, sw, sw, sg, sg],
        scratch_shapes=[pltpu.VMEM((SUBLANES, GP), F32), pltpu.VMEM((T, GP), F32), pltpu.VMEM((T, GP), F32),
                        pltpu.VMEM((width, GP), F32), pltpu.VMEM((width, GP), F32),
                        pltpu.VMEM((GP, width), F32), pltpu.VMEM((GP, width), F32)],
        name="s5_fwd", compiler_params=_cparams(("arbitrary",)),
    )(proj, a_row, b_row, d_row, bmre, bmim, cmre, cmimn)


def _s5_bwd(u, dgy, y, hre, him, a_row, b_row, bmre, bmim, cmre, cmimn, d_row):
    S, width = u.shape
    GP = a_row.shape[1]
    T = min(S5_CHUNK, S)
    nchunk = S // T

    def body(u_ref, dgy_ref, y_ref, hre_ref, him_ref, a_ref, b_ref, d_ref, bre_hbm, bim_hbm, cre_hbm, cim_hbm,
             du_ref, dy_ref, gre_s, gim_s, dd_ref, da_ref, db_ref,
             st_ref, bre_ref, bim_ref, cre_ref, cim_ref):
        @pl.when(pl.program_id(0) == 0)
        def _():
            st_ref[...] = jnp.zeros_like(st_ref)
            for r in (dd_ref, da_ref, db_ref):
                r[...] = jnp.zeros_like(r)

        _load_once([(bre_hbm, bre_ref), (bim_hbm, bim_ref), (cre_hbm, cre_ref), (cim_hbm, cim_ref)])
        u = u_ref[...]
        dy = dgy_ref[...] * _gelu_grad(y_ref[...])
        dy_ref[...] = dy
        gre_s[...] = _dot_hp(dy, cre_ref[...], NT)
        gim_s[...] = _dot_hp(dy, cim_ref[...], NT)
        a = a_ref[...]
        b = b_ref[...]

        def step(ii, carry):
            gre, gim, da, db = carry
            base = pl.multiple_of((T // SUBLANES - 1 - ii) * SUBLANES, SUBLANES)
            dr = gre_s[pl.ds(base, SUBLANES), :]
            di = gim_s[pl.ds(base, SUBLANES), :]
            hr = hre_ref[pl.ds(base, SUBLANES), :]
            hi = him_ref[pl.ds(base, SUBLANES), :]
            rows_re = [None] * SUBLANES
            rows_im = [None] * SUBLANES
            for j in range(SUBLANES - 1, -1, -1):
                hrj = hr[j:j + 1, :]
                hij = hi[j:j + 1, :]
                da = da + gre * hrj + gim * hij
                db = db + gim * hrj - gre * hij
                nre = dr[j:j + 1, :] + a * gre + b * gim
                nim = di[j:j + 1, :] - b * gre + a * gim
                gre, gim = nre, nim
                rows_re[j] = nre
                rows_im[j] = nim
            gre_s[pl.ds(base, SUBLANES), :] = jnp.concatenate(rows_re, axis=0)
            gim_s[pl.ds(base, SUBLANES), :] = jnp.concatenate(rows_im, axis=0)
            return gre, gim, da, db

        zero = jnp.zeros((1, GP), F32)
        gre, gim, da, db = lax.fori_loop(0, T // SUBLANES, step, (st_ref[0:1, :], st_ref[1:2, :], zero, zero))
        st_ref[0:1, :] = gre
        st_ref[1:2, :] = gim
        da_ref[...] += da
        db_ref[...] += db
        du_ref[...] = (_dot_hp(gre_s[...], bre_ref[...], NT) + _dot_hp(gim_s[...], bim_ref[...], NT)
                       + d_ref[...] * dy)
        dd_ref[...] += jnp.sum(dy * u, axis=0, keepdims=True)

    c0 = lambda i: (0, 0)
    rev = lambda i: (nchunk - 1 - i, 0)
    row = pl.BlockSpec((T, width), rev)
    hrow = pl.BlockSpec((T, GP), rev)
    v_gp = pl.BlockSpec((1, GP), c0)
    v_w = pl.BlockSpec((1, width), c0)
    sw = jax.ShapeDtypeStruct((S, width), F32)
    sg = jax.ShapeDtypeStruct((S, GP), F32)
    return pl.pallas_call(
        body, grid=(nchunk,),
        in_specs=[row, row, row, hrow, hrow, v_gp, v_gp, v_w, ANY, ANY, ANY, ANY],
        out_specs=[row, row, hrow, hrow, v_w, v_gp, v_gp],
        out_shape=[sw, sw, sg, sg, jax.ShapeDtypeStruct((1, width), F32),
                   jax.ShapeDtypeStruct((1, GP), F32), jax.ShapeDtypeStruct((1, GP), F32)],
        scratch_shapes=[pltpu.VMEM((SUBLANES, GP), F32),
                        pltpu.VMEM((width, GP), F32), pltpu.VMEM((width, GP), F32),
                        pltpu.VMEM((GP, width), F32), pltpu.VMEM((GP, width), F32)],
        name="s5_bwd", compiler_params=_cparams(("arbitrary",)),
    )(u, dgy, y, hre, him, a_row, b_row, d_row, bmre, bmim, cmre, cmimn)


def _block_diag(x):
    G, A, B = x.shape
    eye = jnp.eye(G, dtype=x.dtype)
    return (eye[:, None, :, None] * x[:, :, None, :]).reshape(G * A, G * B)


def _block_diag_take(m, G):
    A, B = m.shape[0] // G, m.shape[1] // G
    return jnp.einsum("gagb->gab", m.reshape(G, A, G, B))


def _adamw(w, g, m, v):
    shape = w.shape
    C = shape[-1]
    w2, g2, m2, v2 = (t.reshape(-1, C) for t in (w, g, m, v))
    R = w2.shape[0]
    rb = _tile(R, max(SUBLANES, (1 << 19) // C), SUBLANES)
    c1 = 1.0 - ADAM_B1 ** ADAM_STEP
    c2 = 1.0 - ADAM_B2 ** ADAM_STEP

    def body(w_ref, g_ref, m_ref, v_ref, d_ref, nm_ref, nv_ref):
        gv = g_ref[...]
        nm = ADAM_B1 * m_ref[...] + (1.0 - ADAM_B1) * gv
        nv = ADAM_B2 * v_ref[...] + (1.0 - ADAM_B2) * (gv * gv)
        d_ref[...] = -ADAM_LR * ((nm / c1) / (jnp.sqrt(nv / c2) + ADAM_EPS) + ADAM_WD * w_ref[...])
        nm_ref[...] = nm
        nv_ref[...] = nv

    blk = pl.BlockSpec((rb, C), lambda i: (i, 0))
    sd = jax.ShapeDtypeStruct((R, C), F32)
    outs = pl.pallas_call(
        body, grid=(R // rb,), in_specs=[blk] * 4, out_specs=[blk] * 3, out_shape=[sd] * 3,
        name="adamw", compiler_params=_cparams(("parallel",)),
    )(w2, g2, m2, v2)
    return tuple(o.reshape(shape) for o in outs)


def _coords():
    x, y, c = lax.axis_index("x"), lax.axis_index("y"), lax.axis_index("c")
    return x, y, c


def _ag_weights(shards):
    n = len(shards)
    L = shards[0].shape[0]
    out_shapes = []
    for s in shards:
        out_shapes += [jax.ShapeDtypeStruct((N_CHIPS,) + s.shape[1:], s.dtype)] * L

    def body(*refs):
        xs = refs[:n]
        outs = [refs[n + a * L:n + (a + 1) * L] for a in range(n)]
        send_sems, recv_sems, local_sems = refs[n + n * L:]
        x, y, c = _coords()
        chip = 2 * x + y
        sibling = (x, y, 1 - c)
        others = [(1 - x, y), (x, 1 - y), (1 - x, 1 - y)]

        def half_rows(ref, cc):
            hr = ref.shape[-2] // 2
            return pl.ds(cc * hr, hr)

        def rcopy(a, l, k, block, to, src=None):
            cx, cy, cc = block
            dst = outs[a][l].at[2 * cx + cy, half_rows(xs[a], cc)]
            return pltpu.make_async_remote_copy(
                src_ref=dst if src is None else src, dst_ref=dst,
                send_sem=send_sems.at[a * 6 + k], recv_sem=recv_sems.at[a * 6 + k],
                device_id=to, device_id_type=MESH)

        def all_layers(a, k):
            ref = xs[a].at[:, half_rows(xs[a], 0)]
            return pltpu.make_async_remote_copy(
                src_ref=ref, dst_ref=ref, send_sem=send_sems.at[a * 6 + k], recv_sem=recv_sems.at[a * 6 + k],
                device_id=sibling, device_id_type=MESH)

        for a in range(n):
            for l in range(L):
                pltpu.make_async_copy(xs[a].at[l], outs[a][l].at[chip], local_sems.at[a]).start()
        for a in range(n):
            for j, ch in enumerate(others):
                for l in range(L):
                    rcopy(a, l, j, (x, y, c), (*ch, c), src=xs[a].at[l, half_rows(xs[a], c)]).start()
        for a in range(n):
            for j, ch in enumerate(others):
                all_layers(a, j).wait_recv()
                for l in range(L):
                    rcopy(a, l, 3 + j, (*ch, c), sibling).start()
        for a in range(n):
            for j in range(3):
                all_layers(a, 3 + j).wait_recv()
        for a in range(n):
            for k in range(6):
                all_layers(a, k).wait_send()
            pltpu.make_async_copy(xs[a], xs[a], local_sems.at[a]).wait()

    res = pl.pallas_call(
        body, out_shape=out_shapes, in_specs=[ANY] * n, out_specs=[ANY] * (n * L),
        scratch_shapes=[pltpu.SemaphoreType.DMA((n * 6,)), pltpu.SemaphoreType.DMA((n * 6,)),
                        pltpu.SemaphoreType.DMA((n,))],
        name="ag_weights",
    )(*shards)
    return [list(res[a * L:(a + 1) * L]) for a in range(n)]


def _rs_pair(grads):
    n = len(grads)
    out_shapes = [jax.ShapeDtypeStruct((N_CHIPS, g.shape[1] // 2, g.shape[2]), F32) for g in grads]

    def body(*refs):
        gs = refs[:n]
        outs = refs[n:2 * n]
        send_sems, recv_sems = refs[2 * n:]
        x, y, c = _coords()
        copies = []
        for i in range(n):
            hr = gs[i].shape[1] // 2
            cp = pltpu.make_async_remote_copy(
                src_ref=gs[i].at[:, pl.ds((1 - c) * hr, hr)], dst_ref=outs[i],
                send_sem=send_sems.at[i], recv_sem=recv_sems.at[i],
                device_id=(x, y, 1 - c), device_id_type=MESH)
            cp.start()
            copies.append(cp)
        for cp in copies:
            cp.wait_recv()
        for cp in copies:
            cp.wait_send()

    return pl.pallas_call(
        body, out_shape=out_shapes, in_specs=[ANY] * n, out_specs=[ANY] * n,
        scratch_shapes=[pltpu.SemaphoreType.DMA((n,)), pltpu.SemaphoreType.DMA((n,))],
        name="rs_pair",
    )(*grads)


def _pair_add(g, r, place):
    _, R, C = g.shape
    hr = R // 2
    rb = _tile(hr, max(16, (1 << 19) // C), 16)
    nb = hr // rb

    def body(place_ref, g_ref, r_ref, p16_ref, own_ref):
        s = g_ref[...] + r_ref[...]
        p16_ref[...] = s.astype(BF16)

        @pl.when(pl.program_id(1) == place_ref[1])
        def _():
            own_ref[...] = s

    grid_spec = pltpu.PrefetchScalarGridSpec(
        num_scalar_prefetch=1, grid=(nb, N_CHIPS),
        in_specs=[pl.BlockSpec((None, rb, C), lambda i, k, p: (k, p[0] * nb + i, 0)),
                  pl.BlockSpec((None, rb, C), lambda i, k, p: (k, i, 0))],
        out_specs=[pl.BlockSpec((None, rb, C), lambda i, k, p: (k, i, 0)),
                   pl.BlockSpec((rb, C), lambda i, k, p: (i, 0))])
    return pl.pallas_call(
        body, grid_spec=grid_spec,
        out_shape=[jax.ShapeDtypeStruct((N_CHIPS, hr, C), BF16), jax.ShapeDtypeStruct((hr, C), F32)],
        name="pair_add", compiler_params=_cparams(("arbitrary", "arbitrary")),
    )(place, g, r)


def _rs_chips(p16, n_w, L):
    n = len(p16)
    assert L == N_CHIPS
    out_shapes = [jax.ShapeDtypeStruct((3,) + p.shape[1:], BF16) for p in p16]

    def body(*refs):
        ps = refs[:n]
        outs = refs[n:2 * n]
        send_sems, recv_sems = refs[2 * n:]
        x, y, c = _coords()
        for a in range(n_w):
            for r in (1, 2, 3):
                kx = x ^ (r >> 1) if (r >> 1) else x
                ky = y ^ (r & 1) if (r & 1) else y
                for l in range(L):
                    i = a * L + l
                    pltpu.make_async_remote_copy(
                        src_ref=ps[i].at[2 * kx + ky], dst_ref=outs[i].at[r - 1],
                        send_sem=send_sems.at[a * 3 + r - 1], recv_sem=recv_sems.at[a * 3 + r - 1],
                        device_id=(kx, ky, c), device_id_type=MESH).start()
        for a in range(n_w):
            whole = ps[a * L]
            for r in (1, 2, 3):
                pltpu.make_async_remote_copy(
                    src_ref=whole, dst_ref=whole,
                    send_sem=send_sems.at[a * 3 + r - 1], recv_sem=recv_sems.at[a * 3 + r - 1],
                    device_id=(x, y, c), device_id_type=MESH).wait()

    return pl.pallas_call(
        body, out_shape=out_shapes, in_specs=[ANY] * n, out_specs=[ANY] * n,
        scratch_shapes=[pltpu.SemaphoreType.DMA((n_w * 3,)), pltpu.SemaphoreType.DMA((n_w * 3,))],
        name="rs_chips",
    )(*p16)


def _chip_sum(own, recv):
    hr, C = own.shape
    rb = _tile(hr, max(16, (1 << 19) // C), 16)

    def body(o_ref, r_ref, s_ref):
        s = o_ref[...] + r_ref[0].astype(F32)
        s = s + r_ref[1].astype(F32)
        s_ref[...] = s + r_ref[2].astype(F32)

    return pl.pallas_call(
        body, grid=(hr // rb,),
        in_specs=[pl.BlockSpec((rb, C), lambda i: (i, 0)), pl.BlockSpec((3, rb, C), lambda i: (0, i, 0))],
        out_specs=pl.BlockSpec((rb, C), lambda i: (i, 0)),
        out_shape=jax.ShapeDtypeStruct((hr, C), F32), name="chip_sum",
        compiler_params=_cparams(("parallel",)),
    )(own, recv)


def _share_pair(halves, n_w, L):
    out_shapes = []
    for a in range(n_w):
        hr, C = halves[a * L].shape
        out_shapes.append(jax.ShapeDtypeStruct((L, 2 * hr, C), F32))
    n = len(halves)

    def body(*refs):
        hs = refs[:n]
        outs = refs[n:n + n_w]
        send_sems, recv_sems, local_sems = refs[n + n_w:]
        x, y, c = _coords()
        for a in range(n_w):
            hr = hs[a * L].shape[0]
            for l in range(L):
                dst = outs[a].at[l, pl.ds(c * hr, hr)]
                pltpu.make_async_copy(hs[a * L + l], dst, local_sems.at[a]).start()
                pltpu.make_async_remote_copy(
                    src_ref=hs[a * L + l], dst_ref=dst, send_sem=send_sems.at[a], recv_sem=recv_sems.at[a],
                    device_id=(x, y, 1 - c), device_id_type=MESH).start()
        for a in range(n_w):
            hr = hs[a * L].shape[0]
            part = outs[a].at[:, pl.ds(0, hr)]
            pltpu.make_async_remote_copy(
                src_ref=part, dst_ref=part, send_sem=send_sems.at[a], recv_sem=recv_sems.at[a],
                device_id=(x, y, 1 - c), device_id_type=MESH).wait()
            pltpu.make_async_copy(part, part, local_sems.at[a]).wait()

    return pl.pallas_call(
        body, out_shape=out_shapes, in_specs=[ANY] * n, out_specs=[ANY] * n_w,
        scratch_shapes=[pltpu.SemaphoreType.DMA((n_w,)), pltpu.SemaphoreType.DMA((n_w,)),
                        pltpu.SemaphoreType.DMA((n_w,))],
        name="share_pair",
    )(*halves)


def _small_allreduce(packed):
    m_per, ncol = packed.shape

    def body(x_ref, out_ref, tot_ref, send_sems, recv_sems, local_sem):
        x, y, c = _coords()
        me, sibling = (x, y, c), (x, y, 1 - c)
        chips = [(1 - x, y), (x, 1 - y), (1 - x, 1 - y)]

        def rows(px, py, pc):
            return out_ref.at[pl.ds((4 * px + 2 * py + pc) * m_per, m_per), :]

        def copy(k, block, to, src=None):
            return pltpu.make_async_remote_copy(
                src_ref=rows(*block) if src is None else src, dst_ref=rows(*block),
                send_sem=send_sems.at[k], recv_sem=recv_sems.at[k], device_id=to, device_id_type=MESH)

        mine = pltpu.make_async_copy(x_ref, rows(*me), local_sem)
        mine.start()
        first = [copy(0, me, sibling, src=x_ref)]
        first += [copy(1 + j, me, (*chip, c), src=x_ref) for j, chip in enumerate(chips)]
        for cp in first:
            cp.start()
        passed = [copy(4 + j, (*chip, c), sibling) for j, chip in enumerate(chips)]
        for j, chip in enumerate(chips):
            copy(1 + j, (*chip, c), me).wait_recv()
            passed[j].start()
        copy(0, sibling, me).wait_recv()
        for j, chip in enumerate(chips):
            copy(4 + j, (*chip, 1 - c), me).wait_recv()
        for cp in first + passed:
            cp.wait_send()
        mine.wait()
        tot = out_ref[pl.ds(0, m_per), :]
        for d in range(1, N_DEV):
            tot = tot + out_ref[pl.ds(d * m_per, m_per), :]
        tot_ref[...] = tot

    _, tot = pl.pallas_call(
        body,
        out_shape=[jax.ShapeDtypeStruct((N_DEV * m_per, ncol), F32), jax.ShapeDtypeStruct((m_per, ncol), F32)],
        in_specs=[VMEM], out_specs=[VMEM, VMEM],
        scratch_shapes=[pltpu.SemaphoreType.DMA((7,)), pltpu.SemaphoreType.DMA((7,)), pltpu.SemaphoreType.DMA],
        name="small_allreduce",
        compiler_params=pltpu.CompilerParams(vmem_limit_bytes=VMEM_LIMIT_MB * 1024 * 1024),
    )(packed)
    return tot


BIG = ["w_in", "sb_w_out", "ssm_w_glu", "ssm_w_out", "mem_w_kv", "mem_w_out", "w_o", "ffn_w_gate_up", "ffn_w_down"]
KIND = {"w_in": "col", "sb_w_out": "col", "ssm_w_glu": "col", "ssm_w_out": "col", "mem_w_kv": "row",
        "mem_w_out": "col", "w_o": "row", "ffn_w_gate_up": "col", "ffn_w_down": "row"}
SMALL = ["b_in", "ssm_lambda_re", "ssm_lambda_im", "ssm_log_dt", "ssm_b_re", "ssm_b_im", "ssm_c_re", "ssm_c_im",
         "ssm_d", "ln1_g", "ln1_b", "ln2_g", "ln2_b"]
WEIGHTS = ["w_in", "b_in", "sb_w_out", "ssm_lambda_re", "ssm_lambda_im", "ssm_log_dt", "ssm_b_re", "ssm_b_im",
           "ssm_c_re", "ssm_c_im", "ssm_d", "ssm_w_glu", "ssm_w_out", "mem_w_kv", "mem_w_out", "w_o", "ln1_g",
           "ln1_b", "ffn_w_gate_up", "ffn_w_down", "ln2_g", "ln2_b"]


def _pack(arrs):
    flat = jnp.concatenate([a.reshape(-1).astype(F32) for a in arrs])
    n = flat.shape[0]
    rows = -(-n // LANES)
    rows = -(-rows // SUBLANES) * SUBLANES
    return jnp.pad(flat, (0, rows * LANES - n)).reshape(rows, LANES)


def _unpack(packed, like):
    flat = packed.reshape(-1)
    out, off = [], 0
    for a in like:
        out.append(flat[off:off + a.size].reshape(a.shape))
        off += a.size
    return out


def _step(x, mem, target, W, M1, V1):
    S, D = x.shape[1], x.shape[2]
    L = W["w_in"].shape[0]
    x0 = x.reshape(S, D)
    mem2 = mem.reshape(mem.shape[1], D)
    tgt = target.reshape(S, D)
    alpha = (2 * L) ** 0.25
    sbw = W["sb_w_out"].shape[1]
    ssw = W["ssm_d"].shape[1]
    mw = W["mem_w_out"].shape[1]
    heads = sbw // HEAD_DIM
    G, P = W["ssm_lambda_re"].shape[1], W["ssm_lambda_re"].shape[2]
    q_off, k_off, v_off = 0, sbw, 2 * sbw
    u_off = 3 * sbw
    qm_off = u_off + ssw
    gate_off = qm_off + mw

    gathered = _ag_weights([W[n].astype(BF16) for n in BIG])
    Wg = {n: gathered[i] for i, n in enumerate(BIG)}

    saved = []
    xl = x0
    for l in range(L):
        sv = {"x": xl}
        proj = _mm_nn(xl, Wg["w_in"][l], "col", bias=W["b_in"][l][None, :], name="mm_proj")
        sb, sb_ctot = _sb_fwd(proj, q_off, k_off, v_off, heads)
        p_sb = _mm_nn(sb, Wg["sb_w_out"][l], "col", name="mm_sb_out")

        bre_t = W["ssm_b_re"][l].transpose(0, 2, 1)
        bim_t = W["ssm_b_im"][l].transpose(0, 2, 1)
        logdt = W["ssm_log_dt"][l][:, None, None]
        lre3 = W["ssm_lambda_re"][l][:, None, :]
        lim3 = W["ssm_lambda_im"][l][:, None, :]
        a_gp, b_gp, bbre, bbim = _s5_disc(lre3, lim3, logdt, bre_t, bim_t)
        a_row, b_row = a_gp.reshape(1, G * P), b_gp.reshape(1, G * P)
        bmre, bmim = _block_diag(bbre), _block_diag(bbim)
        cmre = _block_diag(W["ssm_c_re"][l].transpose(0, 2, 1))
        cmimn = _block_diag(-W["ssm_c_im"][l].transpose(0, 2, 1))
        d_row = W["ssm_d"][l][None, :]
        u_ssm, y, gy, hre, him = _s5_fwd(proj, u_off, ssw, a_row, b_row, bmre, bmim, cmre, cmimn, d_row)
        glu = _mm_nn(gy, Wg["ssm_w_glu"][l], "col", name="mm_glu")
        zz = _glu_fwd(glu)
        p_ssm = _mm_nn(zz, Wg["ssm_w_out"][l], "col", name="mm_ssm_out")

        kv = _mm_nn(mem2, Wg["mem_w_kv"][l], "row", name="mm_kv")
        mm_o = _mem_fwd(proj, qm_off, mw, kv)
        p_mem = _mm_nn(mm_o, Wg["mem_w_out"][l], "col", name="mm_mem_out")

        merged = _merge_fwd(proj, gate_off, p_sb, p_ssm, p_mem)
        mix = _mm_nn(merged, Wg["w_o"][l], "row", name="mm_wo")
        x1, xh1, rs1 = _ln_fwd(xl, mix, W["ln1_g"][l][None, :], W["ln1_b"][l][None, :], alpha)
        gu = _mm_nn(x1, Wg["ffn_w_gate_up"][l], "col", name="mm_gate_up")
        hid = _swiglu_fwd(gu)
        ffn = _mm_nn(hid, Wg["ffn_w_down"][l], "row", name="mm_down")
        x2, xh2, rs2 = _ln_fwd(x1, ffn, W["ln2_g"][l][None, :], W["ln2_b"][l][None, :], alpha)
        sv.update(proj=proj, sb=sb, sb_ctot=sb_ctot, p_sb=p_sb, y=y, gy=gy, hre=hre, him=him, glu=glu, zz=zz, p_ssm=p_ssm,
                  kv=kv, mm_o=mm_o, p_mem=p_mem, merged=merged, x1=x1, xh1=xh1, rs1=rs1, gu=gu, hid=hid,
                  xh2=xh2, rs2=rs2, u=u_ssm,
                  disc=(lre3, lim3, logdt, bre_t, bim_t, a_row, b_row, bmre, bmim, cmre, cmimn, d_row))
        saved.append(sv)
        xl = x2

    dxl, loss_part = _loss_head(xl, tgt)

    gbig = {n: [None] * L for n in BIG}
    gsmall = {n: [None] * L for n in SMALL}
    for l in range(L - 1, -1, -1):
        sv = saved[l]
        proj = sv["proj"]
        dr2, dg2, db2 = _ln_bwd(dxl, sv["xh2"], sv["rs2"], W["ln2_g"][l][None, :])
        gsmall["ln2_g"][l], gsmall["ln2_b"][l] = dg2[0], db2[0]
        dhid = _mm_nt(dr2, Wg["ffn_w_down"][l], "row", name="mm_d_hid")
        gbig["ffn_w_down"][l] = _mm_tn(sv["hid"], dr2, "row", name="mm_g_down")
        dgu = _swiglu_bwd(dhid, sv["gu"])
        dx1 = _mm_nt(dgu, Wg["ffn_w_gate_up"][l], "col", add=dr2, add_scale=alpha, name="mm_d_x1")
        gbig["ffn_w_gate_up"][l] = _mm_tn(sv["x1"], dgu, "col", name="mm_g_gate_up")

        dr1, dg1, db1 = _ln_bwd(dx1, sv["xh1"], sv["rs1"], W["ln1_g"][l][None, :])
        gsmall["ln1_g"][l], gsmall["ln1_b"][l] = dg1[0], db1[0]
        dmerged = _mm_nt(dr1, Wg["w_o"][l], "row", name="mm_d_merged")
        gbig["w_o"][l] = _mm_tn(sv["merged"], dr1, "row", name="mm_g_wo")
        dp_sb, dp_ssm, dp_mem, dgl0, dgl1, dgl2 = _merge_bwd(
            dmerged, proj, gate_off, sv["p_sb"], sv["p_ssm"], sv["p_mem"])

        dsb = _mm_nt(dp_sb, Wg["sb_w_out"][l], "col", name="mm_d_sb")
        gbig["sb_w_out"][l] = _mm_tn(sv["sb"], dp_sb, "col", name="mm_g_sb_out")

        dzz = _mm_nt(dp_ssm, Wg["ssm_w_out"][l], "col", name="mm_d_zz")
        gbig["ssm_w_out"][l] = _mm_tn(sv["zz"], dp_ssm, "col", name="mm_g_ssm_out")
        dglu = _glu_bwd(dzz, sv["glu"])
        dgy = _mm_nt(dglu, Wg["ssm_w_glu"][l], "col", name="mm_d_gy")
        gbig["ssm_w_glu"][l] = _mm_tn(sv["gy"], dglu, "col", name="mm_g_glu")
        lre3, lim3, logdt, bre_t, bim_t, a_row, b_row, bmre, bmim, cmre, cmimn, d_row = sv["disc"]
        du, dy_ssm, g_re, g_im, dd, da, db = _s5_bwd(
            sv["u"], dgy, sv["y"], sv["hre"], sv["him"], a_row, b_row, bmre, bmim, cmre, cmimn, d_row)
        dbmre = _mm_tn(sv["u"], g_re, "plain", name="mm_g_ssm_bre")
        dbmim = _mm_tn(sv["u"], g_im, "plain", name="mm_g_ssm_bim")
        dcmre = _mm_tn(sv["hre"], dy_ssm, "plain", name="mm_g_ssm_cre")
        dcmimn = _mm_tn(sv["him"], dy_ssm, "plain", name="mm_g_ssm_cim")
        dlre, dlim, dlogdt, dbre_t, dbim_t = _s5_disc_bwd(
            lre3, lim3, logdt, bre_t, bim_t, da.reshape(G, 1, P), db.reshape(G, 1, P),
            _block_diag_take(dbmre, G), _block_diag_take(dbmim, G))
        gsmall["ssm_lambda_re"][l], gsmall["ssm_lambda_im"][l] = dlre.reshape(G, P), dlim.reshape(G, P)
        gsmall["ssm_log_dt"][l] = dlogdt.reshape(G)
        gsmall["ssm_b_re"][l] = dbre_t.transpose(0, 2, 1)
        gsmall["ssm_b_im"][l] = dbim_t.transpose(0, 2, 1)
        gsmall["ssm_c_re"][l] = _block_diag_take(dcmre, G).transpose(0, 2, 1)
        gsmall["ssm_c_im"][l] = -_block_diag_take(dcmimn, G).transpose(0, 2, 1)
        gsmall["ssm_d"][l] = dd[0]

        dmm = _mm_nt(dp_mem, Wg["mem_w_out"][l], "col", name="mm_d_mm")
        gbig["mem_w_out"][l] = _mm_tn(sv["mm_o"], dp_mem, "col", name="mm_g_mem_out")
        dqm, dkv = _mem_bwd(proj, qm_off, mw, sv["kv"], dmm)
        gbig["mem_w_kv"][l] = _mm_tn(mem2, dkv, "row", name="mm_g_kv")

        dq, dk, dv = _sb_bwd(proj, q_off, k_off, v_off, heads, dsb, sv["sb_ctot"])
        dproj, dbin = _assemble_dproj([dq, dk, dv, du, dqm, dgl0, dgl1, dgl2])
        gsmall["b_in"][l] = dbin[0]
        dxl = _mm_nt(dproj, Wg["w_in"][l], "col", add=dr1, add_scale=alpha, name="mm_d_x")
        gbig["w_in"][l] = _mm_tn(sv["x"], dproj, "col", name="mm_g_win")

    grad_x = dxl.reshape(x.shape)

    x_i, y_i, c_i = _coords()
    place = jnp.stack([c_i, 2 * x_i + y_i]).astype(jnp.int32)
    flat = [gbig[n][l] for n in BIG for l in range(L)]
    from_sibling = _rs_pair(flat)
    p16, own = [], []
    for g, r in zip(flat, from_sibling):
        a16, a32 = _pair_add(g, r, place)
        p16.append(a16)
        own.append(a32)
    from_chips = _rs_chips(p16, len(BIG), L)
    halves = [_chip_sum(o, r) for o, r in zip(own, from_chips)]
    reduced = _share_pair(halves, len(BIG), L)
    grads = {n: reduced[i] for i, n in enumerate(BIG)}

    small_local = [jnp.stack(gsmall[n]) for n in SMALL]
    packed = _pack(small_local + [loss_part[0, :1]])
    total = _small_allreduce(packed)
    unpacked = _unpack(total, small_local + [loss_part[0, :1]])
    for n, g in zip(SMALL, unpacked[:-1]):
        grads[n] = g
    loss = unpacked[-1][0]

    delta, new_m, new_v = {}, {}, {}
    for n in BIG:
        delta[n], new_m[n], new_v[n] = _adamw(W[n], grads[n], M1[n], V1[n])
    sm = _adamw(_pack([W[n] for n in SMALL]), _pack([grads[n] for n in SMALL]),
                _pack([M1[n] for n in SMALL]), _pack([V1[n] for n in SMALL]))
    like = [W[n] for n in SMALL]
    for n, d, m_, v_ in zip(SMALL, _unpack(sm[0], like), _unpack(sm[1], like), _unpack(sm[2], like)):
        delta[n], new_m[n], new_v[n] = d, m_, v_

    return (loss, grad_x, *[grads[n] for n in WEIGHTS], *[delta[n] for n in WEIGHTS],
            *[new_m[n] for n in WEIGHTS], *[new_v[n] for n in WEIGHTS])


def kernel(x, mem, w_in, b_in, sb_w_out, ssm_lambda_re, ssm_lambda_im, ssm_log_dt, ssm_b_re, ssm_b_im, ssm_c_re, ssm_c_im, ssm_d, ssm_w_glu, ssm_w_out, mem_w_kv, mem_w_out, w_o, ln1_g, ln1_b, ffn_w_gate_up, ffn_w_down, ln2_g, ln2_b, loss_target, m_w_in, m_b_in, m_sb_w_out, m_ssm_lambda_re, m_ssm_lambda_im, m_ssm_log_dt, m_ssm_b_re, m_ssm_b_im, m_ssm_c_re, m_ssm_c_im, m_ssm_d, m_ssm_w_glu, m_ssm_w_out, m_mem_w_kv, m_mem_w_out, m_w_o, m_ln1_g, m_ln1_b, m_ffn_w_gate_up, m_ffn_w_down, m_ln2_g, m_ln2_b, v_w_in, v_b_in, v_sb_w_out, v_ssm_lambda_re, v_ssm_lambda_im, v_ssm_log_dt, v_ssm_b_re, v_ssm_b_im, v_ssm_c_re, v_ssm_c_im, v_ssm_d, v_ssm_w_glu, v_ssm_w_out, v_mem_w_kv, v_mem_w_out, v_w_o, v_ln1_g, v_ln1_b, v_ffn_w_gate_up, v_ffn_w_down, v_ln2_g, v_ln2_b):
    W = dict(w_in=w_in, b_in=b_in, sb_w_out=sb_w_out, ssm_lambda_re=ssm_lambda_re, ssm_lambda_im=ssm_lambda_im,
             ssm_log_dt=ssm_log_dt, ssm_b_re=ssm_b_re, ssm_b_im=ssm_b_im, ssm_c_re=ssm_c_re, ssm_c_im=ssm_c_im,
             ssm_d=ssm_d, ssm_w_glu=ssm_w_glu, ssm_w_out=ssm_w_out, mem_w_kv=mem_w_kv, mem_w_out=mem_w_out,
             w_o=w_o, ln1_g=ln1_g, ln1_b=ln1_b, ffn_w_gate_up=ffn_w_gate_up, ffn_w_down=ffn_w_down,
             ln2_g=ln2_g, ln2_b=ln2_b)
    M1 = dict(w_in=m_w_in, b_in=m_b_in, sb_w_out=m_sb_w_out, ssm_lambda_re=m_ssm_lambda_re,
              ssm_lambda_im=m_ssm_lambda_im, ssm_log_dt=m_ssm_log_dt, ssm_b_re=m_ssm_b_re, ssm_b_im=m_ssm_b_im,
              ssm_c_re=m_ssm_c_re, ssm_c_im=m_ssm_c_im, ssm_d=m_ssm_d, ssm_w_glu=m_ssm_w_glu,
              ssm_w_out=m_ssm_w_out, mem_w_kv=m_mem_w_kv, mem_w_out=m_mem_w_out, w_o=m_w_o, ln1_g=m_ln1_g,
              ln1_b=m_ln1_b, ffn_w_gate_up=m_ffn_w_gate_up, ffn_w_down=m_ffn_w_down, ln2_g=m_ln2_g, ln2_b=m_ln2_b)
    V1 = dict(w_in=v_w_in, b_in=v_b_in, sb_w_out=v_sb_w_out, ssm_lambda_re=v_ssm_lambda_re,
              ssm_lambda_im=v_ssm_lambda_im, ssm_log_dt=v_ssm_log_dt, ssm_b_re=v_ssm_b_re, ssm_b_im=v_ssm_b_im,
              ssm_c_re=v_ssm_c_re, ssm_c_im=v_ssm_c_im, ssm_d=v_ssm_d, ssm_w_glu=v_ssm_w_glu,
              ssm_w_out=v_ssm_w_out, mem_w_kv=v_mem_w_kv, mem_w_out=v_mem_w_out, w_o=v_w_o, ln1_g=v_ln1_g,
              ln1_b=v_ln1_b, ffn_w_gate_up=v_ffn_w_gate_up, ffn_w_down=v_ffn_w_down, ln2_g=v_ln2_g, ln2_b=v_ln2_b)
    return _step(x, mem, loss_target, W, M1, V1)
```

```python
import functools
import math

import jax
import jax.numpy as jnp
from jax import lax
from jax.experimental import pallas as pl
from jax.experimental.pallas import tpu as pltpu

F32 = jnp.float32
BF16 = jnp.bfloat16
MESH = pl.DeviceIdType.MESH
ANY = pl.BlockSpec(memory_space=pl.ANY)
VMEM = pl.BlockSpec(memory_space=pltpu.VMEM)

HEAD_DIM = 128
SSM_GROUP = 16
N_CHIPS = 4
N_DEV = 8
LN_EPS = 1e-5
ADAM_LR = 0.001
ADAM_B1 = 0.9
ADAM_B2 = 0.999
ADAM_EPS = 1e-08
ADAM_WD = 0.01
ADAM_STEP = 10
LANES = 128
SUBLANES = 8
VMEM_LIMIT_MB = 56


def _cparams(sem, mb=VMEM_LIMIT_MB):
    return pltpu.CompilerParams(dimension_semantics=sem, vmem_limit_bytes=mb * 1024 * 1024)


def _tile(n, pref, mult=LANES):
    best = None
    t = mult
    while t <= min(n, pref):
        if n % t == 0:
            best = t
        t += mult
    return n if best is None else best


def _dot(a, b, dims=(((1,), (0,)), ((), ()))):
    return lax.dot_general(a, b, dims, preferred_element_type=F32)


NT = (((1,), (1,)), ((), ()))
TN = (((0,), (0,)), ((), ()))


def _split2(x):
    hi = x.astype(BF16)
    lo = (x - hi.astype(F32)).astype(BF16)
    return hi, lo


def _dot_hp(a, b, dims=(((1,), (0,)), ((), ()))):
    ah, al = _split2(a)
    bh, bl = _split2(b)
    return _dot(ah, bh, dims) + (_dot(ah, bl, dims) + _dot(al, bh, dims))


def _dot_mask(x, u2):
    hi, lo = _split2(x)
    return _dot(jnp.concatenate([hi, lo], axis=1), u2)


def _tri2(n, rel):
    ri = lax.broadcasted_iota(jnp.int32, (2 * n, n), 0)
    ci = lax.broadcasted_iota(jnp.int32, (2 * n, n), 1)
    ri = jnp.where(ri >= n, ri - n, ri)
    return rel(ri, ci).astype(BF16)


def _mm_call(a, b, *, dims, grid, a_spec, b_spec, out_spec, out_shape, name,
             bias=None, bias_spec=None, add=None, add_spec=None, add_scale=1.0):
    nk = grid[2]
    has_bias = bias is not None
    has_add = add is not None

    def body(*refs):
        a_ref, b_ref = refs[0], refs[1]
        pos = 2
        bias_ref = refs[pos] if has_bias else None
        pos += int(has_bias)
        add_ref = refs[pos] if has_add else None
        pos += int(has_add)
        o_ref = refs[pos]
        acc_ref = refs[pos + 1] if nk > 1 else None
        p = _dot(a_ref[...].astype(BF16), b_ref[...].astype(BF16), dims)

        def finish(acc):
            if has_bias:
                acc = acc + bias_ref[...]
            if has_add:
                acc = acc + add_scale * add_ref[...].astype(F32)
            o_ref[...] = acc.astype(o_ref.dtype)

        if nk == 1:
            finish(p)
        else:
            k = pl.program_id(2)

            @pl.when(k == 0)
            def _():
                acc_ref[...] = p

            @pl.when(k > 0)
            def _():
                acc_ref[...] += p

            @pl.when(k == nk - 1)
            def _():
                finish(acc_ref[...])

    ins = [a, b]
    in_specs = [a_spec, b_spec]
    if has_bias:
        ins.append(bias)
        in_specs.append(bias_spec)
    if has_add:
        ins.append(add)
        in_specs.append(add_spec)
    scratch = []
    if nk > 1:
        blk = [d for d in out_spec.block_shape if d is not None]
        scratch.append(pltpu.VMEM(tuple(blk), F32))
    return pl.pallas_call(
        body, out_shape=out_shape, grid=grid, in_specs=in_specs, out_specs=out_spec,
        scratch_shapes=scratch, name=name,
        compiler_params=_cparams(("parallel", "parallel", "arbitrary")),
    )(*ins)


def _mm_nn(a, w, kind, *, bias=None, out_dtype=F32, name):
    M, K = a.shape
    tm = _tile(M, 512, SUBLANES)
    tk = K if K <= 2048 else _tile(K, 1408)
    if kind == "col":
        Nc = w.shape[2]
        N = N_CHIPS * Nc
        tn = _tile(Nc, 1408)
        npc = Nc // tn
        b_spec = pl.BlockSpec((None, tk, tn), lambda i, j, k: (j // npc, k, j % npc))
    else:
        w = w.reshape(-1, w.shape[-1])
        N = w.shape[1]
        tn = _tile(N, 2048)
        b_spec = pl.BlockSpec((tk, tn), lambda i, j, k: (k, j))
    grid = (M // tm, N // tn, K // tk)
    bias_spec = pl.BlockSpec((1, tn), lambda i, j, k: (0, j)) if bias is not None else None
    return _mm_call(
        a, w, dims=(((1,), (0,)), ((), ())), grid=grid,
        a_spec=pl.BlockSpec((tm, tk), lambda i, j, k: (i, k)), b_spec=b_spec,
        out_spec=pl.BlockSpec((tm, tn), lambda i, j, k: (i, j)),
        out_shape=jax.ShapeDtypeStruct((M, N), out_dtype), name=name,
        bias=bias, bias_spec=bias_spec)


def _mm_nt(dy, w, kind, *, add=None, add_scale=1.0, out_dtype=F32, name):
    M, N = dy.shape
    tm = _tile(M, 512, SUBLANES)
    if kind == "col":
        K, Nc = w.shape[1], w.shape[2]
        tn = _tile(Nc, 1408)
        npc = Nc // tn
        tko = _tile(K, 2048)
        b_spec = pl.BlockSpec((None, tko, tn), lambda i, j, r: (r // npc, j, r % npc))
    else:
        w = w.reshape(-1, w.shape[-1])
        K = w.shape[0]
        tn = _tile(N, 2048)
        tko = _tile(K, 2048)
        b_spec = pl.BlockSpec((tko, tn), lambda i, j, r: (j, r))
    grid = (M // tm, K // tko, N // tn)
    add_spec = pl.BlockSpec((tm, tko), lambda i, j, r: (i, j)) if add is not None else None
    return _mm_call(
        dy, w, dims=NT, grid=grid,
        a_spec=pl.BlockSpec((tm, tn), lambda i, j, r: (i, r)), b_spec=b_spec,
        out_spec=pl.BlockSpec((tm, tko), lambda i, j, r: (i, j)),
        out_shape=jax.ShapeDtypeStruct((M, K), out_dtype), name=name,
        add=add, add_spec=add_spec, add_scale=add_scale)


def _mm_tn(a, dy, kind, *, name):
    M, K = a.shape
    N = dy.shape[1]
    tm = _tile(M, 1024, SUBLANES)
    tkw = _tile(K, 1408)
    if kind == "col":
        Nc = N // N_CHIPS
        tn = _tile(Nc, 1408)
        npc = Nc // tn
        out_spec = pl.BlockSpec((None, tkw, tn), lambda i, j, m: (j // npc, i, j % npc))
        out_shape = jax.ShapeDtypeStruct((N_CHIPS, K, Nc), F32)
    else:
        tn = _tile(N, 1024)
        out_spec = pl.BlockSpec((tkw, tn), lambda i, j, m: (i, j))
        out_shape = jax.ShapeDtypeStruct((K, N), F32)
    grid = (K // tkw, N // tn, M // tm)
    out = _mm_call(
        a, dy, dims=TN, grid=grid,
        a_spec=pl.BlockSpec((tm, tkw), lambda i, j, m: (m, i)),
        b_spec=pl.BlockSpec((tm, tn), lambda i, j, m: (m, j)),
        out_spec=out_spec, out_shape=out_shape, name=name)
    if kind == "row":
        out = out.reshape(N_CHIPS, K // N_CHIPS, N)
    return out


def _gelu_grad(x):
    k = math.sqrt(2.0 / math.pi)
    inner = k * (x + 0.044715 * x * x * x)
    t = jnp.tanh(inner)
    return 0.5 * (1.0 + t) + 0.5 * x * (1.0 - t * t) * k * (1.0 + 3.0 * 0.044715 * x * x)


def _ln_fwd(xin, delta, g, b, alpha):
    S, D = xin.shape
    ts = _tile(S, 256, SUBLANES)

    def body(x_ref, d_ref, g_ref, b_ref, y_ref, yb_ref, xh_ref, rs_ref):
        r = alpha * x_ref[...] + d_ref[...]
        mu = jnp.mean(r, axis=-1, keepdims=True)
        rc = r - mu
        var = jnp.mean(rc * rc, axis=-1, keepdims=True)
        rstd = lax.rsqrt(var + LN_EPS)
        xh = rc * rstd
        y = xh * g_ref[...] + b_ref[...]
        y_ref[...] = y
        yb_ref[...] = y.astype(BF16)
        xh_ref[...] = xh
        rs_ref[...] = rstd

    row = pl.BlockSpec((ts, D), lambda i: (i, 0))
    vec = pl.BlockSpec((1, D), lambda i: (0, 0))
    return pl.pallas_call(
        body, grid=(S // ts,), in_specs=[row, row, vec, vec],
        out_specs=[row, row, row, pl.BlockSpec((ts, 1), lambda i: (i, 0))],
        out_shape=[jax.ShapeDtypeStruct((S, D), F32), jax.ShapeDtypeStruct((S, D), BF16),
                   jax.ShapeDtypeStruct((S, D), F32), jax.ShapeDtypeStruct((S, 1), F32)],
        name="ln_fwd", compiler_params=_cparams(("parallel",)),
    )(xin, delta, g, b)


def _ln_bwd(dy, xh, rstd, g):
    S, D = dy.shape
    ts = _tile(S, 256, SUBLANES)

    def body(dy_ref, xh_ref, rs_ref, g_ref, dr_ref, drb_ref, dg_ref, db_ref):
        @pl.when(pl.program_id(0) == 0)
        def _():
            dg_ref[...] = jnp.zeros_like(dg_ref)
            db_ref[...] = jnp.zeros_like(db_ref)

        dyv = dy_ref[...]
        xhv = xh_ref[...]
        dyg = dyv * g_ref[...]
        m1 = jnp.mean(dyg, axis=-1, keepdims=True)
        m2 = jnp.mean(dyg * xhv, axis=-1, keepdims=True)
        dr = rs_ref[...] * (dyg - m1 - xhv * m2)
        dr_ref[...] = dr
        drb_ref[...] = dr.astype(BF16)
        dg_ref[...] += jnp.sum(dyv * xhv, axis=0, keepdims=True)
        db_ref[...] += jnp.sum(dyv, axis=0, keepdims=True)

    row = pl.BlockSpec((ts, D), lambda i: (i, 0))
    vec = pl.BlockSpec((1, D), lambda i: (0, 0))
    return pl.pallas_call(
        body, grid=(S // ts,),
        in_specs=[row, row, pl.BlockSpec((ts, 1), lambda i: (i, 0)), vec],
        out_specs=[row, row, vec, vec],
        out_shape=[jax.ShapeDtypeStruct((S, D), F32), jax.ShapeDtypeStruct((S, D), BF16),
                   jax.ShapeDtypeStruct((1, D), F32), jax.ShapeDtypeStruct((1, D), F32)],
        name="ln_bwd", compiler_params=_cparams(("arbitrary",)),
    )(dy, xh, rstd, g)


def _merge_fwd(proj, gate_off, p_sb, p_ssm, p_mem):
    S, D = p_sb.shape
    ts = _tile(S, 256, SUBLANES)
    gb = gate_off // D

    def body(g0, g1, g2, a0, a1, a2, o_ref):
        o_ref[...] = (jax.nn.sigmoid(g0[...]) * a0[...] + jax.nn.sigmoid(g1[...]) * a1[...]
                      + jax.nn.sigmoid(g2[...]) * a2[...]).astype(o_ref.dtype)

    row = pl.BlockSpec((ts, D), lambda i: (i, 0))
    gates = [pl.BlockSpec((ts, D), functools.partial(lambda i, n: (i, gb + n), n=n)) for n in range(3)]
    return pl.pallas_call(
        body, grid=(S // ts,), in_specs=gates + [row, row, row], out_specs=row,
        out_shape=jax.ShapeDtypeStruct((S, D), BF16), name="merge_fwd",
        compiler_params=_cparams(("parallel",)),
    )(proj, proj, proj, p_sb, p_ssm, p_mem)


def _merge_bwd(dmerged, proj, gate_off, p_sb, p_ssm, p_mem):
    S, D = p_sb.shape
    ts = _tile(S, 256, SUBLANES)
    gb = gate_off // D

    def body(dm_ref, g0, g1, g2, a0, a1, a2, d0, d1, d2, l0, l1, l2):
        dm = dm_ref[...]
        for g_ref, a_ref, d_ref, l_ref in ((g0, a0, d0, l0), (g1, a1, d1, l1), (g2, a2, d2, l2)):
            s = jax.nn.sigmoid(g_ref[...])
            d_ref[...] = (dm * s).astype(d_ref.dtype)
            l_ref[...] = dm * a_ref[...] * s * (1.0 - s)

    row = pl.BlockSpec((ts, D), lambda i: (i, 0))
    gates = [pl.BlockSpec((ts, D), functools.partial(lambda i, n: (i, gb + n), n=n)) for n in range(3)]
    sd = jax.ShapeDtypeStruct((S, D), F32)
    return pl.pallas_call(
        body, grid=(S // ts,), in_specs=[row] + gates + [row, row, row], out_specs=[row] * 6,
        out_shape=[jax.ShapeDtypeStruct((S, D), BF16)] * 3 + [sd] * 3, name="merge_bwd", compiler_params=_cparams(("parallel",)),
    )(dmerged, proj, proj, proj, p_sb, p_ssm, p_mem)


def _glu_fwd(glu):
    S, W2 = glu.shape
    W = W2 // 2
    ts = _tile(S, 512, SUBLANES)

    def body(x_ref, o_ref):
        o_ref[...] = (x_ref[:, :W] * jax.nn.sigmoid(x_ref[:, W:])).astype(o_ref.dtype)

    return pl.pallas_call(
        body, grid=(S // ts,), in_specs=[pl.BlockSpec((ts, W2), lambda i: (i, 0))],
        out_specs=pl.BlockSpec((ts, W), lambda i: (i, 0)),
        out_shape=jax.ShapeDtypeStruct((S, W), BF16), name="glu_fwd",
        compiler_params=_cparams(("parallel",)),
    )(glu)


def _glu_bwd(dzz, glu):
    S, W2 = glu.shape
    W = W2 // 2
    ts = _tile(S, 512, SUBLANES)

    def body(d_ref, x_ref, o_ref):
        d = d_ref[...]
        a = x_ref[:, :W]
        s = jax.nn.sigmoid(x_ref[:, W:])
        o_ref[:, :W] = (d * s).astype(o_ref.dtype)
        o_ref[:, W:] = (d * a * s * (1.0 - s)).astype(o_ref.dtype)

    return pl.pallas_call(
        body, grid=(S // ts,),
        in_specs=[pl.BlockSpec((ts, W), lambda i: (i, 0)), pl.BlockSpec((ts, W2), lambda i: (i, 0))],
        out_specs=pl.BlockSpec((ts, W2), lambda i: (i, 0)),
        out_shape=jax.ShapeDtypeStruct((S, W2), BF16), name="glu_bwd",
        compiler_params=_cparams(("parallel",)),
    )(dzz, glu)


def _swiglu_fwd(gu):
    S, F2 = gu.shape
    Fh = F2 // 2
    ts = _tile(S, 128, SUBLANES)

    def body(x_ref, o_ref):
        fg = x_ref[:, :Fh]
        o_ref[...] = (fg * jax.nn.sigmoid(fg) * x_ref[:, Fh:]).astype(o_ref.dtype)

    return pl.pallas_call(
        body, grid=(S // ts,), in_specs=[pl.BlockSpec((ts, F2), lambda i: (i, 0))],
        out_specs=pl.BlockSpec((ts, Fh), lambda i: (i, 0)),
        out_shape=jax.ShapeDtypeStruct((S, Fh), BF16), name="swiglu_fwd",
        compiler_params=_cparams(("parallel",)),
    )(gu)


def _swiglu_bwd(dhid, gu):
    S, F2 = gu.shape
    Fh = F2 // 2
    ts = _tile(S, 128, SUBLANES)

    def body(d_ref, x_ref, o_ref):
        d = d_ref[...]
        fg = x_ref[:, :Fh]
        fu = x_ref[:, Fh:]
        s = jax.nn.sigmoid(fg)
        o_ref[:, :Fh] = (d * fu * s * (1.0 + fg * (1.0 - s))).astype(o_ref.dtype)
        o_ref[:, Fh:] = (d * fg * s).astype(o_ref.dtype)

    return pl.pallas_call(
        body, grid=(S // ts,),
        in_specs=[pl.BlockSpec((ts, Fh), lambda i: (i, 0)), pl.BlockSpec((ts, F2), lambda i: (i, 0))],
        out_specs=pl.BlockSpec((ts, F2), lambda i: (i, 0)),
        out_shape=jax.ShapeDtypeStruct((S, F2), BF16), name="swiglu_bwd",
        compiler_params=_cparams(("parallel",)),
    )(dhid, gu)


def _assemble_dproj(pieces):
    S = pieces[0].shape[0]
    widths = [p.shape[1] for p in pieces]
    total = sum(widths)
    ts = _tile(S, 128, SUBLANES)
    n = len(pieces)

    def body(*refs):
        o_ref, b_ref = refs[n], refs[n + 1]

        @pl.when(pl.program_id(0) == 0)
        def _():
            b_ref[...] = jnp.zeros_like(b_ref)

        off = 0
        for r, w in zip(refs[:n], widths):
            v = r[...].astype(F32)
            o_ref[:, off:off + w] = v.astype(o_ref.dtype)
            b_ref[:, off:off + w] += jnp.sum(v, axis=0, keepdims=True)
            off += w

    return pl.pallas_call(
        body, grid=(S // ts,),
        in_specs=[pl.BlockSpec((ts, w), lambda i: (i, 0)) for w in widths],
        out_specs=[pl.BlockSpec((ts, total), lambda i: (i, 0)), pl.BlockSpec((1, total), lambda i: (0, 0))],
        out_shape=[jax.ShapeDtypeStruct((S, total), BF16), jax.ShapeDtypeStruct((1, total), F32)],
        name="assemble_dproj", compiler_params=_cparams(("arbitrary",)),
    )(*pieces)


def _loss_head(y, target):
    S, D = y.shape
    ts = _tile(S, 256, SUBLANES)

    def body(y_ref, t_ref, dy_ref, l_ref):
        @pl.when(pl.program_id(0) == 0)
        def _():
            l_ref[...] = jnp.zeros_like(l_ref)

        e = y_ref[...] - t_ref[...]
        dy_ref[...] = e * (1.0 / D)
        part = jnp.sum(jnp.sum(e * e, axis=1, keepdims=True), axis=0, keepdims=True) * (0.5 / D)
        l_ref[...] += jnp.broadcast_to(part, l_ref.shape)

    row = pl.BlockSpec((ts, D), lambda i: (i, 0))
    return pl.pallas_call(
        body, grid=(S // ts,), in_specs=[row, row],
        out_specs=[row, pl.BlockSpec((1, LANES), lambda i: (0, 0))],
        out_shape=[jax.ShapeDtypeStruct((S, D), F32), jax.ShapeDtypeStruct((1, LANES), F32)],
        name="loss_head", compiler_params=_cparams(("arbitrary",)),
    )(y, target)


SB_TQ = 256
SB_TK = 128


def _sb_tile_terms(q, kb, scale, causal):
    z = _dot(q, kb, NT) * scale
    soft = jnp.log(1.0 + jnp.exp(-jnp.abs(z)))
    ls = jnp.minimum(z, 0.0) - soft
    l1m = jnp.minimum(-z, 0.0) - soft
    if causal is not None:
        l1m = jnp.where(causal, l1m, 0.0)
    return ls, l1m


def _sb_causal(qi, kj, TQ, TK):
    t_idx = qi * TQ + lax.broadcasted_iota(jnp.int32, (TQ, TK), 0)
    s_idx = kj * TK + lax.broadcasted_iota(jnp.int32, (TQ, TK), 1)
    return s_idx < t_idx


def _sb_fwd(proj, q_off, k_off, v_off, heads):
    S = proj.shape[0]
    Dh = HEAD_DIM
    TQ = min(SB_TQ, S)
    TK = SB_TK
    scale = Dh ** -0.5
    qb, kb0, vb0 = q_off // Dh, k_off // Dh, v_off // Dh

    def body(q_ref, k_ref, v_ref, o_ref, c_ref):
        qi = pl.program_id(1)
        q = q_ref[...].astype(BF16)
        upper = _tri2(TK, lambda j, s: j > s)
        nfull = (qi * TQ) // TK

        def block(kj, carry, masked):
            c, acc = carry
            off = pl.multiple_of(kj * TK, TK)
            kblk = k_ref[pl.ds(off, TK), :].astype(BF16)
            vblk = v_ref[pl.ds(off, TK), :].astype(BF16)
            causal = _sb_causal(qi, kj, TQ, TK) if masked else None
            ls, l1m = _sb_tile_terms(q, kblk, scale, causal)
            w = jnp.exp(ls + _dot_mask(l1m, upper) + c)
            if masked:
                w = jnp.where(causal, w, 0.0)
            acc = acc + _dot(w.astype(BF16), vblk)
            c = c + jnp.sum(l1m, axis=1, keepdims=True)
            return c, acc

        carry = (jnp.zeros((TQ, 1), F32), jnp.zeros((TQ, Dh), F32))
        for d in range(TQ // TK - 1, -1, -1):
            carry = block(nfull + d, carry, True)

        def pair(pp, carry):
            kj = nfull - 1 - 2 * pp
            return block(kj - 1, block(kj, carry, False), False)

        c, acc = lax.fori_loop(0, nfull // 2, pair, carry)
        o_ref[...] = acc.astype(o_ref.dtype)
        c_ref[...] = c

    return pl.pallas_call(
        body, grid=(heads, S // TQ),
        in_specs=[pl.BlockSpec((TQ, Dh), lambda h, i: (i, qb + h)),
                  pl.BlockSpec((S, Dh), lambda h, i: (0, kb0 + h)),
                  pl.BlockSpec((S, Dh), lambda h, i: (0, vb0 + h))],
        out_specs=[pl.BlockSpec((TQ, Dh), lambda h, i: (i, h)), pl.BlockSpec((None, TQ, 1), lambda h, i: (h, i, 0))],
        out_shape=[jax.ShapeDtypeStruct((S, heads * Dh), BF16), jax.ShapeDtypeStruct((heads, S, 1), F32)],
        name="sb_fwd", compiler_params=_cparams(("parallel", "arbitrary")),
    )(proj, proj, proj)


def _sb_bwd(proj, q_off, k_off, v_off, heads, dout, ctot):
    S = proj.shape[0]
    Dh = HEAD_DIM
    TQ = min(SB_TQ, S)
    TK = SB_TK
    scale = Dh ** -0.5
    qb, kb0, vb0 = q_off // Dh, k_off // Dh, v_off // Dh

    def body(q_ref, k_ref, v_ref, do_ref, c_ref, dq_ref, dk_ref, dv_ref):
        qi = pl.program_id(1)

        @pl.when(qi == 0)
        def _():
            dk_ref[...] = jnp.zeros_like(dk_ref)
            dv_ref[...] = jnp.zeros_like(dv_ref)

        q = q_ref[...].astype(BF16)
        do = do_ref[...].astype(BF16)
        ctot = c_ref[...]
        lower_incl = _tri2(TK, lambda j, s: j <= s)
        lower = _tri2(TK, lambda j, s: j < s)
        nfull = (qi * TQ) // TK

        def block(kj, carry, masked):
            cl, ce, dq = carry
            off = pl.multiple_of(kj * TK, TK)
            kblk = k_ref[pl.ds(off, TK), :].astype(BF16)
            vblk = v_ref[pl.ds(off, TK), :].astype(BF16)
            causal = _sb_causal(qi, kj, TQ, TK) if masked else None
            ls, l1m = _sb_tile_terms(q, kblk, scale, causal)
            w = jnp.exp(ls + (ctot - cl - _dot_mask(l1m, lower_incl)))
            if masked:
                w = jnp.where(causal, w, 0.0)
            e = w * _dot(do, vblk, NT)
            before = ce + _dot_mask(e, lower)
            beta = jnp.exp(ls)
            dz = (e * (1.0 - beta) - beta * before) * scale
            if masked:
                dz = jnp.where(causal, dz, 0.0)
            dzb = dz.astype(BF16)
            dq = dq + _dot(dzb, kblk)
            dk_ref[pl.ds(off, TK), :] += _dot(dzb, q, TN)
            dv_ref[pl.ds(off, TK), :] += _dot(w.astype(BF16), do, TN)
            cl = cl + jnp.sum(l1m, axis=1, keepdims=True)
            ce = ce + jnp.sum(e, axis=1, keepdims=True)
            return cl, ce, dq

        def pair(pp, carry):
            return block(2 * pp + 1, block(2 * pp, carry, False), False)

        zero = jnp.zeros((TQ, 1), F32)
        carry = lax.fori_loop(0, nfull // 2, pair, (zero, zero, jnp.zeros((TQ, Dh), F32)))
        for d in range(TQ // TK):
            carry = block(nfull + d, carry, True)
        dq_ref[...] = carry[2]

    blk = pl.BlockSpec((TQ, Dh), lambda h, i: (i, h))
    col = pl.BlockSpec((S, Dh), lambda h, i: (0, h))
    sd = jax.ShapeDtypeStruct((S, heads * Dh), F32)
    return pl.pallas_call(
        body, grid=(heads, S // TQ),
        in_specs=[pl.BlockSpec((TQ, Dh), lambda h, i: (i, qb + h)),
                  pl.BlockSpec((S, Dh), lambda h, i: (0, kb0 + h)),
                  pl.BlockSpec((S, Dh), lambda h, i: (0, vb0 + h)), blk,
                  pl.BlockSpec((None, TQ, 1), lambda h, i: (h, i, 0))],
        out_specs=[blk, col, col], out_shape=[sd, sd, sd], name="sb_bwd",
        compiler_params=_cparams(("parallel", "arbitrary")),
    )(proj, proj, proj, dout, ctot)


def _mem_probs(qh, kh, scale):
    s = _dot(qh, kh, NT) * scale
    m = jnp.max(s, axis=-1, keepdims=True)
    p = jnp.exp(s - m)
    return p / jnp.sum(p, axis=-1, keepdims=True)


def _mem_fwd(proj, q_off, width, kv):
    S = proj.shape[0]
    Dh = HEAD_DIM
    heads = width // Dh
    ts = _tile(S, 512, SUBLANES)
    scale = Dh ** -0.5
    M = kv.shape[0]

    def body(q_ref, kv_ref, o_ref):
        for h in range(heads):
            qh = q_ref[:, h * Dh:(h + 1) * Dh].astype(BF16)
            kh = kv_ref[:, h * Dh:(h + 1) * Dh].astype(BF16)
            vh = kv_ref[:, width + h * Dh:width + (h + 1) * Dh].astype(BF16)
            p = _mem_probs(qh, kh, scale)
            o_ref[:, h * Dh:(h + 1) * Dh] = _dot(p.astype(BF16), vh).astype(o_ref.dtype)

    return pl.pallas_call(
        body, grid=(S // ts,),
        in_specs=[pl.BlockSpec((ts, width), lambda i: (i, q_off // width)),
                  pl.BlockSpec((M, 2 * width), lambda i: (0, 0))],
        out_specs=pl.BlockSpec((ts, width), lambda i: (i, 0)),
        out_shape=jax.ShapeDtypeStruct((S, width), BF16), name="mem_fwd",
        compiler_params=_cparams(("parallel",)),
    )(proj, kv)


def _mem_bwd(proj, q_off, width, kv, dmm):
    S = proj.shape[0]
    Dh = HEAD_DIM
    heads = width // Dh
    ts = _tile(S, 512, SUBLANES)
    scale = Dh ** -0.5
    M = kv.shape[0]

    def body(q_ref, kv_ref, d_ref, dq_ref, dkv_ref):
        @pl.when(pl.program_id(0) == 0)
        def _():
            dkv_ref[...] = jnp.zeros_like(dkv_ref)

        for h in range(heads):
            qh = q_ref[:, h * Dh:(h + 1) * Dh].astype(BF16)
            kh = kv_ref[:, h * Dh:(h + 1) * Dh].astype(BF16)
            vh = kv_ref[:, width + h * Dh:width + (h + 1) * Dh].astype(BF16)
            dh = d_ref[:, h * Dh:(h + 1) * Dh].astype(BF16)
            p = _mem_probs(qh, kh, scale)
            dp = _dot(dh, vh, NT)
            ds = p * (dp - jnp.sum(dp * p, axis=-1, keepdims=True)) * scale
            dsb = ds.astype(BF16)
            dq_ref[:, h * Dh:(h + 1) * Dh] = _dot(dsb, kh)
            dkv_ref[:, h * Dh:(h + 1) * Dh] += _dot(dsb, qh, TN)
            dkv_ref[:, width + h * Dh:width + (h + 1) * Dh] += _dot(p.astype(BF16), dh, TN)

    row = pl.BlockSpec((ts, width), lambda i: (i, 0))
    full = pl.BlockSpec((M, 2 * width), lambda i: (0, 0))
    return pl.pallas_call(
        body, grid=(S // ts,),
        in_specs=[pl.BlockSpec((ts, width), lambda i: (i, q_off // width)), full, row],
        out_specs=[row, full],
        out_shape=[jax.ShapeDtypeStruct((S, width), F32), jax.ShapeDtypeStruct((M, 2 * width), F32)],
        name="mem_bwd", compiler_params=_cparams(("arbitrary",)),
    )(proj, kv, dmm)


def _disc_math(lre, lim, logdt, bre_t, bim_t):
    dt = jnp.exp(logdt)
    mag = jnp.exp(lre * dt)
    ang = lim * dt
    a = mag * jnp.cos(ang)
    b = mag * jnp.sin(ang)
    den = lre * lre + lim * lim
    nr = a - 1.0
    fre = (nr * lre + b * lim) / den
    fim = (b * lre - nr * lim) / den
    bbre = fre * bre_t - fim * bim_t
    bbim = fre * bim_t + fim * bre_t
    return a, b, bbre, bbim


def _s5_disc(lre, lim, logdt, bre_t, bim_t):
    G, _, P = lre.shape
    C = bre_t.shape[1]

    def body(lre_ref, lim_ref, dt_ref, br_ref, bi_ref, a_ref, b_ref, bbre_ref, bbim_ref):
        a, b, bbre, bbim = _disc_math(lre_ref[...], lim_ref[...], dt_ref[...], br_ref[...], bi_ref[...])
        a_ref[...] = a
        b_ref[...] = b
        bbre_ref[...] = bbre
        bbim_ref[...] = bbim

    gp = jax.ShapeDtypeStruct((G, 1, P), F32)
    gcp = jax.ShapeDtypeStruct((G, C, P), F32)
    return pl.pallas_call(
        body, in_specs=[VMEM] * 5, out_specs=[VMEM] * 4, out_shape=[gp, gp, gcp, gcp], name="s5_disc",
    )(lre, lim, logdt, bre_t, bim_t)


def _s5_disc_bwd(lre, lim, logdt, bre_t, bim_t, da, db, dbbre, dbbim):
    G, _, P = lre.shape
    C = bre_t.shape[1]

    def body(lre_ref, lim_ref, dt_ref, br_ref, bi_ref, da_ref, db_ref, dbr_ref, dbi_ref,
             o_lre, o_lim, o_dt, o_br, o_bi):
        _, vjp = jax.vjp(_disc_math, lre_ref[...], lim_ref[...], dt_ref[...], br_ref[...], bi_ref[...])
        g = vjp((da_ref[...], db_ref[...], dbr_ref[...], dbi_ref[...]))
        o_lre[...] = g[0]
        o_lim[...] = g[1]
        o_dt[...] = g[2]
        o_br[...] = g[3]
        o_bi[...] = g[4]

    gp = jax.ShapeDtypeStruct((G, 1, P), F32)
    gcp = jax.ShapeDtypeStruct((G, C, P), F32)
    return pl.pallas_call(
        body, in_specs=[VMEM] * 9, out_specs=[VMEM] * 5,
        out_shape=[gp, gp, jax.ShapeDtypeStruct((G, 1, 1), F32), gcp, gcp], name="s5_disc_bwd",
    )(lre, lim, logdt, bre_t, bim_t, da, db, dbbre, dbbim)


S5_CHUNK = 256


def _load_once(pairs):
    @pl.when(pl.program_id(0) == 0)
    def _():
        for src, dst in pairs:
            pltpu.sync_copy(src, dst)


def _s5_fwd(proj, u_off, width, a_row, b_row, bmre, bmim, cmre, cmimn, d_row):
    S = proj.shape[0]
    GP = a_row.shape[1]
    T = min(S5_CHUNK, S)

    def body(u_ref, a_ref, b_ref, d_ref, bre_hbm, bim_hbm, cre_hbm, cim_hbm,
             uo_ref, y_ref, gy_ref, hre_ref, him_ref, st_ref, bure_s, buim_s, bre_ref, bim_ref, cre_ref, cim_ref):
        @pl.when(pl.program_id(0) == 0)
        def _():
            st_ref[...] = jnp.zeros_like(st_ref)

        _load_once([(bre_hbm, bre_ref), (bim_hbm, bim_ref), (cre_hbm, cre_ref), (cim_hbm, cim_ref)])
        u = u_ref[...]
        uo_ref[...] = u
        bure_s[...] = _dot_hp(u, bre_ref[...])
        buim_s[...] = _dot_hp(u, bim_ref[...])
        a = a_ref[...]
        b = b_ref[...]

        def step(ii, carry):
            hre, him = carry
            base = pl.multiple_of(ii * SUBLANES, SUBLANES)
            br = bure_s[pl.ds(base, SUBLANES), :]
            bi = buim_s[pl.ds(base, SUBLANES), :]
            rows_re, rows_im = [], []
            for j in range(SUBLANES):
                nre = a * hre - b * him + br[j:j + 1, :]
                nim = a * him + b * hre + bi[j:j + 1, :]
                hre, him = nre, nim
                rows_re.append(nre)
                rows_im.append(nim)
            hre_ref[pl.ds(base, SUBLANES), :] = jnp.concatenate(rows_re, axis=0)
            him_ref[pl.ds(base, SUBLANES), :] = jnp.concatenate(rows_im, axis=0)
            return hre, him

        hre, him = lax.fori_loop(0, T // SUBLANES, step, (st_ref[0:1, :], st_ref[1:2, :]))
        st_ref[0:1, :] = hre
        st_ref[1:2, :] = him
        y = _dot_hp(hre_ref[...], cre_ref[...]) + _dot_hp(him_ref[...], cim_ref[...]) + d_ref[...] * u
        y_ref[...] = y
        gy_ref[...] = jax.nn.gelu(y).astype(gy_ref.dtype)

    c0 = lambda i: (0, 0)
    urow = pl.BlockSpec((T, width), lambda i: (i, u_off // width))
    row = pl.BlockSpec((T, width), lambda i: (i, 0))
    hrow = pl.BlockSpec((T, GP), lambda i: (i, 0))
    sw = jax.ShapeDtypeStruct((S, width), F32)
    sg = jax.ShapeDtypeStruct((S, GP), F32)
    return pl.pallas_call(
        body, grid=(S // T,),
        in_specs=[urow, pl.BlockSpec((1, GP), c0), pl.BlockSpec((1, GP), c0), pl.BlockSpec((1, width), c0),
                  ANY, ANY, ANY, ANY],
        out_specs=[row, row, row, hrow, hrow],
        out_shape=[sw, sw, jax.ShapeDtypeStruct((S, width), BF16), sg, sg],
        scratch_shapes=[pltpu.VMEM((SUBLANES, GP), F32), pltpu.VMEM((T, GP), F32), pltpu.VMEM((T, GP), F32),
                        pltpu.VMEM((width, GP), F32), pltpu.VMEM((width, GP), F32),
                        pltpu.VMEM((GP, width), F32), pltpu.VMEM((GP, width), F32)],
        name="s5_fwd", compiler_params=_cparams(("arbitrary",)),
    )(proj, a_row, b_row, d_row, bmre, bmim, cmre, cmimn)


def _s5_bwd(u, dgy, y, hre, him, a_row, b_row, bmre, bmim, cmre, cmimn, d_row):
    S, width = u.shape
    GP = a_row.shape[1]
    T = min(S5_CHUNK, S)
    nchunk = S // T

    def body(u_ref, dgy_ref, y_ref, hre_ref, him_ref, a_ref, b_ref, d_ref, bre_hbm, bim_hbm, cre_hbm, cim_hbm,
             du_ref, dy_ref, gre_s, gim_s, dd_ref, da_ref, db_ref,
             st_ref, bre_ref, bim_ref, cre_ref, cim_ref):
        @pl.when(pl.program_id(0) == 0)
        def _():
            st_ref[...] = jnp.zeros_like(st_ref)
            for r in (dd_ref, da_ref, db_ref):
                r[...] = jnp.zeros_like(r)

        _load_once([(bre_hbm, bre_ref), (bim_hbm, bim_ref), (cre_hbm, cre_ref), (cim_hbm, cim_ref)])
        u = u_ref[...]
        dy = dgy_ref[...] * _gelu_grad(y_ref[...])
        dy_ref[...] = dy
        gre_s[...] = _dot_hp(dy, cre_ref[...], NT)
        gim_s[...] = _dot_hp(dy, cim_ref[...], NT)
        a = a_ref[...]
        b = b_ref[...]

        def step(ii, carry):
            gre, gim, da, db = carry
            base = pl.multiple_of((T // SUBLANES - 1 - ii) * SUBLANES, SUBLANES)
            dr = gre_s[pl.ds(base, SUBLANES), :]
            di = gim_s[pl.ds(base, SUBLANES), :]
            hr = hre_ref[pl.ds(base, SUBLANES), :]
            hi = him_ref[pl.ds(base, SUBLANES), :]
            rows_re = [None] * SUBLANES
            rows_im = [None] * SUBLANES
            for j in range(SUBLANES - 1, -1, -1):
                hrj = hr[j:j + 1, :]
                hij = hi[j:j + 1, :]
                da = da + gre * hrj + gim * hij
                db = db + gim * hrj - gre * hij
                nre = dr[j:j + 1, :] + a * gre + b * gim
                nim = di[j:j + 1, :] - b * gre + a * gim
                gre, gim = nre, nim
                rows_re[j] = nre
                rows_im[j] = nim
            gre_s[pl.ds(base, SUBLANES), :] = jnp.concatenate(rows_re, axis=0)
            gim_s[pl.ds(base, SUBLANES), :] = jnp.concatenate(rows_im, axis=0)
            return gre, gim, da, db

        zero = jnp.zeros((1, GP), F32)
        gre, gim, da, db = lax.fori_loop(0, T // SUBLANES, step, (st_ref[0:1, :], st_ref[1:2, :], zero, zero))
        st_ref[0:1, :] = gre
        st_ref[1:2, :] = gim
        da_ref[...] += da
        db_ref[...] += db
        du_ref[...] = (_dot_hp(gre_s[...], bre_ref[...], NT) + _dot_hp(gim_s[...], bim_ref[...], NT)
                       + d_ref[...] * dy)
        dd_ref[...] += jnp.sum(dy * u, axis=0, keepdims=True)

    c0 = lambda i: (0, 0)
    rev = lambda i: (nchunk - 1 - i, 0)
    row = pl.BlockSpec((T, width), rev)
    hrow = pl.BlockSpec((T, GP), rev)
    v_gp = pl.BlockSpec((1, GP), c0)
    v_w = pl.BlockSpec((1, width), c0)
    sw = jax.ShapeDtypeStruct((S, width), F32)
    sg = jax.ShapeDtypeStruct((S, GP), F32)
    return pl.pallas_call(
        body, grid=(nchunk,),
        in_specs=[row, row, row, hrow, hrow, v_gp, v_gp, v_w, ANY, ANY, ANY, ANY],
        out_specs=[row, row, hrow, hrow, v_w, v_gp, v_gp],
        out_shape=[sw, sw, sg, sg, jax.ShapeDtypeStruct((1, width), F32),
                   jax.ShapeDtypeStruct((1, GP), F32), jax.ShapeDtypeStruct((1, GP), F32)],
        scratch_shapes=[pltpu.VMEM((SUBLANES, GP), F32),
                        pltpu.VMEM((width, GP), F32), pltpu.VMEM((width, GP), F32),
                        pltpu.VMEM((GP, width), F32), pltpu.VMEM((GP, width), F32)],
        name="s5_bwd", compiler_params=_cparams(("arbitrary",)),
    )(u, dgy, y, hre, him, a_row, b_row, d_row, bmre, bmim, cmre, cmimn)


def _block_diag(x):
    G, A, B = x.shape
    eye = jnp.eye(G, dtype=x.dtype)
    return (eye[:, None, :, None] * x[:, :, None, :]).reshape(G * A, G * B)


def _block_diag_take(m, G):
    A, B = m.shape[0] // G, m.shape[1] // G
    return jnp.einsum("gagb->gab", m.reshape(G, A, G, B))


def _adamw(w, g, m, v):
    shape = w.shape
    C = shape[-1]
    w2, g2, m2, v2 = (t.reshape(-1, C) for t in (w, g, m, v))
    R = w2.shape[0]
    rb = _tile(R, max(SUBLANES, (1 << 19) // C), SUBLANES)
    c1 = 1.0 - ADAM_B1 ** ADAM_STEP
    c2 = 1.0 - ADAM_B2 ** ADAM_STEP

    def body(w_ref, g_ref, m_ref, v_ref, d_ref, nm_ref, nv_ref):
        gv = g_ref[...]
        nm = ADAM_B1 * m_ref[...] + (1.0 - ADAM_B1) * gv
        nv = ADAM_B2 * v_ref[...] + (1.0 - ADAM_B2) * (gv * gv)
        d_ref[...] = -ADAM_LR * ((nm / c1) / (jnp.sqrt(nv / c2) + ADAM_EPS) + ADAM_WD * w_ref[...])
        nm_ref[...] = nm
        nv_ref[...] = nv

    blk = pl.BlockSpec((rb, C), lambda i: (i, 0))
    sd = jax.ShapeDtypeStruct((R, C), F32)
    outs = pl.pallas_call(
        body, grid=(R // rb,), in_specs=[blk] * 4, out_specs=[blk] * 3, out_shape=[sd] * 3,
        name="adamw", compiler_params=_cparams(("parallel",)),
    )(w2, g2, m2, v2)
    return tuple(o.reshape(shape) for o in outs)


def _coords():
    x, y, c = lax.axis_index("x"), lax.axis_index("y"), lax.axis_index("c")
    return x, y, c


def _place_shard(w, l, place):
    _, R, C = w.shape
    rb = _tile(R, max(16, (1 << 19) // C), 16)

    def body(place_ref, w_ref, o_ref):
        o_ref[...] = w_ref[...].astype(BF16)

    grid_spec = pltpu.PrefetchScalarGridSpec(
        num_scalar_prefetch=1, grid=(R // rb,),
        in_specs=[pl.BlockSpec((None, rb, C), lambda i, p: (l, i, 0))],
        out_specs=pl.BlockSpec((None, rb, C), lambda i, p: (p[1], i, 0)))
    return pl.pallas_call(
        body, grid_spec=grid_spec, out_shape=jax.ShapeDtypeStruct((N_CHIPS, R, C), BF16),
        name="place_shard", compiler_params=_cparams(("arbitrary",)),
    )(place, w)


def _ag_weights(bufs, n_w, L):
    n = len(bufs)
    assert L == N_CHIPS

    def body(*refs):
        outs = refs[n:2 * n]
        send_sems, recv_sems = refs[2 * n:]
        x, y, c = _coords()
        sibling = (x, y, 1 - c)
        others = [(1 - x, y), (x, 1 - y), (1 - x, 1 - y)]

        def rcopy(a, l, k, block, to):
            cx, cy, cc = block
            ref = outs[a * L + l]
            hr = ref.shape[1] // 2
            blk = ref.at[2 * cx + cy, pl.ds(cc * hr, hr)]
            return pltpu.make_async_remote_copy(
                src_ref=blk, dst_ref=blk, send_sem=send_sems.at[a * 6 + k], recv_sem=recv_sems.at[a * 6 + k],
                device_id=to, device_id_type=MESH)

        def all_layers(a, k):
            ref = outs[a * L]
            part = ref.at[:, pl.ds(0, ref.shape[1] // 2)]
            return pltpu.make_async_remote_copy(
                src_ref=part, dst_ref=part, send_sem=send_sems.at[a * 6 + k], recv_sem=recv_sems.at[a * 6 + k],
                device_id=sibling, device_id_type=MESH)

        for a in range(n_w):
            for j, ch in enumerate(others):
                for l in range(L):
                    rcopy(a, l, j, (x, y, c), (*ch, c)).start()
        for a in range(n_w):
            for j, ch in enumerate(others):
                all_layers(a, j).wait_recv()
                for l in range(L):
                    rcopy(a, l, 3 + j, (*ch, c), sibling).start()
        for a in range(n_w):
            for j in range(3):
                all_layers(a, 3 + j).wait_recv()
        for a in range(n_w):
            for k in range(6):
                all_layers(a, k).wait_send()

    res = pl.pallas_call(
        body, out_shape=[jax.ShapeDtypeStruct(b.shape, b.dtype) for b in bufs],
        in_specs=[ANY] * n, out_specs=[ANY] * n, input_output_aliases={i: i for i in range(n)},
        scratch_shapes=[pltpu.SemaphoreType.DMA((n_w * 6,)), pltpu.SemaphoreType.DMA((n_w * 6,))],
        name="ag_weights",
    )(*bufs)
    return [list(res[a * L:(a + 1) * L]) for a in range(n_w)]


def _rs_pair(grads):
    n = len(grads)
    out_shapes = [jax.ShapeDtypeStruct((N_CHIPS, g.shape[1] // 2, g.shape[2]), F32) for g in grads]

    def body(*refs):
        gs = refs[:n]
        outs = refs[n:2 * n]
        send_sems, recv_sems = refs[2 * n:]
        x, y, c = _coords()
        copies = []
        for i in range(n):
            hr = gs[i].shape[1] // 2
            cp = pltpu.make_async_remote_copy(
                src_ref=gs[i].at[:, pl.ds((1 - c) * hr, hr)], dst_ref=outs[i],
                send_sem=send_sems.at[i], recv_sem=recv_sems.at[i],
                device_id=(x, y, 1 - c), device_id_type=MESH)
            cp.start()
            copies.append(cp)
        for cp in copies:
            cp.wait_recv()
        for cp in copies:
            cp.wait_send()

    return pl.pallas_call(
        body, out_shape=out_shapes, in_specs=[ANY] * n, out_specs=[ANY] * n,
        scratch_shapes=[pltpu.SemaphoreType.DMA((n,)), pltpu.SemaphoreType.DMA((n,))],
        name="rs_pair",
    )(*grads)


def _pair_add(g, r, place):
    _, R, C = g.shape
    hr = R // 2
    rb = _tile(hr, max(16, (1 << 19) // C), 16)
    nb = hr // rb

    def body(place_ref, g_ref, r_ref, p16_ref, own_ref):
        s = g_ref[...] + r_ref[...]
        p16_ref[...] = s.astype(BF16)

        @pl.when(pl.program_id(1) == place_ref[1])
        def _():
            own_ref[...] = s

    grid_spec = pltpu.PrefetchScalarGridSpec(
        num_scalar_prefetch=1, grid=(nb, N_CHIPS),
        in_specs=[pl.BlockSpec((None, rb, C), lambda i, k, p: (k, p[0] * nb + i, 0)),
                  pl.BlockSpec((None, rb, C), lambda i, k, p: (k, i, 0))],
        out_specs=[pl.BlockSpec((None, rb, C), lambda i, k, p: (k, i, 0)),
                   pl.BlockSpec((rb, C), lambda i, k, p: (i, 0))])
    return pl.pallas_call(
        body, grid_spec=grid_spec,
        out_shape=[jax.ShapeDtypeStruct((N_CHIPS, hr, C), BF16), jax.ShapeDtypeStruct((hr, C), F32)],
        name="pair_add", compiler_params=_cparams(("arbitrary", "arbitrary")),
    )(place, g, r)


def _rs_chips(p16, n_w, L):
    n = len(p16)
    assert L == N_CHIPS
    out_shapes = [jax.ShapeDtypeStruct((3,) + p.shape[1:], BF16) for p in p16]

    def body(*refs):
        ps = refs[:n]
        outs = refs[n:2 * n]
        send_sems, recv_sems = refs[2 * n:]
        x, y, c = _coords()
        for a in range(n_w):
            for r in (1, 2, 3):
                kx = x ^ (r >> 1) if (r >> 1) else x
                ky = y ^ (r & 1) if (r & 1) else y
                for l in range(L):
                    i = a * L + l
                    pltpu.make_async_remote_copy(
                        src_ref=ps[i].at[2 * kx + ky], dst_ref=outs[i].at[r - 1],
                        send_sem=send_sems.at[a * 3 + r - 1], recv_sem=recv_sems.at[a * 3 + r - 1],
                        device_id=(kx, ky, c), device_id_type=MESH).start()
        for a in range(n_w):
            whole = ps[a * L]
            for r in (1, 2, 3):
                pltpu.make_async_remote_copy(
                    src_ref=whole, dst_ref=whole,
                    send_sem=send_sems.at[a * 3 + r - 1], recv_sem=recv_sems.at[a * 3 + r - 1],
                    device_id=(x, y, c), device_id_type=MESH).wait()

    return pl.pallas_call(
        body, out_shape=out_shapes, in_specs=[ANY] * n, out_specs=[ANY] * n,
        scratch_shapes=[pltpu.SemaphoreType.DMA((n_w * 3,)), pltpu.SemaphoreType.DMA((n_w * 3,))],
        name="rs_chips",
    )(*p16)


def _chip_sum(own, recv, full, l, place):
    hr, C = own.shape
    rb = _tile(hr, max(16, (1 << 19) // C), 16)
    nb = hr // rb

    def body(place_ref, o_ref, r_ref, full_ref, s_ref):
        s = o_ref[...] + r_ref[0].astype(F32)
        s = s + r_ref[1].astype(F32)
        s_ref[...] = s + r_ref[2].astype(F32)

    grid_spec = pltpu.PrefetchScalarGridSpec(
        num_scalar_prefetch=1, grid=(nb,),
        in_specs=[pl.BlockSpec((rb, C), lambda i, p: (i, 0)), pl.BlockSpec((3, rb, C), lambda i, p: (0, i, 0)), ANY],
        out_specs=pl.BlockSpec((None, rb, C), lambda i, p: (l, p[0] * nb + i, 0)))
    return pl.pallas_call(
        body, grid_spec=grid_spec, out_shape=jax.ShapeDtypeStruct(full.shape, F32),
        input_output_aliases={3: 0}, name="chip_sum", compiler_params=_cparams(("arbitrary",)),
    )(place, own, recv, full)


def _share_pair(fulls):
    n = len(fulls)

    def body(*refs):
        outs = refs[n:2 * n]
        send_sems, recv_sems = refs[2 * n:]
        x, y, c = _coords()
        copies = []
        for a in range(n):
            hr = outs[a].shape[1] // 2
            mine = outs[a].at[:, pl.ds(c * hr, hr)]
            cp = pltpu.make_async_remote_copy(
                src_ref=mine, dst_ref=mine, send_sem=send_sems.at[a], recv_sem=recv_sems.at[a],
                device_id=(x, y, 1 - c), device_id_type=MESH)
            cp.start()
            copies.append(cp)
        for cp in copies:
            cp.wait_recv()
        for cp in copies:
            cp.wait_send()

    return pl.pallas_call(
        body, out_shape=[jax.ShapeDtypeStruct(f.shape, f.dtype) for f in fulls],
        in_specs=[ANY] * n, out_specs=[ANY] * n, input_output_aliases={i: i for i in range(n)},
        scratch_shapes=[pltpu.SemaphoreType.DMA((n,)), pltpu.SemaphoreType.DMA((n,))],
        name="share_pair",
    )(*fulls)


def _small_allreduce(packed):
    m_per, ncol = packed.shape

    def body(x_ref, out_ref, tot_ref, send_sems, recv_sems, local_sem):
        x, y, c = _coords()
        me, sibling = (x, y, c), (x, y, 1 - c)
        chips = [(1 - x, y), (x, 1 - y), (1 - x, 1 - y)]

        def rows(px, py, pc):
            return out_ref.at[pl.ds((4 * px + 2 * py + pc) * m_per, m_per), :]

        def copy(k, block, to, src=None):
            return pltpu.make_async_remote_copy(
                src_ref=rows(*block) if src is None else src, dst_ref=rows(*block),
                send_sem=send_sems.at[k], recv_sem=recv_sems.at[k], device_id=to, device_id_type=MESH)

        mine = pltpu.make_async_copy(x_ref, rows(*me), local_sem)
        mine.start()
        first = [copy(0, me, sibling, src=x_ref)]
        first += [copy(1 + j, me, (*chip, c), src=x_ref) for j, chip in enumerate(chips)]
        for cp in first:
            cp.start()
        passed = [copy(4 + j, (*chip, c), sibling) for j, chip in enumerate(chips)]
        for j, chip in enumerate(chips):
            copy(1 + j, (*chip, c), me).wait_recv()
            passed[j].start()
        copy(0, sibling, me).wait_recv()
        for j, chip in enumerate(chips):
            copy(4 + j, (*chip, 1 - c), me).wait_recv()
        for cp in first + passed:
            cp.wait_send()
        mine.wait()
        tot = out_ref[pl.ds(0, m_per), :]
        for d in range(1, N_DEV):
            tot = tot + out_ref[pl.ds(d * m_per, m_per), :]
        tot_ref[...] = tot

    _, tot = pl.pallas_call(
        body,
        out_shape=[jax.ShapeDtypeStruct((N_DEV * m_per, ncol), F32), jax.ShapeDtypeStruct((m_per, ncol), F32)],
        in_specs=[VMEM], out_specs=[VMEM, VMEM],
        scratch_shapes=[pltpu.SemaphoreType.DMA((7,)), pltpu.SemaphoreType.DMA((7,)), pltpu.SemaphoreType.DMA],
        name="small_allreduce",
        compiler_params=pltpu.CompilerParams(vmem_limit_bytes=VMEM_LIMIT_MB * 1024 * 1024),
    )(packed)
    return tot


BIG = ["w_in", "sb_w_out", "ssm_w_glu", "ssm_w_out", "mem_w_kv", "mem_w_out", "w_o", "ffn_w_gate_up", "ffn_w_down"]
KIND = {"w_in": "col", "sb_w_out": "col", "ssm_w_glu": "col", "ssm_w_out": "col", "mem_w_kv": "row",
        "mem_w_out": "col", "w_o": "row", "ffn_w_gate_up": "col", "ffn_w_down": "row"}
SMALL = ["b_in", "ssm_lambda_re", "ssm_lambda_im", "ssm_log_dt", "ssm_b_re", "ssm_b_im", "ssm_c_re", "ssm_c_im",
         "ssm_d", "ln1_g", "ln1_b", "ln2_g", "ln2_b"]
WEIGHTS = ["w_in", "b_in", "sb_w_out", "ssm_lambda_re", "ssm_lambda_im", "ssm_log_dt", "ssm_b_re", "ssm_b_im",
           "ssm_c_re", "ssm_c_im", "ssm_d", "ssm_w_glu", "ssm_w_out", "mem_w_kv", "mem_w_out", "w_o", "ln1_g",
           "ln1_b", "ffn_w_gate_up", "ffn_w_down", "ln2_g", "ln2_b"]


def _pack(arrs):
    flat = jnp.concatenate([a.reshape(-1).astype(F32) for a in arrs])
    n = flat.shape[0]
    rows = -(-n // LANES)
    rows = -(-rows // SUBLANES) * SUBLANES
    return jnp.pad(flat, (0, rows * LANES - n)).reshape(rows, LANES)


def _unpack(packed, like):
    flat = packed.reshape(-1)
    out, off = [], 0
    for a in like:
        out.append(flat[off:off + a.size].reshape(a.shape))
        off += a.size
    return out


def _step(x, mem, target, W, M1, V1):
    S, D = x.shape[1], x.shape[2]
    L = W["w_in"].shape[0]
    x0 = x.reshape(S, D)
    mem2 = mem.reshape(mem.shape[1], D)
    tgt = target.reshape(S, D)
    alpha = (2 * L) ** 0.25
    sbw = W["sb_w_out"].shape[1]
    ssw = W["ssm_d"].shape[1]
    mw = W["mem_w_out"].shape[1]
    heads = sbw // HEAD_DIM
    G, P = W["ssm_lambda_re"].shape[1], W["ssm_lambda_re"].shape[2]
    q_off, k_off, v_off = 0, sbw, 2 * sbw
    u_off = 3 * sbw
    qm_off = u_off + ssw
    gate_off = qm_off + mw

    x_i, y_i, c_i = _coords()
    place = jnp.stack([c_i, 2 * x_i + y_i]).astype(jnp.int32)
    gathered = _ag_weights([_place_shard(W[n], l, place) for n in BIG for l in range(L)], len(BIG), L)
    Wg = {n: gathered[i] for i, n in enumerate(BIG)}

    saved = []
    xl = x0
    xlb = x0.astype(BF16)
    for l in range(L):
        sv = {"x": xlb}
        proj = _mm_nn(xlb, Wg["w_in"][l], "col", bias=W["b_in"][l][None, :], name="mm_proj")
        sb, sb_ctot = _sb_fwd(proj, q_off, k_off, v_off, heads)
        p_sb = _mm_nn(sb, Wg["sb_w_out"][l], "col", name="mm_sb_out")

        bre_t = W["ssm_b_re"][l].transpose(0, 2, 1)
        bim_t = W["ssm_b_im"][l].transpose(0, 2, 1)
        logdt = W["ssm_log_dt"][l][:, None, None]
        lre3 = W["ssm_lambda_re"][l][:, None, :]
        lim3 = W["ssm_lambda_im"][l][:, None, :]
        a_gp, b_gp, bbre, bbim = _s5_disc(lre3, lim3, logdt, bre_t, bim_t)
        a_row, b_row = a_gp.reshape(1, G * P), b_gp.reshape(1, G * P)
        bmre, bmim = _block_diag(bbre), _block_diag(bbim)
        cmre = _block_diag(W["ssm_c_re"][l].transpose(0, 2, 1))
        cmimn = _block_diag(-W["ssm_c_im"][l].transpose(0, 2, 1))
        d_row = W["ssm_d"][l][None, :]
        u_ssm, y, gy, hre, him = _s5_fwd(proj, u_off, ssw, a_row, b_row, bmre, bmim, cmre, cmimn, d_row)
        glu = _mm_nn(gy, Wg["ssm_w_glu"][l], "col", name="mm_glu")
        zz = _glu_fwd(glu)
        p_ssm = _mm_nn(zz, Wg["ssm_w_out"][l], "col", name="mm_ssm_out")

        kv = _mm_nn(mem2, Wg["mem_w_kv"][l], "row", name="mm_kv")
        mm_o = _mem_fwd(proj, qm_off, mw, kv)
        p_mem = _mm_nn(mm_o, Wg["mem_w_out"][l], "col", name="mm_mem_out")

        merged = _merge_fwd(proj, gate_off, p_sb, p_ssm, p_mem)
        mix = _mm_nn(merged, Wg["w_o"][l], "row", name="mm_wo")
        x1, x1b, xh1, rs1 = _ln_fwd(xl, mix, W["ln1_g"][l][None, :], W["ln1_b"][l][None, :], alpha)
        gu = _mm_nn(x1b, Wg["ffn_w_gate_up"][l], "col", name="mm_gate_up")
        hid = _swiglu_fwd(gu)
        ffn = _mm_nn(hid, Wg["ffn_w_down"][l], "row", name="mm_down")
        x2, x2b, xh2, rs2 = _ln_fwd(x1, ffn, W["ln2_g"][l][None, :], W["ln2_b"][l][None, :], alpha)
        sv.update(proj=proj, sb=sb, sb_ctot=sb_ctot, p_sb=p_sb, y=y, gy=gy, hre=hre, him=him, glu=glu, zz=zz,
                  p_ssm=p_ssm, kv=kv, mm_o=mm_o, p_mem=p_mem, merged=merged, x1=x1b, xh1=xh1, rs1=rs1, gu=gu,
                  hid=hid, xh2=xh2, rs2=rs2, u=u_ssm,
                  disc=(lre3, lim3, logdt, bre_t, bim_t, a_row, b_row, bmre, bmim, cmre, cmimn, d_row))
        saved.append(sv)
        xl, xlb = x2, x2b

    dxl, loss_part = _loss_head(xl, tgt)

    gbig = {n: [None] * L for n in BIG}
    gsmall = {n: [None] * L for n in SMALL}
    for l in range(L - 1, -1, -1):
        sv = saved[l]
        proj = sv["proj"]
        dr2, dr2b, dg2, db2 = _ln_bwd(dxl, sv["xh2"], sv["rs2"], W["ln2_g"][l][None, :])
        gsmall["ln2_g"][l], gsmall["ln2_b"][l] = dg2[0], db2[0]
        dhid = _mm_nt(dr2b, Wg["ffn_w_down"][l], "row", name="mm_d_hid")
        gbig["ffn_w_down"][l] = _mm_tn(sv["hid"], dr2b, "row", name="mm_g_down")
        dgu = _swiglu_bwd(dhid, sv["gu"])
        dx1 = _mm_nt(dgu, Wg["ffn_w_gate_up"][l], "col", add=dr2, add_scale=alpha, name="mm_d_x1")
        gbig["ffn_w_gate_up"][l] = _mm_tn(sv["x1"], dgu, "col", name="mm_g_gate_up")

        dr1, dr1b, dg1, db1 = _ln_bwd(dx1, sv["xh1"], sv["rs1"], W["ln1_g"][l][None, :])
        gsmall["ln1_g"][l], gsmall["ln1_b"][l] = dg1[0], db1[0]
        dmerged = _mm_nt(dr1b, Wg["w_o"][l], "row", name="mm_d_merged")
        gbig["w_o"][l] = _mm_tn(sv["merged"], dr1b, "row", name="mm_g_wo")
        dp_sb, dp_ssm, dp_mem, dgl0, dgl1, dgl2 = _merge_bwd(
            dmerged, proj, gate_off, sv["p_sb"], sv["p_ssm"], sv["p_mem"])

        dsb = _mm_nt(dp_sb, Wg["sb_w_out"][l], "col", out_dtype=BF16, name="mm_d_sb")
        gbig["sb_w_out"][l] = _mm_tn(sv["sb"], dp_sb, "col", name="mm_g_sb_out")

        dzz = _mm_nt(dp_ssm, Wg["ssm_w_out"][l], "col", name="mm_d_zz")
        gbig["ssm_w_out"][l] = _mm_tn(sv["zz"], dp_ssm, "col", name="mm_g_ssm_out")
        dglu = _glu_bwd(dzz, sv["glu"])
        dgy = _mm_nt(dglu, Wg["ssm_w_glu"][l], "col", name="mm_d_gy")
        gbig["ssm_w_glu"][l] = _mm_tn(sv["gy"], dglu, "col", name="mm_g_glu")
        lre3, lim3, logdt, bre_t, bim_t, a_row, b_row, bmre, bmim, cmre, cmimn, d_row = sv["disc"]
        du, dy_ssm, g_re, g_im, dd, da, db = _s5_bwd(
            sv["u"], dgy, sv["y"], sv["hre"], sv["him"], a_row, b_row, bmre, bmim, cmre, cmimn, d_row)
        dbmre = _mm_tn(sv["u"], g_re, "plain", name="mm_g_ssm_bre")
        dbmim = _mm_tn(sv["u"], g_im, "plain", name="mm_g_ssm_bim")
        dcmre = _mm_tn(sv["hre"], dy_ssm, "plain", name="mm_g_ssm_cre")
        dcmimn = _mm_tn(sv["him"], dy_ssm, "plain", name="mm_g_ssm_cim")
        dlre, dlim, dlogdt, dbre_t, dbim_t = _s5_disc_bwd(
            lre3, lim3, logdt, bre_t, bim_t, da.reshape(G, 1, P), db.reshape(G, 1, P),
            _block_diag_take(dbmre, G), _block_diag_take(dbmim, G))
        gsmall["ssm_lambda_re"][l], gsmall["ssm_lambda_im"][l] = dlre.reshape(G, P), dlim.reshape(G, P)
        gsmall["ssm_log_dt"][l] = dlogdt.reshape(G)
        gsmall["ssm_b_re"][l] = dbre_t.transpose(0, 2, 1)
        gsmall["ssm_b_im"][l] = dbim_t.transpose(0, 2, 1)
        gsmall["ssm_c_re"][l] = _block_diag_take(dcmre, G).transpose(0, 2, 1)
        gsmall["ssm_c_im"][l] = -_block_diag_take(dcmimn, G).transpose(0, 2, 1)
        gsmall["ssm_d"][l] = dd[0]

        dmm = _mm_nt(dp_mem, Wg["mem_w_out"][l], "col", out_dtype=BF16, name="mm_d_mm")
        gbig["mem_w_out"][l] = _mm_tn(sv["mm_o"], dp_mem, "col", name="mm_g_mem_out")
        dqm, dkv = _mem_bwd(proj, qm_off, mw, sv["kv"], dmm)
        gbig["mem_w_kv"][l] = _mm_tn(mem2, dkv, "row", name="mm_g_kv")

        dq, dk, dv = _sb_bwd(proj, q_off, k_off, v_off, heads, dsb, sv["sb_ctot"])
        dproj, dbin = _assemble_dproj([dq, dk, dv, du, dqm, dgl0, dgl1, dgl2])
        gsmall["b_in"][l] = dbin[0]
        dxl = _mm_nt(dproj, Wg["w_in"][l], "col", add=dr1, add_scale=alpha, name="mm_d_x")
        gbig["w_in"][l] = _mm_tn(sv["x"], dproj, "col", name="mm_g_win")

    grad_x = dxl.reshape(x.shape)

    flat = [gbig[n][l] for n in BIG for l in range(L)]
    from_sibling = _rs_pair(flat)
    p16, own = [], []
    for g, r in zip(flat, from_sibling):
        a16, a32 = _pair_add(g, r, place)
        p16.append(a16)
        own.append(a32)
    from_chips = _rs_chips(p16, len(BIG), L)
    fulls = []
    for a, n in enumerate(BIG):
        full = lax.empty(W[n].shape, F32)
        for l in range(L):
            full = _chip_sum(own[a * L + l], from_chips[a * L + l], full, l, place)
        fulls.append(full)
    reduced = _share_pair(fulls)
    grads = {n: reduced[i] for i, n in enumerate(BIG)}

    small_local = [jnp.stack(gsmall[n]) for n in SMALL]
    packed = _pack(small_local + [loss_part[0, :1]])
    total = _small_allreduce(packed)
    unpacked = _unpack(total, small_local + [loss_part[0, :1]])
    for n, g in zip(SMALL, unpacked[:-1]):
        grads[n] = g
    loss = unpacked[-1][0]

    delta, new_m, new_v = {}, {}, {}
    for n in BIG:
        delta[n], new_m[n], new_v[n] = _adamw(W[n], grads[n], M1[n], V1[n])
    sm = _adamw(_pack([W[n] for n in SMALL]), _pack([grads[n] for n in SMALL]),
                _pack([M1[n] for n in SMALL]), _pack([V1[n] for n in SMALL]))
    like = [W[n] for n in SMALL]
    for n, d, m_, v_ in zip(SMALL, _unpack(sm[0], like), _unpack(sm[1], like), _unpack(sm[2], like)):
        delta[n], new_m[n], new_v[n] = d, m_, v_

    return (loss, grad_x, *[grads[n] for n in WEIGHTS], *[delta[n] for n in WEIGHTS],
            *[new_m[n] for n in WEIGHTS], *[new_v[n] for n in WEIGHTS])


def kernel(x, mem, w_in, b_in, sb_w_out, ssm_lambda_re, ssm_lambda_im, ssm_log_dt, ssm_b_re, ssm_b_im, ssm_c_re, ssm_c_im, ssm_d, ssm_w_glu, ssm_w_out, mem_w_kv, mem_w_out, w_o, ln1_g, ln1_b, ffn_w_gate_up, ffn_w_down, ln2_g, ln2_b, loss_target, m_w_in, m_b_in, m_sb_w_out, m_ssm_lambda_re, m_ssm_lambda_im, m_ssm_log_dt, m_ssm_b_re, m_ssm_b_im, m_ssm_c_re, m_ssm_c_im, m_ssm_d, m_ssm_w_glu, m_ssm_w_out, m_mem_w_kv, m_mem_w_out, m_w_o, m_ln1_g, m_ln1_b, m_ffn_w_gate_up, m_ffn_w_down, m_ln2_g, m_ln2_b, v_w_in, v_b_in, v_sb_w_out, v_ssm_lambda_re, v_ssm_lambda_im, v_ssm_log_dt, v_ssm_b_re, v_ssm_b_im, v_ssm_c_re, v_ssm_c_im, v_ssm_d, v_ssm_w_glu, v_ssm_w_out, v_mem_w_kv, v_mem_w_out, v_w_o, v_ln1_g, v_ln1_b, v_ffn_w_gate_up, v_ffn_w_down, v_ln2_g, v_ln2_b):
    W = dict(w_in=w_in, b_in=b_in, sb_w_out=sb_w_out, ssm_lambda_re=ssm_lambda_re, ssm_lambda_im=ssm_lambda_im,
             ssm_log_dt=ssm_log_dt, ssm_b_re=ssm_b_re, ssm_b_im=ssm_b_im, ssm_c_re=ssm_c_re, ssm_c_im=ssm_c_im,
             ssm_d=ssm_d, ssm_w_glu=ssm_w_glu, ssm_w_out=ssm_w_out, mem_w_kv=mem_w_kv, mem_w_out=mem_w_out,
             w_o=w_o, ln1_g=ln1_g, ln1_b=ln1_b, ffn_w_gate_up=ffn_w_gate_up, ffn_w_down=ffn_w_down,
             ln2_g=ln2_g, ln2_b=ln2_b)
    M1 = dict(w_in=m_w_in, b_in=m_b_in, sb_w_out=m_sb_w_out, ssm_lambda_re=m_ssm_lambda_re,
              ssm_lambda_im=m_ssm_lambda_im, ssm_log_dt=m_ssm_log_dt, ssm_b_re=m_ssm_b_re, ssm_b_im=m_ssm_b_im,
              ssm_c_re=m_ssm_c_re, ssm_c_im=m_ssm_c_im, ssm_d=m_ssm_d, ssm_w_glu=m_ssm_w_glu,
              ssm_w_out=m_ssm_w_out, mem_w_kv=m_mem_w_kv, mem_w_out=m_mem_w_out, w_o=m_w_o, ln1_g=m_ln1_g,
              ln1_b=m_ln1_b, ffn_w_gate_up=m_ffn_w_gate_up, ffn_w_down=m_ffn_w_down, ln2_g=m_ln2_g, ln2_b=m_ln2_b)
    V1 = dict(w_in=v_w_in, b_in=v_b_in, sb_w_out=v_sb_w_out, ssm_lambda_re=v_ssm_lambda_re,
              ssm_lambda_im=v_ssm_lambda_im, ssm_log_dt=v_ssm_log_dt, ssm_b_re=v_ssm_b_re, ssm_b_im=v_ssm_b_im,
              ssm_c_re=v_ssm_c_re, ssm_c_im=v_ssm_c_im, ssm_d=v_ssm_d, ssm_w_glu=v_ssm_w_glu,
              ssm_w_out=v_ssm_w_out, mem_w_kv=v_mem_w_kv, mem_w_out=v_mem_w_out, w_o=v_w_o, ln1_g=v_ln1_g,
              ln1_b=v_ln1_b, ffn_w_gate_up=v_ffn_w_gate_up, ffn_w_down=v_ffn_w_down, ln2_g=v_ln2_g, ln2_b=v_ln2_b)
    return _step(x, mem, loss_target, W, M1, V1)
```

```python
import functools
import math

import jax
import jax.numpy as jnp
from jax import lax
from jax.experimental import pallas as pl
from jax.experimental.pallas import tpu as pltpu

F32 = jnp.float32
BF16 = jnp.bfloat16
MESH = pl.DeviceIdType.MESH
ANY = pl.BlockSpec(memory_space=pl.ANY)
VMEM = pl.BlockSpec(memory_space=pltpu.VMEM)

HEAD_DIM = 128
SSM_GROUP = 16
N_CHIPS = 4
N_DEV = 8
LN_EPS = 1e-5
ADAM_LR = 0.001
ADAM_B1 = 0.9
ADAM_B2 = 0.999
ADAM_EPS = 1e-08
ADAM_WD = 0.01
ADAM_STEP = 10
LANES = 128
SUBLANES = 8
VMEM_LIMIT_MB = 56


def _cparams(sem, mb=VMEM_LIMIT_MB):
    return pltpu.CompilerParams(dimension_semantics=sem, vmem_limit_bytes=mb * 1024 * 1024)


def _tile(n, pref, mult=LANES):
    best = None
    t = mult
    while t <= min(n, pref):
        if n % t == 0:
            best = t
        t += mult
    return n if best is None else best


def _dot(a, b, dims=(((1,), (0,)), ((), ()))):
    return lax.dot_general(a, b, dims, preferred_element_type=F32)


NT = (((1,), (1,)), ((), ()))
TN = (((0,), (0,)), ((), ()))


def _split2(x):
    hi = x.astype(BF16)
    lo = (x - hi.astype(F32)).astype(BF16)
    return hi, lo


def _dot_hp(a, b, dims=(((1,), (0,)), ((), ()))):
    ah, al = _split2(a)
    bh, bl = _split2(b)
    return _dot(ah, bh, dims) + (_dot(ah, bl, dims) + _dot(al, bh, dims))


def _dot_mask(x, u2):
    hi, lo = _split2(x)
    return _dot(jnp.concatenate([hi, lo], axis=1), u2)


def _tri2(n, rel):
    ri = lax.broadcasted_iota(jnp.int32, (2 * n, n), 0)
    ci = lax.broadcasted_iota(jnp.int32, (2 * n, n), 1)
    ri = jnp.where(ri >= n, ri - n, ri)
    return rel(ri, ci).astype(BF16)


def _mm_call(a, b, *, dims, grid, a_spec, b_spec, out_spec, out_shape, name,
             bias=None, bias_spec=None, add=None, add_spec=None, add_scale=1.0):
    nk = grid[2]
    has_bias = bias is not None
    has_add = add is not None

    def body(*refs):
        a_ref, b_ref = refs[0], refs[1]
        pos = 2
        bias_ref = refs[pos] if has_bias else None
        pos += int(has_bias)
        add_ref = refs[pos] if has_add else None
        pos += int(has_add)
        o_ref = refs[pos]
        acc_ref = refs[pos + 1] if nk > 1 else None
        p = _dot(a_ref[...].astype(BF16), b_ref[...].astype(BF16), dims)

        def finish(acc):
            if has_bias:
                acc = acc + bias_ref[...]
            if has_add:
                acc = acc + add_scale * add_ref[...].astype(F32)
            o_ref[...] = acc.astype(o_ref.dtype)

        if nk == 1:
            finish(p)
        else:
            k = pl.program_id(2)

            @pl.when(k == 0)
            def _():
                acc_ref[...] = p

            @pl.when(k > 0)
            def _():
                acc_ref[...] += p

            @pl.when(k == nk - 1)
            def _():
                finish(acc_ref[...])

    ins = [a, b]
    in_specs = [a_spec, b_spec]
    if has_bias:
        ins.append(bias)
        in_specs.append(bias_spec)
    if has_add:
        ins.append(add)
        in_specs.append(add_spec)
    scratch = []
    if nk > 1:
        blk = [d for d in out_spec.block_shape if d is not None]
        scratch.append(pltpu.VMEM(tuple(blk), F32))
    return pl.pallas_call(
        body, out_shape=out_shape, grid=grid, in_specs=in_specs, out_specs=out_spec,
        scratch_shapes=scratch, name=name,
        compiler_params=_cparams(("parallel", "parallel", "arbitrary")),
    )(*ins)


def _mm_nn(a, w, kind, *, bias=None, out_dtype=F32, name):
    M, K = a.shape
    tm = _tile(M, 512, SUBLANES)
    tk = K if K <= 2048 else _tile(K, 1408)
    if kind == "col":
        Nc = w.shape[2]
        N = N_CHIPS * Nc
        tn = _tile(Nc, 1408)
        npc = Nc // tn
        b_spec = pl.BlockSpec((None, tk, tn), lambda i, j, k: (j // npc, k, j % npc))
    else:
        w = w.reshape(-1, w.shape[-1])
        N = w.shape[1]
        tn = _tile(N, 2048)
        b_spec = pl.BlockSpec((tk, tn), lambda i, j, k: (k, j))
    grid = (M // tm, N // tn, K // tk)
    bias_spec = pl.BlockSpec((1, tn), lambda i, j, k: (0, j)) if bias is not None else None
    return _mm_call(
        a, w, dims=(((1,), (0,)), ((), ())), grid=grid,
        a_spec=pl.BlockSpec((tm, tk), lambda i, j, k: (i, k)), b_spec=b_spec,
        out_spec=pl.BlockSpec((tm, tn), lambda i, j, k: (i, j)),
        out_shape=jax.ShapeDtypeStruct((M, N), out_dtype), name=name,
        bias=bias, bias_spec=bias_spec)


def _mm_nt(dy, w, kind, *, add=None, add_scale=1.0, out_dtype=F32, name):
    M, N = dy.shape
    tm = _tile(M, 512, SUBLANES)
    if kind == "col":
        K, Nc = w.shape[1], w.shape[2]
        tn = _tile(Nc, 1408)
        npc = Nc // tn
        tko = _tile(K, 2048)
        b_spec = pl.BlockSpec((None, tko, tn), lambda i, j, r: (r // npc, j, r % npc))
    else:
        w = w.reshape(-1, w.shape[-1])
        K = w.shape[0]
        tn = _tile(N, 2048)
        tko = _tile(K, 2048)
        b_spec = pl.BlockSpec((tko, tn), lambda i, j, r: (j, r))
    grid = (M // tm, K // tko, N // tn)
    add_spec = pl.BlockSpec((tm, tko), lambda i, j, r: (i, j)) if add is not None else None
    return _mm_call(
        dy, w, dims=NT, grid=grid,
        a_spec=pl.BlockSpec((tm, tn), lambda i, j, r: (i, r)), b_spec=b_spec,
        out_spec=pl.BlockSpec((tm, tko), lambda i, j, r: (i, j)),
        out_shape=jax.ShapeDtypeStruct((M, K), out_dtype), name=name,
        add=add, add_spec=add_spec, add_scale=add_scale)


def _mm_tn(a, dy, kind, *, name):
    M, K = a.shape
    N = dy.shape[1]
    tm = _tile(M, 1024, SUBLANES)
    tkw = _tile(K, 1408)
    if kind == "col":
        Nc = N // N_CHIPS
        tn = _tile(Nc, 1408)
        npc = Nc // tn
        out_spec = pl.BlockSpec((None, tkw, tn), lambda i, j, m: (j // npc, i, j % npc))
        out_shape = jax.ShapeDtypeStruct((N_CHIPS, K, Nc), F32)
    else:
        tn = _tile(N, 1024)
        out_spec = pl.BlockSpec((tkw, tn), lambda i, j, m: (i, j))
        out_shape = jax.ShapeDtypeStruct((K, N), F32)
    grid = (K // tkw, N // tn, M // tm)
    out = _mm_call(
        a, dy, dims=TN, grid=grid,
        a_spec=pl.BlockSpec((tm, tkw), lambda i, j, m: (m, i)),
        b_spec=pl.BlockSpec((tm, tn), lambda i, j, m: (m, j)),
        out_spec=out_spec, out_shape=out_shape, name=name)
    if kind == "row":
        out = out.reshape(N_CHIPS, K // N_CHIPS, N)
    return out


def _gelu_grad(x):
    k = math.sqrt(2.0 / math.pi)
    inner = k * (x + 0.044715 * x * x * x)
    t = jnp.tanh(inner)
    return 0.5 * (1.0 + t) + 0.5 * x * (1.0 - t * t) * k * (1.0 + 3.0 * 0.044715 * x * x)


def _ln_fwd(xin, delta, g, b, alpha):
    S, D = xin.shape
    ts = _tile(S, 256, SUBLANES)

    def body(x_ref, d_ref, g_ref, b_ref, y_ref, yb_ref, xh_ref, rs_ref):
        r = alpha * x_ref[...] + d_ref[...]
        mu = jnp.mean(r, axis=-1, keepdims=True)
        rc = r - mu
        var = jnp.mean(rc * rc, axis=-1, keepdims=True)
        rstd = lax.rsqrt(var + LN_EPS)
        xh = rc * rstd
        y = xh * g_ref[...] + b_ref[...]
        y_ref[...] = y
        yb_ref[...] = y.astype(BF16)
        xh_ref[...] = xh
        rs_ref[...] = rstd

    row = pl.BlockSpec((ts, D), lambda i: (i, 0))
    vec = pl.BlockSpec((1, D), lambda i: (0, 0))
    return pl.pallas_call(
        body, grid=(S // ts,), in_specs=[row, row, vec, vec],
        out_specs=[row, row, row, pl.BlockSpec((ts, 1), lambda i: (i, 0))],
        out_shape=[jax.ShapeDtypeStruct((S, D), F32), jax.ShapeDtypeStruct((S, D), BF16),
                   jax.ShapeDtypeStruct((S, D), F32), jax.ShapeDtypeStruct((S, 1), F32)],
        name="ln_fwd", compiler_params=_cparams(("parallel",)),
    )(xin, delta, g, b)


def _ln_bwd(dy, xh, rstd, g):
    S, D = dy.shape
    ts = _tile(S, 256, SUBLANES)

    def body(dy_ref, xh_ref, rs_ref, g_ref, dr_ref, drb_ref, dg_ref, db_ref):
        @pl.when(pl.program_id(0) == 0)
        def _():
            dg_ref[...] = jnp.zeros_like(dg_ref)
            db_ref[...] = jnp.zeros_like(db_ref)

        dyv = dy_ref[...]
        xhv = xh_ref[...]
        dyg = dyv * g_ref[...]
        m1 = jnp.mean(dyg, axis=-1, keepdims=True)
        m2 = jnp.mean(dyg * xhv, axis=-1, keepdims=True)
        dr = rs_ref[...] * (dyg - m1 - xhv * m2)
        dr_ref[...] = dr
        drb_ref[...] = dr.astype(BF16)
        dg_ref[...] += jnp.sum(dyv * xhv, axis=0, keepdims=True)
        db_ref[...] += jnp.sum(dyv, axis=0, keepdims=True)

    row = pl.BlockSpec((ts, D), lambda i: (i, 0))
    vec = pl.BlockSpec((1, D), lambda i: (0, 0))
    return pl.pallas_call(
        body, grid=(S // ts,),
        in_specs=[row, row, pl.BlockSpec((ts, 1), lambda i: (i, 0)), vec],
        out_specs=[row, row, vec, vec],
        out_shape=[jax.ShapeDtypeStruct((S, D), F32), jax.ShapeDtypeStruct((S, D), BF16),
                   jax.ShapeDtypeStruct((1, D), F32), jax.ShapeDtypeStruct((1, D), F32)],
        name="ln_bwd", compiler_params=_cparams(("arbitrary",)),
    )(dy, xh, rstd, g)


def _merge_fwd(proj, gate_off, p_sb, p_ssm, p_mem):
    S, D = p_sb.shape
    ts = _tile(S, 256, SUBLANES)
    gb = gate_off // D

    def body(g0, g1, g2, a0, a1, a2, o_ref):
        o_ref[...] = (jax.nn.sigmoid(g0[...]) * a0[...] + jax.nn.sigmoid(g1[...]) * a1[...]
                      + jax.nn.sigmoid(g2[...]) * a2[...]).astype(o_ref.dtype)

    row = pl.BlockSpec((ts, D), lambda i: (i, 0))
    gates = [pl.BlockSpec((ts, D), functools.partial(lambda i, n: (i, gb + n), n=n)) for n in range(3)]
    return pl.pallas_call(
        body, grid=(S // ts,), in_specs=gates + [row, row, row], out_specs=row,
        out_shape=jax.ShapeDtypeStruct((S, D), BF16), name="merge_fwd",
        compiler_params=_cparams(("parallel",)),
    )(proj, proj, proj, p_sb, p_ssm, p_mem)


def _merge_bwd(dmerged, proj, gate_off, p_sb, p_ssm, p_mem):
    S, D = p_sb.shape
    ts = _tile(S, 256, SUBLANES)
    gb = gate_off // D

    def body(dm_ref, g0, g1, g2, a0, a1, a2, d0, d1, d2, l0, l1, l2):
        dm = dm_ref[...]
        for g_ref, a_ref, d_ref, l_ref in ((g0, a0, d0, l0), (g1, a1, d1, l1), (g2, a2, d2, l2)):
            s = jax.nn.sigmoid(g_ref[...])
            d_ref[...] = (dm * s).astype(d_ref.dtype)
            l_ref[...] = dm * a_ref[...] * s * (1.0 - s)

    row = pl.BlockSpec((ts, D), lambda i: (i, 0))
    gates = [pl.BlockSpec((ts, D), functools.partial(lambda i, n: (i, gb + n), n=n)) for n in range(3)]
    sd = jax.ShapeDtypeStruct((S, D), F32)
    return pl.pallas_call(
        body, grid=(S // ts,), in_specs=[row] + gates + [row, row, row], out_specs=[row] * 6,
        out_shape=[jax.ShapeDtypeStruct((S, D), BF16)] * 3 + [sd] * 3, name="merge_bwd", compiler_params=_cparams(("parallel",)),
    )(dmerged, proj, proj, proj, p_sb, p_ssm, p_mem)


def _glu_fwd(glu):
    S, W2 = glu.shape
    W = W2 // 2
    ts = _tile(S, 512, SUBLANES)

    def body(x_ref, o_ref):
        o_ref[...] = (x_ref[:, :W] * jax.nn.sigmoid(x_ref[:, W:])).astype(o_ref.dtype)

    return pl.pallas_call(
        body, grid=(S // ts,), in_specs=[pl.BlockSpec((ts, W2), lambda i: (i, 0))],
        out_specs=pl.BlockSpec((ts, W), lambda i: (i, 0)),
        out_shape=jax.ShapeDtypeStruct((S, W), BF16), name="glu_fwd",
        compiler_params=_cparams(("parallel",)),
    )(glu)


def _glu_bwd(dzz, glu):
    S, W2 = glu.shape
    W = W2 // 2
    ts = _tile(S, 512, SUBLANES)

    def body(d_ref, x_ref, o_ref):
        d = d_ref[...]
        a = x_ref[:, :W]
        s = jax.nn.sigmoid(x_ref[:, W:])
        o_ref[:, :W] = (d * s).astype(o_ref.dtype)
        o_ref[:, W:] = (d * a * s * (1.0 - s)).astype(o_ref.dtype)

    return pl.pallas_call(
        body, grid=(S // ts,),
        in_specs=[pl.BlockSpec((ts, W), lambda i: (i, 0)), pl.BlockSpec((ts, W2), lambda i: (i, 0))],
        out_specs=pl.BlockSpec((ts, W2), lambda i: (i, 0)),
        out_shape=jax.ShapeDtypeStruct((S, W2), BF16), name="glu_bwd",
        compiler_params=_cparams(("parallel",)),
    )(dzz, glu)


def _swiglu_fwd(gu):
    S, F2 = gu.shape
    Fh = F2 // 2
    ts = _tile(S, 128, SUBLANES)

    def body(x_ref, o_ref):
        fg = x_ref[:, :Fh]
        o_ref[...] = (fg * jax.nn.sigmoid(fg) * x_ref[:, Fh:]).astype(o_ref.dtype)

    return pl.pallas_call(
        body, grid=(S // ts,), in_specs=[pl.BlockSpec((ts, F2), lambda i: (i, 0))],
        out_specs=pl.BlockSpec((ts, Fh), lambda i: (i, 0)),
        out_shape=jax.ShapeDtypeStruct((S, Fh), BF16), name="swiglu_fwd",
        compiler_params=_cparams(("parallel",)),
    )(gu)


def _swiglu_bwd(dhid, gu):
    S, F2 = gu.shape
    Fh = F2 // 2
    ts = _tile(S, 128, SUBLANES)

    def body(d_ref, x_ref, o_ref):
        d = d_ref[...]
        fg = x_ref[:, :Fh]
        fu = x_ref[:, Fh:]
        s = jax.nn.sigmoid(fg)
        o_ref[:, :Fh] = (d * fu * s * (1.0 + fg * (1.0 - s))).astype(o_ref.dtype)
        o_ref[:, Fh:] = (d * fg * s).astype(o_ref.dtype)

    return pl.pallas_call(
        body, grid=(S // ts,),
        in_specs=[pl.BlockSpec((ts, Fh), lambda i: (i, 0)), pl.BlockSpec((ts, F2), lambda i: (i, 0))],
        out_specs=pl.BlockSpec((ts, F2), lambda i: (i, 0)),
        out_shape=jax.ShapeDtypeStruct((S, F2), BF16), name="swiglu_bwd",
        compiler_params=_cparams(("parallel",)),
    )(dhid, gu)


def _assemble_dproj(pieces):
    S = pieces[0].shape[0]
    widths = [p.shape[1] for p in pieces]
    total = sum(widths)
    ts = _tile(S, 128, SUBLANES)
    n = len(pieces)

    def body(*refs):
        o_ref, b_ref = refs[n], refs[n + 1]

        @pl.when(pl.program_id(0) == 0)
        def _():
            b_ref[...] = jnp.zeros_like(b_ref)

        off = 0
        for r, w in zip(refs[:n], widths):
            v = r[...].astype(F32)
            o_ref[:, off:off + w] = v.astype(o_ref.dtype)
            b_ref[:, off:off + w] += jnp.sum(v, axis=0, keepdims=True)
            off += w

    return pl.pallas_call(
        body, grid=(S // ts,),
        in_specs=[pl.BlockSpec((ts, w), lambda i: (i, 0)) for w in widths],
        out_specs=[pl.BlockSpec((ts, total), lambda i: (i, 0)), pl.BlockSpec((1, total), lambda i: (0, 0))],
        out_shape=[jax.ShapeDtypeStruct((S, total), BF16), jax.ShapeDtypeStruct((1, total), F32)],
        name="assemble_dproj", compiler_params=_cparams(("arbitrary",)),
    )(*pieces)


def _loss_head(y, target):
    S, D = y.shape
    ts = _tile(S, 256, SUBLANES)

    def body(y_ref, t_ref, dy_ref, l_ref):
        @pl.when(pl.program_id(0) == 0)
        def _():
            l_ref[...] = jnp.zeros_like(l_ref)

        e = y_ref[...] - t_ref[...]
        dy_ref[...] = e * (1.0 / D)
        part = jnp.sum(jnp.sum(e * e, axis=1, keepdims=True), axis=0, keepdims=True) * (0.5 / D)
        l_ref[...] += jnp.broadcast_to(part, l_ref.shape)

    row = pl.BlockSpec((ts, D), lambda i: (i, 0))
    return pl.pallas_call(
        body, grid=(S // ts,), in_specs=[row, row],
        out_specs=[row, pl.BlockSpec((1, LANES), lambda i: (0, 0))],
        out_shape=[jax.ShapeDtypeStruct((S, D), F32), jax.ShapeDtypeStruct((1, LANES), F32)],
        name="loss_head", compiler_params=_cparams(("arbitrary",)),
    )(y, target)


SB_TQ = 512
SB_TK = 256


def _sb_tile_terms(q, kb, scale, causal):
    z = _dot(q, kb, NT) * scale
    soft = jnp.log(1.0 + jnp.exp(-jnp.abs(z)))
    ls = jnp.minimum(z, 0.0) - soft
    l1m = jnp.minimum(-z, 0.0) - soft
    if causal is not None:
        l1m = jnp.where(causal, l1m, 0.0)
    return ls, l1m


def _sb_causal(qi, kj, TQ, TK):
    t_idx = qi * TQ + lax.broadcasted_iota(jnp.int32, (TQ, TK), 0)
    s_idx = kj * TK + lax.broadcasted_iota(jnp.int32, (TQ, TK), 1)
    return s_idx < t_idx


def _exchange_begin(heads, start, middle=None):
    h, qi = pl.program_id(0), pl.program_id(1)

    @pl.when((h == 0) & (qi == 0))
    def _():
        start()

    if middle is not None:
        @pl.when((h == heads // 2) & (qi == 0))
        def _():
            middle()


def _exchange_end(heads, nq, finish):
    @pl.when((pl.program_id(0) == heads - 1) & (pl.program_id(1) == nq - 1))
    def _():
        finish()


def _sb_fwd(proj, q_off, k_off, v_off, heads, ag=None):
    S = proj.shape[0]
    Dh = HEAD_DIM
    TQ = min(SB_TQ, S)
    TK = SB_TK
    nq = S // TQ
    scale = Dh ** -0.5
    qb, kb0, vb0 = q_off // Dh, k_off // Dh, v_off // Dh
    n_ag = 0 if ag is None else len(ag)
    assert ag is None or heads >= 2

    def body(q_ref, k_ref, v_ref, *rest):
        o_ref, c_ref = rest[n_ag:n_ag + 2]
        if n_ag:
            ag_start, ag_middle, ag_finish = _ag_hooks(rest[n_ag + 2:2 * n_ag + 2], *rest[2 * n_ag + 2:])
            _exchange_begin(heads, ag_start, ag_middle)
        qi = pl.program_id(1)
        q = q_ref[...].astype(BF16)
        upper = _tri2(TK, lambda j, s: j > s)
        nfull = (qi * TQ) // TK

        def block(kj, carry, masked):
            c, acc = carry
            off = pl.multiple_of(kj * TK, TK)
            kblk = k_ref[pl.ds(off, TK), :].astype(BF16)
            vblk = v_ref[pl.ds(off, TK), :].astype(BF16)
            causal = _sb_causal(qi, kj, TQ, TK) if masked else None
            ls, l1m = _sb_tile_terms(q, kblk, scale, causal)
            w = jnp.exp(ls + _dot_mask(l1m, upper) + c)
            if masked:
                w = jnp.where(causal, w, 0.0)
            acc = acc + _dot(w.astype(BF16), vblk)
            c = c + jnp.sum(l1m, axis=1, keepdims=True)
            return c, acc

        carry = (jnp.zeros((TQ, 1), F32), jnp.zeros((TQ, Dh), F32))
        for d in range(TQ // TK - 1, -1, -1):
            carry = block(nfull + d, carry, True)

        c, acc = lax.fori_loop(0, nfull, lambda jj, carry: block(nfull - 1 - jj, carry, False), carry)
        o_ref[...] = acc.astype(o_ref.dtype)
        c_ref[...] = c
        if n_ag:
            _exchange_end(heads, nq, ag_finish)

    ag = [] if ag is None else list(ag)
    res = pl.pallas_call(
        body, grid=(heads, nq),
        in_specs=[pl.BlockSpec((TQ, Dh), lambda h, i: (i, qb + h)),
                  pl.BlockSpec((S, Dh), lambda h, i: (0, kb0 + h)),
                  pl.BlockSpec((S, Dh), lambda h, i: (0, vb0 + h))] + [ANY] * n_ag,
        out_specs=[pl.BlockSpec((TQ, Dh), lambda h, i: (i, h)),
                   pl.BlockSpec((None, TQ, 1), lambda h, i: (h, i, 0))] + [ANY] * n_ag,
        out_shape=[jax.ShapeDtypeStruct((S, heads * Dh), BF16), jax.ShapeDtypeStruct((heads, S, 1), F32)]
        + [jax.ShapeDtypeStruct(b.shape, b.dtype) for b in ag],
        input_output_aliases={3 + i: 2 + i for i in range(n_ag)},
        scratch_shapes=[pltpu.SemaphoreType.DMA((n_ag * 6,)), pltpu.SemaphoreType.DMA((n_ag * 6,))] if n_ag else [],
        name="sb_fwd_ag" if n_ag else "sb_fwd",
        compiler_params=_cparams(("arbitrary", "arbitrary") if n_ag else ("parallel", "arbitrary")),
    )(proj, proj, proj, *ag)
    return res[0], res[1], list(res[2:])


def _sb_bwd(proj, q_off, k_off, v_off, heads, dout, ctot, rs=None):
    S = proj.shape[0]
    Dh = HEAD_DIM
    TQ = min(SB_TQ, S)
    TK = SB_TK
    nq = S // TQ
    scale = Dh ** -0.5
    qb, kb0, vb0 = q_off // Dh, k_off // Dh, v_off // Dh
    n_rs = 0 if rs is None else len(rs)

    def body(q_ref, k_ref, v_ref, do_ref, c_ref, *rest):
        dq_ref, dk_ref, dv_ref = rest[n_rs:n_rs + 3]
        if n_rs:
            rs_start, rs_finish = _rs_chips_hooks(rest[:n_rs], rest[n_rs + 3:2 * n_rs + 3], *rest[2 * n_rs + 3:])
            _exchange_begin(heads, rs_start)
        qi = pl.program_id(1)

        @pl.when(qi == 0)
        def _():
            dk_ref[...] = jnp.zeros_like(dk_ref)
            dv_ref[...] = jnp.zeros_like(dv_ref)

        q = q_ref[...].astype(BF16)
        do = do_ref[...].astype(BF16)
        ctot = c_ref[...]
        lower_incl = _tri2(TK, lambda j, s: j <= s)
        lower = _tri2(TK, lambda j, s: j < s)
        nfull = (qi * TQ) // TK

        def block(kj, carry, masked):
            cl, ce, dq = carry
            off = pl.multiple_of(kj * TK, TK)
            kblk = k_ref[pl.ds(off, TK), :].astype(BF16)
            vblk = v_ref[pl.ds(off, TK), :].astype(BF16)
            causal = _sb_causal(qi, kj, TQ, TK) if masked else None
            ls, l1m = _sb_tile_terms(q, kblk, scale, causal)
            w = jnp.exp(ls + (ctot - cl - _dot_mask(l1m, lower_incl)))
            if masked:
                w = jnp.where(causal, w, 0.0)
            e = w * _dot(do, vblk, NT)
            before = ce + _dot_mask(e, lower)
            beta = jnp.exp(ls)
            dz = (e * (1.0 - beta) - beta * before) * scale
            if masked:
                dz = jnp.where(causal, dz, 0.0)
            dzb = dz.astype(BF16)
            dq = dq + _dot(dzb, kblk)
            dk_ref[pl.ds(off, TK), :] += _dot(dzb, q, TN)
            dv_ref[pl.ds(off, TK), :] += _dot(w.astype(BF16), do, TN)
            cl = cl + jnp.sum(l1m, axis=1, keepdims=True)
            ce = ce + jnp.sum(e, axis=1, keepdims=True)
            return cl, ce, dq

        zero = jnp.zeros((TQ, 1), F32)
        carry = lax.fori_loop(0, nfull, lambda kj, carry: block(kj, carry, False),
                              (zero, zero, jnp.zeros((TQ, Dh), F32)))
        for d in range(TQ // TK):
            carry = block(nfull + d, carry, True)
        dq_ref[...] = carry[2]
        if n_rs:
            _exchange_end(heads, nq, rs_finish)

    rs = [] if rs is None else list(rs)
    blk = pl.BlockSpec((TQ, Dh), lambda h, i: (i, h))
    col = pl.BlockSpec((S, Dh), lambda h, i: (0, h))
    sd = jax.ShapeDtypeStruct((S, heads * Dh), F32)
    res = pl.pallas_call(
        body, grid=(heads, nq),
        in_specs=[pl.BlockSpec((TQ, Dh), lambda h, i: (i, qb + h)),
                  pl.BlockSpec((S, Dh), lambda h, i: (0, kb0 + h)),
                  pl.BlockSpec((S, Dh), lambda h, i: (0, vb0 + h)), blk,
                  pl.BlockSpec((None, TQ, 1), lambda h, i: (h, i, 0))] + [ANY] * n_rs,
        out_specs=[blk, col, col] + [ANY] * n_rs, out_shape=[sd, sd, sd] + _rs_chips_shapes(rs),
        scratch_shapes=[pltpu.SemaphoreType.DMA((n_rs * 3,)), pltpu.SemaphoreType.DMA((n_rs * 3,))] if n_rs else [],
        name="sb_bwd_rs" if n_rs else "sb_bwd",
        compiler_params=_cparams(("arbitrary", "arbitrary") if n_rs else ("parallel", "arbitrary")),
    )(proj, proj, proj, dout, ctot, *rs)
    return res[0], res[1], res[2], list(res[3:])


def _mem_probs(qh, kh, scale):
    s = _dot(qh, kh, NT) * scale
    m = jnp.max(s, axis=-1, keepdims=True)
    p = jnp.exp(s - m)
    return p / jnp.sum(p, axis=-1, keepdims=True)


def _mem_fwd(proj, q_off, width, kv):
    S = proj.shape[0]
    Dh = HEAD_DIM
    heads = width // Dh
    ts = _tile(S, 512, SUBLANES)
    scale = Dh ** -0.5
    M = kv.shape[0]

    def body(q_ref, kv_ref, o_ref):
        for h in range(heads):
            qh = q_ref[:, h * Dh:(h + 1) * Dh].astype(BF16)
            kh = kv_ref[:, h * Dh:(h + 1) * Dh].astype(BF16)
            vh = kv_ref[:, width + h * Dh:width + (h + 1) * Dh].astype(BF16)
            p = _mem_probs(qh, kh, scale)
            o_ref[:, h * Dh:(h + 1) * Dh] = _dot(p.astype(BF16), vh).astype(o_ref.dtype)

    return pl.pallas_call(
        body, grid=(S // ts,),
        in_specs=[pl.BlockSpec((ts, width), lambda i: (i, q_off // width)),
                  pl.BlockSpec((M, 2 * width), lambda i: (0, 0))],
        out_specs=pl.BlockSpec((ts, width), lambda i: (i, 0)),
        out_shape=jax.ShapeDtypeStruct((S, width), BF16), name="mem_fwd",
        compiler_params=_cparams(("parallel",)),
    )(proj, kv)


def _mem_bwd(proj, q_off, width, kv, dmm):
    S = proj.shape[0]
    Dh = HEAD_DIM
    heads = width // Dh
    ts = _tile(S, 512, SUBLANES)
    scale = Dh ** -0.5
    M = kv.shape[0]

    def body(q_ref, kv_ref, d_ref, dq_ref, dkv_ref):
        @pl.when(pl.program_id(0) == 0)
        def _():
            dkv_ref[...] = jnp.zeros_like(dkv_ref)

        for h in range(heads):
            qh = q_ref[:, h * Dh:(h + 1) * Dh].astype(BF16)
            kh = kv_ref[:, h * Dh:(h + 1) * Dh].astype(BF16)
            vh = kv_ref[:, width + h * Dh:width + (h + 1) * Dh].astype(BF16)
            dh = d_ref[:, h * Dh:(h + 1) * Dh].astype(BF16)
            p = _mem_probs(qh, kh, scale)
            dp = _dot(dh, vh, NT)
            ds = p * (dp - jnp.sum(dp * p, axis=-1, keepdims=True)) * scale
            dsb = ds.astype(BF16)
            dq_ref[:, h * Dh:(h + 1) * Dh] = _dot(dsb, kh)
            dkv_ref[:, h * Dh:(h + 1) * Dh] += _dot(dsb, qh, TN)
            dkv_ref[:, width + h * Dh:width + (h + 1) * Dh] += _dot(p.astype(BF16), dh, TN)

    row = pl.BlockSpec((ts, width), lambda i: (i, 0))
    full = pl.BlockSpec((M, 2 * width), lambda i: (0, 0))
    return pl.pallas_call(
        body, grid=(S // ts,),
        in_specs=[pl.BlockSpec((ts, width), lambda i: (i, q_off // width)), full, row],
        out_specs=[row, full],
        out_shape=[jax.ShapeDtypeStruct((S, width), F32), jax.ShapeDtypeStruct((M, 2 * width), F32)],
        name="mem_bwd", compiler_params=_cparams(("arbitrary",)),
    )(proj, kv, dmm)


def _disc_math(lre, lim, logdt, bre_t, bim_t):
    dt = jnp.exp(logdt)
    mag = jnp.exp(lre * dt)
    ang = lim * dt
    a = mag * jnp.cos(ang)
    b = mag * jnp.sin(ang)
    den = lre * lre + lim * lim
    nr = a - 1.0
    fre = (nr * lre + b * lim) / den
    fim = (b * lre - nr * lim) / den
    bbre = fre * bre_t - fim * bim_t
    bbim = fre * bim_t + fim * bre_t
    return a, b, bbre, bbim


def _s5_disc(lre, lim, logdt, bre_t, bim_t):
    G, _, P = lre.shape
    C = bre_t.shape[1]

    def body(lre_ref, lim_ref, dt_ref, br_ref, bi_ref, a_ref, b_ref, bbre_ref, bbim_ref):
        a, b, bbre, bbim = _disc_math(lre_ref[...], lim_ref[...], dt_ref[...], br_ref[...], bi_ref[...])
        a_ref[...] = a
        b_ref[...] = b
        bbre_ref[...] = bbre
        bbim_ref[...] = bbim

    gp = jax.ShapeDtypeStruct((G, 1, P), F32)
    gcp = jax.ShapeDtypeStruct((G, C, P), F32)
    return pl.pallas_call(
        body, in_specs=[VMEM] * 5, out_specs=[VMEM] * 4, out_shape=[gp, gp, gcp, gcp], name="s5_disc",
    )(lre, lim, logdt, bre_t, bim_t)


def _s5_disc_bwd(lre, lim, logdt, bre_t, bim_t, da, db, dbbre, dbbim):
    G, _, P = lre.shape
    C = bre_t.shape[1]

    def body(lre_ref, lim_ref, dt_ref, br_ref, bi_ref, da_ref, db_ref, dbr_ref, dbi_ref,
             o_lre, o_lim, o_dt, o_br, o_bi):
        _, vjp = jax.vjp(_disc_math, lre_ref[...], lim_ref[...], dt_ref[...], br_ref[...], bi_ref[...])
        g = vjp((da_ref[...], db_ref[...], dbr_ref[...], dbi_ref[...]))
        o_lre[...] = g[0]
        o_lim[...] = g[1]
        o_dt[...] = g[2]
        o_br[...] = g[3]
        o_bi[...] = g[4]

    gp = jax.ShapeDtypeStruct((G, 1, P), F32)
    gcp = jax.ShapeDtypeStruct((G, C, P), F32)
    return pl.pallas_call(
        body, in_specs=[VMEM] * 9, out_specs=[VMEM] * 5,
        out_shape=[gp, gp, jax.ShapeDtypeStruct((G, 1, 1), F32), gcp, gcp], name="s5_disc_bwd",
    )(lre, lim, logdt, bre_t, bim_t, da, db, dbbre, dbbim)


S5_CHUNK = 256


def _load_once(pairs):
    @pl.when(pl.program_id(0) == 0)
    def _():
        for src, dst in pairs:
            pltpu.sync_copy(src, dst)


def _s5_fwd(proj, u_off, width, a_row, b_row, bmre, bmim, cmre, cmimn, d_row):
    S = proj.shape[0]
    GP = a_row.shape[1]
    T = min(S5_CHUNK, S)

    def body(u_ref, a_ref, b_ref, d_ref, bre_hbm, bim_hbm, cre_hbm, cim_hbm,
             uo_ref, y_ref, gy_ref, hre_ref, him_ref, st_ref, bure_s, buim_s, bre_ref, bim_ref, cre_ref, cim_ref):
        @pl.when(pl.program_id(0) == 0)
        def _():
            st_ref[...] = jnp.zeros_like(st_ref)

        _load_once([(bre_hbm, bre_ref), (bim_hbm, bim_ref), (cre_hbm, cre_ref), (cim_hbm, cim_ref)])
        u = u_ref[...]
        uo_ref[...] = u
        bure_s[...] = _dot_hp(u, bre_ref[...])
        buim_s[...] = _dot_hp(u, bim_ref[...])
        a = a_ref[...]
        b = b_ref[...]

        def step(ii, carry):
            hre, him = carry
            base = pl.multiple_of(ii * SUBLANES, SUBLANES)
            br = bure_s[pl.ds(base, SUBLANES), :]
            bi = buim_s[pl.ds(base, SUBLANES), :]
            rows_re, rows_im = [], []
            for j in range(SUBLANES):
                nre = a * hre - b * him + br[j:j + 1, :]
                nim = a * him + b * hre + bi[j:j + 1, :]
                hre, him = nre, nim
                rows_re.append(nre)
                rows_im.append(nim)
            hre_ref[pl.ds(base, SUBLANES), :] = jnp.concatenate(rows_re, axis=0)
            him_ref[pl.ds(base, SUBLANES), :] = jnp.concatenate(rows_im, axis=0)
            return hre, him

        hre, him = lax.fori_loop(0, T // SUBLANES, step, (st_ref[0:1, :], st_ref[1:2, :]))
        st_ref[0:1, :] = hre
        st_ref[1:2, :] = him
        y = _dot_hp(hre_ref[...], cre_ref[...]) + _dot_hp(him_ref[...], cim_ref[...]) + d_ref[...] * u
        y_ref[...] = y
        gy_ref[...] = jax.nn.gelu(y).astype(gy_ref.dtype)

    c0 = lambda i: (0, 0)
    urow = pl.BlockSpec((T, width), lambda i: (i, u_off // width))
    row = pl.BlockSpec((T, width), lambda i: (i, 0))
    hrow = pl.BlockSpec((T, GP), lambda i: (i, 0))
    sw = jax.ShapeDtypeStruct((S, width), F32)
    sg = jax.ShapeDtypeStruct((S, GP), F32)
    return pl.pallas_call(
        body, grid=(S // T,),
        in_specs=[urow, pl.BlockSpec((1, GP), c0), pl.BlockSpec((1, GP), c0), pl.BlockSpec((1, width), c0),
                  ANY, ANY, ANY, ANY],
        out_specs=[row, row, row, hrow, hrow],
        out_shape=[sw, sw, jax.ShapeDtypeStruct((S, width), BF16), sg, sg],
        scratch_shapes=[pltpu.VMEM((SUBLANES, GP), F32), pltpu.VMEM((T, GP), F32), pltpu.VMEM((T, GP), F32),
                        pltpu.VMEM((width, GP), F32), pltpu.VMEM((width, GP), F32),
                        pltpu.VMEM((GP, width), F32), pltpu.VMEM((GP, width), F32)],
        name="s5_fwd", compiler_params=_cparams(("arbitrary",)),
    )(proj, a_row, b_row, d_row, bmre, bmim, cmre, cmimn)


def _s5_bwd(u, dgy, y, hre, him, a_row, b_row, bmre, bmim, cmre, cmimn, d_row):
    S, width = u.shape
    GP = a_row.shape[1]
    T = min(S5_CHUNK, S)
    nchunk = S // T

    def body(u_ref, dgy_ref, y_ref, hre_ref, him_ref, a_ref, b_ref, d_ref, bre_hbm, bim_hbm, cre_hbm, cim_hbm,
             du_ref, dy_ref, gre_s, gim_s, dd_ref, da_ref, db_ref,
             st_ref, bre_ref, bim_ref, cre_ref, cim_ref):
        @pl.when(pl.program_id(0) == 0)
        def _():
            st_ref[...] = jnp.zeros_like(st_ref)
            for r in (dd_ref, da_ref, db_ref):
                r[...] = jnp.zeros_like(r)

        _load_once([(bre_hbm, bre_ref), (bim_hbm, bim_ref), (cre_hbm, cre_ref), (cim_hbm, cim_ref)])
        u = u_ref[...]
        dy = dgy_ref[...] * _gelu_grad(y_ref[...])
        dy_ref[...] = dy
        gre_s[...] = _dot_hp(dy, cre_ref[...], NT)
        gim_s[...] = _dot_hp(dy, cim_ref[...], NT)
        a = a_ref[...]
        b = b_ref[...]

        def step(ii, carry):
            gre, gim, da, db = carry
            base = pl.multiple_of((T // SUBLANES - 1 - ii) * SUBLANES, SUBLANES)
            dr = gre_s[pl.ds(base, SUBLANES), :]
            di = gim_s[pl.ds(base, SUBLANES), :]
            hr = hre_ref[pl.ds(base, SUBLANES), :]
            hi = him_ref[pl.ds(base, SUBLANES), :]
            rows_re = [None] * SUBLANES
            rows_im = [None] * SUBLANES
            for j in range(SUBLANES - 1, -1, -1):
                hrj = hr[j:j + 1, :]
                hij = hi[j:j + 1, :]
                da = da + gre * hrj + gim * hij
                db = db + gim * hrj - gre * hij
                nre = dr[j:j + 1, :] + a * gre + b * gim
                nim = di[j:j + 1, :] - b * gre + a * gim
                gre, gim = nre, nim
                rows_re[j] = nre
                rows_im[j] = nim
            gre_s[pl.ds(base, SUBLANES), :] = jnp.concatenate(rows_re, axis=0)
            gim_s[pl.ds(base, SUBLANES), :] = jnp.concatenate(rows_im, axis=0)
            return gre, gim, da, db

        zero = jnp.zeros((1, GP), F32)
        gre, gim, da, db = lax.fori_loop(0, T // SUBLANES, step, (st_ref[0:1, :], st_ref[1:2, :], zero, zero))
        st_ref[0:1, :] = gre
        st_ref[1:2, :] = gim
        da_ref[...] += da
        db_ref[...] += db
        du_ref[...] = (_dot_hp(gre_s[...], bre_ref[...], NT) + _dot_hp(gim_s[...], bim_ref[...], NT)
                       + d_ref[...] * dy)
        dd_ref[...] += jnp.sum(dy * u, axis=0, keepdims=True)

    c0 = lambda i: (0, 0)
    rev = lambda i: (nchunk - 1 - i, 0)
    row = pl.BlockSpec((T, width), rev)
    hrow = pl.BlockSpec((T, GP), rev)
    v_gp = pl.BlockSpec((1, GP), c0)
    v_w = pl.BlockSpec((1, width), c0)
    sw = jax.ShapeDtypeStruct((S, width), F32)
    sg = jax.ShapeDtypeStruct((S, GP), F32)
    return pl.pallas_call(
        body, grid=(nchunk,),
        in_specs=[row, row, row, hrow, hrow, v_gp, v_gp, v_w, ANY, ANY, ANY, ANY],
        out_specs=[row, row, hrow, hrow, v_w, v_gp, v_gp],
        out_shape=[sw, sw, sg, sg, jax.ShapeDtypeStruct((1, width), F32),
                   jax.ShapeDtypeStruct((1, GP), F32), jax.ShapeDtypeStruct((1, GP), F32)],
        scratch_shapes=[pltpu.VMEM((SUBLANES, GP), F32),
                        pltpu.VMEM((width, GP), F32), pltpu.VMEM((width, GP), F32),
                        pltpu.VMEM((GP, width), F32), pltpu.VMEM((GP, width), F32)],
        name="s5_bwd", compiler_params=_cparams(("arbitrary",)),
    )(u, dgy, y, hre, him, a_row, b_row, d_row, bmre, bmim, cmre, cmimn)


def _block_diag(x):
    G, A, B = x.shape
    eye = jnp.eye(G, dtype=x.dtype)
    return (eye[:, None, :, None] * x[:, :, None, :]).reshape(G * A, G * B)


def _block_diag_take(m, G):
    A, B = m.shape[0] // G, m.shape[1] // G
    return jnp.einsum("gagb->gab", m.reshape(G, A, G, B))


def _adamw(w, g, m, v):
    shape = w.shape
    C = shape[-1]
    w2, g2, m2, v2 = (t.reshape(-1, C) for t in (w, g, m, v))
    R = w2.shape[0]
    rb = _tile(R, max(SUBLANES, (1 << 19) // C), SUBLANES)
    c1 = 1.0 - ADAM_B1 ** ADAM_STEP
    c2 = 1.0 - ADAM_B2 ** ADAM_STEP

    def body(w_ref, g_ref, m_ref, v_ref, d_ref, nm_ref, nv_ref):
        gv = g_ref[...]
        nm = ADAM_B1 * m_ref[...] + (1.0 - ADAM_B1) * gv
        nv = ADAM_B2 * v_ref[...] + (1.0 - ADAM_B2) * (gv * gv)
        d_ref[...] = -ADAM_LR * ((nm / c1) / (jnp.sqrt(nv / c2) + ADAM_EPS) + ADAM_WD * w_ref[...])
        nm_ref[...] = nm
        nv_ref[...] = nv

    blk = pl.BlockSpec((rb, C), lambda i: (i, 0))
    sd = jax.ShapeDtypeStruct((R, C), F32)
    outs = pl.pallas_call(
        body, grid=(R // rb,), in_specs=[blk] * 4, out_specs=[blk] * 3, out_shape=[sd] * 3,
        name="adamw", compiler_params=_cparams(("parallel",)),
    )(w2, g2, m2, v2)
    return tuple(o.reshape(shape) for o in outs)


def _coords():
    x, y, c = lax.axis_index("x"), lax.axis_index("y"), lax.axis_index("c")
    return x, y, c


def _place_shard(w, l, place):
    _, R, C = w.shape
    rb = _tile(R, max(16, (1 << 19) // C), 16)

    def body(place_ref, w_ref, o_ref):
        o_ref[...] = w_ref[...].astype(BF16)

    grid_spec = pltpu.PrefetchScalarGridSpec(
        num_scalar_prefetch=1, grid=(R // rb,),
        in_specs=[pl.BlockSpec((None, rb, C), lambda i, p: (l, i, 0))],
        out_specs=pl.BlockSpec((None, rb, C), lambda i, p: (p[1], i, 0)))
    return pl.pallas_call(
        body, grid_spec=grid_spec, out_shape=jax.ShapeDtypeStruct((N_CHIPS, R, C), BF16),
        name="place_shard", compiler_params=_cparams(("arbitrary",)),
    )(place, w)


def _ag_hooks(outs, send_sems, recv_sems):
    n = len(outs)

    def rcopy(a, k, block, to):
        cx, cy, cc = block
        hr = outs[a].shape[1] // 2
        blk = outs[a].at[2 * cx + cy, pl.ds(cc * hr, hr)]
        return pltpu.make_async_remote_copy(
            src_ref=blk, dst_ref=blk, send_sem=send_sems.at[a * 6 + k], recv_sem=recv_sems.at[a * 6 + k],
            device_id=to, device_id_type=MESH)

    def places():
        x, y, c = _coords()
        return (x, y, c), (x, y, 1 - c), [(1 - x, y), (x, 1 - y), (1 - x, 1 - y)]

    def start():
        me, _, others = places()
        for a in range(n):
            for j, ch in enumerate(others):
                rcopy(a, j, me, (*ch, me[2])).start()

    def middle():
        me, sibling, others = places()
        for a in range(n):
            for j, ch in enumerate(others):
                rcopy(a, j, (*ch, me[2]), me).wait_recv()
                rcopy(a, 3 + j, (*ch, me[2]), sibling).start()

    def finish():
        me, sibling, others = places()
        for a in range(n):
            for j, ch in enumerate(others):
                rcopy(a, 3 + j, (*ch, sibling[2]), me).wait_recv()
        for a in range(n):
            for j, ch in enumerate(others):
                rcopy(a, j, me, (*ch, me[2])).wait_send()
                rcopy(a, 3 + j, (*ch, me[2]), sibling).wait_send()

    return start, middle, finish


def _ag_weights(bufs):
    n = len(bufs)

    def body(*refs):
        start, middle, finish = _ag_hooks(refs[n:2 * n], *refs[2 * n:])
        start()
        middle()
        finish()

    return pl.pallas_call(
        body, out_shape=[jax.ShapeDtypeStruct(b.shape, b.dtype) for b in bufs],
        in_specs=[ANY] * n, out_specs=[ANY] * n, input_output_aliases={i: i for i in range(n)},
        scratch_shapes=[pltpu.SemaphoreType.DMA((n * 6,)), pltpu.SemaphoreType.DMA((n * 6,))],
        name="ag_weights",
    )(*bufs)


def _rs_pair(grads):
    n = len(grads)
    out_shapes = [jax.ShapeDtypeStruct((N_CHIPS, g.shape[1] // 2, g.shape[2]), F32) for g in grads]

    def body(*refs):
        gs = refs[:n]
        outs = refs[n:2 * n]
        send_sems, recv_sems = refs[2 * n:]
        x, y, c = _coords()
        copies = []
        for i in range(n):
            hr = gs[i].shape[1] // 2
            cp = pltpu.make_async_remote_copy(
                src_ref=gs[i].at[:, pl.ds((1 - c) * hr, hr)], dst_ref=outs[i],
                send_sem=send_sems.at[i], recv_sem=recv_sems.at[i],
                device_id=(x, y, 1 - c), device_id_type=MESH)
            cp.start()
            copies.append(cp)
        for cp in copies:
            cp.wait_recv()
        for cp in copies:
            cp.wait_send()

    return pl.pallas_call(
        body, out_shape=out_shapes, in_specs=[ANY] * n, out_specs=[ANY] * n,
        scratch_shapes=[pltpu.SemaphoreType.DMA((n,)), pltpu.SemaphoreType.DMA((n,))],
        name="rs_pair",
    )(*grads)


def _pair_add(g, r, place):
    _, R, C = g.shape
    hr = R // 2
    rb = _tile(hr, max(16, (1 << 19) // C), 16)
    nb = hr // rb

    def body(place_ref, g_ref, r_ref, p16_ref, own_ref):
        s = g_ref[...] + r_ref[...]
        p16_ref[...] = s.astype(BF16)

        @pl.when(pl.program_id(1) == place_ref[1])
        def _():
            own_ref[...] = s

    grid_spec = pltpu.PrefetchScalarGridSpec(
        num_scalar_prefetch=1, grid=(nb, N_CHIPS),
        in_specs=[pl.BlockSpec((None, rb, C), lambda i, k, p: (k, p[0] * nb + i, 0)),
                  pl.BlockSpec((None, rb, C), lambda i, k, p: (k, i, 0))],
        out_specs=[pl.BlockSpec((None, rb, C), lambda i, k, p: (k, i, 0)),
                   pl.BlockSpec((rb, C), lambda i, k, p: (i, 0))])
    return pl.pallas_call(
        body, grid_spec=grid_spec,
        out_shape=[jax.ShapeDtypeStruct((N_CHIPS, hr, C), BF16), jax.ShapeDtypeStruct((hr, C), F32)],
        name="pair_add", compiler_params=_cparams(("arbitrary", "arbitrary")),
    )(place, g, r)


def _rs_chips_hooks(ps, outs, send_sems, recv_sems):
    def copies():
        x, y, c = _coords()
        cps = []
        for a in range(len(ps)):
            for r in (1, 2, 3):
                kx = 1 - x if (r >> 1) else x
                ky = 1 - y if (r & 1) else y
                cps.append(pltpu.make_async_remote_copy(
                    src_ref=ps[a].at[2 * kx + ky], dst_ref=outs[a].at[r - 1],
                    send_sem=send_sems.at[a * 3 + r - 1], recv_sem=recv_sems.at[a * 3 + r - 1],
                    device_id=(kx, ky, c), device_id_type=MESH))
        return cps

    def start():
        for cp in copies():
            cp.start()

    def finish():
        for cp in copies():
            cp.wait()

    return start, finish


def _rs_chips_shapes(p16):
    return [jax.ShapeDtypeStruct((3,) + p.shape[1:], BF16) for p in p16]


def _rs_chips(p16):
    n = len(p16)

    def body(*refs):
        start, finish = _rs_chips_hooks(refs[:n], refs[n:2 * n], *refs[2 * n:])
        start()
        finish()

    return pl.pallas_call(
        body, out_shape=_rs_chips_shapes(p16), in_specs=[ANY] * n, out_specs=[ANY] * n,
        scratch_shapes=[pltpu.SemaphoreType.DMA((n * 3,)), pltpu.SemaphoreType.DMA((n * 3,))],
        name="rs_chips",
    )(*p16)


def _chip_sum(own, recv, full, l, place):
    hr, C = own.shape
    rb = _tile(hr, max(16, (1 << 19) // C), 16)
    nb = hr // rb

    def body(place_ref, o_ref, r_ref, full_ref, s_ref):
        s = o_ref[...] + r_ref[0].astype(F32)
        s = s + r_ref[1].astype(F32)
        s_ref[...] = s + r_ref[2].astype(F32)

    grid_spec = pltpu.PrefetchScalarGridSpec(
        num_scalar_prefetch=1, grid=(nb,),
        in_specs=[pl.BlockSpec((rb, C), lambda i, p: (i, 0)), pl.BlockSpec((3, rb, C), lambda i, p: (0, i, 0)), ANY],
        out_specs=pl.BlockSpec((None, rb, C), lambda i, p: (l, p[0] * nb + i, 0)))
    return pl.pallas_call(
        body, grid_spec=grid_spec, out_shape=jax.ShapeDtypeStruct(full.shape, F32),
        input_output_aliases={3: 0}, name="chip_sum", compiler_params=_cparams(("arbitrary",)),
    )(place, own, recv, full)


def _share_pair(fulls):
    n = len(fulls)

    def body(*refs):
        outs = refs[n:2 * n]
        send_sems, recv_sems = refs[2 * n:]
        x, y, c = _coords()
        copies = []
        for a in range(n):
            hr = outs[a].shape[1] // 2
            mine = outs[a].at[:, pl.ds(c * hr, hr)]
            cp = pltpu.make_async_remote_copy(
                src_ref=mine, dst_ref=mine, send_sem=send_sems.at[a], recv_sem=recv_sems.at[a],
                device_id=(x, y, 1 - c), device_id_type=MESH)
            cp.start()
            copies.append(cp)
        for cp in copies:
            cp.wait_recv()
        for cp in copies:
            cp.wait_send()

    return pl.pallas_call(
        body, out_shape=[jax.ShapeDtypeStruct(f.shape, f.dtype) for f in fulls],
        in_specs=[ANY] * n, out_specs=[ANY] * n, input_output_aliases={i: i for i in range(n)},
        scratch_shapes=[pltpu.SemaphoreType.DMA((n,)), pltpu.SemaphoreType.DMA((n,))],
        name="share_pair",
    )(*fulls)


def _small_allreduce(packed):
    m_per, ncol = packed.shape

    def body(x_ref, out_ref, tot_ref, send_sems, recv_sems, local_sem):
        x, y, c = _coords()
        me, sibling = (x, y, c), (x, y, 1 - c)
        chips = [(1 - x, y), (x, 1 - y), (1 - x, 1 - y)]

        def rows(px, py, pc):
            return out_ref.at[pl.ds((4 * px + 2 * py + pc) * m_per, m_per), :]

        def copy(k, block, to, src=None):
            return pltpu.make_async_remote_copy(
                src_ref=rows(*block) if src is None else src, dst_ref=rows(*block),
                send_sem=send_sems.at[k], recv_sem=recv_sems.at[k], device_id=to, device_id_type=MESH)

        mine = pltpu.make_async_copy(x_ref, rows(*me), local_sem)
        mine.start()
        first = [copy(0, me, sibling, src=x_ref)]
        first += [copy(1 + j, me, (*chip, c), src=x_ref) for j, chip in enumerate(chips)]
        for cp in first:
            cp.start()
        passed = [copy(4 + j, (*chip, c), sibling) for j, chip in enumerate(chips)]
        for j, chip in enumerate(chips):
            copy(1 + j, (*chip, c), me).wait_recv()
            passed[j].start()
        copy(0, sibling, me).wait_recv()
        for j, chip in enumerate(chips):
            copy(4 + j, (*chip, 1 - c), me).wait_recv()
        for cp in first + passed:
            cp.wait_send()
        mine.wait()
        tot = out_ref[pl.ds(0, m_per), :]
        for d in range(1, N_DEV):
            tot = tot + out_ref[pl.ds(d * m_per, m_per), :]
        tot_ref[...] = tot

    _, tot = pl.pallas_call(
        body,
        out_shape=[jax.ShapeDtypeStruct((N_DEV * m_per, ncol), F32), jax.ShapeDtypeStruct((m_per, ncol), F32)],
        in_specs=[VMEM], out_specs=[VMEM, VMEM],
        scratch_shapes=[pltpu.SemaphoreType.DMA((7,)), pltpu.SemaphoreType.DMA((7,)), pltpu.SemaphoreType.DMA],
        name="small_allreduce",
        compiler_params=pltpu.CompilerParams(vmem_limit_bytes=VMEM_LIMIT_MB * 1024 * 1024),
    )(packed)
    return tot


BIG = ["w_in", "sb_w_out", "ssm_w_glu", "ssm_w_out", "mem_w_kv", "mem_w_out", "w_o", "ffn_w_gate_up", "ffn_w_down"]
KIND = {"w_in": "col", "sb_w_out": "col", "ssm_w_glu": "col", "ssm_w_out": "col", "mem_w_kv": "row",
        "mem_w_out": "col", "w_o": "row", "ffn_w_gate_up": "col", "ffn_w_down": "row"}
SMALL = ["b_in", "ssm_lambda_re", "ssm_lambda_im", "ssm_log_dt", "ssm_b_re", "ssm_b_im", "ssm_c_re", "ssm_c_im",
         "ssm_d", "ln1_g", "ln1_b", "ln2_g", "ln2_b"]
WEIGHTS = ["w_in", "b_in", "sb_w_out", "ssm_lambda_re", "ssm_lambda_im", "ssm_log_dt", "ssm_b_re", "ssm_b_im",
           "ssm_c_re", "ssm_c_im", "ssm_d", "ssm_w_glu", "ssm_w_out", "mem_w_kv", "mem_w_out", "w_o", "ln1_g",
           "ln1_b", "ffn_w_gate_up", "ffn_w_down", "ln2_g", "ln2_b"]


def _pack(arrs):
    flat = jnp.concatenate([a.reshape(-1).astype(F32) for a in arrs])
    n = flat.shape[0]
    rows = -(-n // LANES)
    rows = -(-rows // SUBLANES) * SUBLANES
    return jnp.pad(flat, (0, rows * LANES - n)).reshape(rows, LANES)


def _unpack(packed, like):
    flat = packed.reshape(-1)
    out, off = [], 0
    for a in like:
        out.append(flat[off:off + a.size].reshape(a.shape))
        off += a.size
    return out


def _step(x, mem, target, W, M1, V1):
    S, D = x.shape[1], x.shape[2]
    L = W["w_in"].shape[0]
    x0 = x.reshape(S, D)
    mem2 = mem.reshape(mem.shape[1], D)
    tgt = target.reshape(S, D)
    alpha = (2 * L) ** 0.25
    sbw = W["sb_w_out"].shape[1]
    ssw = W["ssm_d"].shape[1]
    mw = W["mem_w_out"].shape[1]
    heads = sbw // HEAD_DIM
    G, P = W["ssm_lambda_re"].shape[1], W["ssm_lambda_re"].shape[2]
    q_off, k_off, v_off = 0, sbw, 2 * sbw
    u_off = 3 * sbw
    qm_off = u_off + ssw
    gate_off = qm_off + mw

    x_i, y_i, c_i = _coords()
    place = jnp.stack([c_i, 2 * x_i + y_i]).astype(jnp.int32)
    placed = [[_place_shard(W[n], l, place) for n in BIG] for l in range(L)]
    Wg = {n: [None] * L for n in BIG}
    for n, buf in zip(BIG, _ag_weights(placed[0])):
        Wg[n][0] = buf

    saved = []
    xl = x0
    xlb = x0.astype(BF16)
    for l in range(L):
        sv = {"x": xlb}
        proj = _mm_nn(xlb, Wg["w_in"][l], "col", bias=W["b_in"][l][None, :], name="mm_proj")
        sb, sb_ctot, nxt = _sb_fwd(proj, q_off, k_off, v_off, heads, ag=placed[l + 1] if l + 1 < L else None)
        for n, buf in zip(BIG, nxt):
            Wg[n][l + 1] = buf
        p_sb = _mm_nn(sb, Wg["sb_w_out"][l], "col", name="mm_sb_out")

        bre_t = W["ssm_b_re"][l].transpose(0, 2, 1)
        bim_t = W["ssm_b_im"][l].transpose(0, 2, 1)
        logdt = W["ssm_log_dt"][l][:, None, None]
        lre3 = W["ssm_lambda_re"][l][:, None, :]
        lim3 = W["ssm_lambda_im"][l][:, None, :]
        a_gp, b_gp, bbre, bbim = _s5_disc(lre3, lim3, logdt, bre_t, bim_t)
        a_row, b_row = a_gp.reshape(1, G * P), b_gp.reshape(1, G * P)
        bmre, bmim = _block_diag(bbre), _block_diag(bbim)
        cmre = _block_diag(W["ssm_c_re"][l].transpose(0, 2, 1))
        cmimn = _block_diag(-W["ssm_c_im"][l].transpose(0, 2, 1))
        d_row = W["ssm_d"][l][None, :]
        u_ssm, y, gy, hre, him = _s5_fwd(proj, u_off, ssw, a_row, b_row, bmre, bmim, cmre, cmimn, d_row)
        glu = _mm_nn(gy, Wg["ssm_w_glu"][l], "col", name="mm_glu")
        zz = _glu_fwd(glu)
        p_ssm = _mm_nn(zz, Wg["ssm_w_out"][l], "col", name="mm_ssm_out")

        kv = _mm_nn(mem2, Wg["mem_w_kv"][l], "row", name="mm_kv")
        mm_o = _mem_fwd(proj, qm_off, mw, kv)
        p_mem = _mm_nn(mm_o, Wg["mem_w_out"][l], "col", name="mm_mem_out")

        merged = _merge_fwd(proj, gate_off, p_sb, p_ssm, p_mem)
        mix = _mm_nn(merged, Wg["w_o"][l], "row", name="mm_wo")
        x1, x1b, xh1, rs1 = _ln_fwd(xl, mix, W["ln1_g"][l][None, :], W["ln1_b"][l][None, :], alpha)
        gu = _mm_nn(x1b, Wg["ffn_w_gate_up"][l], "col", name="mm_gate_up")
        hid = _swiglu_fwd(gu)
        ffn = _mm_nn(hid, Wg["ffn_w_down"][l], "row", name="mm_down")
        x2, x2b, xh2, rs2 = _ln_fwd(x1, ffn, W["ln2_g"][l][None, :], W["ln2_b"][l][None, :], alpha)
        sv.update(proj=proj, sb=sb, sb_ctot=sb_ctot, p_sb=p_sb, y=y, gy=gy, hre=hre, him=him, glu=glu, zz=zz,
                  p_ssm=p_ssm, kv=kv, mm_o=mm_o, p_mem=p_mem, merged=merged, x1=x1b, xh1=xh1, rs1=rs1, gu=gu,
                  hid=hid, xh2=xh2, rs2=rs2, u=u_ssm,
                  disc=(lre3, lim3, logdt, bre_t, bim_t, a_row, b_row, bmre, bmim, cmre, cmimn, d_row))
        saved.append(sv)
        xl, xlb = x2, x2b

    dxl, loss_part = _loss_head(xl, tgt)

    gbig = {n: [None] * L for n in BIG}
    gsmall = {n: [None] * L for n in SMALL}
    own = [None] * L
    from_chips = [None] * L
    pending = None
    for l in range(L - 1, -1, -1):
        sv = saved[l]
        proj = sv["proj"]
        dr2, dr2b, dg2, db2 = _ln_bwd(dxl, sv["xh2"], sv["rs2"], W["ln2_g"][l][None, :])
        gsmall["ln2_g"][l], gsmall["ln2_b"][l] = dg2[0], db2[0]
        dhid = _mm_nt(dr2b, Wg["ffn_w_down"][l], "row", name="mm_d_hid")
        gbig["ffn_w_down"][l] = _mm_tn(sv["hid"], dr2b, "row", name="mm_g_down")
        dgu = _swiglu_bwd(dhid, sv["gu"])
        dx1 = _mm_nt(dgu, Wg["ffn_w_gate_up"][l], "col", add=dr2, add_scale=alpha, name="mm_d_x1")
        gbig["ffn_w_gate_up"][l] = _mm_tn(sv["x1"], dgu, "col", name="mm_g_gate_up")

        dr1, dr1b, dg1, db1 = _ln_bwd(dx1, sv["xh1"], sv["rs1"], W["ln1_g"][l][None, :])
        gsmall["ln1_g"][l], gsmall["ln1_b"][l] = dg1[0], db1[0]
        dmerged = _mm_nt(dr1b, Wg["w_o"][l], "row", name="mm_d_merged")
        gbig["w_o"][l] = _mm_tn(sv["merged"], dr1b, "row", name="mm_g_wo")
        dp_sb, dp_ssm, dp_mem, dgl0, dgl1, dgl2 = _merge_bwd(
            dmerged, proj, gate_off, sv["p_sb"], sv["p_ssm"], sv["p_mem"])

        dsb = _mm_nt(dp_sb, Wg["sb_w_out"][l], "col", out_dtype=BF16, name="mm_d_sb")
        gbig["sb_w_out"][l] = _mm_tn(sv["sb"], dp_sb, "col", name="mm_g_sb_out")

        dzz = _mm_nt(dp_ssm, Wg["ssm_w_out"][l], "col", name="mm_d_zz")
        gbig["ssm_w_out"][l] = _mm_tn(sv["zz"], dp_ssm, "col", name="mm_g_ssm_out")
        dglu = _glu_bwd(dzz, sv["glu"])
        dgy = _mm_nt(dglu, Wg["ssm_w_glu"][l], "col", name="mm_d_gy")
        gbig["ssm_w_glu"][l] = _mm_tn(sv["gy"], dglu, "col", name="mm_g_glu")
        lre3, lim3, logdt, bre_t, bim_t, a_row, b_row, bmre, bmim, cmre, cmimn, d_row = sv["disc"]
        du, dy_ssm, g_re, g_im, dd, da, db = _s5_bwd(
            sv["u"], dgy, sv["y"], sv["hre"], sv["him"], a_row, b_row, bmre, bmim, cmre, cmimn, d_row)
        dbmre = _mm_tn(sv["u"], g_re, "plain", name="mm_g_ssm_bre")
        dbmim = _mm_tn(sv["u"], g_im, "plain", name="mm_g_ssm_bim")
        dcmre = _mm_tn(sv["hre"], dy_ssm, "plain", name="mm_g_ssm_cre")
        dcmimn = _mm_tn(sv["him"], dy_ssm, "plain", name="mm_g_ssm_cim")
        dlre, dlim, dlogdt, dbre_t, dbim_t = _s5_disc_bwd(
            lre3, lim3, logdt, bre_t, bim_t, da.reshape(G, 1, P), db.reshape(G, 1, P),
            _block_diag_take(dbmre, G), _block_diag_take(dbmim, G))
        gsmall["ssm_lambda_re"][l], gsmall["ssm_lambda_im"][l] = dlre.reshape(G, P), dlim.reshape(G, P)
        gsmall["ssm_log_dt"][l] = dlogdt.reshape(G)
        gsmall["ssm_b_re"][l] = dbre_t.transpose(0, 2, 1)
        gsmall["ssm_b_im"][l] = dbim_t.transpose(0, 2, 1)
        gsmall["ssm_c_re"][l] = _block_diag_take(dcmre, G).transpose(0, 2, 1)
        gsmall["ssm_c_im"][l] = -_block_diag_take(dcmimn, G).transpose(0, 2, 1)
        gsmall["ssm_d"][l] = dd[0]

        dmm = _mm_nt(dp_mem, Wg["mem_w_out"][l], "col", out_dtype=BF16, name="mm_d_mm")
        gbig["mem_w_out"][l] = _mm_tn(sv["mm_o"], dp_mem, "col", name="mm_g_mem_out")
        dqm, dkv = _mem_bwd(proj, qm_off, mw, sv["kv"], dmm)
        gbig["mem_w_kv"][l] = _mm_tn(mem2, dkv, "row", name="mm_g_kv")

        dq, dk, dv, arrived = _sb_bwd(proj, q_off, k_off, v_off, heads, dsb, sv["sb_ctot"], rs=pending)
        if pending is not None:
            from_chips[l + 1] = arrived
        dproj, dbin = _assemble_dproj([dq, dk, dv, du, dqm, dgl0, dgl1, dgl2])
        gsmall["b_in"][l] = dbin[0]
        dxl = _mm_nt(dproj, Wg["w_in"][l], "col", add=dr1, add_scale=alpha, name="mm_d_x")
        gbig["w_in"][l] = _mm_tn(sv["x"], dproj, "col", name="mm_g_win")

        layer_g = [gbig[n][l] for n in BIG]
        sums = [_pair_add(g, r, place) for g, r in zip(layer_g, _rs_pair(layer_g))]
        pending = [s[0] for s in sums]
        own[l] = [s[1] for s in sums]

    from_chips[0] = _rs_chips(pending)
    grad_x = dxl.reshape(x.shape)

    fulls = []
    for a, n in enumerate(BIG):
        full = lax.empty(W[n].shape, F32)
        for l in range(L - 1, -1, -1):
            full = _chip_sum(own[l][a], from_chips[l][a], full, l, place)
        fulls.append(full)
    reduced = _share_pair(fulls)
    grads = {n: reduced[i] for i, n in enumerate(BIG)}

    small_local = [jnp.stack(gsmall[n]) for n in SMALL]
    packed = _pack(small_local + [loss_part[0, :1]])
    total = _small_allreduce(packed)
    unpacked = _unpack(total, small_local + [loss_part[0, :1]])
    for n, g in zip(SMALL, unpacked[:-1]):
        grads[n] = g
    loss = unpacked[-1][0]

    delta, new_m, new_v = {}, {}, {}
    for n in BIG:
        delta[n], new_m[n], new_v[n] = _adamw(W[n], grads[n], M1[n], V1[n])
    sm = _adamw(_pack([W[n] for n in SMALL]), _pack([grads[n] for n in SMALL]),
                _pack([M1[n] for n in SMALL]), _pack([V1[n] for n in SMALL]))
    like = [W[n] for n in SMALL]
    for n, d, m_, v_ in zip(SMALL, _unpack(sm[0], like), _unpack(sm[1], like), _unpack(sm[2], like)):
        delta[n], new_m[n], new_v[n] = d, m_, v_

    return (loss, grad_x, *[grads[n] for n in WEIGHTS], *[delta[n] for n in WEIGHTS],
            *[new_m[n] for n in WEIGHTS], *[new_v[n] for n in WEIGHTS])


def kernel(x, mem, w_in, b_in, sb_w_out, ssm_lambda_re, ssm_lambda_im, ssm_log_dt, ssm_b_re, ssm_b_im, ssm_c_re, ssm_c_im, ssm_d, ssm_w_glu, ssm_w_out, mem_w_kv, mem_w_out, w_o, ln1_g, ln1_b, ffn_w_gate_up, ffn_w_down, ln2_g, ln2_b, loss_target, m_w_in, m_b_in, m_sb_w_out, m_ssm_lambda_re, m_ssm_lambda_im, m_ssm_log_dt, m_ssm_b_re, m_ssm_b_im, m_ssm_c_re, m_ssm_c_im, m_ssm_d, m_ssm_w_glu, m_ssm_w_out, m_mem_w_kv, m_mem_w_out, m_w_o, m_ln1_g, m_ln1_b, m_ffn_w_gate_up, m_ffn_w_down, m_ln2_g, m_ln2_b, v_w_in, v_b_in, v_sb_w_out, v_ssm_lambda_re, v_ssm_lambda_im, v_ssm_log_dt, v_ssm_b_re, v_ssm_b_im, v_ssm_c_re, v_ssm_c_im, v_ssm_d, v_ssm_w_glu, v_ssm_w_out, v_mem_w_kv, v_mem_w_out, v_w_o, v_ln1_g, v_ln1_b, v_ffn_w_gate_up, v_ffn_w_down, v_ln2_g, v_ln2_b):
    W = dict(w_in=w_in, b_in=b_in, sb_w_out=sb_w_out, ssm_lambda_re=ssm_lambda_re, ssm_lambda_im=ssm_lambda_im,
             ssm_log_dt=ssm_log_dt, ssm_b_re=ssm_b_re, ssm_b_im=ssm_b_im, ssm_c_re=ssm_c_re, ssm_c_im=ssm_c_im,
             ssm_d=ssm_d, ssm_w_glu=ssm_w_glu, ssm_w_out=ssm_w_out, mem_w_kv=mem_w_kv, mem_w_out=mem_w_out,
             w_o=w_o, ln1_g=ln1_g, ln1_b=ln1_b, ffn_w_gate_up=ffn_w_gate_up, ffn_w_down=ffn_w_down,
             ln2_g=ln2_g, ln2_b=ln2_b)
    M1 = dict(w_in=m_w_in, b_in=m_b_in, sb_w_out=m_sb_w_out, ssm_lambda_re=m_ssm_lambda_re,
              ssm_lambda_im=m_ssm_lambda_im, ssm_log_dt=m_ssm_log_dt, ssm_b_re=m_ssm_b_re, ssm_b_im=m_ssm_b_im,
              ssm_c_re=m_ssm_c_re, ssm_c_im=m_ssm_c_im, ssm_d=m_ssm_d, ssm_w_glu=m_ssm_w_glu,
              ssm_w_out=m_ssm_w_out, mem_w_kv=m_mem_w_kv, mem_w_out=m_mem_w_out, w_o=m_w_o, ln1_g=m_ln1_g,
              ln1_b=m_ln1_b, ffn_w_gate_up=m_ffn_w_gate_up, ffn_w_down=m_ffn_w_down, ln2_g=m_ln2_g, ln2_b=m_ln2_b)
    V1 = dict(w_in=v_w_in, b_in=v_b_in, sb_w_out=v_sb_w_out, ssm_lambda_re=v_ssm_lambda_re,
              ssm_lambda_im=v_ssm_lambda_im, ssm_log_dt=v_ssm_log_dt, ssm_b_re=v_ssm_b_re, ssm_b_im=v_ssm_b_im,
              ssm_c_re=v_ssm_c_re, ssm_c_im=v_ssm_c_im, ssm_d=v_ssm_d, ssm_w_glu=v_ssm_w_glu,
              ssm_w_out=v_ssm_w_out, mem_w_kv=v_mem_w_kv, mem_w_out=v_mem_w_out, w_o=v_w_o, ln1_g=v_ln1_g,
              ln1_b=v_ln1_b, ffn_w_gate_up=v_ffn_w_gate_up, ffn_w_down=v_ffn_w_down, ln2_g=v_ln2_g, ln2_b=v_ln2_b)
    return _step(x, mem, loss_target, W, M1, V1)
```

```python
import functools
import math

import jax
import jax.numpy as jnp
from jax import lax
from jax.experimental import pallas as pl
from jax.experimental.pallas import tpu as pltpu

F32 = jnp.float32
BF16 = jnp.bfloat16
MESH = pl.DeviceIdType.MESH
ANY = pl.BlockSpec(memory_space=pl.ANY)
VMEM = pl.BlockSpec(memory_space=pltpu.VMEM)

HEAD_DIM = 128
SSM_GROUP = 16
N_CHIPS = 4
N_DEV = 8
LN_EPS = 1e-5
ADAM_LR = 0.001
ADAM_B1 = 0.9
ADAM_B2 = 0.999
ADAM_EPS = 1e-08
ADAM_WD = 0.01
ADAM_STEP = 10
LANES = 128
SUBLANES = 8
VMEM_LIMIT_MB = 56


def _cparams(sem, mb=VMEM_LIMIT_MB):
    return pltpu.CompilerParams(dimension_semantics=sem, vmem_limit_bytes=mb * 1024 * 1024)


def _tile(n, pref, mult=LANES):
    best = None
    t = mult
    while t <= min(n, pref):
        if n % t == 0:
            best = t
        t += mult
    return n if best is None else best


def _dot(a, b, dims=(((1,), (0,)), ((), ()))):
    return lax.dot_general(a, b, dims, preferred_element_type=F32)


NT = (((1,), (1,)), ((), ()))
TN = (((0,), (0,)), ((), ()))


def _split2(x):
    hi = x.astype(BF16)
    lo = (x - hi.astype(F32)).astype(BF16)
    return hi, lo


def _dot_a2(a, b, dims=(((1,), (0,)), ((), ()))):
    ah, al = _split2(a)
    bb = b.astype(BF16)
    return _dot(ah, bb, dims) + _dot(al, bb, dims)


def _bd_blocks(n_in, n_out):
    if (n_in // 2) % LANES == 0 and (n_out // 2) % LANES == 0:
        return [(slice(0, n_in // 2), slice(0, n_out // 2)), (slice(n_in // 2, n_in), slice(n_out // 2, n_out))]
    return [(slice(0, n_in), slice(0, n_out))]


def _dot_mask(x, u2):
    hi, lo = _split2(x)
    return _dot(jnp.concatenate([hi, lo], axis=1), u2)


def _tri2(n, rel):
    ri = lax.broadcasted_iota(jnp.int32, (2 * n, n), 0)
    ci = lax.broadcasted_iota(jnp.int32, (2 * n, n), 1)
    ri = jnp.where(ri >= n, ri - n, ri)
    return rel(ri, ci).astype(BF16)


def _mm_call(a, b, *, dims, grid, a_spec, b_spec, out_spec, out_shape, name,
             bias=None, bias_spec=None, add=None, add_spec=None, add_scale=1.0, comm=None):
    nk = grid[2]
    has_bias = bias is not None
    has_add = add is not None
    n_ci = len(comm["ins"]) if comm else 0
    n_co = len(comm["out_shapes"]) if comm else 0

    def body(*refs):
        a_ref, b_ref = refs[0], refs[1]
        pos = 2
        bias_ref = refs[pos] if has_bias else None
        pos += int(has_bias)
        add_ref = refs[pos] if has_add else None
        pos += int(has_add)
        comm_in = refs[pos:pos + n_ci]
        pos += n_ci
        o_ref = refs[pos]
        comm_out = refs[pos + 1:pos + 1 + n_co]
        pos += 1 + n_co
        acc_ref = refs[pos] if nk > 1 else None
        pos += int(nk > 1)
        if comm:
            ex_start, ex_middle, ex_finish = comm["hooks"](comm_in, comm_out, refs[pos], refs[pos + 1])
            ids = [pl.program_id(d) for d in range(3)]

            @pl.when((ids[0] == 0) & (ids[1] == 0) & (ids[2] == 0))
            def _():
                ex_start()

        p = _dot(a_ref[...].astype(BF16), b_ref[...].astype(BF16), dims)

        def finish(acc):
            if has_bias:
                acc = acc + bias_ref[...]
            if has_add:
                acc = acc + add_scale * add_ref[...].astype(F32)
            o_ref[...] = acc.astype(o_ref.dtype)

        if nk == 1:
            finish(p)
        else:
            k = pl.program_id(2)

            @pl.when(k == 0)
            def _():
                acc_ref[...] = p

            @pl.when(k > 0)
            def _():
                acc_ref[...] += p

            @pl.when(k == nk - 1)
            def _():
                finish(acc_ref[...])

        if comm:
            @pl.when((ids[0] == grid[0] - 1) & (ids[1] == grid[1] - 1) & (ids[2] == grid[2] - 1))
            def _():
                if ex_middle is not None:
                    ex_middle()
                ex_finish()

    ins = [a, b]
    in_specs = [a_spec, b_spec]
    if has_bias:
        ins.append(bias)
        in_specs.append(bias_spec)
    if has_add:
        ins.append(add)
        in_specs.append(add_spec)
    scratch = []
    if nk > 1:
        blk = [d for d in out_spec.block_shape if d is not None]
        scratch.append(pltpu.VMEM(tuple(blk), F32))
    if not comm:
        return pl.pallas_call(
            body, out_shape=out_shape, grid=grid, in_specs=in_specs, out_specs=out_spec,
            scratch_shapes=scratch, name=name,
            compiler_params=_cparams(("parallel", "parallel", "arbitrary")),
        )(*ins)
    n_main = len(ins)
    scratch += [pltpu.SemaphoreType.DMA((comm["n_sems"],)), pltpu.SemaphoreType.DMA((comm["n_sems"],))]
    res = pl.pallas_call(
        body, out_shape=[out_shape] + list(comm["out_shapes"]), grid=grid,
        in_specs=in_specs + [ANY] * n_ci, out_specs=[out_spec] + [ANY] * n_co, scratch_shapes=scratch,
        input_output_aliases={n_main + i: 1 + i for i in range(n_ci)} if comm["alias"] else {},
        name=name, compiler_params=_cparams(("arbitrary", "arbitrary", "arbitrary")),
    )(*ins, *comm["ins"])
    return res[0], list(res[1:])


def _mm_nn(a, w, kind, *, bias=None, out_dtype=F32, name, comm=None):
    M, K = a.shape
    tm = _tile(M, 1024, SUBLANES)
    tk = K if K <= 2048 else _tile(K, 1408)
    if kind == "col":
        Nc = w.shape[2]
        N = N_CHIPS * Nc
        tn = _tile(Nc, 1408)
        npc = Nc // tn
        b_spec = pl.BlockSpec((None, tk, tn), lambda i, j, k: (j // npc, k, j % npc))
    else:
        w = w.reshape(-1, w.shape[-1])
        N = w.shape[1]
        tn = _tile(N, 2048)
        b_spec = pl.BlockSpec((tk, tn), lambda i, j, k: (k, j))
    grid = (M // tm, N // tn, K // tk)
    bias_spec = pl.BlockSpec((1, tn), lambda i, j, k: (0, j)) if bias is not None else None
    return _mm_call(
        a, w, dims=(((1,), (0,)), ((), ())), grid=grid,
        a_spec=pl.BlockSpec((tm, tk), lambda i, j, k: (i, k)), b_spec=b_spec,
        out_spec=pl.BlockSpec((tm, tn), lambda i, j, k: (i, j)),
        out_shape=jax.ShapeDtypeStruct((M, N), out_dtype), name=name,
        bias=bias, bias_spec=bias_spec, comm=comm)


def _mm_nt(dy, w, kind, *, add=None, add_scale=1.0, out_dtype=F32, name, comm=None):
    M, N = dy.shape
    tm = _tile(M, 512, SUBLANES)
    if kind == "col":
        K, Nc = w.shape[1], w.shape[2]
        tn = _tile(Nc, 1408)
        npc = Nc // tn
        tko = _tile(K, 2048)
        b_spec = pl.BlockSpec((None, tko, tn), lambda i, j, r: (r // npc, j, r % npc))
    else:
        w = w.reshape(-1, w.shape[-1])
        K = w.shape[0]
        tn = _tile(N, 2048)
        tko = _tile(K, 2048)
        b_spec = pl.BlockSpec((tko, tn), lambda i, j, r: (j, r))
    grid = (M // tm, K // tko, N // tn)
    add_spec = pl.BlockSpec((tm, tko), lambda i, j, r: (i, j)) if add is not None else None
    return _mm_call(
        dy, w, dims=NT, grid=grid,
        a_spec=pl.BlockSpec((tm, tn), lambda i, j, r: (i, r)), b_spec=b_spec,
        out_spec=pl.BlockSpec((tm, tko), lambda i, j, r: (i, j)),
        out_shape=jax.ShapeDtypeStruct((M, K), out_dtype), name=name,
        add=add, add_spec=add_spec, add_scale=add_scale, comm=comm)


def _mm_tn(a, dy, kind, *, name):
    M, K = a.shape
    N = dy.shape[1]
    tm = _tile(M, 2048, SUBLANES)
    tkw = _tile(K, 1408)
    if kind == "col":
        Nc = N // N_CHIPS
        tn = _tile(Nc, 1408)
        npc = Nc // tn
        out_spec = pl.BlockSpec((None, tkw, tn), lambda i, j, m: (j // npc, i, j % npc))
        out_shape = jax.ShapeDtypeStruct((N_CHIPS, K, Nc), F32)
    else:
        tn = _tile(N, 1024)
        out_spec = pl.BlockSpec((tkw, tn), lambda i, j, m: (i, j))
        out_shape = jax.ShapeDtypeStruct((K, N), F32)
    grid = (K // tkw, N // tn, M // tm)
    out = _mm_call(
        a, dy, dims=TN, grid=grid,
        a_spec=pl.BlockSpec((tm, tkw), lambda i, j, m: (m, i)),
        b_spec=pl.BlockSpec((tm, tn), lambda i, j, m: (m, j)),
        out_spec=out_spec, out_shape=out_shape, name=name)
    if kind == "row":
        out = out.reshape(N_CHIPS, K // N_CHIPS, N)
    return out


def _gelu_grad(x):
    k = math.sqrt(2.0 / math.pi)
    inner = k * (x + 0.044715 * x * x * x)
    t = jnp.tanh(inner)
    return 0.5 * (1.0 + t) + 0.5 * x * (1.0 - t * t) * k * (1.0 + 3.0 * 0.044715 * x * x)


def _ln_fwd(xin, delta, g, b, alpha):
    S, D = xin.shape
    ts = _tile(S, 256, SUBLANES)

    def body(x_ref, d_ref, g_ref, b_ref, y_ref, yb_ref, xh_ref, rs_ref):
        r = alpha * x_ref[...] + d_ref[...]
        mu = jnp.mean(r, axis=-1, keepdims=True)
        rc = r - mu
        var = jnp.mean(rc * rc, axis=-1, keepdims=True)
        rstd = lax.rsqrt(var + LN_EPS)
        xh = rc * rstd
        y = xh * g_ref[...] + b_ref[...]
        y_ref[...] = y
        yb_ref[...] = y.astype(BF16)
        xh_ref[...] = xh
        rs_ref[...] = rstd

    row = pl.BlockSpec((ts, D), lambda i: (i, 0))
    vec = pl.BlockSpec((1, D), lambda i: (0, 0))
    return pl.pallas_call(
        body, grid=(S // ts,), in_specs=[row, row, vec, vec],
        out_specs=[row, row, row, pl.BlockSpec((ts, 1), lambda i: (i, 0))],
        out_shape=[jax.ShapeDtypeStruct((S, D), F32), jax.ShapeDtypeStruct((S, D), BF16),
                   jax.ShapeDtypeStruct((S, D), F32), jax.ShapeDtypeStruct((S, 1), F32)],
        name="ln_fwd", compiler_params=_cparams(("parallel",)),
    )(xin, delta, g, b)


def _ln_bwd(dy, xh, rstd, g):
    S, D = dy.shape
    ts = _tile(S, 256, SUBLANES)

    def body(dy_ref, xh_ref, rs_ref, g_ref, dr_ref, drb_ref, dg_ref, db_ref):
        @pl.when(pl.program_id(0) == 0)
        def _():
            dg_ref[...] = jnp.zeros_like(dg_ref)
            db_ref[...] = jnp.zeros_like(db_ref)

        dyv = dy_ref[...]
        xhv = xh_ref[...]
        dyg = dyv * g_ref[...]
        m1 = jnp.mean(dyg, axis=-1, keepdims=True)
        m2 = jnp.mean(dyg * xhv, axis=-1, keepdims=True)
        dr = rs_ref[...] * (dyg - m1 - xhv * m2)
        dr_ref[...] = dr
        drb_ref[...] = dr.astype(BF16)
        dg_ref[...] += jnp.sum(dyv * xhv, axis=0, keepdims=True)
        db_ref[...] += jnp.sum(dyv, axis=0, keepdims=True)

    row = pl.BlockSpec((ts, D), lambda i: (i, 0))
    vec = pl.BlockSpec((1, D), lambda i: (0, 0))
    return pl.pallas_call(
        body, grid=(S // ts,),
        in_specs=[row, row, pl.BlockSpec((ts, 1), lambda i: (i, 0)), vec],
        out_specs=[row, row, vec, vec],
        out_shape=[jax.ShapeDtypeStruct((S, D), F32), jax.ShapeDtypeStruct((S, D), BF16),
                   jax.ShapeDtypeStruct((1, D), F32), jax.ShapeDtypeStruct((1, D), F32)],
        name="ln_bwd", compiler_params=_cparams(("arbitrary",)),
    )(dy, xh, rstd, g)


def _merge_fwd(proj, gate_off, p_sb, p_ssm, p_mem):
    S, D = p_sb.shape
    ts = _tile(S, 256, SUBLANES)
    gb = gate_off // D

    def body(g0, g1, g2, a0, a1, a2, o_ref):
        o_ref[...] = (jax.nn.sigmoid(g0[...]) * a0[...] + jax.nn.sigmoid(g1[...]) * a1[...]
                      + jax.nn.sigmoid(g2[...]) * a2[...]).astype(o_ref.dtype)

    row = pl.BlockSpec((ts, D), lambda i: (i, 0))
    gates = [pl.BlockSpec((ts, D), functools.partial(lambda i, n: (i, gb + n), n=n)) for n in range(3)]
    return pl.pallas_call(
        body, grid=(S // ts,), in_specs=gates + [row, row, row], out_specs=row,
        out_shape=jax.ShapeDtypeStruct((S, D), BF16), name="merge_fwd",
        compiler_params=_cparams(("parallel",)),
    )(proj, proj, proj, p_sb, p_ssm, p_mem)


def _merge_bwd(dmerged, proj, gate_off, p_sb, p_ssm, p_mem):
    S, D = p_sb.shape
    ts = _tile(S, 256, SUBLANES)
    gb = gate_off // D

    def body(dm_ref, g0, g1, g2, a0, a1, a2, d0, d1, d2, l0, l1, l2):
        dm = dm_ref[...]
        for g_ref, a_ref, d_ref, l_ref in ((g0, a0, d0, l0), (g1, a1, d1, l1), (g2, a2, d2, l2)):
            s = jax.nn.sigmoid(g_ref[...])
            d_ref[...] = (dm * s).astype(d_ref.dtype)
            l_ref[...] = dm * a_ref[...] * s * (1.0 - s)

    row = pl.BlockSpec((ts, D), lambda i: (i, 0))
    gates = [pl.BlockSpec((ts, D), functools.partial(lambda i, n: (i, gb + n), n=n)) for n in range(3)]
    sd = jax.ShapeDtypeStruct((S, D), F32)
    return pl.pallas_call(
        body, grid=(S // ts,), in_specs=[row] + gates + [row, row, row], out_specs=[row] * 6,
        out_shape=[jax.ShapeDtypeStruct((S, D), BF16)] * 3 + [sd] * 3, name="merge_bwd", compiler_params=_cparams(("parallel",)),
    )(dmerged, proj, proj, proj, p_sb, p_ssm, p_mem)


def _glu_fwd(glu):
    S, W2 = glu.shape
    W = W2 // 2
    ts = _tile(S, 512, SUBLANES)

    def body(x_ref, o_ref):
        o_ref[...] = (x_ref[:, :W] * jax.nn.sigmoid(x_ref[:, W:])).astype(o_ref.dtype)

    return pl.pallas_call(
        body, grid=(S // ts,), in_specs=[pl.BlockSpec((ts, W2), lambda i: (i, 0))],
        out_specs=pl.BlockSpec((ts, W), lambda i: (i, 0)),
        out_shape=jax.ShapeDtypeStruct((S, W), BF16), name="glu_fwd",
        compiler_params=_cparams(("parallel",)),
    )(glu)


def _glu_bwd(dzz, glu):
    S, W2 = glu.shape
    W = W2 // 2
    ts = _tile(S, 512, SUBLANES)

    def body(d_ref, x_ref, o_ref):
        d = d_ref[...]
        a = x_ref[:, :W]
        s = jax.nn.sigmoid(x_ref[:, W:])
        o_ref[:, :W] = (d * s).astype(o_ref.dtype)
        o_ref[:, W:] = (d * a * s * (1.0 - s)).astype(o_ref.dtype)

    return pl.pallas_call(
        body, grid=(S // ts,),
        in_specs=[pl.BlockSpec((ts, W), lambda i: (i, 0)), pl.BlockSpec((ts, W2), lambda i: (i, 0))],
        out_specs=pl.BlockSpec((ts, W2), lambda i: (i, 0)),
        out_shape=jax.ShapeDtypeStruct((S, W2), BF16), name="glu_bwd",
        compiler_params=_cparams(("parallel",)),
    )(dzz, glu)


def _swiglu_fwd(gu):
    S, F2 = gu.shape
    Fh = F2 // 2
    ts = _tile(S, 128, SUBLANES)

    def body(x_ref, o_ref):
        fg = x_ref[:, :Fh]
        o_ref[...] = (fg * jax.nn.sigmoid(fg) * x_ref[:, Fh:]).astype(o_ref.dtype)

    return pl.pallas_call(
        body, grid=(S // ts,), in_specs=[pl.BlockSpec((ts, F2), lambda i: (i, 0))],
        out_specs=pl.BlockSpec((ts, Fh), lambda i: (i, 0)),
        out_shape=jax.ShapeDtypeStruct((S, Fh), BF16), name="swiglu_fwd",
        compiler_params=_cparams(("parallel",)),
    )(gu)


def _swiglu_bwd(dhid, gu):
    S, F2 = gu.shape
    Fh = F2 // 2
    ts = _tile(S, 128, SUBLANES)

    def body(d_ref, x_ref, o_ref):
        d = d_ref[...]
        fg = x_ref[:, :Fh]
        fu = x_ref[:, Fh:]
        s = jax.nn.sigmoid(fg)
        o_ref[:, :Fh] = (d * fu * s * (1.0 + fg * (1.0 - s))).astype(o_ref.dtype)
        o_ref[:, Fh:] = (d * fg * s).astype(o_ref.dtype)

    return pl.pallas_call(
        body, grid=(S // ts,),
        in_specs=[pl.BlockSpec((ts, Fh), lambda i: (i, 0)), pl.BlockSpec((ts, F2), lambda i: (i, 0))],
        out_specs=pl.BlockSpec((ts, F2), lambda i: (i, 0)),
        out_shape=jax.ShapeDtypeStruct((S, F2), BF16), name="swiglu_bwd",
        compiler_params=_cparams(("parallel",)),
    )(dhid, gu)


def _assemble_dproj(pieces):
    S = pieces[0].shape[0]
    widths = [p.shape[1] for p in pieces]
    total = sum(widths)
    ts = _tile(S, 128, SUBLANES)
    n = len(pieces)

    def body(*refs):
        o_ref, b_ref = refs[n], refs[n + 1]

        @pl.when(pl.program_id(0) == 0)
        def _():
            b_ref[...] = jnp.zeros_like(b_ref)

        off = 0
        for r, w in zip(refs[:n], widths):
            v = r[...].astype(F32)
            o_ref[:, off:off + w] = v.astype(o_ref.dtype)
            b_ref[:, off:off + w] += jnp.sum(v, axis=0, keepdims=True)
            off += w

    return pl.pallas_call(
        body, grid=(S // ts,),
        in_specs=[pl.BlockSpec((ts, w), lambda i: (i, 0)) for w in widths],
        out_specs=[pl.BlockSpec((ts, total), lambda i: (i, 0)), pl.BlockSpec((1, total), lambda i: (0, 0))],
        out_shape=[jax.ShapeDtypeStruct((S, total), BF16), jax.ShapeDtypeStruct((1, total), F32)],
        name="assemble_dproj", compiler_params=_cparams(("arbitrary",)),
    )(*pieces)


def _loss_head(y, target):
    S, D = y.shape
    ts = _tile(S, 256, SUBLANES)

    def body(y_ref, t_ref, dy_ref, l_ref):
        @pl.when(pl.program_id(0) == 0)
        def _():
            l_ref[...] = jnp.zeros_like(l_ref)

        e = y_ref[...] - t_ref[...]
        dy_ref[...] = e * (1.0 / D)
        part = jnp.sum(jnp.sum(e * e, axis=1, keepdims=True), axis=0, keepdims=True) * (0.5 / D)
        l_ref[...] += jnp.broadcast_to(part, l_ref.shape)

    row = pl.BlockSpec((ts, D), lambda i: (i, 0))
    return pl.pallas_call(
        body, grid=(S // ts,), in_specs=[row, row],
        out_specs=[row, pl.BlockSpec((1, LANES), lambda i: (0, 0))],
        out_shape=[jax.ShapeDtypeStruct((S, D), F32), jax.ShapeDtypeStruct((1, LANES), F32)],
        name="loss_head", compiler_params=_cparams(("arbitrary",)),
    )(y, target)


SB_TQ = 512
SB_TK = 256


def _sb_tile_terms(q, kb, scale, causal):
    z = _dot(q, kb, NT) * scale
    soft = jnp.log(1.0 + jnp.exp(-jnp.abs(z)))
    ls = jnp.minimum(z, 0.0) - soft
    l1m = jnp.minimum(-z, 0.0) - soft
    if causal is not None:
        l1m = jnp.where(causal, l1m, 0.0)
    return ls, l1m


def _sb_causal(qi, kj, TQ, TK):
    t_idx = qi * TQ + lax.broadcasted_iota(jnp.int32, (TQ, TK), 0)
    s_idx = kj * TK + lax.broadcasted_iota(jnp.int32, (TQ, TK), 1)
    return s_idx < t_idx


def _exchange_begin(heads, nq, start, middle=None):
    h, qi = pl.program_id(0), pl.program_id(1)

    @pl.when((h == 0) & (qi == 0))
    def _():
        start()

    if middle is not None:
        @pl.when((h == heads - 1) & (qi == nq - 1))
        def _():
            middle()


def _exchange_end(heads, nq, finish):
    @pl.when((pl.program_id(0) == heads - 1) & (pl.program_id(1) == nq - 1))
    def _():
        finish()


def _sb_fwd(proj, q_off, k_off, v_off, heads, ag=None):
    S = proj.shape[0]
    Dh = HEAD_DIM
    TQ = min(SB_TQ, S)
    TK = SB_TK
    nq = S // TQ
    scale = Dh ** -0.5
    qb, kb0, vb0 = q_off // Dh, k_off // Dh, v_off // Dh
    n_ag = 0 if ag is None else len(ag)
    assert ag is None or heads >= 2

    def body(q_ref, k_ref, v_ref, *rest):
        o_ref, c_ref = rest[n_ag:n_ag + 2]
        if n_ag:
            ag_start, ag_middle, ag_finish = _ag_hooks(rest[n_ag + 2:2 * n_ag + 2], *rest[2 * n_ag + 2:])
            _exchange_begin(heads, nq, ag_start, ag_middle)
        qi = pl.program_id(1)
        q = q_ref[...].astype(BF16)
        upper = _tri2(TK, lambda j, s: j > s)
        nfull = (qi * TQ) // TK

        def block(kj, carry, masked):
            c, acc = carry
            off = pl.multiple_of(kj * TK, TK)
            kblk = k_ref[pl.ds(off, TK), :].astype(BF16)
            vblk = v_ref[pl.ds(off, TK), :].astype(BF16)
            causal = _sb_causal(qi, kj, TQ, TK) if masked else None
            ls, l1m = _sb_tile_terms(q, kblk, scale, causal)
            w = jnp.exp(ls + _dot_mask(l1m, upper) + c)
            if masked:
                w = jnp.where(causal, w, 0.0)
            acc = acc + _dot(w.astype(BF16), vblk)
            c = c + jnp.sum(l1m, axis=1, keepdims=True)
            return c, acc

        carry = (jnp.zeros((TQ, 1), F32), jnp.zeros((TQ, Dh), F32))
        for d in range(TQ // TK - 1, -1, -1):
            carry = block(nfull + d, carry, True)

        c, acc = lax.fori_loop(0, nfull, lambda jj, carry: block(nfull - 1 - jj, carry, False), carry)
        o_ref[...] = acc.astype(o_ref.dtype)
        c_ref[...] = c
        if n_ag:
            _exchange_end(heads, nq, ag_finish)

    ag = [] if ag is None else list(ag)
    res = pl.pallas_call(
        body, grid=(heads, nq),
        in_specs=[pl.BlockSpec((TQ, Dh), lambda h, i: (i, qb + h)),
                  pl.BlockSpec((S, Dh), lambda h, i: (0, kb0 + h)),
                  pl.BlockSpec((S, Dh), lambda h, i: (0, vb0 + h))] + [ANY] * n_ag,
        out_specs=[pl.BlockSpec((TQ, Dh), lambda h, i: (i, h)),
                   pl.BlockSpec((None, TQ, 1), lambda h, i: (h, i, 0))] + [ANY] * n_ag,
        out_shape=[jax.ShapeDtypeStruct((S, heads * Dh), BF16), jax.ShapeDtypeStruct((heads, S, 1), F32)]
        + [jax.ShapeDtypeStruct(b.shape, b.dtype) for b in ag],
        input_output_aliases={3 + i: 2 + i for i in range(n_ag)},
        scratch_shapes=[pltpu.SemaphoreType.DMA((n_ag * 6,)), pltpu.SemaphoreType.DMA((n_ag * 6,))] if n_ag else [],
        name="sb_fwd_ag" if n_ag else "sb_fwd",
        compiler_params=_cparams(("arbitrary", "arbitrary") if n_ag else ("parallel", "arbitrary")),
    )(proj, proj, proj, *ag)
    return res[0], res[1], list(res[2:])


def _sb_bwd(proj, q_off, k_off, v_off, heads, dout, ctot, rs=None):
    S = proj.shape[0]
    Dh = HEAD_DIM
    TQ = min(SB_TQ, S)
    TK = SB_TK
    nq = S // TQ
    scale = Dh ** -0.5
    qb, kb0, vb0 = q_off // Dh, k_off // Dh, v_off // Dh
    n_rs = 0 if rs is None else len(rs)

    def body(q_ref, k_ref, v_ref, do_ref, c_ref, *rest):
        dq_ref, dk_ref, dv_ref = rest[n_rs:n_rs + 3]
        if n_rs:
            rs_start, rs_finish = _rs_chips_hooks(rest[:n_rs], rest[n_rs + 3:2 * n_rs + 3], *rest[2 * n_rs + 3:])
            _exchange_begin(heads, nq, rs_start)
        qi = pl.program_id(1)

        @pl.when(qi == 0)
        def _():
            dk_ref[...] = jnp.zeros_like(dk_ref)
            dv_ref[...] = jnp.zeros_like(dv_ref)

        q = q_ref[...].astype(BF16)
        do = do_ref[...].astype(BF16)
        ctot = c_ref[...]
        lower_incl = _tri2(TK, lambda j, s: j <= s)
        lower = _tri2(TK, lambda j, s: j < s)
        nfull = (qi * TQ) // TK

        def block(kj, carry, masked):
            cl, ce, dq = carry
            off = pl.multiple_of(kj * TK, TK)
            kblk = k_ref[pl.ds(off, TK), :].astype(BF16)
            vblk = v_ref[pl.ds(off, TK), :].astype(BF16)
            causal = _sb_causal(qi, kj, TQ, TK) if masked else None
            ls, l1m = _sb_tile_terms(q, kblk, scale, causal)
            w = jnp.exp(ls + (ctot - cl - _dot_mask(l1m, lower_incl)))
            if masked:
                w = jnp.where(causal, w, 0.0)
            e = w * _dot(do, vblk, NT)
            before = ce + _dot_mask(e, lower)
            beta = jnp.exp(ls)
            dz = (e * (1.0 - beta) - beta * before) * scale
            if masked:
                dz = jnp.where(causal, dz, 0.0)
            dzb = dz.astype(BF16)
            dq = dq + _dot(dzb, kblk)
            dk_ref[pl.ds(off, TK), :] += _dot(dzb, q, TN)
            dv_ref[pl.ds(off, TK), :] += _dot(w.astype(BF16), do, TN)
            cl = cl + jnp.sum(l1m, axis=1, keepdims=True)
            ce = ce + jnp.sum(e, axis=1, keepdims=True)
            return cl, ce, dq

        zero = jnp.zeros((TQ, 1), F32)
        carry = lax.fori_loop(0, nfull, lambda kj, carry: block(kj, carry, False),
                              (zero, zero, jnp.zeros((TQ, Dh), F32)))
        for d in range(TQ // TK):
            carry = block(nfull + d, carry, True)
        dq_ref[...] = carry[2]
        if n_rs:
            _exchange_end(heads, nq, rs_finish)

    rs = [] if rs is None else list(rs)
    blk = pl.BlockSpec((TQ, Dh), lambda h, i: (i, h))
    col = pl.BlockSpec((S, Dh), lambda h, i: (0, h))
    sd = jax.ShapeDtypeStruct((S, heads * Dh), F32)
    res = pl.pallas_call(
        body, grid=(heads, nq),
        in_specs=[pl.BlockSpec((TQ, Dh), lambda h, i: (i, qb + h)),
                  pl.BlockSpec((S, Dh), lambda h, i: (0, kb0 + h)),
                  pl.BlockSpec((S, Dh), lambda h, i: (0, vb0 + h)), blk,
                  pl.BlockSpec((None, TQ, 1), lambda h, i: (h, i, 0))] + [ANY] * n_rs,
        out_specs=[blk, col, col] + [ANY] * n_rs, out_shape=[sd, sd, sd] + _rs_chips_shapes(rs),
        scratch_shapes=[pltpu.SemaphoreType.DMA((n_rs * 3,)), pltpu.SemaphoreType.DMA((n_rs * 3,))] if n_rs else [],
        name="sb_bwd_rs" if n_rs else "sb_bwd",
        compiler_params=_cparams(("arbitrary", "arbitrary") if n_rs else ("parallel", "arbitrary")),
    )(proj, proj, proj, dout, ctot, *rs)
    return res[0], res[1], res[2], list(res[3:])


def _mem_probs(qh, kh, scale):
    s = _dot(qh, kh, NT) * scale
    m = jnp.max(s, axis=-1, keepdims=True)
    p = jnp.exp(s - m)
    return p / jnp.sum(p, axis=-1, keepdims=True)


def _mem_fwd(proj, q_off, width, kv):
    S = proj.shape[0]
    Dh = HEAD_DIM
    heads = width // Dh
    ts = _tile(S, 512, SUBLANES)
    scale = Dh ** -0.5
    M = kv.shape[0]

    def body(q_ref, kv_ref, o_ref):
        for h in range(heads):
            qh = q_ref[:, h * Dh:(h + 1) * Dh].astype(BF16)
            kh = kv_ref[:, h * Dh:(h + 1) * Dh].astype(BF16)
            vh = kv_ref[:, width + h * Dh:width + (h + 1) * Dh].astype(BF16)
            p = _mem_probs(qh, kh, scale)
            o_ref[:, h * Dh:(h + 1) * Dh] = _dot(p.astype(BF16), vh).astype(o_ref.dtype)

    return pl.pallas_call(
        body, grid=(S // ts,),
        in_specs=[pl.BlockSpec((ts, width), lambda i: (i, q_off // width)),
                  pl.BlockSpec((M, 2 * width), lambda i: (0, 0))],
        out_specs=pl.BlockSpec((ts, width), lambda i: (i, 0)),
        out_shape=jax.ShapeDtypeStruct((S, width), BF16), name="mem_fwd",
        compiler_params=_cparams(("parallel",)),
    )(proj, kv)


def _mem_bwd(proj, q_off, width, kv, dmm):
    S = proj.shape[0]
    Dh = HEAD_DIM
    heads = width // Dh
    ts = _tile(S, 512, SUBLANES)
    scale = Dh ** -0.5
    M = kv.shape[0]

    def body(q_ref, kv_ref, d_ref, dq_ref, dkv_ref):
        @pl.when(pl.program_id(0) == 0)
        def _():
            dkv_ref[...] = jnp.zeros_like(dkv_ref)

        for h in range(heads):
            qh = q_ref[:, h * Dh:(h + 1) * Dh].astype(BF16)
            kh = kv_ref[:, h * Dh:(h + 1) * Dh].astype(BF16)
            vh = kv_ref[:, width + h * Dh:width + (h + 1) * Dh].astype(BF16)
            dh = d_ref[:, h * Dh:(h + 1) * Dh].astype(BF16)
            p = _mem_probs(qh, kh, scale)
            dp = _dot(dh, vh, NT)
            ds = p * (dp - jnp.sum(dp * p, axis=-1, keepdims=True)) * scale
            dsb = ds.astype(BF16)
            dq_ref[:, h * Dh:(h + 1) * Dh] = _dot(dsb, kh)
            dkv_ref[:, h * Dh:(h + 1) * Dh] += _dot(dsb, qh, TN)
            dkv_ref[:, width + h * Dh:width + (h + 1) * Dh] += _dot(p.astype(BF16), dh, TN)

    row = pl.BlockSpec((ts, width), lambda i: (i, 0))
    full = pl.BlockSpec((M, 2 * width), lambda i: (0, 0))
    return pl.pallas_call(
        body, grid=(S // ts,),
        in_specs=[pl.BlockSpec((ts, width), lambda i: (i, q_off // width)), full, row],
        out_specs=[row, full],
        out_shape=[jax.ShapeDtypeStruct((S, width), F32), jax.ShapeDtypeStruct((M, 2 * width), F32)],
        name="mem_bwd", compiler_params=_cparams(("arbitrary",)),
    )(proj, kv, dmm)


def _disc_math(lre, lim, logdt, bre_t, bim_t):
    dt = jnp.exp(logdt)
    mag = jnp.exp(lre * dt)
    ang = lim * dt
    a = mag * jnp.cos(ang)
    b = mag * jnp.sin(ang)
    den = lre * lre + lim * lim
    nr = a - 1.0
    fre = (nr * lre + b * lim) / den
    fim = (b * lre - nr * lim) / den
    bbre = fre * bre_t - fim * bim_t
    bbim = fre * bim_t + fim * bre_t
    return a, b, bbre, bbim


def _s5_disc(lre, lim, logdt, bre_t, bim_t):
    G, _, P = lre.shape
    C = bre_t.shape[1]

    def body(lre_ref, lim_ref, dt_ref, br_ref, bi_ref, a_ref, b_ref, bbre_ref, bbim_ref):
        a, b, bbre, bbim = _disc_math(lre_ref[...], lim_ref[...], dt_ref[...], br_ref[...], bi_ref[...])
        a_ref[...] = a
        b_ref[...] = b
        bbre_ref[...] = bbre
        bbim_ref[...] = bbim

    gp = jax.ShapeDtypeStruct((G, 1, P), F32)
    gcp = jax.ShapeDtypeStruct((G, C, P), F32)
    return pl.pallas_call(
        body, in_specs=[VMEM] * 5, out_specs=[VMEM] * 4, out_shape=[gp, gp, gcp, gcp], name="s5_disc",
    )(lre, lim, logdt, bre_t, bim_t)


def _s5_disc_bwd(lre, lim, logdt, bre_t, bim_t, da, db, dbbre, dbbim):
    G, _, P = lre.shape
    C = bre_t.shape[1]

    def body(lre_ref, lim_ref, dt_ref, br_ref, bi_ref, da_ref, db_ref, dbr_ref, dbi_ref,
             o_lre, o_lim, o_dt, o_br, o_bi):
        _, vjp = jax.vjp(_disc_math, lre_ref[...], lim_ref[...], dt_ref[...], br_ref[...], bi_ref[...])
        g = vjp((da_ref[...], db_ref[...], dbr_ref[...], dbi_ref[...]))
        o_lre[...] = g[0]
        o_lim[...] = g[1]
        o_dt[...] = g[2]
        o_br[...] = g[3]
        o_bi[...] = g[4]

    gp = jax.ShapeDtypeStruct((G, 1, P), F32)
    gcp = jax.ShapeDtypeStruct((G, C, P), F32)
    return pl.pallas_call(
        body, in_specs=[VMEM] * 9, out_specs=[VMEM] * 5,
        out_shape=[gp, gp, jax.ShapeDtypeStruct((G, 1, 1), F32), gcp, gcp], name="s5_disc_bwd",
    )(lre, lim, logdt, bre_t, bim_t, da, db, dbbre, dbbim)


S5_CHUNK = 256


def _load_once(pairs):
    @pl.when(pl.program_id(0) == 0)
    def _():
        for src, dst in pairs:
            pltpu.sync_copy(src, dst)


def _s5_fwd(proj, u_off, width, a_row, b_row, bmre, bmim, cmre, cmimn, d_row):
    S = proj.shape[0]
    GP = a_row.shape[1]
    T = min(S5_CHUNK, S)

    def body(u_ref, a_ref, b_ref, d_ref, bre_hbm, bim_hbm, cre_hbm, cim_hbm,
             uo_ref, y_ref, gy_ref, hre_ref, him_ref, st_ref, bure_s, buim_s, bre_ref, bim_ref, cre_ref, cim_ref):
        @pl.when(pl.program_id(0) == 0)
        def _():
            st_ref[...] = jnp.zeros_like(st_ref)

        _load_once([(bre_hbm, bre_ref), (bim_hbm, bim_ref), (cre_hbm, cre_ref), (cim_hbm, cim_ref)])
        u = u_ref[...]
        uo_ref[...] = u
        for ws, gs in _bd_blocks(width, GP):
            bure_s[:, gs] = _dot_a2(u[:, ws], bre_ref[ws, gs])
            buim_s[:, gs] = _dot_a2(u[:, ws], bim_ref[ws, gs])
        a = a_ref[...]
        b = b_ref[...]

        def step(ii, carry):
            hre, him = carry
            base = pl.multiple_of(ii * SUBLANES, SUBLANES)
            br = bure_s[pl.ds(base, SUBLANES), :]
            bi = buim_s[pl.ds(base, SUBLANES), :]
            rows_re, rows_im = [], []
            for j in range(SUBLANES):
                nre = a * hre - b * him + br[j:j + 1, :]
                nim = a * him + b * hre + bi[j:j + 1, :]
                hre, him = nre, nim
                rows_re.append(nre)
                rows_im.append(nim)
            hre_ref[pl.ds(base, SUBLANES), :] = jnp.concatenate(rows_re, axis=0)
            him_ref[pl.ds(base, SUBLANES), :] = jnp.concatenate(rows_im, axis=0)
            return hre, him

        hre, him = lax.fori_loop(0, T // SUBLANES, step, (st_ref[0:1, :], st_ref[1:2, :]))
        st_ref[0:1, :] = hre
        st_ref[1:2, :] = him
        for ws, gs in _bd_blocks(width, GP):
            y = (_dot_a2(hre_ref[:, gs], cre_ref[gs, ws]) + _dot_a2(him_ref[:, gs], cim_ref[gs, ws])
                 + d_ref[:, ws] * u[:, ws])
            y_ref[:, ws] = y
            gy_ref[:, ws] = jax.nn.gelu(y).astype(gy_ref.dtype)

    c0 = lambda i: (0, 0)
    urow = pl.BlockSpec((T, width), lambda i: (i, u_off // width))
    row = pl.BlockSpec((T, width), lambda i: (i, 0))
    hrow = pl.BlockSpec((T, GP), lambda i: (i, 0))
    sw = jax.ShapeDtypeStruct((S, width), F32)
    sg = jax.ShapeDtypeStruct((S, GP), F32)
    return pl.pallas_call(
        body, grid=(S // T,),
        in_specs=[urow, pl.BlockSpec((1, GP), c0), pl.BlockSpec((1, GP), c0), pl.BlockSpec((1, width), c0),
                  ANY, ANY, ANY, ANY],
        out_specs=[row, row, row, hrow, hrow],
        out_shape=[sw, sw, jax.ShapeDtypeStruct((S, width), BF16), sg, sg],
        scratch_shapes=[pltpu.VMEM((SUBLANES, GP), F32), pltpu.VMEM((T, GP), F32), pltpu.VMEM((T, GP), F32),
                        pltpu.VMEM((width, GP), F32), pltpu.VMEM((width, GP), F32),
                        pltpu.VMEM((GP, width), F32), pltpu.VMEM((GP, width), F32)],
        name="s5_fwd", compiler_params=_cparams(("arbitrary",)),
    )(proj, a_row, b_row, d_row, bmre, bmim, cmre, cmimn)


def _s5_bwd(u, dgy, y, hre, him, a_row, b_row, bmre, bmim, cmre, cmimn, d_row):
    S, width = u.shape
    GP = a_row.shape[1]
    T = min(S5_CHUNK, S)
    nchunk = S // T

    def body(u_ref, dgy_ref, y_ref, hre_ref, him_ref, a_ref, b_ref, d_ref, bre_hbm, bim_hbm, cre_hbm, cim_hbm,
             du_ref, dy_ref, gre_s, gim_s, dd_ref, da_ref, db_ref,
             st_ref, bre_ref, bim_ref, cre_ref, cim_ref):
        @pl.when(pl.program_id(0) == 0)
        def _():
            st_ref[...] = jnp.zeros_like(st_ref)
            for r in (dd_ref, da_ref, db_ref):
                r[...] = jnp.zeros_like(r)

        _load_once([(bre_hbm, bre_ref), (bim_hbm, bim_ref), (cre_hbm, cre_ref), (cim_hbm, cim_ref)])
        u = u_ref[...]
        dy = dgy_ref[...] * _gelu_grad(y_ref[...])
        dy_ref[...] = dy
        for ws, gs in _bd_blocks(width, GP):
            gre_s[:, gs] = _dot_a2(dy[:, ws], cre_ref[gs, ws], NT)
            gim_s[:, gs] = _dot_a2(dy[:, ws], cim_ref[gs, ws], NT)
        a = a_ref[...]
        b = b_ref[...]
        g_in_re = st_ref[0:1, :]
        g_in_im = st_ref[1:2, :]

        def step(ii, carry):
            gre, gim = carry
            base = pl.multiple_of((T // SUBLANES - 1 - ii) * SUBLANES, SUBLANES)
            dr = gre_s[pl.ds(base, SUBLANES), :]
            di = gim_s[pl.ds(base, SUBLANES), :]
            rows_re = [None] * SUBLANES
            rows_im = [None] * SUBLANES
            for j in range(SUBLANES - 1, -1, -1):
                nre = dr[j:j + 1, :] + a * gre + b * gim
                nim = di[j:j + 1, :] - b * gre + a * gim
                gre, gim = nre, nim
                rows_re[j] = nre
                rows_im[j] = nim
            gre_s[pl.ds(base, SUBLANES), :] = jnp.concatenate(rows_re, axis=0)
            gim_s[pl.ds(base, SUBLANES), :] = jnp.concatenate(rows_im, axis=0)
            return gre, gim

        gre, gim = lax.fori_loop(0, T // SUBLANES, step, (g_in_re, g_in_im))
        st_ref[0:1, :] = gre
        st_ref[1:2, :] = gim
        last = lax.broadcasted_iota(jnp.int32, (T, 1), 0) == T - 1
        nxt_re = jnp.where(last, g_in_re, pltpu.roll(gre_s[...], T - 1, 0))
        nxt_im = jnp.where(last, g_in_im, pltpu.roll(gim_s[...], T - 1, 0))
        hre = hre_ref[...]
        him = him_ref[...]
        da_ref[...] += jnp.sum(nxt_re * hre + nxt_im * him, axis=0, keepdims=True)
        db_ref[...] += jnp.sum(nxt_im * hre - nxt_re * him, axis=0, keepdims=True)
        for ws, gs in _bd_blocks(width, GP):
            du_ref[:, ws] = (_dot_a2(gre_s[:, gs], bre_ref[ws, gs], NT) + _dot_a2(gim_s[:, gs], bim_ref[ws, gs], NT)
                             + d_ref[:, ws] * dy[:, ws])
        dd_ref[...] += jnp.sum(dy * u, axis=0, keepdims=True)

    c0 = lambda i: (0, 0)
    rev = lambda i: (nchunk - 1 - i, 0)
    row = pl.BlockSpec((T, width), rev)
    hrow = pl.BlockSpec((T, GP), rev)
    v_gp = pl.BlockSpec((1, GP), c0)
    v_w = pl.BlockSpec((1, width), c0)
    sw = jax.ShapeDtypeStruct((S, width), F32)
    sg = jax.ShapeDtypeStruct((S, GP), F32)
    return pl.pallas_call(
        body, grid=(nchunk,),
        in_specs=[row, row, row, hrow, hrow, v_gp, v_gp, v_w, ANY, ANY, ANY, ANY],
        out_specs=[row, row, hrow, hrow, v_w, v_gp, v_gp],
        out_shape=[sw, sw, sg, sg, jax.ShapeDtypeStruct((1, width), F32),
                   jax.ShapeDtypeStruct((1, GP), F32), jax.ShapeDtypeStruct((1, GP), F32)],
        scratch_shapes=[pltpu.VMEM((SUBLANES, GP), F32),
                        pltpu.VMEM((width, GP), F32), pltpu.VMEM((width, GP), F32),
                        pltpu.VMEM((GP, width), F32), pltpu.VMEM((GP, width), F32)],
        name="s5_bwd", compiler_params=_cparams(("arbitrary",)),
    )(u, dgy, y, hre, him, a_row, b_row, d_row, bmre, bmim, cmre, cmimn)


def _block_diag(x):
    G, A, B = x.shape
    eye = jnp.eye(G, dtype=x.dtype)
    return (eye[:, None, :, None] * x[:, :, None, :]).reshape(G * A, G * B)


def _block_diag_take(m, G):
    A, B = m.shape[0] // G, m.shape[1] // G
    return jnp.einsum("gagb->gab", m.reshape(G, A, G, B))


def _adamw(w, g, m, v):
    shape = w.shape
    C = shape[-1]
    w2, g2, m2, v2 = (t.reshape(-1, C) for t in (w, g, m, v))
    R = w2.shape[0]
    rb = _tile(R, max(SUBLANES, (1 << 19) // C), SUBLANES)
    c1 = 1.0 - ADAM_B1 ** ADAM_STEP
    c2 = 1.0 - ADAM_B2 ** ADAM_STEP

    def body(w_ref, g_ref, m_ref, v_ref, d_ref, nm_ref, nv_ref):
        gv = g_ref[...]
        nm = ADAM_B1 * m_ref[...] + (1.0 - ADAM_B1) * gv
        nv = ADAM_B2 * v_ref[...] + (1.0 - ADAM_B2) * (gv * gv)
        d_ref[...] = -ADAM_LR * ((nm / c1) / (jnp.sqrt(nv / c2) + ADAM_EPS) + ADAM_WD * w_ref[...])
        nm_ref[...] = nm
        nv_ref[...] = nv

    blk = pl.BlockSpec((rb, C), lambda i: (i, 0))
    sd = jax.ShapeDtypeStruct((R, C), F32)
    outs = pl.pallas_call(
        body, grid=(R // rb,), in_specs=[blk] * 4, out_specs=[blk] * 3, out_shape=[sd] * 3,
        name="adamw", compiler_params=_cparams(("parallel",)),
    )(w2, g2, m2, v2)
    return tuple(o.reshape(shape) for o in outs)


def _coords():
    x, y, c = lax.axis_index("x"), lax.axis_index("y"), lax.axis_index("c")
    return x, y, c


def _place_shard(w, l, place):
    _, R, C = w.shape
    rb = _tile(R, max(16, (1 << 19) // C), 16)

    def body(place_ref, w_ref, o_ref):
        o_ref[...] = w_ref[...].astype(BF16)

    grid_spec = pltpu.PrefetchScalarGridSpec(
        num_scalar_prefetch=1, grid=(R // rb,),
        in_specs=[pl.BlockSpec((None, rb, C), lambda i, p: (l, i, 0))],
        out_specs=pl.BlockSpec((None, rb, C), lambda i, p: (p[1], i, 0)))
    return pl.pallas_call(
        body, grid_spec=grid_spec, out_shape=jax.ShapeDtypeStruct((N_CHIPS, R, C), BF16),
        name="place_shard", compiler_params=_cparams(("arbitrary",)),
    )(place, w)


def _ag_hooks(outs, send_sems, recv_sems):
    n = len(outs)

    def rcopy(a, k, block, to):
        cx, cy, cc = block
        hr = outs[a].shape[1] // 2
        blk = outs[a].at[2 * cx + cy, pl.ds(cc * hr, hr)]
        return pltpu.make_async_remote_copy(
            src_ref=blk, dst_ref=blk, send_sem=send_sems.at[a * 6 + k], recv_sem=recv_sems.at[a * 6 + k],
            device_id=to, device_id_type=MESH)

    def places():
        x, y, c = _coords()
        return (x, y, c), (x, y, 1 - c), [(1 - x, y), (x, 1 - y), (1 - x, 1 - y)]

    def start():
        me, _, others = places()
        for a in range(n):
            for j, ch in enumerate(others):
                rcopy(a, j, me, (*ch, me[2])).start()

    def middle():
        me, sibling, others = places()
        for a in range(n):
            for j, ch in enumerate(others):
                rcopy(a, j, (*ch, me[2]), me).wait_recv()
                rcopy(a, 3 + j, (*ch, me[2]), sibling).start()

    def finish():
        me, sibling, others = places()
        for a in range(n):
            for j, ch in enumerate(others):
                rcopy(a, 3 + j, (*ch, sibling[2]), me).wait_recv()
        for a in range(n):
            for j, ch in enumerate(others):
                rcopy(a, j, me, (*ch, me[2])).wait_send()
                rcopy(a, 3 + j, (*ch, me[2]), sibling).wait_send()

    return start, middle, finish


def _ag_weights(bufs):
    n = len(bufs)

    def body(*refs):
        start, middle, finish = _ag_hooks(refs[n:2 * n], *refs[2 * n:])
        start()
        middle()
        finish()

    return pl.pallas_call(
        body, out_shape=[jax.ShapeDtypeStruct(b.shape, b.dtype) for b in bufs],
        in_specs=[ANY] * n, out_specs=[ANY] * n, input_output_aliases={i: i for i in range(n)},
        scratch_shapes=[pltpu.SemaphoreType.DMA((n * 6,)), pltpu.SemaphoreType.DMA((n * 6,))],
        name="ag_weights",
    )(*bufs)


def _rs_pair_hooks(gs, outs, send_sems, recv_sems):
    def copies():
        x, y, c = _coords()
        cps = []
        for i in range(len(gs)):
            hr = gs[i].shape[1] // 2
            cps.append(pltpu.make_async_remote_copy(
                src_ref=gs[i].at[:, pl.ds((1 - c) * hr, hr)], dst_ref=outs[i],
                send_sem=send_sems.at[i], recv_sem=recv_sems.at[i],
                device_id=(x, y, 1 - c), device_id_type=MESH))
        return cps

    def start():
        for cp in copies():
            cp.start()

    def finish():
        for cp in copies():
            cp.wait()

    return start, None, finish


def _rs_pair_comm(grads):
    return dict(ins=list(grads), alias=False, n_sems=len(grads), hooks=_rs_pair_hooks,
                out_shapes=[jax.ShapeDtypeStruct((N_CHIPS, g.shape[1] // 2, g.shape[2]), F32) for g in grads])


def _ag_comm(bufs):
    return dict(ins=list(bufs), alias=True, n_sems=6 * len(bufs),
                hooks=lambda ins, outs, send_sems, recv_sems: _ag_hooks(outs, send_sems, recv_sems),
                out_shapes=[jax.ShapeDtypeStruct(b.shape, b.dtype) for b in bufs])


def _rs_pair(grads):
    n = len(grads)
    comm = _rs_pair_comm(grads)

    def body(*refs):
        start, _, finish = _rs_pair_hooks(refs[:n], refs[n:2 * n], *refs[2 * n:])
        start()
        finish()

    return pl.pallas_call(
        body, out_shape=comm["out_shapes"], in_specs=[ANY] * n, out_specs=[ANY] * n,
        scratch_shapes=[pltpu.SemaphoreType.DMA((n,)), pltpu.SemaphoreType.DMA((n,))],
        name="rs_pair",
    )(*grads)


def _pair_add(g, r, place):
    _, R, C = g.shape
    hr = R // 2
    rb = _tile(hr, max(16, (1 << 19) // C), 16)
    nb = hr // rb

    def body(place_ref, g_ref, r_ref, p16_ref, own_ref):
        s = g_ref[...] + r_ref[...]
        p16_ref[...] = s.astype(BF16)

        @pl.when(pl.program_id(1) == place_ref[1])
        def _():
            own_ref[...] = s

    grid_spec = pltpu.PrefetchScalarGridSpec(
        num_scalar_prefetch=1, grid=(nb, N_CHIPS),
        in_specs=[pl.BlockSpec((None, rb, C), lambda i, k, p: (k, p[0] * nb + i, 0)),
                  pl.BlockSpec((None, rb, C), lambda i, k, p: (k, i, 0))],
        out_specs=[pl.BlockSpec((None, rb, C), lambda i, k, p: (k, i, 0)),
                   pl.BlockSpec((rb, C), lambda i, k, p: (i, 0))])
    return pl.pallas_call(
        body, grid_spec=grid_spec,
        out_shape=[jax.ShapeDtypeStruct((N_CHIPS, hr, C), BF16), jax.ShapeDtypeStruct((hr, C), F32)],
        name="pair_add", compiler_params=_cparams(("arbitrary", "arbitrary")),
    )(place, g, r)


def _rs_chips_hooks(ps, outs, send_sems, recv_sems):
    def copies():
        x, y, c = _coords()
        cps = []
        for a in range(len(ps)):
            for r in (1, 2, 3):
                kx = 1 - x if (r >> 1) else x
                ky = 1 - y if (r & 1) else y
                cps.append(pltpu.make_async_remote_copy(
                    src_ref=ps[a].at[2 * kx + ky], dst_ref=outs[a].at[r - 1],
                    send_sem=send_sems.at[a * 3 + r - 1], recv_sem=recv_sems.at[a * 3 + r - 1],
                    device_id=(kx, ky, c), device_id_type=MESH))
        return cps

    def start():
        for cp in copies():
            cp.start()

    def finish():
        for cp in copies():
            cp.wait()

    return start, finish


def _rs_chips_shapes(p16):
    return [jax.ShapeDtypeStruct((3,) + p.shape[1:], BF16) for p in p16]


def _rs_chips(p16):
    n = len(p16)

    def body(*refs):
        start, finish = _rs_chips_hooks(refs[:n], refs[n:2 * n], *refs[2 * n:])
        start()
        finish()

    return pl.pallas_call(
        body, out_shape=_rs_chips_shapes(p16), in_specs=[ANY] * n, out_specs=[ANY] * n,
        scratch_shapes=[pltpu.SemaphoreType.DMA((n * 3,)), pltpu.SemaphoreType.DMA((n * 3,))],
        name="rs_chips",
    )(*p16)


def _chip_sum(own, recv, full, l, place):
    hr, C = own.shape
    rb = _tile(hr, max(16, (1 << 19) // C), 16)
    nb = hr // rb

    def body(place_ref, o_ref, r_ref, full_ref, s_ref):
        s = o_ref[...] + r_ref[0].astype(F32)
        s = s + r_ref[1].astype(F32)
        s_ref[...] = s + r_ref[2].astype(F32)

    grid_spec = pltpu.PrefetchScalarGridSpec(
        num_scalar_prefetch=1, grid=(nb,),
        in_specs=[pl.BlockSpec((rb, C), lambda i, p: (i, 0)), pl.BlockSpec((3, rb, C), lambda i, p: (0, i, 0)), ANY],
        out_specs=pl.BlockSpec((None, rb, C), lambda i, p: (l, p[0] * nb + i, 0)))
    return pl.pallas_call(
        body, grid_spec=grid_spec, out_shape=jax.ShapeDtypeStruct(full.shape, F32),
        input_output_aliases={3: 0}, name="chip_sum", compiler_params=_cparams(("arbitrary",)),
    )(place, own, recv, full)


def _share_pair(fulls):
    n = len(fulls)

    def body(*refs):
        outs = refs[n:2 * n]
        send_sems, recv_sems = refs[2 * n:]
        x, y, c = _coords()
        copies = []
        for a in range(n):
            hr = outs[a].shape[1] // 2
            mine = outs[a].at[:, pl.ds(c * hr, hr)]
            cp = pltpu.make_async_remote_copy(
                src_ref=mine, dst_ref=mine, send_sem=send_sems.at[a], recv_sem=recv_sems.at[a],
                device_id=(x, y, 1 - c), device_id_type=MESH)
            cp.start()
            copies.append(cp)
        for cp in copies:
            cp.wait_recv()
        for cp in copies:
            cp.wait_send()

    return pl.pallas_call(
        body, out_shape=[jax.ShapeDtypeStruct(f.shape, f.dtype) for f in fulls],
        in_specs=[ANY] * n, out_specs=[ANY] * n, input_output_aliases={i: i for i in range(n)},
        scratch_shapes=[pltpu.SemaphoreType.DMA((n,)), pltpu.SemaphoreType.DMA((n,))],
        name="share_pair",
    )(*fulls)


def _small_allreduce(packed):
    m_per, ncol = packed.shape

    def body(x_ref, out_ref, tot_ref, send_sems, recv_sems, local_sem):
        x, y, c = _coords()
        me, sibling = (x, y, c), (x, y, 1 - c)
        chips = [(1 - x, y), (x, 1 - y), (1 - x, 1 - y)]

        def rows(px, py, pc):
            return out_ref.at[pl.ds((4 * px + 2 * py + pc) * m_per, m_per), :]

        def copy(k, block, to, src=None):
            return pltpu.make_async_remote_copy(
                src_ref=rows(*block) if src is None else src, dst_ref=rows(*block),
                send_sem=send_sems.at[k], recv_sem=recv_sems.at[k], device_id=to, device_id_type=MESH)

        mine = pltpu.make_async_copy(x_ref, rows(*me), local_sem)
        mine.start()
        first = [copy(0, me, sibling, src=x_ref)]
        first += [copy(1 + j, me, (*chip, c), src=x_ref) for j, chip in enumerate(chips)]
        for cp in first:
            cp.start()
        passed = [copy(4 + j, (*chip, c), sibling) for j, chip in enumerate(chips)]
        for j, chip in enumerate(chips):
            copy(1 + j, (*chip, c), me).wait_recv()
            passed[j].start()
        copy(0, sibling, me).wait_recv()
        for j, chip in enumerate(chips):
            copy(4 + j, (*chip, 1 - c), me).wait_recv()
        for cp in first + passed:
            cp.wait_send()
        mine.wait()
        tot = out_ref[pl.ds(0, m_per), :]
        for d in range(1, N_DEV):
            tot = tot + out_ref[pl.ds(d * m_per, m_per), :]
        tot_ref[...] = tot

    _, tot = pl.pallas_call(
        body,
        out_shape=[jax.ShapeDtypeStruct((N_DEV * m_per, ncol), F32), jax.ShapeDtypeStruct((m_per, ncol), F32)],
        in_specs=[VMEM], out_specs=[VMEM, VMEM],
        scratch_shapes=[pltpu.SemaphoreType.DMA((7,)), pltpu.SemaphoreType.DMA((7,)), pltpu.SemaphoreType.DMA],
        name="small_allreduce",
        compiler_params=pltpu.CompilerParams(vmem_limit_bytes=VMEM_LIMIT_MB * 1024 * 1024),
    )(packed)
    return tot


BIG = ["w_in", "sb_w_out", "ssm_w_glu", "ssm_w_out", "mem_w_kv", "mem_w_out", "w_o", "ffn_w_gate_up", "ffn_w_down"]
KIND = {"w_in": "col", "sb_w_out": "col", "ssm_w_glu": "col", "ssm_w_out": "col", "mem_w_kv": "row",
        "mem_w_out": "col", "w_o": "row", "ffn_w_gate_up": "col", "ffn_w_down": "row"}
AG_WITH_PROJ = ["ffn_w_gate_up"]
AG_WITH_SB = ["w_in", "sb_w_out", "ssm_w_glu", "ssm_w_out", "mem_w_kv", "mem_w_out", "w_o"]
AG_WITH_GATE_UP = ["ffn_w_down"]
SMALL = ["b_in", "ssm_lambda_re", "ssm_lambda_im", "ssm_log_dt", "ssm_b_re", "ssm_b_im", "ssm_c_re", "ssm_c_im",
         "ssm_d", "ln1_g", "ln1_b", "ln2_g", "ln2_b"]
WEIGHTS = ["w_in", "b_in", "sb_w_out", "ssm_lambda_re", "ssm_lambda_im", "ssm_log_dt", "ssm_b_re", "ssm_b_im",
           "ssm_c_re", "ssm_c_im", "ssm_d", "ssm_w_glu", "ssm_w_out", "mem_w_kv", "mem_w_out", "w_o", "ln1_g",
           "ln1_b", "ffn_w_gate_up", "ffn_w_down", "ln2_g", "ln2_b"]


def _pack(arrs):
    flat = jnp.concatenate([a.reshape(-1).astype(F32) for a in arrs])
    n = flat.shape[0]
    rows = -(-n // LANES)
    rows = -(-rows // SUBLANES) * SUBLANES
    return jnp.pad(flat, (0, rows * LANES - n)).reshape(rows, LANES)


def _unpack(packed, like):
    flat = packed.reshape(-1)
    out, off = [], 0
    for a in like:
        out.append(flat[off:off + a.size].reshape(a.shape))
        off += a.size
    return out


def _step(x, mem, target, W, M1, V1):
    S, D = x.shape[1], x.shape[2]
    L = W["w_in"].shape[0]
    x0 = x.reshape(S, D)
    mem2 = mem.reshape(mem.shape[1], D)
    tgt = target.reshape(S, D)
    alpha = (2 * L) ** 0.25
    sbw = W["sb_w_out"].shape[1]
    ssw = W["ssm_d"].shape[1]
    mw = W["mem_w_out"].shape[1]
    heads = sbw // HEAD_DIM
    G, P = W["ssm_lambda_re"].shape[1], W["ssm_lambda_re"].shape[2]
    q_off, k_off, v_off = 0, sbw, 2 * sbw
    u_off = 3 * sbw
    qm_off = u_off + ssw
    gate_off = qm_off + mw

    x_i, y_i, c_i = _coords()
    place = jnp.stack([c_i, 2 * x_i + y_i]).astype(jnp.int32)
    placed = [[_place_shard(W[n], l, place) for n in BIG] for l in range(L)]
    Wg = {n: [None] * L for n in BIG}
    for n, buf in zip(BIG, _ag_weights(placed[0])):
        Wg[n][0] = buf

    saved = []
    xl = x0
    xlb = x0.astype(BF16)
    for l in range(L):
        sv = {"x": xlb}
        nxt = dict(zip(BIG, placed[l + 1])) if l + 1 < L else None
        if nxt is None:
            proj = _mm_nn(xlb, Wg["w_in"][l], "col", bias=W["b_in"][l][None, :], name="mm_proj")
            sb, sb_ctot, _ = _sb_fwd(proj, q_off, k_off, v_off, heads)
        else:
            proj, got = _mm_nn(xlb, Wg["w_in"][l], "col", bias=W["b_in"][l][None, :], name="mm_proj_ag",
                               comm=_ag_comm([nxt[n] for n in AG_WITH_PROJ]))
            for n, buf in zip(AG_WITH_PROJ, got):
                Wg[n][l + 1] = buf
            sb, sb_ctot, got = _sb_fwd(proj, q_off, k_off, v_off, heads, ag=[nxt[n] for n in AG_WITH_SB])
            for n, buf in zip(AG_WITH_SB, got):
                Wg[n][l + 1] = buf
        p_sb = _mm_nn(sb, Wg["sb_w_out"][l], "col", name="mm_sb_out")

        bre_t = W["ssm_b_re"][l].transpose(0, 2, 1)
        bim_t = W["ssm_b_im"][l].transpose(0, 2, 1)
        logdt = W["ssm_log_dt"][l][:, None, None]
        lre3 = W["ssm_lambda_re"][l][:, None, :]
        lim3 = W["ssm_lambda_im"][l][:, None, :]
        a_gp, b_gp, bbre, bbim = _s5_disc(lre3, lim3, logdt, bre_t, bim_t)
        a_row, b_row = a_gp.reshape(1, G * P), b_gp.reshape(1, G * P)
        bmre, bmim = _block_diag(bbre), _block_diag(bbim)
        cmre = _block_diag(W["ssm_c_re"][l].transpose(0, 2, 1))
        cmimn = _block_diag(-W["ssm_c_im"][l].transpose(0, 2, 1))
        d_row = W["ssm_d"][l][None, :]
        u_ssm, y, gy, hre, him = _s5_fwd(proj, u_off, ssw, a_row, b_row, bmre, bmim, cmre, cmimn, d_row)
        glu = _mm_nn(gy, Wg["ssm_w_glu"][l], "col", name="mm_glu")
        zz = _glu_fwd(glu)
        p_ssm = _mm_nn(zz, Wg["ssm_w_out"][l], "col", name="mm_ssm_out")

        kv = _mm_nn(mem2, Wg["mem_w_kv"][l], "row", name="mm_kv")
        mm_o = _mem_fwd(proj, qm_off, mw, kv)
        p_mem = _mm_nn(mm_o, Wg["mem_w_out"][l], "col", name="mm_mem_out")

        merged = _merge_fwd(proj, gate_off, p_sb, p_ssm, p_mem)
        mix = _mm_nn(merged, Wg["w_o"][l], "row", name="mm_wo")
        x1, x1b, xh1, rs1 = _ln_fwd(xl, mix, W["ln1_g"][l][None, :], W["ln1_b"][l][None, :], alpha)
        if nxt is None:
            gu = _mm_nn(x1b, Wg["ffn_w_gate_up"][l], "col", name="mm_gate_up")
        else:
            gu, got = _mm_nn(x1b, Wg["ffn_w_gate_up"][l], "col", name="mm_gate_up_ag",
                             comm=_ag_comm([nxt[n] for n in AG_WITH_GATE_UP]))
            for n, buf in zip(AG_WITH_GATE_UP, got):
                Wg[n][l + 1] = buf
        hid = _swiglu_fwd(gu)
        ffn = _mm_nn(hid, Wg["ffn_w_down"][l], "row", name="mm_down")
        x2, x2b, xh2, rs2 = _ln_fwd(x1, ffn, W["ln2_g"][l][None, :], W["ln2_b"][l][None, :], alpha)
        sv.update(proj=proj, sb=sb, sb_ctot=sb_ctot, p_sb=p_sb, y=y, gy=gy, hre=hre, him=him, glu=glu, zz=zz,
                  p_ssm=p_ssm, kv=kv, mm_o=mm_o, p_mem=p_mem, merged=merged, x1=x1b, xh1=xh1, rs1=rs1, gu=gu,
                  hid=hid, xh2=xh2, rs2=rs2, u=u_ssm,
                  disc=(lre3, lim3, logdt, bre_t, bim_t, a_row, b_row, bmre, bmim, cmre, cmimn, d_row))
        saved.append(sv)
        xl, xlb = x2, x2b

    dxl, loss_part = _loss_head(xl, tgt)

    gbig = {n: [None] * L for n in BIG}
    gsmall = {n: [None] * L for n in SMALL}
    own = [None] * L
    from_chips = [None] * L
    above = None
    pending = None
    for l in range(L - 1, -1, -1):
        sv = saved[l]
        proj = sv["proj"]
        dr2, dr2b, dg2, db2 = _ln_bwd(dxl, sv["xh2"], sv["rs2"], W["ln2_g"][l][None, :])
        gsmall["ln2_g"][l], gsmall["ln2_b"][l] = dg2[0], db2[0]
        dhid = _mm_nt(dr2b, Wg["ffn_w_down"][l], "row", name="mm_d_hid")
        gbig["ffn_w_down"][l] = _mm_tn(sv["hid"], dr2b, "row", name="mm_g_down")
        dgu = _swiglu_bwd(dhid, sv["gu"])
        if above is None:
            dx1 = _mm_nt(dgu, Wg["ffn_w_gate_up"][l], "col", add=dr2, add_scale=alpha, name="mm_d_x1")
        else:
            dx1, from_sibling = _mm_nt(dgu, Wg["ffn_w_gate_up"][l], "col", add=dr2, add_scale=alpha,
                                       name="mm_d_x1_rs", comm=_rs_pair_comm(above))
            sums = [_pair_add(g, r, place) for g, r in zip(above, from_sibling)]
            pending = [s[0] for s in sums]
            own[l + 1] = [s[1] for s in sums]
        gbig["ffn_w_gate_up"][l] = _mm_tn(sv["x1"], dgu, "col", name="mm_g_gate_up")

        dr1, dr1b, dg1, db1 = _ln_bwd(dx1, sv["xh1"], sv["rs1"], W["ln1_g"][l][None, :])
        gsmall["ln1_g"][l], gsmall["ln1_b"][l] = dg1[0], db1[0]
        dmerged = _mm_nt(dr1b, Wg["w_o"][l], "row", name="mm_d_merged")
        gbig["w_o"][l] = _mm_tn(sv["merged"], dr1b, "row", name="mm_g_wo")
        dp_sb, dp_ssm, dp_mem, dgl0, dgl1, dgl2 = _merge_bwd(
            dmerged, proj, gate_off, sv["p_sb"], sv["p_ssm"], sv["p_mem"])

        dsb = _mm_nt(dp_sb, Wg["sb_w_out"][l], "col", out_dtype=BF16, name="mm_d_sb")
        gbig["sb_w_out"][l] = _mm_tn(sv["sb"], dp_sb, "col", name="mm_g_sb_out")

        dzz = _mm_nt(dp_ssm, Wg["ssm_w_out"][l], "col", name="mm_d_zz")
        gbig["ssm_w_out"][l] = _mm_tn(sv["zz"], dp_ssm, "col", name="mm_g_ssm_out")
        dglu = _glu_bwd(dzz, sv["glu"])
        dgy = _mm_nt(dglu, Wg["ssm_w_glu"][l], "col", name="mm_d_gy")
        gbig["ssm_w_glu"][l] = _mm_tn(sv["gy"], dglu, "col", name="mm_g_glu")
        lre3, lim3, logdt, bre_t, bim_t, a_row, b_row, bmre, bmim, cmre, cmimn, d_row = sv["disc"]
        du, dy_ssm, g_re, g_im, dd, da, db = _s5_bwd(
            sv["u"], dgy, sv["y"], sv["hre"], sv["him"], a_row, b_row, bmre, bmim, cmre, cmimn, d_row)
        dbmre = _mm_tn(sv["u"], g_re, "plain", name="mm_g_ssm_bre")
        dbmim = _mm_tn(sv["u"], g_im, "plain", name="mm_g_ssm_bim")
        dcmre = _mm_tn(sv["hre"], dy_ssm, "plain", name="mm_g_ssm_cre")
        dcmimn = _mm_tn(sv["him"], dy_ssm, "plain", name="mm_g_ssm_cim")
        dlre, dlim, dlogdt, dbre_t, dbim_t = _s5_disc_bwd(
            lre3, lim3, logdt, bre_t, bim_t, da.reshape(G, 1, P), db.reshape(G, 1, P),
            _block_diag_take(dbmre, G), _block_diag_take(dbmim, G))
        gsmall["ssm_lambda_re"][l], gsmall["ssm_lambda_im"][l] = dlre.reshape(G, P), dlim.reshape(G, P)
        gsmall["ssm_log_dt"][l] = dlogdt.reshape(G)
        gsmall["ssm_b_re"][l] = dbre_t.transpose(0, 2, 1)
        gsmall["ssm_b_im"][l] = dbim_t.transpose(0, 2, 1)
        gsmall["ssm_c_re"][l] = _block_diag_take(dcmre, G).transpose(0, 2, 1)
        gsmall["ssm_c_im"][l] = -_block_diag_take(dcmimn, G).transpose(0, 2, 1)
        gsmall["ssm_d"][l] = dd[0]

        dmm = _mm_nt(dp_mem, Wg["mem_w_out"][l], "col", out_dtype=BF16, name="mm_d_mm")
        gbig["mem_w_out"][l] = _mm_tn(sv["mm_o"], dp_mem, "col", name="mm_g_mem_out")
        dqm, dkv = _mem_bwd(proj, qm_off, mw, sv["kv"], dmm)
        gbig["mem_w_kv"][l] = _mm_tn(mem2, dkv, "row", name="mm_g_kv")

        dq, dk, dv, arrived = _sb_bwd(proj, q_off, k_off, v_off, heads, dsb, sv["sb_ctot"], rs=pending)
        if pending is not None:
            from_chips[l + 1] = arrived
        dproj, dbin = _assemble_dproj([dq, dk, dv, du, dqm, dgl0, dgl1, dgl2])
        gsmall["b_in"][l] = dbin[0]
        dxl = _mm_nt(dproj, Wg["w_in"][l], "col", add=dr1, add_scale=alpha, name="mm_d_x")
        gbig["w_in"][l] = _mm_tn(sv["x"], dproj, "col", name="mm_g_win")

        above = [gbig[n][l] for n in BIG]

    sums = [_pair_add(g, r, place) for g, r in zip(above, _rs_pair(above))]
    own[0] = [s[1] for s in sums]
    from_chips[0] = _rs_chips([s[0] for s in sums])
    grad_x = dxl.reshape(x.shape)

    fulls = []
    for a, n in enumerate(BIG):
        full = lax.empty(W[n].shape, F32)
        for l in range(L - 1, -1, -1):
            full = _chip_sum(own[l][a], from_chips[l][a], full, l, place)
        fulls.append(full)
    reduced = _share_pair(fulls)
    grads = {n: reduced[i] for i, n in enumerate(BIG)}

    small_local = [jnp.stack(gsmall[n]) for n in SMALL]
    packed = _pack(small_local + [loss_part[0, :1]])
    total = _small_allreduce(packed)
    unpacked = _unpack(total, small_local + [loss_part[0, :1]])
    for n, g in zip(SMALL, unpacked[:-1]):
        grads[n] = g
    loss = unpacked[-1][0]

    delta, new_m, new_v = {}, {}, {}
    for n in BIG:
        delta[n], new_m[n], new_v[n] = _adamw(W[n], grads[n], M1[n], V1[n])
    sm = _adamw(_pack([W[n] for n in SMALL]), _pack([grads[n] for n in SMALL]),
                _pack([M1[n] for n in SMALL]), _pack([V1[n] for n in SMALL]))
    like = [W[n] for n in SMALL]
    for n, d, m_, v_ in zip(SMALL, _unpack(sm[0], like), _unpack(sm[1], like), _unpack(sm[2], like)):
        delta[n], new_m[n], new_v[n] = d, m_, v_

    return (loss, grad_x, *[grads[n] for n in WEIGHTS], *[delta[n] for n in WEIGHTS],
            *[new_m[n] for n in WEIGHTS], *[new_v[n] for n in WEIGHTS])


def kernel(x, mem, w_in, b_in, sb_w_out, ssm_lambda_re, ssm_lambda_im, ssm_log_dt, ssm_b_re, ssm_b_im, ssm_c_re, ssm_c_im, ssm_d, ssm_w_glu, ssm_w_out, mem_w_kv, mem_w_out, w_o, ln1_g, ln1_b, ffn_w_gate_up, ffn_w_down, ln2_g, ln2_b, loss_target, m_w_in, m_b_in, m_sb_w_out, m_ssm_lambda_re, m_ssm_lambda_im, m_ssm_log_dt, m_ssm_b_re, m_ssm_b_im, m_ssm_c_re, m_ssm_c_im, m_ssm_d, m_ssm_w_glu, m_ssm_w_out, m_mem_w_kv, m_mem_w_out, m_w_o, m_ln1_g, m_ln1_b, m_ffn_w_gate_up, m_ffn_w_down, m_ln2_g, m_ln2_b, v_w_in, v_b_in, v_sb_w_out, v_ssm_lambda_re, v_ssm_lambda_im, v_ssm_log_dt, v_ssm_b_re, v_ssm_b_im, v_ssm_c_re, v_ssm_c_im, v_ssm_d, v_ssm_w_glu, v_ssm_w_out, v_mem_w_kv, v_mem_w_out, v_w_o, v_ln1_g, v_ln1_b, v_ffn_w_gate_up, v_ffn_w_down, v_ln2_g, v_ln2_b):
    W = dict(w_in=w_in, b_in=b_in, sb_w_out=sb_w_out, ssm_lambda_re=ssm_lambda_re, ssm_lambda_im=ssm_lambda_im,
             ssm_log_dt=ssm_log_dt, ssm_b_re=ssm_b_re, ssm_b_im=ssm_b_im, ssm_c_re=ssm_c_re, ssm_c_im=ssm_c_im,
             ssm_d=ssm_d, ssm_w_glu=ssm_w_glu, ssm_w_out=ssm_w_out, mem_w_kv=mem_w_kv, mem_w_out=mem_w_out,
             w_o=w_o, ln1_g=ln1_g, ln1_b=ln1_b, ffn_w_gate_up=ffn_w_gate_up, ffn_w_down=ffn_w_down,
             ln2_g=ln2_g, ln2_b=ln2_b)
    M1 = dict(w_in=m_w_in, b_in=m_b_in, sb_w_out=m_sb_w_out, ssm_lambda_re=m_ssm_lambda_re,
              ssm_lambda_im=m_ssm_lambda_im, ssm_log_dt=m_ssm_log_dt, ssm_b_re=m_ssm_b_re, ssm_b_im=m_ssm_b_im,
              ssm_c_re=m_ssm_c_re, ssm_c_im=m_ssm_c_im, ssm_d=m_ssm_d, ssm_w_glu=m_ssm_w_glu,
              ssm_w_out=m_ssm_w_out, mem_w_kv=m_mem_w_kv, mem_w_out=m_mem_w_out, w_o=m_w_o, ln1_g=m_ln1_g,
              ln1_b=m_ln1_b, ffn_w_gate_up=m_ffn_w_gate_up, ffn_w_down=m_ffn_w_down, ln2_g=m_ln2_g, ln2_b=m_ln2_b)
    V1 = dict(w_in=v_w_in, b_in=v_b_in, sb_w_out=v_sb_w_out, ssm_lambda_re=v_ssm_lambda_re,
              ssm_lambda_im=v_ssm_lambda_im, ssm_log_dt=v_ssm_log_dt, ssm_b_re=v_ssm_b_re, ssm_b_im=v_ssm_b_im,
              ssm_c_re=v_ssm_c_re, ssm_c_im=v_ssm_c_im, ssm_d=v_ssm_d, ssm_w_glu=v_ssm_w_glu,
              ssm_w_out=v_ssm_w_out, mem_w_kv=v_mem_w_kv, mem_w_out=v_mem_w_out, w_o=v_w_o, ln1_g=v_ln1_g,
              ln1_b=v_ln1_b, ffn_w_gate_up=v_ffn_w_gate_up, ffn_w_down=v_ffn_w_down, ln2_g=v_ln2_g, ln2_b=v_ln2_b)
    return _step(x, mem, loss_target, W, M1, V1)
```

```python
import functools
import math

import jax
import jax.numpy as jnp
from jax import lax
from jax.experimental import pallas as pl
from jax.experimental.pallas import tpu as pltpu

F32 = jnp.float32
BF16 = jnp.bfloat16
MESH = pl.DeviceIdType.MESH
ANY = pl.BlockSpec(memory_space=pl.ANY)
VMEM = pl.BlockSpec(memory_space=pltpu.VMEM)

HEAD_DIM = 128
SSM_GROUP = 16
N_CHIPS = 4
N_DEV = 8
LN_EPS = 1e-5
ADAM_LR = 0.001
ADAM_B1 = 0.9
ADAM_B2 = 0.999
ADAM_EPS = 1e-08
ADAM_WD = 0.01
ADAM_STEP = 10
LANES = 128
SUBLANES = 8
VMEM_LIMIT_MB = 56


def _cparams(sem, mb=VMEM_LIMIT_MB):
    return pltpu.CompilerParams(dimension_semantics=sem, vmem_limit_bytes=mb * 1024 * 1024)


def _tile(n, pref, mult=LANES):
    best = None
    t = mult
    while t <= min(n, pref):
        if n % t == 0:
            best = t
        t += mult
    return n if best is None else best


def _dot(a, b, dims=(((1,), (0,)), ((), ()))):
    return lax.dot_general(a, b, dims, preferred_element_type=F32)


NT = (((1,), (1,)), ((), ()))
TN = (((0,), (0,)), ((), ()))


def _split2(x):
    hi = x.astype(BF16)
    lo = (x - hi.astype(F32)).astype(BF16)
    return hi, lo


def _dot_a2(a, b, dims=(((1,), (0,)), ((), ()))):
    ah, al = _split2(a)
    bb = b.astype(BF16)
    return _dot(ah, bb, dims) + _dot(al, bb, dims)


def _bd_blocks(n_in, n_out):
    if (n_in // 2) % LANES == 0 and (n_out // 2) % LANES == 0:
        return [(slice(0, n_in // 2), slice(0, n_out // 2)), (slice(n_in // 2, n_in), slice(n_out // 2, n_out))]
    return [(slice(0, n_in), slice(0, n_out))]


def _dot_mask(x, u2):
    hi, lo = _split2(x)
    return _dot(jnp.concatenate([hi, lo], axis=1), u2)


def _tri2(n, rel):
    ri = lax.broadcasted_iota(jnp.int32, (2 * n, n), 0)
    ci = lax.broadcasted_iota(jnp.int32, (2 * n, n), 1)
    ri = jnp.where(ri >= n, ri - n, ri)
    return rel(ri, ci).astype(BF16)


def _mm_call(a, b, *, dims, grid, a_spec, b_spec, out_spec, out_shape, name,
             bias=None, bias_spec=None, add=None, add_spec=None, add_scale=1.0, comm=None, prod=None):
    nk = grid[2]
    has_bias = bias is not None
    has_add = add is not None
    n_ci = len(comm["ins"]) if comm else 0
    n_co = len(comm["out_shapes"]) if comm else 0

    def body(*refs):
        a_ref, b_ref = refs[0], refs[1]
        pos = 2
        bias_ref = refs[pos] if has_bias else None
        pos += int(has_bias)
        add_ref = refs[pos] if has_add else None
        pos += int(has_add)
        comm_in = refs[pos:pos + n_ci]
        pos += n_ci
        o_ref = refs[pos]
        comm_out = refs[pos + 1:pos + 1 + n_co]
        pos += 1 + n_co
        acc_ref = refs[pos] if nk > 1 else None
        pos += int(nk > 1)
        if comm:
            ex_start, ex_middle, ex_finish = comm["hooks"](comm_in, comm_out, refs[pos], refs[pos + 1])
            ids = [pl.program_id(d) for d in range(3)]

            @pl.when((ids[0] == 0) & (ids[1] == 0) & (ids[2] == 0))
            def _():
                ex_start()

        if prod is None:
            p = _dot(a_ref[...].astype(BF16), b_ref[...].astype(BF16), dims)
        else:
            p = prod(a_ref, b_ref)

        def finish(acc):
            if has_bias:
                acc = acc + bias_ref[...]
            if has_add:
                acc = acc + add_scale * add_ref[...].astype(F32)
            o_ref[...] = acc.astype(o_ref.dtype)

        if nk == 1:
            finish(p)
        else:
            k = pl.program_id(2)

            @pl.when(k == 0)
            def _():
                acc_ref[...] = p

            @pl.when(k > 0)
            def _():
                acc_ref[...] += p

            @pl.when(k == nk - 1)
            def _():
                finish(acc_ref[...])

        if comm:
            @pl.when((ids[0] == grid[0] - 1) & (ids[1] == grid[1] - 1) & (ids[2] == grid[2] - 1))
            def _():
                if ex_middle is not None:
                    ex_middle()
                ex_finish()

    ins = [a, b]
    in_specs = [a_spec, b_spec]
    if has_bias:
        ins.append(bias)
        in_specs.append(bias_spec)
    if has_add:
        ins.append(add)
        in_specs.append(add_spec)
    scratch = []
    if nk > 1:
        blk = [d for d in out_spec.block_shape if d is not None]
        scratch.append(pltpu.VMEM(tuple(blk), F32))
    if not comm:
        return pl.pallas_call(
            body, out_shape=out_shape, grid=grid, in_specs=in_specs, out_specs=out_spec,
            scratch_shapes=scratch, name=name,
            compiler_params=_cparams(("parallel", "parallel", "arbitrary")),
        )(*ins)
    n_main = len(ins)
    scratch += [pltpu.SemaphoreType.DMA((comm["n_sems"],)), pltpu.SemaphoreType.DMA((comm["n_sems"],))]
    res = pl.pallas_call(
        body, out_shape=[out_shape] + list(comm["out_shapes"]), grid=grid,
        in_specs=in_specs + [ANY] * n_ci, out_specs=[out_spec] + [ANY] * n_co, scratch_shapes=scratch,
        input_output_aliases={n_main + i: 1 + i for i in range(n_ci)} if comm["alias"] else {},
        name=name, compiler_params=_cparams(("arbitrary", "arbitrary", "arbitrary")),
    )(*ins, *comm["ins"])
    return res[0], list(res[1:])


def _mm_nn(a, w, kind, *, bias=None, out_dtype=F32, name, comm=None):
    M, K = a.shape
    tm = _tile(M, 1024, SUBLANES)
    tk = K if K <= 2048 else _tile(K, 1408)
    if kind == "col":
        Nc = w.shape[2]
        N = N_CHIPS * Nc
        tn = _tile(Nc, 1408)
        npc = Nc // tn
        b_spec = pl.BlockSpec((None, tk, tn), lambda i, j, k: (j // npc, k, j % npc))
    else:
        w = w.reshape(-1, w.shape[-1])
        N = w.shape[1]
        tn = _tile(N, 2048)
        if K > 2048 and a.dtype == BF16:
            tk, tn = K, _tile(N, 512)
        b_spec = pl.BlockSpec((tk, tn), lambda i, j, k: (k, j))
    grid = (M // tm, N // tn, K // tk)
    bias_spec = pl.BlockSpec((1, tn), lambda i, j, k: (0, j)) if bias is not None else None
    return _mm_call(
        a, w, dims=(((1,), (0,)), ((), ())), grid=grid,
        a_spec=pl.BlockSpec((tm, tk), lambda i, j, k: (i, k)), b_spec=b_spec,
        out_spec=pl.BlockSpec((tm, tn), lambda i, j, k: (i, j)),
        out_shape=jax.ShapeDtypeStruct((M, N), out_dtype), name=name,
        bias=bias, bias_spec=bias_spec, comm=comm)


def _mm_nt(dy, w, kind, *, add=None, add_scale=1.0, out_dtype=F32, name, comm=None):
    M, N = dy.shape
    tm = _tile(M, 512, SUBLANES)
    if kind == "col" and dy.dtype == BF16:
        K, Nc = w.shape[1], w.shape[2]
        cps = 2
        tm = _tile(M, 1024, SUBLANES)
        tko = _tile(K, 512)

        def prod(a_ref, b_ref):
            acc = None
            for c in range(cps):
                t = _dot(a_ref[:, c * Nc:(c + 1) * Nc], b_ref[c], NT)
                acc = t if acc is None else acc + t
            return acc

        add_spec = pl.BlockSpec((tm, tko), lambda i, j, r: (i, j)) if add is not None else None
        return _mm_call(
            dy, w, dims=NT, grid=(M // tm, K // tko, N_CHIPS // cps), prod=prod,
            a_spec=pl.BlockSpec((tm, cps * Nc), lambda i, j, r: (i, r)),
            b_spec=pl.BlockSpec((cps, tko, Nc), lambda i, j, r: (r, j, 0)),
            out_spec=pl.BlockSpec((tm, tko), lambda i, j, r: (i, j)),
            out_shape=jax.ShapeDtypeStruct((M, K), out_dtype), name=name,
            add=add, add_spec=add_spec, add_scale=add_scale, comm=comm)
    if kind == "col":
        K, Nc = w.shape[1], w.shape[2]
        tn = _tile(Nc, 1408)
        npc = Nc // tn
        tko = _tile(K, 2048)
        b_spec = pl.BlockSpec((None, tko, tn), lambda i, j, r: (r // npc, j, r % npc))
    else:
        w = w.reshape(-1, w.shape[-1])
        K = w.shape[0]
        tn = _tile(N, 2048)
        tko = _tile(K, 2048)
        b_spec = pl.BlockSpec((tko, tn), lambda i, j, r: (j, r))
    grid = (M // tm, K // tko, N // tn)
    add_spec = pl.BlockSpec((tm, tko), lambda i, j, r: (i, j)) if add is not None else None
    return _mm_call(
        dy, w, dims=NT, grid=grid,
        a_spec=pl.BlockSpec((tm, tn), lambda i, j, r: (i, r)), b_spec=b_spec,
        out_spec=pl.BlockSpec((tm, tko), lambda i, j, r: (i, j)),
        out_shape=jax.ShapeDtypeStruct((M, K), out_dtype), name=name,
        add=add, add_spec=add_spec, add_scale=add_scale, comm=comm)


def _mm_tn(a, dy, kind, *, name):
    M, K = a.shape
    N = dy.shape[1]
    tm = M if a.dtype == BF16 and dy.dtype == BF16 else _tile(M, 2048, SUBLANES)
    tkw = _tile(K, 512)
    tn = _tile(N // N_CHIPS, 1408) if kind == "col" else _tile(N, 1024)
    a_inner = tkw * a.dtype.itemsize <= tn * dy.dtype.itemsize
    ij = (lambda o, n: (n, o)) if a_inner else (lambda o, n: (o, n))
    if kind == "col":
        Nc = N // N_CHIPS
        npc = Nc // tn

        def out_map(o, n, m):
            i, j = ij(o, n)
            return (j // npc, i, j % npc)

        out_spec = pl.BlockSpec((None, tkw, tn), out_map)
        out_shape = jax.ShapeDtypeStruct((N_CHIPS, K, Nc), F32)
    else:
        out_spec = pl.BlockSpec((tkw, tn), lambda o, n, m: ij(o, n))
        out_shape = jax.ShapeDtypeStruct((K, N), F32)
    ni, nj = K // tkw, N // tn
    grid = (nj, ni, M // tm) if a_inner else (ni, nj, M // tm)
    out = _mm_call(
        a, dy, dims=TN, grid=grid,
        a_spec=pl.BlockSpec((tm, tkw), lambda o, n, m: (m, ij(o, n)[0])),
        b_spec=pl.BlockSpec((tm, tn), lambda o, n, m: (m, ij(o, n)[1])),
        out_spec=out_spec, out_shape=out_shape, name=name)
    if kind == "row":
        out = out.reshape(N_CHIPS, K // N_CHIPS, N)
    return out


def _gelu_grad(x):
    k = math.sqrt(2.0 / math.pi)
    inner = k * (x + 0.044715 * x * x * x)
    t = jnp.tanh(inner)
    return 0.5 * (1.0 + t) + 0.5 * x * (1.0 - t * t) * k * (1.0 + 3.0 * 0.044715 * x * x)


def _ln_fwd(xin, delta, g, b, alpha):
    S, D = xin.shape
    ts = _tile(S, 256, SUBLANES)

    def body(x_ref, d_ref, g_ref, b_ref, y_ref, yb_ref, xh_ref, rs_ref):
        r = alpha * x_ref[...] + d_ref[...]
        mu = jnp.mean(r, axis=-1, keepdims=True)
        rc = r - mu
        var = jnp.mean(rc * rc, axis=-1, keepdims=True)
        rstd = lax.rsqrt(var + LN_EPS)
        xh = rc * rstd
        y = xh * g_ref[...] + b_ref[...]
        y_ref[...] = y
        yb_ref[...] = y.astype(BF16)
        xh_ref[...] = xh
        rs_ref[...] = rstd

    row = pl.BlockSpec((ts, D), lambda i: (i, 0))
    vec = pl.BlockSpec((1, D), lambda i: (0, 0))
    return pl.pallas_call(
        body, grid=(S // ts,), in_specs=[row, row, vec, vec],
        out_specs=[row, row, row, pl.BlockSpec((ts, 1), lambda i: (i, 0))],
        out_shape=[jax.ShapeDtypeStruct((S, D), F32), jax.ShapeDtypeStruct((S, D), BF16),
                   jax.ShapeDtypeStruct((S, D), F32), jax.ShapeDtypeStruct((S, 1), F32)],
        name="ln_fwd", compiler_params=_cparams(("parallel",)),
    )(xin, delta, g, b)


def _ln_bwd(dy, xh, rstd, g):
    S, D = dy.shape
    ts = _tile(S, 256, SUBLANES)

    def body(dy_ref, xh_ref, rs_ref, g_ref, dr_ref, drb_ref, dg_ref, db_ref):
        @pl.when(pl.program_id(0) == 0)
        def _():
            dg_ref[...] = jnp.zeros_like(dg_ref)
            db_ref[...] = jnp.zeros_like(db_ref)

        dyv = dy_ref[...]
        xhv = xh_ref[...]
        dyg = dyv * g_ref[...]
        m1 = jnp.mean(dyg, axis=-1, keepdims=True)
        m2 = jnp.mean(dyg * xhv, axis=-1, keepdims=True)
        dr = rs_ref[...] * (dyg - m1 - xhv * m2)
        dr_ref[...] = dr
        drb_ref[...] = dr.astype(BF16)
        dg_ref[...] += jnp.sum(dyv * xhv, axis=0, keepdims=True)
        db_ref[...] += jnp.sum(dyv, axis=0, keepdims=True)

    row = pl.BlockSpec((ts, D), lambda i: (i, 0))
    vec = pl.BlockSpec((1, D), lambda i: (0, 0))
    return pl.pallas_call(
        body, grid=(S // ts,),
        in_specs=[row, row, pl.BlockSpec((ts, 1), lambda i: (i, 0)), vec],
        out_specs=[row, row, vec, vec],
        out_shape=[jax.ShapeDtypeStruct((S, D), F32), jax.ShapeDtypeStruct((S, D), BF16),
                   jax.ShapeDtypeStruct((1, D), F32), jax.ShapeDtypeStruct((1, D), F32)],
        name="ln_bwd", compiler_params=_cparams(("arbitrary",)),
    )(dy, xh, rstd, g)


def _merge_fwd(proj, gate_off, p_sb, p_ssm, p_mem):
    S, D = p_sb.shape
    ts = _tile(S, 256, SUBLANES)
    gb = gate_off // D

    def body(g0, g1, g2, a0, a1, a2, o_ref):
        o_ref[...] = (jax.nn.sigmoid(g0[...]) * a0[...] + jax.nn.sigmoid(g1[...]) * a1[...]
                      + jax.nn.sigmoid(g2[...]) * a2[...]).astype(o_ref.dtype)

    row = pl.BlockSpec((ts, D), lambda i: (i, 0))
    gates = [pl.BlockSpec((ts, D), functools.partial(lambda i, n: (i, gb + n), n=n)) for n in range(3)]
    return pl.pallas_call(
        body, grid=(S // ts,), in_specs=gates + [row, row, row], out_specs=row,
        out_shape=jax.ShapeDtypeStruct((S, D), BF16), name="merge_fwd",
        compiler_params=_cparams(("parallel",)),
    )(proj, proj, proj, p_sb, p_ssm, p_mem)


def _merge_bwd(dmerged, proj, gate_off, p_sb, p_ssm, p_mem):
    S, D = p_sb.shape
    ts = _tile(S, 256, SUBLANES)
    gb = gate_off // D

    def body(dm_ref, g0, g1, g2, a0, a1, a2, d0, d1, d2, l0, l1, l2):
        dm = dm_ref[...]
        for g_ref, a_ref, d_ref, l_ref in ((g0, a0, d0, l0), (g1, a1, d1, l1), (g2, a2, d2, l2)):
            s = jax.nn.sigmoid(g_ref[...])
            d_ref[...] = (dm * s).astype(d_ref.dtype)
            l_ref[...] = dm * a_ref[...] * s * (1.0 - s)

    row = pl.BlockSpec((ts, D), lambda i: (i, 0))
    gates = [pl.BlockSpec((ts, D), functools.partial(lambda i, n: (i, gb + n), n=n)) for n in range(3)]
    sd = jax.ShapeDtypeStruct((S, D), F32)
    return pl.pallas_call(
        body, grid=(S // ts,), in_specs=[row] + gates + [row, row, row], out_specs=[row] * 6,
        out_shape=[jax.ShapeDtypeStruct((S, D), BF16)] * 3 + [sd] * 3, name="merge_bwd", compiler_params=_cparams(("parallel",)),
    )(dmerged, proj, proj, proj, p_sb, p_ssm, p_mem)


def _glu_fwd(glu):
    S, W2 = glu.shape
    W = W2 // 2
    ts = _tile(S, 512, SUBLANES)

    def body(x_ref, o_ref):
        o_ref[...] = (x_ref[:, :W] * jax.nn.sigmoid(x_ref[:, W:])).astype(o_ref.dtype)

    return pl.pallas_call(
        body, grid=(S // ts,), in_specs=[pl.BlockSpec((ts, W2), lambda i: (i, 0))],
        out_specs=pl.BlockSpec((ts, W), lambda i: (i, 0)),
        out_shape=jax.ShapeDtypeStruct((S, W), BF16), name="glu_fwd",
        compiler_params=_cparams(("parallel",)),
    )(glu)


def _glu_bwd(dzz, glu):
    S, W2 = glu.shape
    W = W2 // 2
    ts = _tile(S, 512, SUBLANES)

    def body(d_ref, x_ref, o_ref):
        d = d_ref[...]
        a = x_ref[:, :W]
        s = jax.nn.sigmoid(x_ref[:, W:])
        o_ref[:, :W] = (d * s).astype(o_ref.dtype)
        o_ref[:, W:] = (d * a * s * (1.0 - s)).astype(o_ref.dtype)

    return pl.pallas_call(
        body, grid=(S // ts,),
        in_specs=[pl.BlockSpec((ts, W), lambda i: (i, 0)), pl.BlockSpec((ts, W2), lambda i: (i, 0))],
        out_specs=pl.BlockSpec((ts, W2), lambda i: (i, 0)),
        out_shape=jax.ShapeDtypeStruct((S, W2), BF16), name="glu_bwd",
        compiler_params=_cparams(("parallel",)),
    )(dzz, glu)


def _swiglu_fwd(gu):
    S, F2 = gu.shape
    Fh = F2 // 2
    ts = _tile(S, 128, SUBLANES)

    def body(x_ref, o_ref):
        fg = x_ref[:, :Fh]
        o_ref[...] = (fg * jax.nn.sigmoid(fg) * x_ref[:, Fh:]).astype(o_ref.dtype)

    return pl.pallas_call(
        body, grid=(S // ts,), in_specs=[pl.BlockSpec((ts, F2), lambda i: (i, 0))],
        out_specs=pl.BlockSpec((ts, Fh), lambda i: (i, 0)),
        out_shape=jax.ShapeDtypeStruct((S, Fh), BF16), name="swiglu_fwd",
        compiler_params=_cparams(("parallel",)),
    )(gu)


def _swiglu_bwd(dhid, gu):
    S, F2 = gu.shape
    Fh = F2 // 2
    ts = _tile(S, 128, SUBLANES)

    def body(d_ref, x_ref, o_ref):
        d = d_ref[...]
        fg = x_ref[:, :Fh]
        fu = x_ref[:, Fh:]
        s = jax.nn.sigmoid(fg)
        o_ref[:, :Fh] = (d * fu * s * (1.0 + fg * (1.0 - s))).astype(o_ref.dtype)
        o_ref[:, Fh:] = (d * fg * s).astype(o_ref.dtype)

    return pl.pallas_call(
        body, grid=(S // ts,),
        in_specs=[pl.BlockSpec((ts, Fh), lambda i: (i, 0)), pl.BlockSpec((ts, F2), lambda i: (i, 0))],
        out_specs=pl.BlockSpec((ts, F2), lambda i: (i, 0)),
        out_shape=jax.ShapeDtypeStruct((S, F2), BF16), name="swiglu_bwd",
        compiler_params=_cparams(("parallel",)),
    )(dhid, gu)


def _assemble_dproj(pieces):
    S = pieces[0].shape[0]
    widths = [p.shape[1] for p in pieces]
    total = sum(widths)
    ts = _tile(S, 128, SUBLANES)
    n = len(pieces)

    def body(*refs):
        o_ref, b_ref = refs[n], refs[n + 1]

        @pl.when(pl.program_id(0) == 0)
        def _():
            b_ref[...] = jnp.zeros_like(b_ref)

        off = 0
        for r, w in zip(refs[:n], widths):
            v = r[...].astype(F32)
            o_ref[:, off:off + w] = v.astype(o_ref.dtype)
            b_ref[:, off:off + w] += jnp.sum(v, axis=0, keepdims=True)
            off += w

    return pl.pallas_call(
        body, grid=(S // ts,),
        in_specs=[pl.BlockSpec((ts, w), lambda i: (i, 0)) for w in widths],
        out_specs=[pl.BlockSpec((ts, total), lambda i: (i, 0)), pl.BlockSpec((1, total), lambda i: (0, 0))],
        out_shape=[jax.ShapeDtypeStruct((S, total), BF16), jax.ShapeDtypeStruct((1, total), F32)],
        name="assemble_dproj", compiler_params=_cparams(("arbitrary",)),
    )(*pieces)


def _loss_head(y, target):
    S, D = y.shape
    ts = _tile(S, 256, SUBLANES)

    def body(y_ref, t_ref, dy_ref, l_ref):
        @pl.when(pl.program_id(0) == 0)
        def _():
            l_ref[...] = jnp.zeros_like(l_ref)

        e = y_ref[...] - t_ref[...]
        dy_ref[...] = e * (1.0 / D)
        part = jnp.sum(jnp.sum(e * e, axis=1, keepdims=True), axis=0, keepdims=True) * (0.5 / D)
        l_ref[...] += jnp.broadcast_to(part, l_ref.shape)

    row = pl.BlockSpec((ts, D), lambda i: (i, 0))
    return pl.pallas_call(
        body, grid=(S // ts,), in_specs=[row, row],
        out_specs=[row, pl.BlockSpec((1, LANES), lambda i: (0, 0))],
        out_shape=[jax.ShapeDtypeStruct((S, D), F32), jax.ShapeDtypeStruct((1, LANES), F32)],
        name="loss_head", compiler_params=_cparams(("arbitrary",)),
    )(y, target)


SB_TQ = 512
SB_TK = 512


def _sb_tile_terms(q, kb, scale, causal):
    z = _dot(q, kb, NT) * scale
    soft = jnp.log(1.0 + jnp.exp(-jnp.abs(z)))
    ls = jnp.minimum(z, 0.0) - soft
    l1m = jnp.minimum(-z, 0.0) - soft
    if causal is not None:
        l1m = jnp.where(causal, l1m, 0.0)
    return ls, l1m


def _sb_causal(qi, kj, TQ, TK):
    t_idx = qi * TQ + lax.broadcasted_iota(jnp.int32, (TQ, TK), 0)
    s_idx = kj * TK + lax.broadcasted_iota(jnp.int32, (TQ, TK), 1)
    return s_idx < t_idx


def _exchange_begin(heads, nq, start, middle=None):
    h, qi = pl.program_id(0), pl.program_id(1)

    @pl.when((h == 0) & (qi == 0))
    def _():
        start()

    if middle is not None:
        @pl.when((h == heads - 1) & (qi == nq - 1))
        def _():
            middle()


def _exchange_end(heads, nq, finish):
    @pl.when((pl.program_id(0) == heads - 1) & (pl.program_id(1) == nq - 1))
    def _():
        finish()


def _sb_fwd(proj, q_off, k_off, v_off, heads, ag=None):
    S = proj.shape[0]
    Dh = HEAD_DIM
    TQ = min(SB_TQ, S)
    TK = min(SB_TK, TQ)
    nq = S // TQ
    scale = Dh ** -0.5
    qb, kb0, vb0 = q_off // Dh, k_off // Dh, v_off // Dh
    n_ag = 0 if ag is None else len(ag)
    assert ag is None or heads >= 2

    def body(q_ref, k_ref, v_ref, *rest):
        o_ref, c_ref = rest[n_ag:n_ag + 2]
        if n_ag:
            ag_start, ag_middle, ag_finish = _ag_hooks(rest[n_ag + 2:2 * n_ag + 2], *rest[2 * n_ag + 2:])
            _exchange_begin(heads, nq, ag_start, ag_middle)
        qi = pl.program_id(1)
        q = q_ref[...].astype(BF16)
        upper = _tri2(TK, lambda j, s: j > s)
        nfull = (qi * TQ) // TK

        def block(kj, carry, masked):
            c, acc = carry
            off = pl.multiple_of(kj * TK, TK)
            kblk = k_ref[pl.ds(off, TK), :].astype(BF16)
            vblk = v_ref[pl.ds(off, TK), :].astype(BF16)
            causal = _sb_causal(qi, kj, TQ, TK) if masked else None
            ls, l1m = _sb_tile_terms(q, kblk, scale, causal)
            w = jnp.exp(ls + _dot_mask(l1m, upper) + c)
            if masked:
                w = jnp.where(causal, w, 0.0)
            acc = acc + _dot(w.astype(BF16), vblk)
            c = c + jnp.sum(l1m, axis=1, keepdims=True)
            return c, acc

        carry = (jnp.zeros((TQ, 1), F32), jnp.zeros((TQ, Dh), F32))
        for d in range(TQ // TK - 1, -1, -1):
            carry = block(nfull + d, carry, True)

        c, acc = lax.fori_loop(0, nfull, lambda jj, carry: block(nfull - 1 - jj, carry, False), carry)
        o_ref[...] = acc.astype(o_ref.dtype)
        c_ref[...] = c
        if n_ag:
            _exchange_end(heads, nq, ag_finish)

    ag = [] if ag is None else list(ag)
    res = pl.pallas_call(
        body, grid=(heads, nq),
        in_specs=[pl.BlockSpec((TQ, Dh), lambda h, i: (i, qb + h)),
                  pl.BlockSpec((S, Dh), lambda h, i: (0, kb0 + h)),
                  pl.BlockSpec((S, Dh), lambda h, i: (0, vb0 + h))] + [ANY] * n_ag,
        out_specs=[pl.BlockSpec((TQ, Dh), lambda h, i: (i, h)),
                   pl.BlockSpec((None, TQ, 1), lambda h, i: (h, i, 0))] + [ANY] * n_ag,
        out_shape=[jax.ShapeDtypeStruct((S, heads * Dh), BF16), jax.ShapeDtypeStruct((heads, S, 1), F32)]
        + [jax.ShapeDtypeStruct(b.shape, b.dtype) for b in ag],
        input_output_aliases={3 + i: 2 + i for i in range(n_ag)},
        scratch_shapes=[pltpu.SemaphoreType.DMA((n_ag * 6,)), pltpu.SemaphoreType.DMA((n_ag * 6,))] if n_ag else [],
        name="sb_fwd_ag" if n_ag else "sb_fwd",
        compiler_params=_cparams(("arbitrary", "arbitrary") if n_ag else ("parallel", "arbitrary")),
    )(proj, proj, proj, *ag)
    return res[0], res[1], list(res[2:])


def _sb_bwd(proj, q_off, k_off, v_off, heads, dout, ctot, rs=None):
    S = proj.shape[0]
    Dh = HEAD_DIM
    TQ = min(SB_TQ, S)
    TK = min(SB_TK, TQ)
    nq = S // TQ
    scale = Dh ** -0.5
    qb, kb0, vb0 = q_off // Dh, k_off // Dh, v_off // Dh
    n_rs = 0 if rs is None else len(rs)

    def body(q_ref, k_ref, v_ref, do_ref, c_ref, *rest):
        dq_ref, dk_ref, dv_ref = rest[n_rs:n_rs + 3]
        if n_rs:
            rs_start, rs_finish = _rs_chips_hooks(rest[:n_rs], rest[n_rs + 3:2 * n_rs + 3], *rest[2 * n_rs + 3:])
            _exchange_begin(heads, nq, rs_start)
        qi = pl.program_id(1)

        @pl.when(qi == 0)
        def _():
            dk_ref[...] = jnp.zeros_like(dk_ref)
            dv_ref[...] = jnp.zeros_like(dv_ref)

        q = q_ref[...].astype(BF16)
        do = do_ref[...].astype(BF16)
        ctot = c_ref[...]
        lower_incl = _tri2(TK, lambda j, s: j <= s)
        lower = _tri2(TK, lambda j, s: j < s)
        nfull = (qi * TQ) // TK

        def block(kj, carry, masked):
            cl, ce, dq = carry
            off = pl.multiple_of(kj * TK, TK)
            kblk = k_ref[pl.ds(off, TK), :].astype(BF16)
            vblk = v_ref[pl.ds(off, TK), :].astype(BF16)
            causal = _sb_causal(qi, kj, TQ, TK) if masked else None
            ls, l1m = _sb_tile_terms(q, kblk, scale, causal)
            w = jnp.exp(ls + (ctot - cl - _dot_mask(l1m, lower_incl)))
            if masked:
                w = jnp.where(causal, w, 0.0)
            e = w * _dot(do, vblk, NT)
            before = ce + _dot_mask(e, lower)
            beta = jnp.exp(ls)
            dz = (e * (1.0 - beta) - beta * before) * scale
            if masked:
                dz = jnp.where(causal, dz, 0.0)
            dzb = dz.astype(BF16)
            dq = dq + _dot(dzb, kblk)
            dk_ref[pl.ds(off, TK), :] += _dot(dzb, q, TN)
            dv_ref[pl.ds(off, TK), :] += _dot(w.astype(BF16), do, TN)
            cl = cl + jnp.sum(l1m, axis=1, keepdims=True)
            ce = ce + jnp.sum(e, axis=1, keepdims=True)
            return cl, ce, dq

        zero = jnp.zeros((TQ, 1), F32)
        carry = lax.fori_loop(0, nfull, lambda kj, carry: block(kj, carry, False),
                              (zero, zero, jnp.zeros((TQ, Dh), F32)))
        for d in range(TQ // TK):
            carry = block(nfull + d, carry, True)
        dq_ref[...] = carry[2]
        if n_rs:
            _exchange_end(heads, nq, rs_finish)

    rs = [] if rs is None else list(rs)
    blk = pl.BlockSpec((TQ, Dh), lambda h, i: (i, h))
    col = pl.BlockSpec((S, Dh), lambda h, i: (0, h))
    sd = jax.ShapeDtypeStruct((S, heads * Dh), F32)
    res = pl.pallas_call(
        body, grid=(heads, nq),
        in_specs=[pl.BlockSpec((TQ, Dh), lambda h, i: (i, qb + h)),
                  pl.BlockSpec((S, Dh), lambda h, i: (0, kb0 + h)),
                  pl.BlockSpec((S, Dh), lambda h, i: (0, vb0 + h)), blk,
                  pl.BlockSpec((None, TQ, 1), lambda h, i: (h, i, 0))] + [ANY] * n_rs,
        out_specs=[blk, col, col] + [ANY] * n_rs, out_shape=[sd, sd, sd] + _rs_chips_shapes(rs),
        scratch_shapes=[pltpu.SemaphoreType.DMA((n_rs * 3,)), pltpu.SemaphoreType.DMA((n_rs * 3,))] if n_rs else [],
        name="sb_bwd_rs" if n_rs else "sb_bwd",
        compiler_params=_cparams(("arbitrary", "arbitrary") if n_rs else ("parallel", "arbitrary")),
    )(proj, proj, proj, dout, ctot, *rs)
    return res[0], res[1], res[2], list(res[3:])


def _mem_probs(qh, kh, scale):
    s = _dot(qh, kh, NT) * scale
    m = jnp.max(s, axis=-1, keepdims=True)
    p = jnp.exp(s - m)
    return p / jnp.sum(p, axis=-1, keepdims=True)


def _mem_fwd(proj, q_off, width, kv):
    S = proj.shape[0]
    Dh = HEAD_DIM
    heads = width // Dh
    ts = _tile(S, 512, SUBLANES)
    scale = Dh ** -0.5
    M = kv.shape[0]

    def body(q_ref, kv_ref, o_ref):
        for h in range(heads):
            qh = q_ref[:, h * Dh:(h + 1) * Dh].astype(BF16)
            kh = kv_ref[:, h * Dh:(h + 1) * Dh].astype(BF16)
            vh = kv_ref[:, width + h * Dh:width + (h + 1) * Dh].astype(BF16)
            p = _mem_probs(qh, kh, scale)
            o_ref[:, h * Dh:(h + 1) * Dh] = _dot(p.astype(BF16), vh).astype(o_ref.dtype)

    return pl.pallas_call(
        body, grid=(S // ts,),
        in_specs=[pl.BlockSpec((ts, width), lambda i: (i, q_off // width)),
                  pl.BlockSpec((M, 2 * width), lambda i: (0, 0))],
        out_specs=pl.BlockSpec((ts, width), lambda i: (i, 0)),
        out_shape=jax.ShapeDtypeStruct((S, width), BF16), name="mem_fwd",
        compiler_params=_cparams(("parallel",)),
    )(proj, kv)


def _mem_bwd(proj, q_off, width, kv, dmm):
    S = proj.shape[0]
    Dh = HEAD_DIM
    heads = width // Dh
    ts = _tile(S, 512, SUBLANES)
    scale = Dh ** -0.5
    M = kv.shape[0]

    def body(q_ref, kv_ref, d_ref, dq_ref, dkv_ref):
        @pl.when(pl.program_id(0) == 0)
        def _():
            dkv_ref[...] = jnp.zeros_like(dkv_ref)

        for h in range(heads):
            qh = q_ref[:, h * Dh:(h + 1) * Dh].astype(BF16)
            kh = kv_ref[:, h * Dh:(h + 1) * Dh].astype(BF16)
            vh = kv_ref[:, width + h * Dh:width + (h + 1) * Dh].astype(BF16)
            dh = d_ref[:, h * Dh:(h + 1) * Dh].astype(BF16)
            p = _mem_probs(qh, kh, scale)
            dp = _dot(dh, vh, NT)
            ds = p * (dp - jnp.sum(dp * p, axis=-1, keepdims=True)) * scale
            dsb = ds.astype(BF16)
            dq_ref[:, h * Dh:(h + 1) * Dh] = _dot(dsb, kh)
            dkv_ref[:, h * Dh:(h + 1) * Dh] += _dot(dsb, qh, TN)
            dkv_ref[:, width + h * Dh:width + (h + 1) * Dh] += _dot(p.astype(BF16), dh, TN)

    row = pl.BlockSpec((ts, width), lambda i: (i, 0))
    full = pl.BlockSpec((M, 2 * width), lambda i: (0, 0))
    return pl.pallas_call(
        body, grid=(S // ts,),
        in_specs=[pl.BlockSpec((ts, width), lambda i: (i, q_off // width)), full, row],
        out_specs=[row, full],
        out_shape=[jax.ShapeDtypeStruct((S, width), F32), jax.ShapeDtypeStruct((M, 2 * width), F32)],
        name="mem_bwd", compiler_params=_cparams(("arbitrary",)),
    )(proj, kv, dmm)


def _disc_math(lre, lim, logdt, bre_t, bim_t):
    dt = jnp.exp(logdt)
    mag = jnp.exp(lre * dt)
    ang = lim * dt
    a = mag * jnp.cos(ang)
    b = mag * jnp.sin(ang)
    den = lre * lre + lim * lim
    nr = a - 1.0
    fre = (nr * lre + b * lim) / den
    fim = (b * lre - nr * lim) / den
    bbre = fre * bre_t - fim * bim_t
    bbim = fre * bim_t + fim * bre_t
    return a, b, bbre, bbim


def _s5_disc(lre, lim, logdt, bre_t, bim_t):
    G, _, P = lre.shape
    C = bre_t.shape[1]

    def body(lre_ref, lim_ref, dt_ref, br_ref, bi_ref, a_ref, b_ref, bbre_ref, bbim_ref):
        a, b, bbre, bbim = _disc_math(lre_ref[...], lim_ref[...], dt_ref[...], br_ref[...], bi_ref[...])
        a_ref[...] = a
        b_ref[...] = b
        bbre_ref[...] = bbre
        bbim_ref[...] = bbim

    gp = jax.ShapeDtypeStruct((G, 1, P), F32)
    gcp = jax.ShapeDtypeStruct((G, C, P), F32)
    return pl.pallas_call(
        body, in_specs=[VMEM] * 5, out_specs=[VMEM] * 4, out_shape=[gp, gp, gcp, gcp], name="s5_disc",
    )(lre, lim, logdt, bre_t, bim_t)


def _s5_disc_bwd(lre, lim, logdt, bre_t, bim_t, da, db, dbbre, dbbim):
    G, _, P = lre.shape
    C = bre_t.shape[1]

    def body(lre_ref, lim_ref, dt_ref, br_ref, bi_ref, da_ref, db_ref, dbr_ref, dbi_ref,
             o_lre, o_lim, o_dt, o_br, o_bi):
        _, vjp = jax.vjp(_disc_math, lre_ref[...], lim_ref[...], dt_ref[...], br_ref[...], bi_ref[...])
        g = vjp((da_ref[...], db_ref[...], dbr_ref[...], dbi_ref[...]))
        o_lre[...] = g[0]
        o_lim[...] = g[1]
        o_dt[...] = g[2]
        o_br[...] = g[3]
        o_bi[...] = g[4]

    gp = jax.ShapeDtypeStruct((G, 1, P), F32)
    gcp = jax.ShapeDtypeStruct((G, C, P), F32)
    return pl.pallas_call(
        body, in_specs=[VMEM] * 9, out_specs=[VMEM] * 5,
        out_shape=[gp, gp, jax.ShapeDtypeStruct((G, 1, 1), F32), gcp, gcp], name="s5_disc_bwd",
    )(lre, lim, logdt, bre_t, bim_t, da, db, dbbre, dbbim)


S5_CHUNK = 256


def _load_once(pairs):
    @pl.when(pl.program_id(0) == 0)
    def _():
        for src, dst in pairs:
            pltpu.sync_copy(src, dst)


def _s5_fwd(proj, u_off, width, a_row, b_row, bmre, bmim, cmre, cmimn, d_row):
    S = proj.shape[0]
    GP = a_row.shape[1]
    T = min(S5_CHUNK, S)

    def body(u_ref, a_ref, b_ref, d_ref, bre_hbm, bim_hbm, cre_hbm, cim_hbm,
             uo_ref, y_ref, gy_ref, hre_ref, him_ref, st_ref, bure_s, buim_s, bre_ref, bim_ref, cre_ref, cim_ref):
        @pl.when(pl.program_id(0) == 0)
        def _():
            st_ref[...] = jnp.zeros_like(st_ref)

        _load_once([(bre_hbm, bre_ref), (bim_hbm, bim_ref), (cre_hbm, cre_ref), (cim_hbm, cim_ref)])
        u = u_ref[...]
        uo_ref[...] = u
        for ws, gs in _bd_blocks(width, GP):
            bure_s[:, gs] = _dot_a2(u[:, ws], bre_ref[ws, gs])
            buim_s[:, gs] = _dot_a2(u[:, ws], bim_ref[ws, gs])
        a = a_ref[...]
        b = b_ref[...]

        def step(ii, carry):
            hre, him = carry
            base = pl.multiple_of(ii * SUBLANES, SUBLANES)
            br = bure_s[pl.ds(base, SUBLANES), :]
            bi = buim_s[pl.ds(base, SUBLANES), :]
            rows_re, rows_im = [], []
            for j in range(SUBLANES):
                nre = a * hre - b * him + br[j:j + 1, :]
                nim = a * him + b * hre + bi[j:j + 1, :]
                hre, him = nre, nim
                rows_re.append(nre)
                rows_im.append(nim)
            hre_ref[pl.ds(base, SUBLANES), :] = jnp.concatenate(rows_re, axis=0)
            him_ref[pl.ds(base, SUBLANES), :] = jnp.concatenate(rows_im, axis=0)
            return hre, him

        hre, him = lax.fori_loop(0, T // SUBLANES, step, (st_ref[0:1, :], st_ref[1:2, :]))
        st_ref[0:1, :] = hre
        st_ref[1:2, :] = him
        for ws, gs in _bd_blocks(width, GP):
            y = (_dot_a2(hre_ref[:, gs], cre_ref[gs, ws]) + _dot_a2(him_ref[:, gs], cim_ref[gs, ws])
                 + d_ref[:, ws] * u[:, ws])
            y_ref[:, ws] = y
            gy_ref[:, ws] = jax.nn.gelu(y).astype(gy_ref.dtype)

    c0 = lambda i: (0, 0)
    urow = pl.BlockSpec((T, width), lambda i: (i, u_off // width))
    row = pl.BlockSpec((T, width), lambda i: (i, 0))
    hrow = pl.BlockSpec((T, GP), lambda i: (i, 0))
    sw = jax.ShapeDtypeStruct((S, width), F32)
    sg = jax.ShapeDtypeStruct((S, GP), F32)
    return pl.pallas_call(
        body, grid=(S // T,),
        in_specs=[urow, pl.BlockSpec((1, GP), c0), pl.BlockSpec((1, GP), c0), pl.BlockSpec((1, width), c0),
                  ANY, ANY, ANY, ANY],
        out_specs=[row, row, row, hrow, hrow],
        out_shape=[sw, sw, jax.ShapeDtypeStruct((S, width), BF16), sg, sg],
        scratch_shapes=[pltpu.VMEM((SUBLANES, GP), F32), pltpu.VMEM((T, GP), F32), pltpu.VMEM((T, GP), F32),
                        pltpu.VMEM((width, GP), F32), pltpu.VMEM((width, GP), F32),
                        pltpu.VMEM((GP, width), F32), pltpu.VMEM((GP, width), F32)],
        name="s5_fwd", compiler_params=_cparams(("arbitrary",)),
    )(proj, a_row, b_row, d_row, bmre, bmim, cmre, cmimn)


def _s5_bwd(u, dgy, y, hre, him, a_row, b_row, bmre, bmim, cmre, cmimn, d_row):
    S, width = u.shape
    GP = a_row.shape[1]
    T = min(S5_CHUNK, S)
    nchunk = S // T

    def body(u_ref, dgy_ref, y_ref, hre_ref, him_ref, a_ref, b_ref, d_ref, bre_hbm, bim_hbm, cre_hbm, cim_hbm,
             du_ref, dy_ref, gre_s, gim_s, dd_ref, da_ref, db_ref,
             st_ref, bre_ref, bim_ref, cre_ref, cim_ref):
        @pl.when(pl.program_id(0) == 0)
        def _():
            st_ref[...] = jnp.zeros_like(st_ref)
            for r in (dd_ref, da_ref, db_ref):
                r[...] = jnp.zeros_like(r)

        _load_once([(bre_hbm, bre_ref), (bim_hbm, bim_ref), (cre_hbm, cre_ref), (cim_hbm, cim_ref)])
        u = u_ref[...]
        dy = dgy_ref[...] * _gelu_grad(y_ref[...])
        dy_ref[...] = dy
        for ws, gs in _bd_blocks(width, GP):
            gre_s[:, gs] = _dot_a2(dy[:, ws], cre_ref[gs, ws], NT)
            gim_s[:, gs] = _dot_a2(dy[:, ws], cim_ref[gs, ws], NT)
        a = a_ref[...]
        b = b_ref[...]
        g_in_re = st_ref[0:1, :]
        g_in_im = st_ref[1:2, :]

        def step(ii, carry):
            gre, gim = carry
            base = pl.multiple_of((T // SUBLANES - 1 - ii) * SUBLANES, SUBLANES)
            dr = gre_s[pl.ds(base, SUBLANES), :]
            di = gim_s[pl.ds(base, SUBLANES), :]
            rows_re = [None] * SUBLANES
            rows_im = [None] * SUBLANES
            for j in range(SUBLANES - 1, -1, -1):
                nre = dr[j:j + 1, :] + a * gre + b * gim
                nim = di[j:j + 1, :] - b * gre + a * gim
                gre, gim = nre, nim
                rows_re[j] = nre
                rows_im[j] = nim
            gre_s[pl.ds(base, SUBLANES), :] = jnp.concatenate(rows_re, axis=0)
            gim_s[pl.ds(base, SUBLANES), :] = jnp.concatenate(rows_im, axis=0)
            return gre, gim

        gre, gim = lax.fori_loop(0, T // SUBLANES, step, (g_in_re, g_in_im))
        st_ref[0:1, :] = gre
        st_ref[1:2, :] = gim
        last = lax.broadcasted_iota(jnp.int32, (T, 1), 0) == T - 1
        nxt_re = jnp.where(last, g_in_re, pltpu.roll(gre_s[...], T - 1, 0))
        nxt_im = jnp.where(last, g_in_im, pltpu.roll(gim_s[...], T - 1, 0))
        hre = hre_ref[...]
        him = him_ref[...]
        da_ref[...] += jnp.sum(nxt_re * hre + nxt_im * him, axis=0, keepdims=True)
        db_ref[...] += jnp.sum(nxt_im * hre - nxt_re * him, axis=0, keepdims=True)
        for ws, gs in _bd_blocks(width, GP):
            du_ref[:, ws] = (_dot_a2(gre_s[:, gs], bre_ref[ws, gs], NT) + _dot_a2(gim_s[:, gs], bim_ref[ws, gs], NT)
                             + d_ref[:, ws] * dy[:, ws])
        dd_ref[...] += jnp.sum(dy * u, axis=0, keepdims=True)

    c0 = lambda i: (0, 0)
    rev = lambda i: (nchunk - 1 - i, 0)
    row = pl.BlockSpec((T, width), rev)
    hrow = pl.BlockSpec((T, GP), rev)
    v_gp = pl.BlockSpec((1, GP), c0)
    v_w = pl.BlockSpec((1, width), c0)
    sw = jax.ShapeDtypeStruct((S, width), F32)
    sg = jax.ShapeDtypeStruct((S, GP), F32)
    return pl.pallas_call(
        body, grid=(nchunk,),
        in_specs=[row, row, row, hrow, hrow, v_gp, v_gp, v_w, ANY, ANY, ANY, ANY],
        out_specs=[row, row, hrow, hrow, v_w, v_gp, v_gp],
        out_shape=[sw, sw, sg, sg, jax.ShapeDtypeStruct((1, width), F32),
                   jax.ShapeDtypeStruct((1, GP), F32), jax.ShapeDtypeStruct((1, GP), F32)],
        scratch_shapes=[pltpu.VMEM((SUBLANES, GP), F32),
                        pltpu.VMEM((width, GP), F32), pltpu.VMEM((width, GP), F32),
                        pltpu.VMEM((GP, width), F32), pltpu.VMEM((GP, width), F32)],
        name="s5_bwd", compiler_params=_cparams(("arbitrary",)),
    )(u, dgy, y, hre, him, a_row, b_row, d_row, bmre, bmim, cmre, cmimn)


def _block_diag(x):
    G, A, B = x.shape
    eye = jnp.eye(G, dtype=x.dtype)
    return (eye[:, None, :, None] * x[:, :, None, :]).reshape(G * A, G * B)


def _block_diag_take(m, G):
    A, B = m.shape[0] // G, m.shape[1] // G
    return jnp.einsum("gagb->gab", m.reshape(G, A, G, B))


def _adamw(w, g, m, v):
    shape = w.shape
    C = shape[-1]
    w2, g2, m2, v2 = (t.reshape(-1, C) for t in (w, g, m, v))
    R = w2.shape[0]
    rb = _tile(R, max(SUBLANES, (1 << 19) // C), SUBLANES)
    c1 = 1.0 - ADAM_B1 ** ADAM_STEP
    c2 = 1.0 - ADAM_B2 ** ADAM_STEP

    def body(w_ref, g_ref, m_ref, v_ref, d_ref, nm_ref, nv_ref):
        gv = g_ref[...]
        nm = ADAM_B1 * m_ref[...] + (1.0 - ADAM_B1) * gv
        nv = ADAM_B2 * v_ref[...] + (1.0 - ADAM_B2) * (gv * gv)
        d_ref[...] = -ADAM_LR * ((nm / c1) / (jnp.sqrt(nv / c2) + ADAM_EPS) + ADAM_WD * w_ref[...])
        nm_ref[...] = nm
        nv_ref[...] = nv

    blk = pl.BlockSpec((rb, C), lambda i: (i, 0))
    sd = jax.ShapeDtypeStruct((R, C), F32)
    outs = pl.pallas_call(
        body, grid=(R // rb,), in_specs=[blk] * 4, out_specs=[blk] * 3, out_shape=[sd] * 3,
        name="adamw", compiler_params=_cparams(("parallel",)),
    )(w2, g2, m2, v2)
    return tuple(o.reshape(shape) for o in outs)


def _coords():
    x, y, c = lax.axis_index("x"), lax.axis_index("y"), lax.axis_index("c")
    return x, y, c


def _place_shard(w, l, place):
    _, R, C = w.shape
    rb = _tile(R, max(16, (1 << 19) // C), 16)

    def body(place_ref, w_ref, o_ref):
        o_ref[...] = w_ref[...].astype(BF16)

    grid_spec = pltpu.PrefetchScalarGridSpec(
        num_scalar_prefetch=1, grid=(R // rb,),
        in_specs=[pl.BlockSpec((None, rb, C), lambda i, p: (l, i, 0))],
        out_specs=pl.BlockSpec((None, rb, C), lambda i, p: (p[1], i, 0)))
    return pl.pallas_call(
        body, grid_spec=grid_spec, out_shape=jax.ShapeDtypeStruct((N_CHIPS, R, C), BF16),
        name="place_shard", compiler_params=_cparams(("arbitrary",)),
    )(place, w)


def _ag_hooks(outs, send_sems, recv_sems):
    n = len(outs)

    def rcopy(a, k, block, to):
        cx, cy, cc = block
        hr = outs[a].shape[1] // 2
        blk = outs[a].at[2 * cx + cy, pl.ds(cc * hr, hr)]
        return pltpu.make_async_remote_copy(
            src_ref=blk, dst_ref=blk, send_sem=send_sems.at[a * 6 + k], recv_sem=recv_sems.at[a * 6 + k],
            device_id=to, device_id_type=MESH)

    def places():
        x, y, c = _coords()
        return (x, y, c), (x, y, 1 - c), [(1 - x, y), (x, 1 - y), (1 - x, 1 - y)]

    def start():
        me, _, others = places()
        for a in range(n):
            for j, ch in enumerate(others):
                rcopy(a, j, me, (*ch, me[2])).start()

    def middle():
        me, sibling, others = places()
        for a in range(n):
            for j, ch in enumerate(others):
                rcopy(a, j, (*ch, me[2]), me).wait_recv()
                rcopy(a, 3 + j, (*ch, me[2]), sibling).start()

    def finish():
        me, sibling, others = places()
        for a in range(n):
            for j, ch in enumerate(others):
                rcopy(a, 3 + j, (*ch, sibling[2]), me).wait_recv()
        for a in range(n):
            for j, ch in enumerate(others):
                rcopy(a, j, me, (*ch, me[2])).wait_send()
                rcopy(a, 3 + j, (*ch, me[2]), sibling).wait_send()

    return start, middle, finish


def _ag_weights(bufs):
    n = len(bufs)

    def body(*refs):
        start, middle, finish = _ag_hooks(refs[n:2 * n], *refs[2 * n:])
        start()
        middle()
        finish()

    return pl.pallas_call(
        body, out_shape=[jax.ShapeDtypeStruct(b.shape, b.dtype) for b in bufs],
        in_specs=[ANY] * n, out_specs=[ANY] * n, input_output_aliases={i: i for i in range(n)},
        scratch_shapes=[pltpu.SemaphoreType.DMA((n * 6,)), pltpu.SemaphoreType.DMA((n * 6,))],
        name="ag_weights",
    )(*bufs)


def _rs_pair_hooks(gs, outs, send_sems, recv_sems):
    def copies():
        x, y, c = _coords()
        cps = []
        for i in range(len(gs)):
            hr = gs[i].shape[1] // 2
            cps.append(pltpu.make_async_remote_copy(
                src_ref=gs[i].at[:, pl.ds((1 - c) * hr, hr)], dst_ref=outs[i],
                send_sem=send_sems.at[i], recv_sem=recv_sems.at[i],
                device_id=(x, y, 1 - c), device_id_type=MESH))
        return cps

    def start():
        for cp in copies():
            cp.start()

    def finish():
        for cp in copies():
            cp.wait()

    return start, None, finish


def _rs_pair_comm(grads):
    return dict(ins=list(grads), alias=False, n_sems=len(grads), hooks=_rs_pair_hooks,
                out_shapes=[jax.ShapeDtypeStruct((N_CHIPS, g.shape[1] // 2, g.shape[2]), F32) for g in grads])


def _ag_comm(bufs):
    return dict(ins=list(bufs), alias=True, n_sems=6 * len(bufs),
                hooks=lambda ins, outs, send_sems, recv_sems: _ag_hooks(outs, send_sems, recv_sems),
                out_shapes=[jax.ShapeDtypeStruct(b.shape, b.dtype) for b in bufs])


def _rs_pair(grads):
    n = len(grads)
    comm = _rs_pair_comm(grads)

    def body(*refs):
        start, _, finish = _rs_pair_hooks(refs[:n], refs[n:2 * n], *refs[2 * n:])
        start()
        finish()

    return pl.pallas_call(
        body, out_shape=comm["out_shapes"], in_specs=[ANY] * n, out_specs=[ANY] * n,
        scratch_shapes=[pltpu.SemaphoreType.DMA((n,)), pltpu.SemaphoreType.DMA((n,))],
        name="rs_pair",
    )(*grads)


def _pair_add(g, r, place):
    _, R, C = g.shape
    hr = R // 2
    rb = _tile(hr, max(16, (1 << 19) // C), 16)
    nb = hr // rb

    def body(place_ref, g_ref, r_ref, p16_ref, own_ref):
        s = g_ref[...] + r_ref[...]
        p16_ref[...] = s.astype(BF16)

        @pl.when(pl.program_id(1) == place_ref[1])
        def _():
            own_ref[...] = s

    grid_spec = pltpu.PrefetchScalarGridSpec(
        num_scalar_prefetch=1, grid=(nb, N_CHIPS),
        in_specs=[pl.BlockSpec((None, rb, C), lambda i, k, p: (k, p[0] * nb + i, 0)),
                  pl.BlockSpec((None, rb, C), lambda i, k, p: (k, i, 0))],
        out_specs=[pl.BlockSpec((None, rb, C), lambda i, k, p: (k, i, 0)),
                   pl.BlockSpec((rb, C), lambda i, k, p: (i, 0))])
    return pl.pallas_call(
        body, grid_spec=grid_spec,
        out_shape=[jax.ShapeDtypeStruct((N_CHIPS, hr, C), BF16), jax.ShapeDtypeStruct((hr, C), F32)],
        name="pair_add", compiler_params=_cparams(("arbitrary", "arbitrary")),
    )(place, g, r)


def _rs_chips_hooks(ps, outs, send_sems, recv_sems):
    def copies():
        x, y, c = _coords()
        cps = []
        for a in range(len(ps)):
            for r in (1, 2, 3):
                kx = 1 - x if (r >> 1) else x
                ky = 1 - y if (r & 1) else y
                cps.append(pltpu.make_async_remote_copy(
                    src_ref=ps[a].at[2 * kx + ky], dst_ref=outs[a].at[r - 1],
                    send_sem=send_sems.at[a * 3 + r - 1], recv_sem=recv_sems.at[a * 3 + r - 1],
                    device_id=(kx, ky, c), device_id_type=MESH))
        return cps

    def start():
        for cp in copies():
            cp.start()

    def finish():
        for cp in copies():
            cp.wait()

    return start, finish


def _rs_chips_shapes(p16):
    return [jax.ShapeDtypeStruct((3,) + p.shape[1:], BF16) for p in p16]


def _rs_chips(p16):
    n = len(p16)

    def body(*refs):
        start, finish = _rs_chips_hooks(refs[:n], refs[n:2 * n], *refs[2 * n:])
        start()
        finish()

    return pl.pallas_call(
        body, out_shape=_rs_chips_shapes(p16), in_specs=[ANY] * n, out_specs=[ANY] * n,
        scratch_shapes=[pltpu.SemaphoreType.DMA((n * 3,)), pltpu.SemaphoreType.DMA((n * 3,))],
        name="rs_chips",
    )(*p16)


def _chip_sum(own, recv, full, l, place):
    hr, C = own.shape
    rb = _tile(hr, max(16, (1 << 19) // C), 16)
    nb = hr // rb

    def body(place_ref, o_ref, r_ref, full_ref, s_ref):
        s = o_ref[...] + r_ref[0].astype(F32)
        s = s + r_ref[1].astype(F32)
        s_ref[...] = s + r_ref[2].astype(F32)

    grid_spec = pltpu.PrefetchScalarGridSpec(
        num_scalar_prefetch=1, grid=(nb,),
        in_specs=[pl.BlockSpec((rb, C), lambda i, p: (i, 0)), pl.BlockSpec((3, rb, C), lambda i, p: (0, i, 0)), ANY],
        out_specs=pl.BlockSpec((None, rb, C), lambda i, p: (l, p[0] * nb + i, 0)))
    return pl.pallas_call(
        body, grid_spec=grid_spec, out_shape=jax.ShapeDtypeStruct(full.shape, F32),
        input_output_aliases={3: 0}, name="chip_sum", compiler_params=_cparams(("arbitrary",)),
    )(place, own, recv, full)


def _share_pair(fulls):
    n = len(fulls)

    def body(*refs):
        outs = refs[n:2 * n]
        send_sems, recv_sems = refs[2 * n:]
        x, y, c = _coords()
        copies = []
        for a in range(n):
            hr = outs[a].shape[1] // 2
            mine = outs[a].at[:, pl.ds(c * hr, hr)]
            cp = pltpu.make_async_remote_copy(
                src_ref=mine, dst_ref=mine, send_sem=send_sems.at[a], recv_sem=recv_sems.at[a],
                device_id=(x, y, 1 - c), device_id_type=MESH)
            cp.start()
            copies.append(cp)
        for cp in copies:
            cp.wait_recv()
        for cp in copies:
            cp.wait_send()

    return pl.pallas_call(
        body, out_shape=[jax.ShapeDtypeStruct(f.shape, f.dtype) for f in fulls],
        in_specs=[ANY] * n, out_specs=[ANY] * n, input_output_aliases={i: i for i in range(n)},
        scratch_shapes=[pltpu.SemaphoreType.DMA((n,)), pltpu.SemaphoreType.DMA((n,))],
        name="share_pair",
    )(*fulls)


def _small_allreduce(packed):
    m_per, ncol = packed.shape

    def body(x_ref, out_ref, tot_ref, send_sems, recv_sems, local_sem):
        x, y, c = _coords()
        me, sibling = (x, y, c), (x, y, 1 - c)
        chips = [(1 - x, y), (x, 1 - y), (1 - x, 1 - y)]

        def rows(px, py, pc):
            return out_ref.at[pl.ds((4 * px + 2 * py + pc) * m_per, m_per), :]

        def copy(k, block, to, src=None):
            return pltpu.make_async_remote_copy(
                src_ref=rows(*block) if src is None else src, dst_ref=rows(*block),
                send_sem=send_sems.at[k], recv_sem=recv_sems.at[k], device_id=to, device_id_type=MESH)

        mine = pltpu.make_async_copy(x_ref, rows(*me), local_sem)
        mine.start()
        first = [copy(0, me, sibling, src=x_ref)]
        first += [copy(1 + j, me, (*chip, c), src=x_ref) for j, chip in enumerate(chips)]
        for cp in first:
            cp.start()
        passed = [copy(4 + j, (*chip, c), sibling) for j, chip in enumerate(chips)]
        for j, chip in enumerate(chips):
            copy(1 + j, (*chip, c), me).wait_recv()
            passed[j].start()
        copy(0, sibling, me).wait_recv()
        for j, chip in enumerate(chips):
            copy(4 + j, (*chip, 1 - c), me).wait_recv()
        for cp in first + passed:
            cp.wait_send()
        mine.wait()
        tot = out_ref[pl.ds(0, m_per), :]
        for d in range(1, N_DEV):
            tot = tot + out_ref[pl.ds(d * m_per, m_per), :]
        tot_ref[...] = tot

    _, tot = pl.pallas_call(
        body,
        out_shape=[jax.ShapeDtypeStruct((N_DEV * m_per, ncol), F32), jax.ShapeDtypeStruct((m_per, ncol), F32)],
        in_specs=[VMEM], out_specs=[VMEM, VMEM],
        scratch_shapes=[pltpu.SemaphoreType.DMA((7,)), pltpu.SemaphoreType.DMA((7,)), pltpu.SemaphoreType.DMA],
        name="small_allreduce",
        compiler_params=pltpu.CompilerParams(vmem_limit_bytes=VMEM_LIMIT_MB * 1024 * 1024),
    )(packed)
    return tot


BIG = ["w_in", "sb_w_out", "ssm_w_glu", "ssm_w_out", "mem_w_kv", "mem_w_out", "w_o", "ffn_w_gate_up", "ffn_w_down"]
KIND = {"w_in": "col", "sb_w_out": "col", "ssm_w_glu": "col", "ssm_w_out": "col", "mem_w_kv": "row",
        "mem_w_out": "col", "w_o": "row", "ffn_w_gate_up": "col", "ffn_w_down": "row"}
AG_WITH_PROJ = ["ffn_w_gate_up"]
AG_WITH_SB = ["w_in", "sb_w_out", "ssm_w_glu", "ssm_w_out", "mem_w_kv", "mem_w_out", "w_o"]
AG_WITH_GATE_UP = ["ffn_w_down"]
SMALL = ["b_in", "ssm_lambda_re", "ssm_lambda_im", "ssm_log_dt", "ssm_b_re", "ssm_b_im", "ssm_c_re", "ssm_c_im",
         "ssm_d", "ln1_g", "ln1_b", "ln2_g", "ln2_b"]
WEIGHTS = ["w_in", "b_in", "sb_w_out", "ssm_lambda_re", "ssm_lambda_im", "ssm_log_dt", "ssm_b_re", "ssm_b_im",
           "ssm_c_re", "ssm_c_im", "ssm_d", "ssm_w_glu", "ssm_w_out", "mem_w_kv", "mem_w_out", "w_o", "ln1_g",
           "ln1_b", "ffn_w_gate_up", "ffn_w_down", "ln2_g", "ln2_b"]


def _pack(arrs):
    flat = jnp.concatenate([a.reshape(-1).astype(F32) for a in arrs])
    n = flat.shape[0]
    rows = -(-n // LANES)
    rows = -(-rows // SUBLANES) * SUBLANES
    return jnp.pad(flat, (0, rows * LANES - n)).reshape(rows, LANES)


def _unpack(packed, like):
    flat = packed.reshape(-1)
    out, off = [], 0
    for a in like:
        out.append(flat[off:off + a.size].reshape(a.shape))
        off += a.size
    return out


def _step(x, mem, target, W, M1, V1):
    S, D = x.shape[1], x.shape[2]
    L = W["w_in"].shape[0]
    x0 = x.reshape(S, D)
    mem2 = mem.reshape(mem.shape[1], D)
    tgt = target.reshape(S, D)
    alpha = (2 * L) ** 0.25
    sbw = W["sb_w_out"].shape[1]
    ssw = W["ssm_d"].shape[1]
    mw = W["mem_w_out"].shape[1]
    heads = sbw // HEAD_DIM
    G, P = W["ssm_lambda_re"].shape[1], W["ssm_lambda_re"].shape[2]
    q_off, k_off, v_off = 0, sbw, 2 * sbw
    u_off = 3 * sbw
    qm_off = u_off + ssw
    gate_off = qm_off + mw

    x_i, y_i, c_i = _coords()
    place = jnp.stack([c_i, 2 * x_i + y_i]).astype(jnp.int32)
    placed = [[_place_shard(W[n], l, place) for n in BIG] for l in range(L)]
    Wg = {n: [None] * L for n in BIG}
    for n, buf in zip(BIG, _ag_weights(placed[0])):
        Wg[n][0] = buf

    saved = []
    xl = x0
    xlb = x0.astype(BF16)
    for l in range(L):
        sv = {"x": xlb}
        nxt = dict(zip(BIG, placed[l + 1])) if l + 1 < L else None
        if nxt is None:
            proj = _mm_nn(xlb, Wg["w_in"][l], "col", bias=W["b_in"][l][None, :], name="mm_proj")
            sb, sb_ctot, _ = _sb_fwd(proj, q_off, k_off, v_off, heads)
        else:
            proj, got = _mm_nn(xlb, Wg["w_in"][l], "col", bias=W["b_in"][l][None, :], name="mm_proj_ag",
                               comm=_ag_comm([nxt[n] for n in AG_WITH_PROJ]))
            for n, buf in zip(AG_WITH_PROJ, got):
                Wg[n][l + 1] = buf
            sb, sb_ctot, got = _sb_fwd(proj, q_off, k_off, v_off, heads, ag=[nxt[n] for n in AG_WITH_SB])
            for n, buf in zip(AG_WITH_SB, got):
                Wg[n][l + 1] = buf
        p_sb = _mm_nn(sb, Wg["sb_w_out"][l], "col", name="mm_sb_out")

        bre_t = W["ssm_b_re"][l].transpose(0, 2, 1)
        bim_t = W["ssm_b_im"][l].transpose(0, 2, 1)
        logdt = W["ssm_log_dt"][l][:, None, None]
        lre3 = W["ssm_lambda_re"][l][:, None, :]
        lim3 = W["ssm_lambda_im"][l][:, None, :]
        a_gp, b_gp, bbre, bbim = _s5_disc(lre3, lim3, logdt, bre_t, bim_t)
        a_row, b_row = a_gp.reshape(1, G * P), b_gp.reshape(1, G * P)
        bmre, bmim = _block_diag(bbre), _block_diag(bbim)
        cmre = _block_diag(W["ssm_c_re"][l].transpose(0, 2, 1))
        cmimn = _block_diag(-W["ssm_c_im"][l].transpose(0, 2, 1))
        d_row = W["ssm_d"][l][None, :]
        u_ssm, y, gy, hre, him = _s5_fwd(proj, u_off, ssw, a_row, b_row, bmre, bmim, cmre, cmimn, d_row)
        glu = _mm_nn(gy, Wg["ssm_w_glu"][l], "col", name="mm_glu")
        zz = _glu_fwd(glu)
        p_ssm = _mm_nn(zz, Wg["ssm_w_out"][l], "col", name="mm_ssm_out")

        kv = _mm_nn(mem2, Wg["mem_w_kv"][l], "row", name="mm_kv")
        mm_o = _mem_fwd(proj, qm_off, mw, kv)
        p_mem = _mm_nn(mm_o, Wg["mem_w_out"][l], "col", name="mm_mem_out")

        merged = _merge_fwd(proj, gate_off, p_sb, p_ssm, p_mem)
        mix = _mm_nn(merged, Wg["w_o"][l], "row", name="mm_wo")
        x1, x1b, xh1, rs1 = _ln_fwd(xl, mix, W["ln1_g"][l][None, :], W["ln1_b"][l][None, :], alpha)
        if nxt is None:
            gu = _mm_nn(x1b, Wg["ffn_w_gate_up"][l], "col", name="mm_gate_up")
        else:
            gu, got = _mm_nn(x1b, Wg["ffn_w_gate_up"][l], "col", name="mm_gate_up_ag",
                             comm=_ag_comm([nxt[n] for n in AG_WITH_GATE_UP]))
            for n, buf in zip(AG_WITH_GATE_UP, got):
                Wg[n][l + 1] = buf
        hid = _swiglu_fwd(gu)
        ffn = _mm_nn(hid, Wg["ffn_w_down"][l], "row", name="mm_down")
        x2, x2b, xh2, rs2 = _ln_fwd(x1, ffn, W["ln2_g"][l][None, :], W["ln2_b"][l][None, :], alpha)
        sv.update(proj=proj, sb=sb, sb_ctot=sb_ctot, p_sb=p_sb, y=y, gy=gy, hre=hre, him=him, glu=glu, zz=zz,
                  p_ssm=p_ssm, kv=kv, mm_o=mm_o, p_mem=p_mem, merged=merged, x1=x1b, xh1=xh1, rs1=rs1, gu=gu,
                  hid=hid, xh2=xh2, rs2=rs2, u=u_ssm,
                  disc=(lre3, lim3, logdt, bre_t, bim_t, a_row, b_row, bmre, bmim, cmre, cmimn, d_row))
        saved.append(sv)
        xl, xlb = x2, x2b

    dxl, loss_part = _loss_head(xl, tgt)

    gbig = {n: [None] * L for n in BIG}
    gsmall = {n: [None] * L for n in SMALL}
    own = [None] * L
    from_chips = [None] * L
    above = None
    pending = None
    for l in range(L - 1, -1, -1):
        sv = saved[l]
        proj = sv["proj"]
        dr2, dr2b, dg2, db2 = _ln_bwd(dxl, sv["xh2"], sv["rs2"], W["ln2_g"][l][None, :])
        gsmall["ln2_g"][l], gsmall["ln2_b"][l] = dg2[0], db2[0]
        dhid = _mm_nt(dr2b, Wg["ffn_w_down"][l], "row", name="mm_d_hid")
        gbig["ffn_w_down"][l] = _mm_tn(sv["hid"], dr2b, "row", name="mm_g_down")
        dgu = _swiglu_bwd(dhid, sv["gu"])
        if above is None:
            dx1 = _mm_nt(dgu, Wg["ffn_w_gate_up"][l], "col", add=dr2, add_scale=alpha, name="mm_d_x1")
        else:
            dx1, from_sibling = _mm_nt(dgu, Wg["ffn_w_gate_up"][l], "col", add=dr2, add_scale=alpha,
                                       name="mm_d_x1_rs", comm=_rs_pair_comm(above))
            sums = [_pair_add(g, r, place) for g, r in zip(above, from_sibling)]
            pending = [s[0] for s in sums]
            own[l + 1] = [s[1] for s in sums]
        gbig["ffn_w_gate_up"][l] = _mm_tn(sv["x1"], dgu, "col", name="mm_g_gate_up")

        dr1, dr1b, dg1, db1 = _ln_bwd(dx1, sv["xh1"], sv["rs1"], W["ln1_g"][l][None, :])
        gsmall["ln1_g"][l], gsmall["ln1_b"][l] = dg1[0], db1[0]
        dmerged = _mm_nt(dr1b, Wg["w_o"][l], "row", name="mm_d_merged")
        gbig["w_o"][l] = _mm_tn(sv["merged"], dr1b, "row", name="mm_g_wo")
        dp_sb, dp_ssm, dp_mem, dgl0, dgl1, dgl2 = _merge_bwd(
            dmerged, proj, gate_off, sv["p_sb"], sv["p_ssm"], sv["p_mem"])

        dsb = _mm_nt(dp_sb, Wg["sb_w_out"][l], "col", out_dtype=BF16, name="mm_d_sb")
        gbig["sb_w_out"][l] = _mm_tn(sv["sb"], dp_sb, "col", name="mm_g_sb_out")

        dzz = _mm_nt(dp_ssm, Wg["ssm_w_out"][l], "col", name="mm_d_zz")
        gbig["ssm_w_out"][l] = _mm_tn(sv["zz"], dp_ssm, "col", name="mm_g_ssm_out")
        dglu = _glu_bwd(dzz, sv["glu"])
        dgy = _mm_nt(dglu, Wg["ssm_w_glu"][l], "col", name="mm_d_gy")
        gbig["ssm_w_glu"][l] = _mm_tn(sv["gy"], dglu, "col", name="mm_g_glu")
        lre3, lim3, logdt, bre_t, bim_t, a_row, b_row, bmre, bmim, cmre, cmimn, d_row = sv["disc"]
        du, dy_ssm, g_re, g_im, dd, da, db = _s5_bwd(
            sv["u"], dgy, sv["y"], sv["hre"], sv["him"], a_row, b_row, bmre, bmim, cmre, cmimn, d_row)
        dbmre = _mm_tn(sv["u"], g_re, "plain", name="mm_g_ssm_bre")
        dbmim = _mm_tn(sv["u"], g_im, "plain", name="mm_g_ssm_bim")
        dcmre = _mm_tn(sv["hre"], dy_ssm, "plain", name="mm_g_ssm_cre")
        dcmimn = _mm_tn(sv["him"], dy_ssm, "plain", name="mm_g_ssm_cim")
        dlre, dlim, dlogdt, dbre_t, dbim_t = _s5_disc_bwd(
            lre3, lim3, logdt, bre_t, bim_t, da.reshape(G, 1, P), db.reshape(G, 1, P),
            _block_diag_take(dbmre, G), _block_diag_take(dbmim, G))
        gsmall["ssm_lambda_re"][l], gsmall["ssm_lambda_im"][l] = dlre.reshape(G, P), dlim.reshape(G, P)
        gsmall["ssm_log_dt"][l] = dlogdt.reshape(G)
        gsmall["ssm_b_re"][l] = dbre_t.transpose(0, 2, 1)
        gsmall["ssm_b_im"][l] = dbim_t.transpose(0, 2, 1)
        gsmall["ssm_c_re"][l] = _block_diag_take(dcmre, G).transpose(0, 2, 1)
        gsmall["ssm_c_im"][l] = -_block_diag_take(dcmimn, G).transpose(0, 2, 1)
        gsmall["ssm_d"][l] = dd[0]

        dmm = _mm_nt(dp_mem, Wg["mem_w_out"][l], "col", out_dtype=BF16, name="mm_d_mm")
        gbig["mem_w_out"][l] = _mm_tn(sv["mm_o"], dp_mem, "col", name="mm_g_mem_out")
        dqm, dkv = _mem_bwd(proj, qm_off, mw, sv["kv"], dmm)
        gbig["mem_w_kv"][l] = _mm_tn(mem2, dkv, "row", name="mm_g_kv")

        dq, dk, dv, arrived = _sb_bwd(proj, q_off, k_off, v_off, heads, dsb, sv["sb_ctot"], rs=pending)
        if pending is not None:
            from_chips[l + 1] = arrived
        dproj, dbin = _assemble_dproj([dq, dk, dv, du, dqm, dgl0, dgl1, dgl2])
        gsmall["b_in"][l] = dbin[0]
        dxl = _mm_nt(dproj, Wg["w_in"][l], "col", add=dr1, add_scale=alpha, name="mm_d_x")
        gbig["w_in"][l] = _mm_tn(sv["x"], dproj, "col", name="mm_g_win")

        above = [gbig[n][l] for n in BIG]

    sums = [_pair_add(g, r, place) for g, r in zip(above, _rs_pair(above))]
    own[0] = [s[1] for s in sums]
    from_chips[0] = _rs_chips([s[0] for s in sums])
    grad_x = dxl.reshape(x.shape)

    fulls = []
    for a, n in enumerate(BIG):
        full = lax.empty(W[n].shape, F32)
        for l in range(L - 1, -1, -1):
            full = _chip_sum(own[l][a], from_chips[l][a], full, l, place)
        fulls.append(full)
    reduced = _share_pair(fulls)
    grads = {n: reduced[i] for i, n in enumerate(BIG)}

    small_local = [jnp.stack(gsmall[n]) for n in SMALL]
    packed = _pack(small_local + [loss_part[0, :1]])
    total = _small_allreduce(packed)
    unpacked = _unpack(total, small_local + [loss_part[0, :1]])
    for n, g in zip(SMALL, unpacked[:-1]):
        grads[n] = g
    loss = unpacked[-1][0]

    delta, new_m, new_v = {}, {}, {}
    for n in BIG:
        delta[n], new_m[n], new_v[n] = _adamw(W[n], grads[n], M1[n], V1[n])
    sm = _adamw(_pack([W[n] for n in SMALL]), _pack([grads[n] for n in SMALL]),
                _pack([M1[n] for n in SMALL]), _pack([V1[n] for n in SMALL]))
    like = [W[n] for n in SMALL]
    for n, d, m_, v_ in zip(SMALL, _unpack(sm[0], like), _unpack(sm[1], like), _unpack(sm[2], like)):
        delta[n], new_m[n], new_v[n] = d, m_, v_

    return (loss, grad_x, *[grads[n] for n in WEIGHTS], *[delta[n] for n in WEIGHTS],
            *[new_m[n] for n in WEIGHTS], *[new_v[n] for n in WEIGHTS])


def kernel(x, mem, w_in, b_in, sb_w_out, ssm_lambda_re, ssm_lambda_im, ssm_log_dt, ssm_b_re, ssm_b_im, ssm_c_re, ssm_c_im, ssm_d, ssm_w_glu, ssm_w_out, mem_w_kv, mem_w_out, w_o, ln1_g, ln1_b, ffn_w_gate_up, ffn_w_down, ln2_g, ln2_b, loss_target, m_w_in, m_b_in, m_sb_w_out, m_ssm_lambda_re, m_ssm_lambda_im, m_ssm_log_dt, m_ssm_b_re, m_ssm_b_im, m_ssm_c_re, m_ssm_c_im, m_ssm_d, m_ssm_w_glu, m_ssm_w_out, m_mem_w_kv, m_mem_w_out, m_w_o, m_ln1_g, m_ln1_b, m_ffn_w_gate_up, m_ffn_w_down, m_ln2_g, m_ln2_b, v_w_in, v_b_in, v_sb_w_out, v_ssm_lambda_re, v_ssm_lambda_im, v_ssm_log_dt, v_ssm_b_re, v_ssm_b_im, v_ssm_c_re, v_ssm_c_im, v_ssm_d, v_ssm_w_glu, v_ssm_w_out, v_mem_w_kv, v_mem_w_out, v_w_o, v_ln1_g, v_ln1_b, v_ffn_w_gate_up, v_ffn_w_down, v_ln2_g, v_ln2_b):
    W = dict(w_in=w_in, b_in=b_in, sb_w_out=sb_w_out, ssm_lambda_re=ssm_lambda_re, ssm_lambda_im=ssm_lambda_im,
             ssm_log_dt=ssm_log_dt, ssm_b_re=ssm_b_re, ssm_b_im=ssm_b_im, ssm_c_re=ssm_c_re, ssm_c_im=ssm_c_im,
             ssm_d=ssm_d, ssm_w_glu=ssm_w_glu, ssm_w_out=ssm_w_out, mem_w_kv=mem_w_kv, mem_w_out=mem_w_out,
             w_o=w_o, ln1_g=ln1_g, ln1_b=ln1_b, ffn_w_gate_up=ffn_w_gate_up, ffn_w_down=ffn_w_down,
             ln2_g=ln2_g, ln2_b=ln2_b)
    M1 = dict(w_in=m_w_in, b_in=m_b_in, sb_w_out=m_sb_w_out, ssm_lambda_re=m_ssm_lambda_re,
              ssm_lambda_im=m_ssm_lambda_im, ssm_log_dt=m_ssm_log_dt, ssm_b_re=m_ssm_b_re, ssm_b_im=m_ssm_b_im,
              ssm_c_re=m_ssm_c_re, ssm_c_im=m_ssm_c_im, ssm_d=m_ssm_d, ssm_w_glu=m_ssm_w_glu,
              ssm_w_out=m_ssm_w_out, mem_w_kv=m_mem_w_kv, mem_w_out=m_mem_w_out, w_o=m_w_o, ln1_g=m_ln1_g,
              ln1_b=m_ln1_b, ffn_w_gate_up=m_ffn_w_gate_up, ffn_w_down=m_ffn_w_down, ln2_g=m_ln2_g, ln2_b=m_ln2_b)
    V1 = dict(w_in=v_w_in, b_in=v_b_in, sb_w_out=v_sb_w_out, ssm_lambda_re=v_ssm_lambda_re,
              ssm_lambda_im=v_ssm_lambda_im, ssm_log_dt=v_ssm_log_dt, ssm_b_re=v_ssm_b_re, ssm_b_im=v_ssm_b_im,
              ssm_c_re=v_ssm_c_re, ssm_c_im=v_ssm_c_im, ssm_d=v_ssm_d, ssm_w_glu=v_ssm_w_glu,
              ssm_w_out=v_ssm_w_out, mem_w_kv=v_mem_w_kv, mem_w_out=v_mem_w_out, w_o=v_w_o, ln1_g=v_ln1_g,
              ln1_b=v_ln1_b, ffn_w_gate_up=v_ffn_w_gate_up, ffn_w_down=v_ffn_w_down, ln2_g=v_ln2_g, ln2_b=v_ln2_b)
    return _step(x, mem, loss_target, W, M1, V1)
```

```python
import functools
import math

import jax
import jax.numpy as jnp
from jax import lax
from jax.experimental import pallas as pl
from jax.experimental.pallas import tpu as pltpu

F32 = jnp.float32
BF16 = jnp.bfloat16
MESH = pl.DeviceIdType.MESH
ANY = pl.BlockSpec(memory_space=pl.ANY)
VMEM = pl.BlockSpec(memory_space=pltpu.VMEM)

HEAD_DIM = 128
SSM_GROUP = 16
N_CHIPS = 4
N_DEV = 8
LN_EPS = 1e-5
ADAM_LR = 0.001
ADAM_B1 = 0.9
ADAM_B2 = 0.999
ADAM_EPS = 1e-08
ADAM_WD = 0.01
ADAM_STEP = 10
LANES = 128
SUBLANES = 8
VMEM_LIMIT_MB = 56


def _cparams(sem, mb=VMEM_LIMIT_MB):
    return pltpu.CompilerParams(dimension_semantics=sem, vmem_limit_bytes=mb * 1024 * 1024)


def _tile(n, pref, mult=LANES):
    best = None
    t = mult
    while t <= min(n, pref):
        if n % t == 0:
            best = t
        t += mult
    return n if best is None else best


def _dot(a, b, dims=(((1,), (0,)), ((), ()))):
    return lax.dot_general(a, b, dims, preferred_element_type=F32)


NT = (((1,), (1,)), ((), ()))
TN = (((0,), (0,)), ((), ()))


def _split2(x):
    hi = x.astype(BF16)
    lo = (x - hi.astype(F32)).astype(BF16)
    return hi, lo


def _dot_a2(a, b, dims=(((1,), (0,)), ((), ()))):
    ah, al = _split2(a)
    bb = b.astype(BF16)
    return _dot(ah, bb, dims) + _dot(al, bb, dims)


def _bd_blocks(n_in, n_out):
    if (n_in // 2) % LANES == 0 and (n_out // 2) % LANES == 0:
        return [(slice(0, n_in // 2), slice(0, n_out // 2)), (slice(n_in // 2, n_in), slice(n_out // 2, n_out))]
    return [(slice(0, n_in), slice(0, n_out))]


def _dot_mask(x, u2):
    hi, lo = _split2(x)
    return _dot(jnp.concatenate([hi, lo], axis=1), u2)


def _tri2(n, rel):
    ri = lax.broadcasted_iota(jnp.int32, (2 * n, n), 0)
    ci = lax.broadcasted_iota(jnp.int32, (2 * n, n), 1)
    ri = jnp.where(ri >= n, ri - n, ri)
    return rel(ri, ci).astype(BF16)


def _mm_call(a, b, *, dims, grid, a_spec, b_spec, out_spec, out_shape, name,
             bias=None, bias_spec=None, add=None, add_spec=None, add_scale=1.0, comm=None, prod=None):
    nk = grid[2]
    has_bias = bias is not None
    has_add = add is not None
    n_ci = len(comm["ins"]) if comm else 0
    n_co = len(comm["out_shapes"]) if comm else 0

    def body(*refs):
        a_ref, b_ref = refs[0], refs[1]
        pos = 2
        bias_ref = refs[pos] if has_bias else None
        pos += int(has_bias)
        add_ref = refs[pos] if has_add else None
        pos += int(has_add)
        comm_in = refs[pos:pos + n_ci]
        pos += n_ci
        o_ref = refs[pos]
        comm_out = refs[pos + 1:pos + 1 + n_co]
        pos += 1 + n_co
        acc_ref = refs[pos] if nk > 1 else None
        pos += int(nk > 1)
        if comm:
            ex_start, ex_middle, ex_finish = comm["hooks"](comm_in, comm_out, refs[pos], refs[pos + 1])
            ids = [pl.program_id(d) for d in range(3)]

            @pl.when((ids[0] == 0) & (ids[1] == 0) & (ids[2] == 0))
            def _():
                ex_start()

        if prod is None:
            p = _dot(a_ref[...].astype(BF16), b_ref[...].astype(BF16), dims)
        else:
            p = prod(a_ref, b_ref)

        def finish(acc):
            if has_bias:
                acc = acc + bias_ref[...]
            if has_add:
                acc = acc + add_scale * add_ref[...].astype(F32)
            o_ref[...] = acc.astype(o_ref.dtype)

        if nk == 1:
            finish(p)
        else:
            k = pl.program_id(2)

            @pl.when(k == 0)
            def _():
                acc_ref[...] = p

            @pl.when(k > 0)
            def _():
                acc_ref[...] += p

            @pl.when(k == nk - 1)
            def _():
                finish(acc_ref[...])

        if comm:
            @pl.when((ids[0] == grid[0] - 1) & (ids[1] == grid[1] - 1) & (ids[2] == grid[2] - 1))
            def _():
                if ex_middle is not None:
                    ex_middle()
                ex_finish()

    ins = [a, b]
    in_specs = [a_spec, b_spec]
    if has_bias:
        ins.append(bias)
        in_specs.append(bias_spec)
    if has_add:
        ins.append(add)
        in_specs.append(add_spec)
    scratch = []
    if nk > 1:
        blk = [d for d in out_spec.block_shape if d is not None]
        scratch.append(pltpu.VMEM(tuple(blk), F32))
    if not comm:
        return pl.pallas_call(
            body, out_shape=out_shape, grid=grid, in_specs=in_specs, out_specs=out_spec,
            scratch_shapes=scratch, name=name,
            compiler_params=_cparams(("parallel", "parallel", "arbitrary")),
        )(*ins)
    n_main = len(ins)
    scratch += [pltpu.SemaphoreType.DMA((comm["n_sems"],)), pltpu.SemaphoreType.DMA((comm["n_sems"],))]
    res = pl.pallas_call(
        body, out_shape=[out_shape] + list(comm["out_shapes"]), grid=grid,
        in_specs=in_specs + [ANY] * n_ci, out_specs=[out_spec] + [ANY] * n_co, scratch_shapes=scratch,
        input_output_aliases={n_main + i: 1 + i for i in range(n_ci)} if comm["alias"] else {},
        name=name, compiler_params=_cparams(("arbitrary", "arbitrary", "arbitrary")),
    )(*ins, *comm["ins"])
    return res[0], list(res[1:])


def _mm_nn(a, w, kind, *, bias=None, out_dtype=F32, name, comm=None):
    M, K = a.shape
    tm = _tile(M, 1024, SUBLANES)
    tk = K if K <= 2048 else _tile(K, 1408)
    if kind == "col":
        Nc = w.shape[2]
        N = N_CHIPS * Nc
        tn = _tile(Nc, 1408)
        npc = Nc // tn
        b_spec = pl.BlockSpec((None, tk, tn), lambda i, j, k: (j // npc, k, j % npc))
    else:
        w = w.reshape(-1, w.shape[-1])
        N = w.shape[1]
        tn = _tile(N, 2048)
        if K > 2048 and a.dtype == BF16:
            tk, tn = K, _tile(N, 512)
        b_spec = pl.BlockSpec((tk, tn), lambda i, j, k: (k, j))
    grid = (M // tm, N // tn, K // tk)
    bias_spec = pl.BlockSpec((1, tn), lambda i, j, k: (0, j)) if bias is not None else None
    return _mm_call(
        a, w, dims=(((1,), (0,)), ((), ())), grid=grid,
        a_spec=pl.BlockSpec((tm, tk), lambda i, j, k: (i, k)), b_spec=b_spec,
        out_spec=pl.BlockSpec((tm, tn), lambda i, j, k: (i, j)),
        out_shape=jax.ShapeDtypeStruct((M, N), out_dtype), name=name,
        bias=bias, bias_spec=bias_spec, comm=comm)


def _mm_nt(dy, w, kind, *, add=None, add_scale=1.0, out_dtype=F32, name, comm=None):
    M, N = dy.shape
    tm = _tile(M, 512, SUBLANES)
    if kind == "col" and dy.dtype == BF16:
        K, Nc = w.shape[1], w.shape[2]
        cps = 2
        tm = _tile(M, 1024, SUBLANES)
        tko = _tile(K, 512)

        def prod(a_ref, b_ref):
            acc = None
            for c in range(cps):
                t = _dot(a_ref[:, c * Nc:(c + 1) * Nc], b_ref[c], NT)
                acc = t if acc is None else acc + t
            return acc

        add_spec = pl.BlockSpec((tm, tko), lambda i, j, r: (i, j)) if add is not None else None
        return _mm_call(
            dy, w, dims=NT, grid=(M // tm, K // tko, N_CHIPS // cps), prod=prod,
            a_spec=pl.BlockSpec((tm, cps * Nc), lambda i, j, r: (i, r)),
            b_spec=pl.BlockSpec((cps, tko, Nc), lambda i, j, r: (r, j, 0)),
            out_spec=pl.BlockSpec((tm, tko), lambda i, j, r: (i, j)),
            out_shape=jax.ShapeDtypeStruct((M, K), out_dtype), name=name,
            add=add, add_spec=add_spec, add_scale=add_scale, comm=comm)
    if kind == "col":
        K, Nc = w.shape[1], w.shape[2]
        tn = _tile(Nc, 1408)
        npc = Nc // tn
        tko = _tile(K, 2048)
        b_spec = pl.BlockSpec((None, tko, tn), lambda i, j, r: (r // npc, j, r % npc))
    else:
        w = w.reshape(-1, w.shape[-1])
        K = w.shape[0]
        tn = _tile(N, 2048)
        tko = _tile(K, 2048)
        b_spec = pl.BlockSpec((tko, tn), lambda i, j, r: (j, r))
    grid = (M // tm, K // tko, N // tn)
    add_spec = pl.BlockSpec((tm, tko), lambda i, j, r: (i, j)) if add is not None else None
    return _mm_call(
        dy, w, dims=NT, grid=grid,
        a_spec=pl.BlockSpec((tm, tn), lambda i, j, r: (i, r)), b_spec=b_spec,
        out_spec=pl.BlockSpec((tm, tko), lambda i, j, r: (i, j)),
        out_shape=jax.ShapeDtypeStruct((M, K), out_dtype), name=name,
        add=add, add_spec=add_spec, add_scale=add_scale, comm=comm)


def _mm_tn(a, dy, kind, *, name):
    M, K = a.shape
    N = dy.shape[1]
    tm = M if a.dtype == BF16 and dy.dtype == BF16 else _tile(M, 2048, SUBLANES)
    tkw = _tile(K, 512)
    tn = _tile(N // N_CHIPS, 1408) if kind == "col" else _tile(N, 1024)
    a_inner = tkw * a.dtype.itemsize <= tn * dy.dtype.itemsize
    ij = (lambda o, n: (n, o)) if a_inner else (lambda o, n: (o, n))
    if kind == "col":
        Nc = N // N_CHIPS
        npc = Nc // tn

        def out_map(o, n, m):
            i, j = ij(o, n)
            return (j // npc, i, j % npc)

        out_spec = pl.BlockSpec((None, tkw, tn), out_map)
        out_shape = jax.ShapeDtypeStruct((N_CHIPS, K, Nc), F32)
    else:
        out_spec = pl.BlockSpec((tkw, tn), lambda o, n, m: ij(o, n))
        out_shape = jax.ShapeDtypeStruct((K, N), F32)
    ni, nj = K // tkw, N // tn
    grid = (nj, ni, M // tm) if a_inner else (ni, nj, M // tm)
    out = _mm_call(
        a, dy, dims=TN, grid=grid,
        a_spec=pl.BlockSpec((tm, tkw), lambda o, n, m: (m, ij(o, n)[0])),
        b_spec=pl.BlockSpec((tm, tn), lambda o, n, m: (m, ij(o, n)[1])),
        out_spec=out_spec, out_shape=out_shape, name=name)
    if kind == "row":
        out = out.reshape(N_CHIPS, K // N_CHIPS, N)
    return out


def _gelu_grad(x):
    k = math.sqrt(2.0 / math.pi)
    inner = k * (x + 0.044715 * x * x * x)
    t = jnp.tanh(inner)
    return 0.5 * (1.0 + t) + 0.5 * x * (1.0 - t * t) * k * (1.0 + 3.0 * 0.044715 * x * x)


def _ln_fwd(xin, delta, g, b, alpha):
    S, D = xin.shape
    ts = _tile(S, 256, SUBLANES)

    def body(x_ref, d_ref, g_ref, b_ref, y_ref, yb_ref, xh_ref, rs_ref):
        r = alpha * x_ref[...] + d_ref[...]
        mu = jnp.mean(r, axis=-1, keepdims=True)
        rc = r - mu
        var = jnp.mean(rc * rc, axis=-1, keepdims=True)
        rstd = lax.rsqrt(var + LN_EPS)
        xh = rc * rstd
        y = xh * g_ref[...] + b_ref[...]
        y_ref[...] = y
        yb_ref[...] = y.astype(BF16)
        xh_ref[...] = xh
        rs_ref[...] = rstd

    row = pl.BlockSpec((ts, D), lambda i: (i, 0))
    vec = pl.BlockSpec((1, D), lambda i: (0, 0))
    return pl.pallas_call(
        body, grid=(S // ts,), in_specs=[row, row, vec, vec],
        out_specs=[row, row, row, pl.BlockSpec((ts, 1), lambda i: (i, 0))],
        out_shape=[jax.ShapeDtypeStruct((S, D), F32), jax.ShapeDtypeStruct((S, D), BF16),
                   jax.ShapeDtypeStruct((S, D), F32), jax.ShapeDtypeStruct((S, 1), F32)],
        name="ln_fwd", compiler_params=_cparams(("parallel",)),
    )(xin, delta, g, b)


def _ln_bwd(dy, xh, rstd, g):
    S, D = dy.shape
    ts = _tile(S, 256, SUBLANES)

    def body(dy_ref, xh_ref, rs_ref, g_ref, dr_ref, drb_ref, dg_ref, db_ref):
        @pl.when(pl.program_id(0) == 0)
        def _():
            dg_ref[...] = jnp.zeros_like(dg_ref)
            db_ref[...] = jnp.zeros_like(db_ref)

        dyv = dy_ref[...]
        xhv = xh_ref[...]
        dyg = dyv * g_ref[...]
        m1 = jnp.mean(dyg, axis=-1, keepdims=True)
        m2 = jnp.mean(dyg * xhv, axis=-1, keepdims=True)
        dr = rs_ref[...] * (dyg - m1 - xhv * m2)
        dr_ref[...] = dr
        drb_ref[...] = dr.astype(BF16)
        dg_ref[...] += jnp.sum(dyv * xhv, axis=0, keepdims=True)
        db_ref[...] += jnp.sum(dyv, axis=0, keepdims=True)

    row = pl.BlockSpec((ts, D), lambda i: (i, 0))
    vec = pl.BlockSpec((1, D), lambda i: (0, 0))
    return pl.pallas_call(
        body, grid=(S // ts,),
        in_specs=[row, row, pl.BlockSpec((ts, 1), lambda i: (i, 0)), vec],
        out_specs=[row, row, vec, vec],
        out_shape=[jax.ShapeDtypeStruct((S, D), F32), jax.ShapeDtypeStruct((S, D), BF16),
                   jax.ShapeDtypeStruct((1, D), F32), jax.ShapeDtypeStruct((1, D), F32)],
        name="ln_bwd", compiler_params=_cparams(("arbitrary",)),
    )(dy, xh, rstd, g)


def _merge_fwd(proj, gate_off, p_sb, p_ssm, p_mem):
    S, D = p_sb.shape
    ts = _tile(S, 256, SUBLANES)
    gb = gate_off // D

    def body(g0, g1, g2, a0, a1, a2, o_ref):
        o_ref[...] = (jax.nn.sigmoid(g0[...]) * a0[...] + jax.nn.sigmoid(g1[...]) * a1[...]
                      + jax.nn.sigmoid(g2[...]) * a2[...]).astype(o_ref.dtype)

    row = pl.BlockSpec((ts, D), lambda i: (i, 0))
    gates = [pl.BlockSpec((ts, D), functools.partial(lambda i, n: (i, gb + n), n=n)) for n in range(3)]
    return pl.pallas_call(
        body, grid=(S // ts,), in_specs=gates + [row, row, row], out_specs=row,
        out_shape=jax.ShapeDtypeStruct((S, D), BF16), name="merge_fwd",
        compiler_params=_cparams(("parallel",)),
    )(proj, proj, proj, p_sb, p_ssm, p_mem)


def _merge_bwd(dmerged, proj, gate_off, p_sb, p_ssm, p_mem):
    S, D = p_sb.shape
    ts = _tile(S, 256, SUBLANES)
    gb = gate_off // D

    def body(dm_ref, g0, g1, g2, a0, a1, a2, d0, d1, d2, l0, l1, l2):
        dm = dm_ref[...]
        for g_ref, a_ref, d_ref, l_ref in ((g0, a0, d0, l0), (g1, a1, d1, l1), (g2, a2, d2, l2)):
            s = jax.nn.sigmoid(g_ref[...])
            d_ref[...] = (dm * s).astype(d_ref.dtype)
            l_ref[...] = dm * a_ref[...] * s * (1.0 - s)

    row = pl.BlockSpec((ts, D), lambda i: (i, 0))
    gates = [pl.BlockSpec((ts, D), functools.partial(lambda i, n: (i, gb + n), n=n)) for n in range(3)]
    sd = jax.ShapeDtypeStruct((S, D), F32)
    return pl.pallas_call(
        body, grid=(S // ts,), in_specs=[row] + gates + [row, row, row], out_specs=[row] * 6,
        out_shape=[jax.ShapeDtypeStruct((S, D), BF16)] * 3 + [sd] * 3, name="merge_bwd", compiler_params=_cparams(("parallel",)),
    )(dmerged, proj, proj, proj, p_sb, p_ssm, p_mem)


def _glu_fwd(glu):
    S, W2 = glu.shape
    W = W2 // 2
    ts = _tile(S, 512, SUBLANES)

    def body(x_ref, o_ref):
        o_ref[...] = (x_ref[:, :W] * jax.nn.sigmoid(x_ref[:, W:])).astype(o_ref.dtype)

    return pl.pallas_call(
        body, grid=(S // ts,), in_specs=[pl.BlockSpec((ts, W2), lambda i: (i, 0))],
        out_specs=pl.BlockSpec((ts, W), lambda i: (i, 0)),
        out_shape=jax.ShapeDtypeStruct((S, W), BF16), name="glu_fwd",
        compiler_params=_cparams(("parallel",)),
    )(glu)


def _glu_bwd(dzz, glu):
    S, W2 = glu.shape
    W = W2 // 2
    ts = _tile(S, 512, SUBLANES)

    def body(d_ref, x_ref, o_ref):
        d = d_ref[...]
        a = x_ref[:, :W]
        s = jax.nn.sigmoid(x_ref[:, W:])
        o_ref[:, :W] = (d * s).astype(o_ref.dtype)
        o_ref[:, W:] = (d * a * s * (1.0 - s)).astype(o_ref.dtype)

    return pl.pallas_call(
        body, grid=(S // ts,),
        in_specs=[pl.BlockSpec((ts, W), lambda i: (i, 0)), pl.BlockSpec((ts, W2), lambda i: (i, 0))],
        out_specs=pl.BlockSpec((ts, W2), lambda i: (i, 0)),
        out_shape=jax.ShapeDtypeStruct((S, W2), BF16), name="glu_bwd",
        compiler_params=_cparams(("parallel",)),
    )(dzz, glu)


def _swiglu_fwd(gu):
    S, F2 = gu.shape
    Fh = F2 // 2
    ts = _tile(S, 128, SUBLANES)

    def body(x_ref, o_ref):
        fg = x_ref[:, :Fh]
        o_ref[...] = (fg * jax.nn.sigmoid(fg) * x_ref[:, Fh:]).astype(o_ref.dtype)

    return pl.pallas_call(
        body, grid=(S // ts,), in_specs=[pl.BlockSpec((ts, F2), lambda i: (i, 0))],
        out_specs=pl.BlockSpec((ts, Fh), lambda i: (i, 0)),
        out_shape=jax.ShapeDtypeStruct((S, Fh), BF16), name="swiglu_fwd",
        compiler_params=_cparams(("parallel",)),
    )(gu)


def _swiglu_bwd(dhid, gu):
    S, F2 = gu.shape
    Fh = F2 // 2
    ts = _tile(S, 128, SUBLANES)

    def body(d_ref, x_ref, o_ref):
        d = d_ref[...]
        fg = x_ref[:, :Fh]
        fu = x_ref[:, Fh:]
        s = jax.nn.sigmoid(fg)
        o_ref[:, :Fh] = (d * fu * s * (1.0 + fg * (1.0 - s))).astype(o_ref.dtype)
        o_ref[:, Fh:] = (d * fg * s).astype(o_ref.dtype)

    return pl.pallas_call(
        body, grid=(S // ts,),
        in_specs=[pl.BlockSpec((ts, Fh), lambda i: (i, 0)), pl.BlockSpec((ts, F2), lambda i: (i, 0))],
        out_specs=pl.BlockSpec((ts, F2), lambda i: (i, 0)),
        out_shape=jax.ShapeDtypeStruct((S, F2), BF16), name="swiglu_bwd",
        compiler_params=_cparams(("parallel",)),
    )(dhid, gu)


def _assemble_dproj(pieces):
    S = pieces[0].shape[0]
    widths = [p.shape[1] for p in pieces]
    total = sum(widths)
    ts = _tile(S, 128, SUBLANES)
    n = len(pieces)

    def body(*refs):
        o_ref, b_ref = refs[n], refs[n + 1]

        @pl.when(pl.program_id(0) == 0)
        def _():
            b_ref[...] = jnp.zeros_like(b_ref)

        off = 0
        for r, w in zip(refs[:n], widths):
            v = r[...].astype(F32)
            o_ref[:, off:off + w] = v.astype(o_ref.dtype)
            b_ref[:, off:off + w] += jnp.sum(v, axis=0, keepdims=True)
            off += w

    return pl.pallas_call(
        body, grid=(S // ts,),
        in_specs=[pl.BlockSpec((ts, w), lambda i: (i, 0)) for w in widths],
        out_specs=[pl.BlockSpec((ts, total), lambda i: (i, 0)), pl.BlockSpec((1, total), lambda i: (0, 0))],
        out_shape=[jax.ShapeDtypeStruct((S, total), BF16), jax.ShapeDtypeStruct((1, total), F32)],
        name="assemble_dproj", compiler_params=_cparams(("arbitrary",)),
    )(*pieces)


def _loss_head(y, target):
    S, D = y.shape
    ts = _tile(S, 256, SUBLANES)

    def body(y_ref, t_ref, dy_ref, l_ref):
        @pl.when(pl.program_id(0) == 0)
        def _():
            l_ref[...] = jnp.zeros_like(l_ref)

        e = y_ref[...] - t_ref[...]
        dy_ref[...] = e * (1.0 / D)
        part = jnp.sum(jnp.sum(e * e, axis=1, keepdims=True), axis=0, keepdims=True) * (0.5 / D)
        l_ref[...] += jnp.broadcast_to(part, l_ref.shape)

    row = pl.BlockSpec((ts, D), lambda i: (i, 0))
    return pl.pallas_call(
        body, grid=(S // ts,), in_specs=[row, row],
        out_specs=[row, pl.BlockSpec((1, LANES), lambda i: (0, 0))],
        out_shape=[jax.ShapeDtypeStruct((S, D), F32), jax.ShapeDtypeStruct((1, LANES), F32)],
        name="loss_head", compiler_params=_cparams(("arbitrary",)),
    )(y, target)


SB_TQ = 512
SB_TK = 512


def _sb_tile_terms(q, kb, scale, causal):
    z = _dot(q, kb, NT) * scale
    soft = jnp.log(1.0 + jnp.exp(-jnp.abs(z)))
    ls = jnp.minimum(z, 0.0) - soft
    l1m = jnp.minimum(-z, 0.0) - soft
    if causal is not None:
        l1m = jnp.where(causal, l1m, 0.0)
    return ls, l1m


def _sb_causal(qi, kj, TQ, TK):
    t_idx = qi * TQ + lax.broadcasted_iota(jnp.int32, (TQ, TK), 0)
    s_idx = kj * TK + lax.broadcasted_iota(jnp.int32, (TQ, TK), 1)
    return s_idx < t_idx


def _exchange_begin(heads, nq, start, middle=None):
    h, qi = pl.program_id(0), pl.program_id(1)

    @pl.when((h == 0) & (qi == 0))
    def _():
        start()

    if middle is not None:
        @pl.when((h == heads - 1) & (qi == nq - 1))
        def _():
            middle()


def _exchange_end(heads, nq, finish):
    @pl.when((pl.program_id(0) == heads - 1) & (pl.program_id(1) == nq - 1))
    def _():
        finish()


def _sb_fwd(proj, q_off, k_off, v_off, heads, ag=None):
    S = proj.shape[0]
    Dh = HEAD_DIM
    TQ = min(SB_TQ, S)
    TK = min(SB_TK, TQ)
    nq = S // TQ
    scale = Dh ** -0.5
    qb, kb0, vb0 = q_off // Dh, k_off // Dh, v_off // Dh
    n_ag = 0 if ag is None else len(ag)
    assert ag is None or heads >= 2

    def body(q_ref, k_ref, v_ref, *rest):
        o_ref, c_ref = rest[n_ag:n_ag + 2]
        if n_ag:
            ag_start, ag_middle, ag_finish = _ag_hooks(rest[n_ag + 2:2 * n_ag + 2], *rest[2 * n_ag + 2:])
            _exchange_begin(heads, nq, ag_start, ag_middle)
        qi = pl.program_id(1)
        q = q_ref[...].astype(BF16)
        upper = _tri2(TK, lambda j, s: j > s)
        nfull = (qi * TQ) // TK

        def block(kj, carry, masked):
            c, acc = carry
            off = pl.multiple_of(kj * TK, TK)
            kblk = k_ref[pl.ds(off, TK), :].astype(BF16)
            vblk = v_ref[pl.ds(off, TK), :].astype(BF16)
            causal = _sb_causal(qi, kj, TQ, TK) if masked else None
            ls, l1m = _sb_tile_terms(q, kblk, scale, causal)
            w = jnp.exp(ls + _dot_mask(l1m, upper) + c)
            if masked:
                w = jnp.where(causal, w, 0.0)
            acc = acc + _dot(w.astype(BF16), vblk)
            c = c + jnp.sum(l1m, axis=1, keepdims=True)
            return c, acc

        carry = (jnp.zeros((TQ, 1), F32), jnp.zeros((TQ, Dh), F32))
        for d in range(TQ // TK - 1, -1, -1):
            carry = block(nfull + d, carry, True)

        c, acc = lax.fori_loop(0, nfull, lambda jj, carry: block(nfull - 1 - jj, carry, False), carry)
        o_ref[...] = acc.astype(o_ref.dtype)
        c_ref[...] = c
        if n_ag:
            _exchange_end(heads, nq, ag_finish)

    ag = [] if ag is None else list(ag)
    res = pl.pallas_call(
        body, grid=(heads, nq),
        in_specs=[pl.BlockSpec((TQ, Dh), lambda h, i: (i, qb + h)),
                  pl.BlockSpec((S, Dh), lambda h, i: (0, kb0 + h)),
                  pl.BlockSpec((S, Dh), lambda h, i: (0, vb0 + h))] + [ANY] * n_ag,
        out_specs=[pl.BlockSpec((TQ, Dh), lambda h, i: (i, h)),
                   pl.BlockSpec((None, TQ, 1), lambda h, i: (h, i, 0))] + [ANY] * n_ag,
        out_shape=[jax.ShapeDtypeStruct((S, heads * Dh), BF16), jax.ShapeDtypeStruct((heads, S, 1), F32)]
        + [jax.ShapeDtypeStruct(b.shape, b.dtype) for b in ag],
        input_output_aliases={3 + i: 2 + i for i in range(n_ag)},
        scratch_shapes=[pltpu.SemaphoreType.DMA((n_ag * 6,)), pltpu.SemaphoreType.DMA((n_ag * 6,))] if n_ag else [],
        name="sb_fwd_ag" if n_ag else "sb_fwd",
        compiler_params=_cparams(("arbitrary", "arbitrary") if n_ag else ("parallel", "arbitrary")),
    )(proj, proj, proj, *ag)
    return res[0], res[1], list(res[2:])


def _sb_bwd(proj, q_off, k_off, v_off, heads, dout, ctot, rs=None):
    S = proj.shape[0]
    Dh = HEAD_DIM
    TQ = min(SB_TQ, S)
    TK = min(SB_TK, TQ)
    nq = S // TQ
    scale = Dh ** -0.5
    qb, kb0, vb0 = q_off // Dh, k_off // Dh, v_off // Dh
    n_rs = 0 if rs is None else len(rs)

    def body(q_ref, k_ref, v_ref, do_ref, c_ref, *rest):
        dq_ref, dk_ref, dv_ref = rest[n_rs:n_rs + 3]
        if n_rs:
            rs_start, rs_finish = _rs_chips_hooks(rest[:n_rs], rest[n_rs + 3:2 * n_rs + 3], *rest[2 * n_rs + 3:])
            _exchange_begin(heads, nq, rs_start)
        qi = pl.program_id(1)

        @pl.when(qi == 0)
        def _():
            dk_ref[...] = jnp.zeros_like(dk_ref)
            dv_ref[...] = jnp.zeros_like(dv_ref)

        q = q_ref[...].astype(BF16)
        do = do_ref[...].astype(BF16)
        ctot = c_ref[...]
        lower_incl = _tri2(TK, lambda j, s: j <= s)
        lower = _tri2(TK, lambda j, s: j < s)
        nfull = (qi * TQ) // TK

        def block(kj, carry, masked):
            cl, ce, dq = carry
            off = pl.multiple_of(kj * TK, TK)
            kblk = k_ref[pl.ds(off, TK), :].astype(BF16)
            vblk = v_ref[pl.ds(off, TK), :].astype(BF16)
            causal = _sb_causal(qi, kj, TQ, TK) if masked else None
            ls, l1m = _sb_tile_terms(q, kblk, scale, causal)
            w = jnp.exp(ls + (ctot - cl - _dot_mask(l1m, lower_incl)))
            if masked:
                w = jnp.where(causal, w, 0.0)
            e = w * _dot(do, vblk, NT)
            before = ce + _dot_mask(e, lower)
            beta = jnp.exp(ls)
            dz = (e * (1.0 - beta) - beta * before) * scale
            if masked:
                dz = jnp.where(causal, dz, 0.0)
            dzb = dz.astype(BF16)
            dq = dq + _dot(dzb, kblk)
            dk_ref[pl.ds(off, TK), :] += _dot(dzb, q, TN)
            dv_ref[pl.ds(off, TK), :] += _dot(w.astype(BF16), do, TN)
            cl = cl + jnp.sum(l1m, axis=1, keepdims=True)
            ce = ce + jnp.sum(e, axis=1, keepdims=True)
            return cl, ce, dq

        zero = jnp.zeros((TQ, 1), F32)
        carry = lax.fori_loop(0, nfull, lambda kj, carry: block(kj, carry, False),
                              (zero, zero, jnp.zeros((TQ, Dh), F32)))
        for d in range(TQ // TK):
            carry = block(nfull + d, carry, True)
        dq_ref[...] = carry[2]
        if n_rs:
            _exchange_end(heads, nq, rs_finish)

    rs = [] if rs is None else list(rs)
    blk = pl.BlockSpec((TQ, Dh), lambda h, i: (i, h))
    col = pl.BlockSpec((S, Dh), lambda h, i: (0, h))
    sd = jax.ShapeDtypeStruct((S, heads * Dh), F32)
    res = pl.pallas_call(
        body, grid=(heads, nq),
        in_specs=[pl.BlockSpec((TQ, Dh), lambda h, i: (i, qb + h)),
                  pl.BlockSpec((S, Dh), lambda h, i: (0, kb0 + h)),
                  pl.BlockSpec((S, Dh), lambda h, i: (0, vb0 + h)), blk,
                  pl.BlockSpec((None, TQ, 1), lambda h, i: (h, i, 0))] + [ANY] * n_rs,
        out_specs=[blk, col, col] + [ANY] * n_rs, out_shape=[sd, sd, sd] + _rs_chips_shapes(rs),
        scratch_shapes=[pltpu.SemaphoreType.DMA((n_rs * 3,)), pltpu.SemaphoreType.DMA((n_rs * 3,))] if n_rs else [],
        name="sb_bwd_rs" if n_rs else "sb_bwd",
        compiler_params=_cparams(("arbitrary", "arbitrary") if n_rs else ("parallel", "arbitrary")),
    )(proj, proj, proj, dout, ctot, *rs)
    return res[0], res[1], res[2], list(res[3:])


def _mem_probs(qh, kh, scale):
    s = _dot(qh, kh, NT) * scale
    m = jnp.max(s, axis=-1, keepdims=True)
    p = jnp.exp(s - m)
    return p / jnp.sum(p, axis=-1, keepdims=True)


def _mem_fwd(proj, q_off, width, kv):
    S = proj.shape[0]
    Dh = HEAD_DIM
    heads = width // Dh
    ts = _tile(S, 512, SUBLANES)
    scale = Dh ** -0.5
    M = kv.shape[0]

    def body(q_ref, kv_ref, o_ref):
        for h in range(heads):
            qh = q_ref[:, h * Dh:(h + 1) * Dh].astype(BF16)
            kh = kv_ref[:, h * Dh:(h + 1) * Dh].astype(BF16)
            vh = kv_ref[:, width + h * Dh:width + (h + 1) * Dh].astype(BF16)
            p = _mem_probs(qh, kh, scale)
            o_ref[:, h * Dh:(h + 1) * Dh] = _dot(p.astype(BF16), vh).astype(o_ref.dtype)

    return pl.pallas_call(
        body, grid=(S // ts,),
        in_specs=[pl.BlockSpec((ts, width), lambda i: (i, q_off // width)),
                  pl.BlockSpec((M, 2 * width), lambda i: (0, 0))],
        out_specs=pl.BlockSpec((ts, width), lambda i: (i, 0)),
        out_shape=jax.ShapeDtypeStruct((S, width), BF16), name="mem_fwd",
        compiler_params=_cparams(("parallel",)),
    )(proj, kv)


def _mem_bwd(proj, q_off, width, kv, dmm):
    S = proj.shape[0]
    Dh = HEAD_DIM
    heads = width // Dh
    ts = _tile(S, 512, SUBLANES)
    scale = Dh ** -0.5
    M = kv.shape[0]

    def body(q_ref, kv_ref, d_ref, dq_ref, dkv_ref):
        @pl.when(pl.program_id(0) == 0)
        def _():
            dkv_ref[...] = jnp.zeros_like(dkv_ref)

        for h in range(heads):
            qh = q_ref[:, h * Dh:(h + 1) * Dh].astype(BF16)
            kh = kv_ref[:, h * Dh:(h + 1) * Dh].astype(BF16)
            vh = kv_ref[:, width + h * Dh:width + (h + 1) * Dh].astype(BF16)
            dh = d_ref[:, h * Dh:(h + 1) * Dh].astype(BF16)
            p = _mem_probs(qh, kh, scale)
            dp = _dot(dh, vh, NT)
            ds = p * (dp - jnp.sum(dp * p, axis=-1, keepdims=True)) * scale
            dsb = ds.astype(BF16)
            dq_ref[:, h * Dh:(h + 1) * Dh] = _dot(dsb, kh)
            dkv_ref[:, h * Dh:(h + 1) * Dh] += _dot(dsb, qh, TN)
            dkv_ref[:, width + h * Dh:width + (h + 1) * Dh] += _dot(p.astype(BF16), dh, TN)

    row = pl.BlockSpec((ts, width), lambda i: (i, 0))
    full = pl.BlockSpec((M, 2 * width), lambda i: (0, 0))
    return pl.pallas_call(
        body, grid=(S // ts,),
        in_specs=[pl.BlockSpec((ts, width), lambda i: (i, q_off // width)), full, row],
        out_specs=[row, full],
        out_shape=[jax.ShapeDtypeStruct((S, width), F32), jax.ShapeDtypeStruct((M, 2 * width), F32)],
        name="mem_bwd", compiler_params=_cparams(("arbitrary",)),
    )(proj, kv, dmm)


def _disc_math(lre, lim, logdt, bre_t, bim_t):
    dt = jnp.exp(logdt)
    mag = jnp.exp(lre * dt)
    ang = lim * dt
    a = mag * jnp.cos(ang)
    b = mag * jnp.sin(ang)
    den = lre * lre + lim * lim
    nr = a - 1.0
    fre = (nr * lre + b * lim) / den
    fim = (b * lre - nr * lim) / den
    bbre = fre * bre_t - fim * bim_t
    bbim = fre * bim_t + fim * bre_t
    return a, b, bbre, bbim


def _s5_disc(lre, lim, logdt, bre_t, bim_t):
    G, _, P = lre.shape
    C = bre_t.shape[1]

    def body(lre_ref, lim_ref, dt_ref, br_ref, bi_ref, a_ref, b_ref, bbre_ref, bbim_ref):
        a, b, bbre, bbim = _disc_math(lre_ref[...], lim_ref[...], dt_ref[...], br_ref[...], bi_ref[...])
        a_ref[...] = a
        b_ref[...] = b
        bbre_ref[...] = bbre
        bbim_ref[...] = bbim

    gp = jax.ShapeDtypeStruct((G, 1, P), F32)
    gcp = jax.ShapeDtypeStruct((G, C, P), F32)
    return pl.pallas_call(
        body, in_specs=[VMEM] * 5, out_specs=[VMEM] * 4, out_shape=[gp, gp, gcp, gcp], name="s5_disc",
    )(lre, lim, logdt, bre_t, bim_t)


def _s5_disc_bwd(lre, lim, logdt, bre_t, bim_t, da, db, dbbre, dbbim):
    G, _, P = lre.shape
    C = bre_t.shape[1]

    def body(lre_ref, lim_ref, dt_ref, br_ref, bi_ref, da_ref, db_ref, dbr_ref, dbi_ref,
             o_lre, o_lim, o_dt, o_br, o_bi):
        _, vjp = jax.vjp(_disc_math, lre_ref[...], lim_ref[...], dt_ref[...], br_ref[...], bi_ref[...])
        g = vjp((da_ref[...], db_ref[...], dbr_ref[...], dbi_ref[...]))
        o_lre[...] = g[0]
        o_lim[...] = g[1]
        o_dt[...] = g[2]
        o_br[...] = g[3]
        o_bi[...] = g[4]

    gp = jax.ShapeDtypeStruct((G, 1, P), F32)
    gcp = jax.ShapeDtypeStruct((G, C, P), F32)
    return pl.pallas_call(
        body, in_specs=[VMEM] * 9, out_specs=[VMEM] * 5,
        out_shape=[gp, gp, jax.ShapeDtypeStruct((G, 1, 1), F32), gcp, gcp], name="s5_disc_bwd",
    )(lre, lim, logdt, bre_t, bim_t, da, db, dbbre, dbbim)


S5_CHUNK = 256


def _load_once(pairs):
    @pl.when(pl.program_id(0) == 0)
    def _():
        for src, dst in pairs:
            pltpu.sync_copy(src, dst)


def _s5_fwd(proj, u_off, width, a_row, b_row, bmre, bmim, cmre, cmimn, d_row):
    S = proj.shape[0]
    GP = a_row.shape[1]
    T = min(S5_CHUNK, S)

    def body(u_ref, a_ref, b_ref, d_ref, bre_hbm, bim_hbm, cre_hbm, cim_hbm,
             uo_ref, y_ref, gy_ref, hre_ref, him_ref, st_ref, bure_s, buim_s, bre_ref, bim_ref, cre_ref, cim_ref):
        @pl.when(pl.program_id(0) == 0)
        def _():
            st_ref[...] = jnp.zeros_like(st_ref)

        _load_once([(bre_hbm, bre_ref), (bim_hbm, bim_ref), (cre_hbm, cre_ref), (cim_hbm, cim_ref)])
        u = u_ref[...]
        uo_ref[...] = u
        for ws, gs in _bd_blocks(width, GP):
            bure_s[:, gs] = _dot_a2(u[:, ws], bre_ref[ws, gs])
            buim_s[:, gs] = _dot_a2(u[:, ws], bim_ref[ws, gs])
        a = a_ref[...]
        b = b_ref[...]

        def step(ii, carry):
            hre, him = carry
            base = pl.multiple_of(ii * SUBLANES, SUBLANES)
            br = bure_s[pl.ds(base, SUBLANES), :]
            bi = buim_s[pl.ds(base, SUBLANES), :]
            rows_re, rows_im = [], []
            for j in range(SUBLANES):
                nre = a * hre - b * him + br[j:j + 1, :]
                nim = a * him + b * hre + bi[j:j + 1, :]
                hre, him = nre, nim
                rows_re.append(nre)
                rows_im.append(nim)
            hre_ref[pl.ds(base, SUBLANES), :] = jnp.concatenate(rows_re, axis=0)
            him_ref[pl.ds(base, SUBLANES), :] = jnp.concatenate(rows_im, axis=0)
            return hre, him

        hre, him = lax.fori_loop(0, T // SUBLANES, step, (st_ref[0:1, :], st_ref[1:2, :]))
        st_ref[0:1, :] = hre
        st_ref[1:2, :] = him
        for ws, gs in _bd_blocks(width, GP):
            y = (_dot_a2(hre_ref[:, gs], cre_ref[gs, ws]) + _dot_a2(him_ref[:, gs], cim_ref[gs, ws])
                 + d_ref[:, ws] * u[:, ws])
            y_ref[:, ws] = y
            gy_ref[:, ws] = jax.nn.gelu(y).astype(gy_ref.dtype)

    c0 = lambda i: (0, 0)
    urow = pl.BlockSpec((T, width), lambda i: (i, u_off // width))
    row = pl.BlockSpec((T, width), lambda i: (i, 0))
    hrow = pl.BlockSpec((T, GP), lambda i: (i, 0))
    sw = jax.ShapeDtypeStruct((S, width), F32)
    sg = jax.ShapeDtypeStruct((S, GP), F32)
    return pl.pallas_call(
        body, grid=(S // T,),
        in_specs=[urow, pl.BlockSpec((1, GP), c0), pl.BlockSpec((1, GP), c0), pl.BlockSpec((1, width), c0),
                  ANY, ANY, ANY, ANY],
        out_specs=[row, row, row, hrow, hrow],
        out_shape=[sw, sw, jax.ShapeDtypeStruct((S, width), BF16), sg, sg],
        scratch_shapes=[pltpu.VMEM((SUBLANES, GP), F32), pltpu.VMEM((T, GP), F32), pltpu.VMEM((T, GP), F32),
                        pltpu.VMEM((width, GP), F32), pltpu.VMEM((width, GP), F32),
                        pltpu.VMEM((GP, width), F32), pltpu.VMEM((GP, width), F32)],
        name="s5_fwd", compiler_params=_cparams(("arbitrary",)),
    )(proj, a_row, b_row, d_row, bmre, bmim, cmre, cmimn)


def _s5_bwd(u, dgy, y, hre, him, a_row, b_row, bmre, bmim, cmre, cmimn, d_row):
    S, width = u.shape
    GP = a_row.shape[1]
    T = min(S5_CHUNK, S)
    nchunk = S // T

    def body(u_ref, dgy_ref, y_ref, hre_ref, him_ref, a_ref, b_ref, d_ref, bre_hbm, bim_hbm, cre_hbm, cim_hbm,
             du_ref, dy_ref, gre_s, gim_s, dd_ref, da_ref, db_ref,
             st_ref, bre_ref, bim_ref, cre_ref, cim_ref):
        @pl.when(pl.program_id(0) == 0)
        def _():
            st_ref[...] = jnp.zeros_like(st_ref)
            for r in (dd_ref, da_ref, db_ref):
                r[...] = jnp.zeros_like(r)

        _load_once([(bre_hbm, bre_ref), (bim_hbm, bim_ref), (cre_hbm, cre_ref), (cim_hbm, cim_ref)])
        u = u_ref[...]
        dy = dgy_ref[...] * _gelu_grad(y_ref[...])
        dy_ref[...] = dy
        for ws, gs in _bd_blocks(width, GP):
            gre_s[:, gs] = _dot_a2(dy[:, ws], cre_ref[gs, ws], NT)
            gim_s[:, gs] = _dot_a2(dy[:, ws], cim_ref[gs, ws], NT)
        a = a_ref[...]
        b = b_ref[...]
        g_in_re = st_ref[0:1, :]
        g_in_im = st_ref[1:2, :]

        def step(ii, carry):
            gre, gim = carry
            base = pl.multiple_of((T // SUBLANES - 1 - ii) * SUBLANES, SUBLANES)
            dr = gre_s[pl.ds(base, SUBLANES), :]
            di = gim_s[pl.ds(base, SUBLANES), :]
            rows_re = [None] * SUBLANES
            rows_im = [None] * SUBLANES
            for j in range(SUBLANES - 1, -1, -1):
                nre = dr[j:j + 1, :] + a * gre + b * gim
                nim = di[j:j + 1, :] - b * gre + a * gim
                gre, gim = nre, nim
                rows_re[j] = nre
                rows_im[j] = nim
            gre_s[pl.ds(base, SUBLANES), :] = jnp.concatenate(rows_re, axis=0)
            gim_s[pl.ds(base, SUBLANES), :] = jnp.concatenate(rows_im, axis=0)
            return gre, gim

        gre, gim = lax.fori_loop(0, T // SUBLANES, step, (g_in_re, g_in_im))
        st_ref[0:1, :] = gre
        st_ref[1:2, :] = gim
        last = lax.broadcasted_iota(jnp.int32, (T, 1), 0) == T - 1
        nxt_re = jnp.where(last, g_in_re, pltpu.roll(gre_s[...], T - 1, 0))
        nxt_im = jnp.where(last, g_in_im, pltpu.roll(gim_s[...], T - 1, 0))
        hre = hre_ref[...]
        him = him_ref[...]
        da_ref[...] += jnp.sum(nxt_re * hre + nxt_im * him, axis=0, keepdims=True)
        db_ref[...] += jnp.sum(nxt_im * hre - nxt_re * him, axis=0, keepdims=True)
        for ws, gs in _bd_blocks(width, GP):
            du_ref[:, ws] = (_dot_a2(gre_s[:, gs], bre_ref[ws, gs], NT) + _dot_a2(gim_s[:, gs], bim_ref[ws, gs], NT)
                             + d_ref[:, ws] * dy[:, ws])
        dd_ref[...] += jnp.sum(dy * u, axis=0, keepdims=True)

    c0 = lambda i: (0, 0)
    rev = lambda i: (nchunk - 1 - i, 0)
    row = pl.BlockSpec((T, width), rev)
    hrow = pl.BlockSpec((T, GP), rev)
    v_gp = pl.BlockSpec((1, GP), c0)
    v_w = pl.BlockSpec((1, width), c0)
    sw = jax.ShapeDtypeStruct((S, width), F32)
    sg = jax.ShapeDtypeStruct((S, GP), F32)
    return pl.pallas_call(
        body, grid=(nchunk,),
        in_specs=[row, row, row, hrow, hrow, v_gp, v_gp, v_w, ANY, ANY, ANY, ANY],
        out_specs=[row, row, hrow, hrow, v_w, v_gp, v_gp],
        out_shape=[sw, sw, sg, sg, jax.ShapeDtypeStruct((1, width), F32),
                   jax.ShapeDtypeStruct((1, GP), F32), jax.ShapeDtypeStruct((1, GP), F32)],
        scratch_shapes=[pltpu.VMEM((SUBLANES, GP), F32),
                        pltpu.VMEM((width, GP), F32), pltpu.VMEM((width, GP), F32),
                        pltpu.VMEM((GP, width), F32), pltpu.VMEM((GP, width), F32)],
        name="s5_bwd", compiler_params=_cparams(("arbitrary",)),
    )(u, dgy, y, hre, him, a_row, b_row, d_row, bmre, bmim, cmre, cmimn)


def _block_diag(x):
    G, A, B = x.shape
    eye = jnp.eye(G, dtype=x.dtype)
    return (eye[:, None, :, None] * x[:, :, None, :]).reshape(G * A, G * B)


def _block_diag_take(m, G):
    A, B = m.shape[0] // G, m.shape[1] // G
    return jnp.einsum("gagb->gab", m.reshape(G, A, G, B))


def _adamw(w, g, m, v):
    shape = w.shape
    C = shape[-1]
    w2, g2, m2, v2 = (t.reshape(-1, C) for t in (w, g, m, v))
    R = w2.shape[0]
    rb = _tile(R, max(SUBLANES, (1 << 19) // C), SUBLANES)
    c1 = 1.0 - ADAM_B1 ** ADAM_STEP
    c2 = 1.0 - ADAM_B2 ** ADAM_STEP

    def body(w_ref, g_ref, m_ref, v_ref, d_ref, nm_ref, nv_ref):
        gv = g_ref[...]
        nm = ADAM_B1 * m_ref[...] + (1.0 - ADAM_B1) * gv
        nv = ADAM_B2 * v_ref[...] + (1.0 - ADAM_B2) * (gv * gv)
        d_ref[...] = -ADAM_LR * ((nm / c1) / (jnp.sqrt(nv / c2) + ADAM_EPS) + ADAM_WD * w_ref[...])
        nm_ref[...] = nm
        nv_ref[...] = nv

    blk = pl.BlockSpec((rb, C), lambda i: (i, 0))
    sd = jax.ShapeDtypeStruct((R, C), F32)
    outs = pl.pallas_call(
        body, grid=(R // rb,), in_specs=[blk] * 4, out_specs=[blk] * 3, out_shape=[sd] * 3,
        name="adamw", compiler_params=_cparams(("parallel",)),
    )(w2, g2, m2, v2)
    return tuple(o.reshape(shape) for o in outs)


def _coords():
    x, y, c = lax.axis_index("x"), lax.axis_index("y"), lax.axis_index("c")
    return x, y, c


def _place_shard(w, l, place):
    _, R, C = w.shape
    rb = _tile(R, max(16, (1 << 19) // C), 16)

    def body(place_ref, w_ref, o_ref):
        o_ref[...] = w_ref[...].astype(BF16)

    grid_spec = pltpu.PrefetchScalarGridSpec(
        num_scalar_prefetch=1, grid=(R // rb,),
        in_specs=[pl.BlockSpec((None, rb, C), lambda i, p: (l, i, 0))],
        out_specs=pl.BlockSpec((None, rb, C), lambda i, p: (p[1], i, 0)))
    return pl.pallas_call(
        body, grid_spec=grid_spec, out_shape=jax.ShapeDtypeStruct((N_CHIPS, R, C), BF16),
        name="place_shard", compiler_params=_cparams(("arbitrary",)),
    )(place, w)


def _ag_hooks(outs, send_sems, recv_sems):
    n = len(outs)

    def rcopy(a, k, block, to):
        cx, cy, cc = block
        hr = outs[a].shape[1] // 2
        blk = outs[a].at[2 * cx + cy, pl.ds(cc * hr, hr)]
        return pltpu.make_async_remote_copy(
            src_ref=blk, dst_ref=blk, send_sem=send_sems.at[a * 6 + k], recv_sem=recv_sems.at[a * 6 + k],
            device_id=to, device_id_type=MESH)

    def places():
        x, y, c = _coords()
        return (x, y, c), (x, y, 1 - c), [(1 - x, y), (x, 1 - y), (1 - x, 1 - y)]

    def start():
        me, _, others = places()
        for a in range(n):
            for j, ch in enumerate(others):
                rcopy(a, j, me, (*ch, me[2])).start()

    def middle():
        me, sibling, others = places()
        for a in range(n):
            for j, ch in enumerate(others):
                rcopy(a, j, (*ch, me[2]), me).wait_recv()
                rcopy(a, 3 + j, (*ch, me[2]), sibling).start()

    def finish():
        me, sibling, others = places()
        for a in range(n):
            for j, ch in enumerate(others):
                rcopy(a, 3 + j, (*ch, sibling[2]), me).wait_recv()
        for a in range(n):
            for j, ch in enumerate(others):
                rcopy(a, j, me, (*ch, me[2])).wait_send()
                rcopy(a, 3 + j, (*ch, me[2]), sibling).wait_send()

    return start, middle, finish


def _ag_weights(bufs):
    n = len(bufs)

    def body(*refs):
        start, middle, finish = _ag_hooks(refs[n:2 * n], *refs[2 * n:])
        start()
        middle()
        finish()

    return pl.pallas_call(
        body, out_shape=[jax.ShapeDtypeStruct(b.shape, b.dtype) for b in bufs],
        in_specs=[ANY] * n, out_specs=[ANY] * n, input_output_aliases={i: i for i in range(n)},
        scratch_shapes=[pltpu.SemaphoreType.DMA((n * 6,)), pltpu.SemaphoreType.DMA((n * 6,))],
        name="ag_weights",
    )(*bufs)


def _rs_pair_hooks(gs, outs, send_sems, recv_sems):
    def copies():
        x, y, c = _coords()
        cps = []
        for i in range(len(gs)):
            hr = gs[i].shape[1] // 2
            cps.append(pltpu.make_async_remote_copy(
                src_ref=gs[i].at[:, pl.ds((1 - c) * hr, hr)], dst_ref=outs[i],
                send_sem=send_sems.at[i], recv_sem=recv_sems.at[i],
                device_id=(x, y, 1 - c), device_id_type=MESH))
        return cps

    def start():
        for cp in copies():
            cp.start()

    def finish():
        for cp in copies():
            cp.wait()

    return start, None, finish


def _rs_pair_comm(grads):
    return dict(ins=list(grads), alias=False, n_sems=len(grads), hooks=_rs_pair_hooks,
                out_shapes=[jax.ShapeDtypeStruct((N_CHIPS, g.shape[1] // 2, g.shape[2]), F32) for g in grads])


def _ag_comm(bufs):
    return dict(ins=list(bufs), alias=True, n_sems=6 * len(bufs),
                hooks=lambda ins, outs, send_sems, recv_sems: _ag_hooks(outs, send_sems, recv_sems),
                out_shapes=[jax.ShapeDtypeStruct(b.shape, b.dtype) for b in bufs])


def _rs_pair(grads):
    n = len(grads)
    comm = _rs_pair_comm(grads)

    def body(*refs):
        start, _, finish = _rs_pair_hooks(refs[:n], refs[n:2 * n], *refs[2 * n:])
        start()
        finish()

    return pl.pallas_call(
        body, out_shape=comm["out_shapes"], in_specs=[ANY] * n, out_specs=[ANY] * n,
        scratch_shapes=[pltpu.SemaphoreType.DMA((n,)), pltpu.SemaphoreType.DMA((n,))],
        name="rs_pair",
    )(*grads)


def _pair_add(g, r, place):
    _, R, C = g.shape
    hr = R // 2
    rb = _tile(hr, max(16, (1 << 19) // C), 16)
    nb = hr // rb

    def body(place_ref, g_ref, r_ref, p16_ref, own_ref):
        s = g_ref[...] + r_ref[...]
        p16_ref[...] = s.astype(BF16)

        @pl.when(pl.program_id(1) == place_ref[1])
        def _():
            own_ref[...] = s

    grid_spec = pltpu.PrefetchScalarGridSpec(
        num_scalar_prefetch=1, grid=(nb, N_CHIPS),
        in_specs=[pl.BlockSpec((None, rb, C), lambda i, k, p: (k, p[0] * nb + i, 0)),
                  pl.BlockSpec((None, rb, C), lambda i, k, p: (k, i, 0))],
        out_specs=[pl.BlockSpec((None, rb, C), lambda i, k, p: (k, i, 0)),
                   pl.BlockSpec((rb, C), lambda i, k, p: (i, 0))])
    return pl.pallas_call(
        body, grid_spec=grid_spec,
        out_shape=[jax.ShapeDtypeStruct((N_CHIPS, hr, C), BF16), jax.ShapeDtypeStruct((hr, C), F32)],
        name="pair_add", compiler_params=_cparams(("arbitrary", "arbitrary")),
    )(place, g, r)


def _rs_chips_hooks(ps, outs, send_sems, recv_sems):
    def copies():
        x, y, c = _coords()
        cps = []
        for a in range(len(ps)):
            for r in (1, 2, 3):
                kx = 1 - x if (r >> 1) else x
                ky = 1 - y if (r & 1) else y
                cps.append(pltpu.make_async_remote_copy(
                    src_ref=ps[a].at[2 * kx + ky], dst_ref=outs[a].at[r - 1],
                    send_sem=send_sems.at[a * 3 + r - 1], recv_sem=recv_sems.at[a * 3 + r - 1],
                    device_id=(kx, ky, c), device_id_type=MESH))
        return cps

    def start():
        for cp in copies():
            cp.start()

    def finish():
        for cp in copies():
            cp.wait()

    return start, finish


def _rs_chips_shapes(p16):
    return [jax.ShapeDtypeStruct((3,) + p.shape[1:], BF16) for p in p16]


def _rs_chips(p16):
    n = len(p16)

    def body(*refs):
        start, finish = _rs_chips_hooks(refs[:n], refs[n:2 * n], *refs[2 * n:])
        start()
        finish()

    return pl.pallas_call(
        body, out_shape=_rs_chips_shapes(p16), in_specs=[ANY] * n, out_specs=[ANY] * n,
        scratch_shapes=[pltpu.SemaphoreType.DMA((n * 3,)), pltpu.SemaphoreType.DMA((n * 3,))],
        name="rs_chips",
    )(*p16)


def _chip_sum(own, recv, full, l, place):
    hr, C = own.shape
    rb = _tile(hr, max(16, (1 << 19) // C), 16)
    nb = hr // rb

    def body(place_ref, o_ref, r_ref, full_ref, s_ref):
        s = o_ref[...] + r_ref[0].astype(F32)
        s = s + r_ref[1].astype(F32)
        s_ref[...] = s + r_ref[2].astype(F32)

    grid_spec = pltpu.PrefetchScalarGridSpec(
        num_scalar_prefetch=1, grid=(nb,),
        in_specs=[pl.BlockSpec((rb, C), lambda i, p: (i, 0)), pl.BlockSpec((3, rb, C), lambda i, p: (0, i, 0)), ANY],
        out_specs=pl.BlockSpec((None, rb, C), lambda i, p: (l, p[0] * nb + i, 0)))
    return pl.pallas_call(
        body, grid_spec=grid_spec, out_shape=jax.ShapeDtypeStruct(full.shape, F32),
        input_output_aliases={3: 0}, name="chip_sum", compiler_params=_cparams(("arbitrary",)),
    )(place, own, recv, full)


def _share_pair(fulls):
    n = len(fulls)

    def body(*refs):
        outs = refs[n:2 * n]
        send_sems, recv_sems = refs[2 * n:]
        x, y, c = _coords()
        copies = []
        for a in range(n):
            hr = outs[a].shape[1] // 2
            mine = outs[a].at[:, pl.ds(c * hr, hr)]
            cp = pltpu.make_async_remote_copy(
                src_ref=mine, dst_ref=mine, send_sem=send_sems.at[a], recv_sem=recv_sems.at[a],
                device_id=(x, y, 1 - c), device_id_type=MESH)
            cp.start()
            copies.append(cp)
        for cp in copies:
            cp.wait_recv()
        for cp in copies:
            cp.wait_send()

    return pl.pallas_call(
        body, out_shape=[jax.ShapeDtypeStruct(f.shape, f.dtype) for f in fulls],
        in_specs=[ANY] * n, out_specs=[ANY] * n, input_output_aliases={i: i for i in range(n)},
        scratch_shapes=[pltpu.SemaphoreType.DMA((n,)), pltpu.SemaphoreType.DMA((n,))],
        name="share_pair",
    )(*fulls)


def _small_allreduce(packed):
    m_per, ncol = packed.shape

    def body(x_ref, out_ref, tot_ref, send_sems, recv_sems, local_sem):
        x, y, c = _coords()
        me, sibling = (x, y, c), (x, y, 1 - c)
        chips = [(1 - x, y), (x, 1 - y), (1 - x, 1 - y)]

        def rows(px, py, pc):
            return out_ref.at[pl.ds((4 * px + 2 * py + pc) * m_per, m_per), :]

        def copy(k, block, to, src=None):
            return pltpu.make_async_remote_copy(
                src_ref=rows(*block) if src is None else src, dst_ref=rows(*block),
                send_sem=send_sems.at[k], recv_sem=recv_sems.at[k], device_id=to, device_id_type=MESH)

        mine = pltpu.make_async_copy(x_ref, rows(*me), local_sem)
        mine.start()
        first = [copy(0, me, sibling, src=x_ref)]
        first += [copy(1 + j, me, (*chip, c), src=x_ref) for j, chip in enumerate(chips)]
        for cp in first:
            cp.start()
        passed = [copy(4 + j, (*chip, c), sibling) for j, chip in enumerate(chips)]
        for j, chip in enumerate(chips):
            copy(1 + j, (*chip, c), me).wait_recv()
            passed[j].start()
        copy(0, sibling, me).wait_recv()
        for j, chip in enumerate(chips):
            copy(4 + j, (*chip, 1 - c), me).wait_recv()
        for cp in first + passed:
            cp.wait_send()
        mine.wait()
        tot = out_ref[pl.ds(0, m_per), :]
        for d in range(1, N_DEV):
            tot = tot + out_ref[pl.ds(d * m_per, m_per), :]
        tot_ref[...] = tot

    _, tot = pl.pallas_call(
        body,
        out_shape=[jax.ShapeDtypeStruct((N_DEV * m_per, ncol), F32), jax.ShapeDtypeStruct((m_per, ncol), F32)],
        in_specs=[VMEM], out_specs=[VMEM, VMEM],
        scratch_shapes=[pltpu.SemaphoreType.DMA((7,)), pltpu.SemaphoreType.DMA((7,)), pltpu.SemaphoreType.DMA],
        name="small_allreduce",
        compiler_params=pltpu.CompilerParams(vmem_limit_bytes=VMEM_LIMIT_MB * 1024 * 1024),
    )(packed)
    return tot


BIG = ["w_in", "sb_w_out", "ssm_w_glu", "ssm_w_out", "mem_w_kv", "mem_w_out", "w_o", "ffn_w_gate_up", "ffn_w_down"]
KIND = {"w_in": "col", "sb_w_out": "col", "ssm_w_glu": "col", "ssm_w_out": "col", "mem_w_kv": "row",
        "mem_w_out": "col", "w_o": "row", "ffn_w_gate_up": "col", "ffn_w_down": "row"}
AG_EARLY = ["w_in", "sb_w_out", "ssm_w_glu", "ssm_w_out", "mem_w_kv", "mem_w_out", "w_o"]
SMALL = ["b_in", "ssm_lambda_re", "ssm_lambda_im", "ssm_log_dt", "ssm_b_re", "ssm_b_im", "ssm_c_re", "ssm_c_im",
         "ssm_d", "ln1_g", "ln1_b", "ln2_g", "ln2_b"]
WEIGHTS = ["w_in", "b_in", "sb_w_out", "ssm_lambda_re", "ssm_lambda_im", "ssm_log_dt", "ssm_b_re", "ssm_b_im",
           "ssm_c_re", "ssm_c_im", "ssm_d", "ssm_w_glu", "ssm_w_out", "mem_w_kv", "mem_w_out", "w_o", "ln1_g",
           "ln1_b", "ffn_w_gate_up", "ffn_w_down", "ln2_g", "ln2_b"]


def _pack(arrs):
    flat = jnp.concatenate([a.reshape(-1).astype(F32) for a in arrs])
    n = flat.shape[0]
    rows = -(-n // LANES)
    rows = -(-rows // SUBLANES) * SUBLANES
    return jnp.pad(flat, (0, rows * LANES - n)).reshape(rows, LANES)


def _unpack(packed, like):
    flat = packed.reshape(-1)
    out, off = [], 0
    for a in like:
        out.append(flat[off:off + a.size].reshape(a.shape))
        off += a.size
    return out


def _step(x, mem, target, W, M1, V1):
    S, D = x.shape[1], x.shape[2]
    L = W["w_in"].shape[0]
    x0 = x.reshape(S, D)
    mem2 = mem.reshape(mem.shape[1], D)
    tgt = target.reshape(S, D)
    alpha = (2 * L) ** 0.25
    sbw = W["sb_w_out"].shape[1]
    ssw = W["ssm_d"].shape[1]
    mw = W["mem_w_out"].shape[1]
    heads = sbw // HEAD_DIM
    G, P = W["ssm_lambda_re"].shape[1], W["ssm_lambda_re"].shape[2]
    q_off, k_off, v_off = 0, sbw, 2 * sbw
    u_off = 3 * sbw
    qm_off = u_off + ssw
    gate_off = qm_off + mw

    x_i, y_i, c_i = _coords()
    place = jnp.stack([c_i, 2 * x_i + y_i]).astype(jnp.int32)
    placed = [[_place_shard(W[n], l, place) for n in BIG] for l in range(L)]
    Wg = {n: [None] * L for n in BIG}
    placed = [dict(zip(BIG, row)) for row in placed]
    for n, buf in zip(AG_EARLY, _ag_weights([placed[0][n] for n in AG_EARLY])):
        Wg[n][0] = buf

    def hosted(plan):
        return _ag_comm([placed[ll][n] for n, ll in plan]) if plan else None

    def keep(plan, bufs):
        for (n, ll), buf in zip(plan, bufs):
            Wg[n][ll] = buf

    saved = []
    xl = x0
    xlb = x0.astype(BF16)
    for l in range(L):
        sv = {"x": xlb}
        more = l + 1 < L
        plan_proj = [("ffn_w_gate_up", 0)] if l == 0 else []
        plan_sb = ([("ffn_w_down", 0)] if l == 0 else []) + ([(n, l + 1) for n in AG_EARLY] if more else [])
        plan_gate_up = [("ffn_w_gate_up", l + 1)] if more else []
        plan_down = [("ffn_w_down", l + 1)] if more else []

        proj = _mm_nn(xlb, Wg["w_in"][l], "col", bias=W["b_in"][l][None, :], comm=hosted(plan_proj),
                      name="mm_proj_ag" if plan_proj else "mm_proj")
        if plan_proj:
            proj, got = proj
            keep(plan_proj, got)
        sb, sb_ctot, got = _sb_fwd(proj, q_off, k_off, v_off, heads,
                                   ag=[placed[ll][n] for n, ll in plan_sb] if plan_sb else None)
        keep(plan_sb, got)
        p_sb = _mm_nn(sb, Wg["sb_w_out"][l], "col", name="mm_sb_out")

        bre_t = W["ssm_b_re"][l].transpose(0, 2, 1)
        bim_t = W["ssm_b_im"][l].transpose(0, 2, 1)
        logdt = W["ssm_log_dt"][l][:, None, None]
        lre3 = W["ssm_lambda_re"][l][:, None, :]
        lim3 = W["ssm_lambda_im"][l][:, None, :]
        a_gp, b_gp, bbre, bbim = _s5_disc(lre3, lim3, logdt, bre_t, bim_t)
        a_row, b_row = a_gp.reshape(1, G * P), b_gp.reshape(1, G * P)
        bmre, bmim = _block_diag(bbre), _block_diag(bbim)
        cmre = _block_diag(W["ssm_c_re"][l].transpose(0, 2, 1))
        cmimn = _block_diag(-W["ssm_c_im"][l].transpose(0, 2, 1))
        d_row = W["ssm_d"][l][None, :]
        u_ssm, y, gy, hre, him = _s5_fwd(proj, u_off, ssw, a_row, b_row, bmre, bmim, cmre, cmimn, d_row)
        glu = _mm_nn(gy, Wg["ssm_w_glu"][l], "col", name="mm_glu")
        zz = _glu_fwd(glu)
        p_ssm = _mm_nn(zz, Wg["ssm_w_out"][l], "col", name="mm_ssm_out")

        kv = _mm_nn(mem2, Wg["mem_w_kv"][l], "row", name="mm_kv")
        mm_o = _mem_fwd(proj, qm_off, mw, kv)
        p_mem = _mm_nn(mm_o, Wg["mem_w_out"][l], "col", name="mm_mem_out")

        merged = _merge_fwd(proj, gate_off, p_sb, p_ssm, p_mem)
        mix = _mm_nn(merged, Wg["w_o"][l], "row", name="mm_wo")
        x1, x1b, xh1, rs1 = _ln_fwd(xl, mix, W["ln1_g"][l][None, :], W["ln1_b"][l][None, :], alpha)
        gu = _mm_nn(x1b, Wg["ffn_w_gate_up"][l], "col", comm=hosted(plan_gate_up),
                    name="mm_gate_up_ag" if plan_gate_up else "mm_gate_up")
        if plan_gate_up:
            gu, got = gu
            keep(plan_gate_up, got)
        hid = _swiglu_fwd(gu)
        ffn = _mm_nn(hid, Wg["ffn_w_down"][l], "row", comm=hosted(plan_down),
                     name="mm_down_ag" if plan_down else "mm_down")
        if plan_down:
            ffn, got = ffn
            keep(plan_down, got)
        x2, x2b, xh2, rs2 = _ln_fwd(x1, ffn, W["ln2_g"][l][None, :], W["ln2_b"][l][None, :], alpha)
        sv.update(proj=proj, sb=sb, sb_ctot=sb_ctot, p_sb=p_sb, y=y, gy=gy, hre=hre, him=him, glu=glu, zz=zz,
                  p_ssm=p_ssm, kv=kv, mm_o=mm_o, p_mem=p_mem, merged=merged, x1=x1b, xh1=xh1, rs1=rs1, gu=gu,
                  hid=hid, xh2=xh2, rs2=rs2, u=u_ssm,
                  disc=(lre3, lim3, logdt, bre_t, bim_t, a_row, b_row, bmre, bmim, cmre, cmimn, d_row))
        saved.append(sv)
        xl, xlb = x2, x2b

    dxl, loss_part = _loss_head(xl, tgt)

    gbig = {n: [None] * L for n in BIG}
    gsmall = {n: [None] * L for n in SMALL}
    own = [None] * L
    from_chips = [None] * L
    above = None
    pending = None
    for l in range(L - 1, -1, -1):
        sv = saved[l]
        proj = sv["proj"]
        dr2, dr2b, dg2, db2 = _ln_bwd(dxl, sv["xh2"], sv["rs2"], W["ln2_g"][l][None, :])
        gsmall["ln2_g"][l], gsmall["ln2_b"][l] = dg2[0], db2[0]
        dhid = _mm_nt(dr2b, Wg["ffn_w_down"][l], "row", name="mm_d_hid")
        gbig["ffn_w_down"][l] = _mm_tn(sv["hid"], dr2b, "row", name="mm_g_down")
        dgu = _swiglu_bwd(dhid, sv["gu"])
        if above is None:
            dx1 = _mm_nt(dgu, Wg["ffn_w_gate_up"][l], "col", add=dr2, add_scale=alpha, name="mm_d_x1")
        else:
            dx1, from_sibling = _mm_nt(dgu, Wg["ffn_w_gate_up"][l], "col", add=dr2, add_scale=alpha,
                                       name="mm_d_x1_rs", comm=_rs_pair_comm(above))
            sums = [_pair_add(g, r, place) for g, r in zip(above, from_sibling)]
            pending = [s[0] for s in sums]
            own[l + 1] = [s[1] for s in sums]
        gbig["ffn_w_gate_up"][l] = _mm_tn(sv["x1"], dgu, "col", name="mm_g_gate_up")

        dr1, dr1b, dg1, db1 = _ln_bwd(dx1, sv["xh1"], sv["rs1"], W["ln1_g"][l][None, :])
        gsmall["ln1_g"][l], gsmall["ln1_b"][l] = dg1[0], db1[0]
        dmerged = _mm_nt(dr1b, Wg["w_o"][l], "row", name="mm_d_merged")
        gbig["w_o"][l] = _mm_tn(sv["merged"], dr1b, "row", name="mm_g_wo")
        dp_sb, dp_ssm, dp_mem, dgl0, dgl1, dgl2 = _merge_bwd(
            dmerged, proj, gate_off, sv["p_sb"], sv["p_ssm"], sv["p_mem"])

        dsb = _mm_nt(dp_sb, Wg["sb_w_out"][l], "col", out_dtype=BF16, name="mm_d_sb")
        gbig["sb_w_out"][l] = _mm_tn(sv["sb"], dp_sb, "col", name="mm_g_sb_out")

        dzz = _mm_nt(dp_ssm, Wg["ssm_w_out"][l], "col", name="mm_d_zz")
        gbig["ssm_w_out"][l] = _mm_tn(sv["zz"], dp_ssm, "col", name="mm_g_ssm_out")
        dglu = _glu_bwd(dzz, sv["glu"])
        dgy = _mm_nt(dglu, Wg["ssm_w_glu"][l], "col", name="mm_d_gy")
        gbig["ssm_w_glu"][l] = _mm_tn(sv["gy"], dglu, "col", name="mm_g_glu")
        lre3, lim3, logdt, bre_t, bim_t, a_row, b_row, bmre, bmim, cmre, cmimn, d_row = sv["disc"]
        du, dy_ssm, g_re, g_im, dd, da, db = _s5_bwd(
            sv["u"], dgy, sv["y"], sv["hre"], sv["him"], a_row, b_row, bmre, bmim, cmre, cmimn, d_row)
        dbmre = _mm_tn(sv["u"], g_re, "plain", name="mm_g_ssm_bre")
        dbmim = _mm_tn(sv["u"], g_im, "plain", name="mm_g_ssm_bim")
        dcmre = _mm_tn(sv["hre"], dy_ssm, "plain", name="mm_g_ssm_cre")
        dcmimn = _mm_tn(sv["him"], dy_ssm, "plain", name="mm_g_ssm_cim")
        dlre, dlim, dlogdt, dbre_t, dbim_t = _s5_disc_bwd(
            lre3, lim3, logdt, bre_t, bim_t, da.reshape(G, 1, P), db.reshape(G, 1, P),
            _block_diag_take(dbmre, G), _block_diag_take(dbmim, G))
        gsmall["ssm_lambda_re"][l], gsmall["ssm_lambda_im"][l] = dlre.reshape(G, P), dlim.reshape(G, P)
        gsmall["ssm_log_dt"][l] = dlogdt.reshape(G)
        gsmall["ssm_b_re"][l] = dbre_t.transpose(0, 2, 1)
        gsmall["ssm_b_im"][l] = dbim_t.transpose(0, 2, 1)
        gsmall["ssm_c_re"][l] = _block_diag_take(dcmre, G).transpose(0, 2, 1)
        gsmall["ssm_c_im"][l] = -_block_diag_take(dcmimn, G).transpose(0, 2, 1)
        gsmall["ssm_d"][l] = dd[0]

        dmm = _mm_nt(dp_mem, Wg["mem_w_out"][l], "col", out_dtype=BF16, name="mm_d_mm")
        gbig["mem_w_out"][l] = _mm_tn(sv["mm_o"], dp_mem, "col", name="mm_g_mem_out")
        dqm, dkv = _mem_bwd(proj, qm_off, mw, sv["kv"], dmm)
        gbig["mem_w_kv"][l] = _mm_tn(mem2, dkv, "row", name="mm_g_kv")

        dq, dk, dv, arrived = _sb_bwd(proj, q_off, k_off, v_off, heads, dsb, sv["sb_ctot"], rs=pending)
        if pending is not None:
            from_chips[l + 1] = arrived
        dproj, dbin = _assemble_dproj([dq, dk, dv, du, dqm, dgl0, dgl1, dgl2])
        gsmall["b_in"][l] = dbin[0]
        dxl = _mm_nt(dproj, Wg["w_in"][l], "col", add=dr1, add_scale=alpha, name="mm_d_x")
        gbig["w_in"][l] = _mm_tn(sv["x"], dproj, "col", name="mm_g_win")

        above = [gbig[n][l] for n in BIG]

    sums = [_pair_add(g, r, place) for g, r in zip(above, _rs_pair(above))]
    own[0] = [s[1] for s in sums]
    from_chips[0] = _rs_chips([s[0] for s in sums])
    grad_x = dxl.reshape(x.shape)

    fulls = []
    for a, n in enumerate(BIG):
        full = lax.empty(W[n].shape, F32)
        for l in range(L - 1, -1, -1):
            full = _chip_sum(own[l][a], from_chips[l][a], full, l, place)
        fulls.append(full)
    reduced = _share_pair(fulls)
    grads = {n: reduced[i] for i, n in enumerate(BIG)}

    small_local = [jnp.stack(gsmall[n]) for n in SMALL]
    packed = _pack(small_local + [loss_part[0, :1]])
    total = _small_allreduce(packed)
    unpacked = _unpack(total, small_local + [loss_part[0, :1]])
    for n, g in zip(SMALL, unpacked[:-1]):
        grads[n] = g
    loss = unpacked[-1][0]

    delta, new_m, new_v = {}, {}, {}
    for n in BIG:
        delta[n], new_m[n], new_v[n] = _adamw(W[n], grads[n], M1[n], V1[n])
    sm = _adamw(_pack([W[n] for n in SMALL]), _pack([grads[n] for n in SMALL]),
                _pack([M1[n] for n in SMALL]), _pack([V1[n] for n in SMALL]))
    like = [W[n] for n in SMALL]
    for n, d, m_, v_ in zip(SMALL, _unpack(sm[0], like), _unpack(sm[1], like), _unpack(sm[2], like)):
        delta[n], new_m[n], new_v[n] = d, m_, v_

    return (loss, grad_x, *[grads[n] for n in WEIGHTS], *[delta[n] for n in WEIGHTS],
            *[new_m[n] for n in WEIGHTS], *[new_v[n] for n in WEIGHTS])


def kernel(x, mem, w_in, b_in, sb_w_out, ssm_lambda_re, ssm_lambda_im, ssm_log_dt, ssm_b_re, ssm_b_im, ssm_c_re, ssm_c_im, ssm_d, ssm_w_glu, ssm_w_out, mem_w_kv, mem_w_out, w_o, ln1_g, ln1_b, ffn_w_gate_up, ffn_w_down, ln2_g, ln2_b, loss_target, m_w_in, m_b_in, m_sb_w_out, m_ssm_lambda_re, m_ssm_lambda_im, m_ssm_log_dt, m_ssm_b_re, m_ssm_b_im, m_ssm_c_re, m_ssm_c_im, m_ssm_d, m_ssm_w_glu, m_ssm_w_out, m_mem_w_kv, m_mem_w_out, m_w_o, m_ln1_g, m_ln1_b, m_ffn_w_gate_up, m_ffn_w_down, m_ln2_g, m_ln2_b, v_w_in, v_b_in, v_sb_w_out, v_ssm_lambda_re, v_ssm_lambda_im, v_ssm_log_dt, v_ssm_b_re, v_ssm_b_im, v_ssm_c_re, v_ssm_c_im, v_ssm_d, v_ssm_w_glu, v_ssm_w_out, v_mem_w_kv, v_mem_w_out, v_w_o, v_ln1_g, v_ln1_b, v_ffn_w_gate_up, v_ffn_w_down, v_ln2_g, v_ln2_b):
    W = dict(w_in=w_in, b_in=b_in, sb_w_out=sb_w_out, ssm_lambda_re=ssm_lambda_re, ssm_lambda_im=ssm_lambda_im,
             ssm_log_dt=ssm_log_dt, ssm_b_re=ssm_b_re, ssm_b_im=ssm_b_im, ssm_c_re=ssm_c_re, ssm_c_im=ssm_c_im,
             ssm_d=ssm_d, ssm_w_glu=ssm_w_glu, ssm_w_out=ssm_w_out, mem_w_kv=mem_w_kv, mem_w_out=mem_w_out,
             w_o=w_o, ln1_g=ln1_g, ln1_b=ln1_b, ffn_w_gate_up=ffn_w_gate_up, ffn_w_down=ffn_w_down,
             ln2_g=ln2_g, ln2_b=ln2_b)
    M1 = dict(w_in=m_w_in, b_in=m_b_in, sb_w_out=m_sb_w_out, ssm_lambda_re=m_ssm_lambda_re,
              ssm_lambda_im=m_ssm_lambda_im, ssm_log_dt=m_ssm_log_dt, ssm_b_re=m_ssm_b_re, ssm_b_im=m_ssm_b_im,
              ssm_c_re=m_ssm_c_re, ssm_c_im=m_ssm_c_im, ssm_d=m_ssm_d, ssm_w_glu=m_ssm_w_glu,
              ssm_w_out=m_ssm_w_out, mem_w_kv=m_mem_w_kv, mem_w_out=m_mem_w_out, w_o=m_w_o, ln1_g=m_ln1_g,
              ln1_b=m_ln1_b, ffn_w_gate_up=m_ffn_w_gate_up, ffn_w_down=m_ffn_w_down, ln2_g=m_ln2_g, ln2_b=m_ln2_b)
    V1 = dict(w_in=v_w_in, b_in=v_b_in, sb_w_out=v_sb_w_out, ssm_lambda_re=v_ssm_lambda_re,
              ssm_lambda_im=v_ssm_lambda_im, ssm_log_dt=v_ssm_log_dt, ssm_b_re=v_ssm_b_re, ssm_b_im=v_ssm_b_im,
              ssm_c_re=v_ssm_c_re, ssm_c_im=v_ssm_c_im, ssm_d=v_ssm_d, ssm_w_glu=v_ssm_w_glu,
              ssm_w_out=v_ssm_w_out, mem_w_kv=v_mem_w_kv, mem_w_out=v_mem_w_out, w_o=v_w_o, ln1_g=v_ln1_g,
              ln1_b=v_ln1_b, ffn_w_gate_up=v_ffn_w_gate_up, ffn_w_down=v_ffn_w_down, ln2_g=v_ln2_g, ln2_b=v_ln2_b)
    return _step(x, mem, loss_target, W, M1, V1)
```

```python
import functools
import math

import jax
import jax.numpy as jnp
from jax import lax
from jax.experimental import pallas as pl
from jax.experimental.pallas import tpu as pltpu

F32 = jnp.float32
BF16 = jnp.bfloat16
MESH = pl.DeviceIdType.MESH
ANY = pl.BlockSpec(memory_space=pl.ANY)
VMEM = pl.BlockSpec(memory_space=pltpu.VMEM)

HEAD_DIM = 128
SSM_GROUP = 16
N_CHIPS = 4
N_DEV = 8
LN_EPS = 1e-5
ADAM_LR = 0.001
ADAM_B1 = 0.9
ADAM_B2 = 0.999
ADAM_EPS = 1e-08
ADAM_WD = 0.01
ADAM_STEP = 10
LANES = 128
SUBLANES = 8
VMEM_LIMIT_MB = 56


def _cparams(sem, mb=VMEM_LIMIT_MB):
    return pltpu.CompilerParams(dimension_semantics=sem, vmem_limit_bytes=mb * 1024 * 1024)


def _tile(n, pref, mult=LANES):
    best = None
    t = mult
    while t <= min(n, pref):
        if n % t == 0:
            best = t
        t += mult
    return n if best is None else best


def _dot(a, b, dims=(((1,), (0,)), ((), ()))):
    return lax.dot_general(a, b, dims, preferred_element_type=F32)


NT = (((1,), (1,)), ((), ()))
TN = (((0,), (0,)), ((), ()))


def _split2(x):
    hi = x.astype(BF16)
    lo = (x - hi.astype(F32)).astype(BF16)
    return hi, lo


def _dot_a2(a, b, dims=(((1,), (0,)), ((), ()))):
    ah, al = _split2(a)
    bb = b.astype(BF16)
    return _dot(ah, bb, dims) + _dot(al, bb, dims)


def _bd_blocks(n_in, n_out):
    if (n_in // 2) % LANES == 0 and (n_out // 2) % LANES == 0:
        return [(slice(0, n_in // 2), slice(0, n_out // 2)), (slice(n_in // 2, n_in), slice(n_out // 2, n_out))]
    return [(slice(0, n_in), slice(0, n_out))]


def _dot_mask(x, u2):
    hi, lo = _split2(x)
    return _dot(jnp.concatenate([hi, lo], axis=1), u2)


def _tri2(n, rel):
    ri = lax.broadcasted_iota(jnp.int32, (2 * n, n), 0)
    ci = lax.broadcasted_iota(jnp.int32, (2 * n, n), 1)
    ri = jnp.where(ri >= n, ri - n, ri)
    return rel(ri, ci).astype(BF16)


def _mm_call(a, b, *, dims, grid, a_spec, b_spec, out_spec, out_shape, name,
             bias=None, bias_spec=None, add=None, add_spec=None, add_scale=1.0, comm=None, prod=None):
    nk = grid[2]
    has_bias = bias is not None
    has_add = add is not None
    n_ci = len(comm["ins"]) if comm else 0
    n_co = len(comm["out_shapes"]) if comm else 0

    def body(*refs):
        a_ref, b_ref = refs[0], refs[1]
        pos = 2
        bias_ref = refs[pos] if has_bias else None
        pos += int(has_bias)
        add_ref = refs[pos] if has_add else None
        pos += int(has_add)
        comm_in = refs[pos:pos + n_ci]
        pos += n_ci
        o_ref = refs[pos]
        comm_out = refs[pos + 1:pos + 1 + n_co]
        pos += 1 + n_co
        acc_ref = refs[pos] if nk > 1 else None
        pos += int(nk > 1)
        if comm:
            ex_start, ex_middle, ex_finish = comm["hooks"](comm_in, comm_out, refs[pos], refs[pos + 1])
            ids = [pl.program_id(d) for d in range(3)]

            @pl.when((ids[0] == 0) & (ids[1] == 0) & (ids[2] == 0))
            def _():
                ex_start()

        if prod is None:
            p = _dot(a_ref[...].astype(BF16), b_ref[...].astype(BF16), dims)
        else:
            p = prod(a_ref, b_ref)

        def finish(acc):
            if has_bias:
                acc = acc + bias_ref[...]
            if has_add:
                acc = acc + add_scale * add_ref[...].astype(F32)
            o_ref[...] = acc.astype(o_ref.dtype)

        if nk == 1:
            finish(p)
        else:
            k = pl.program_id(2)

            @pl.when(k == 0)
            def _():
                acc_ref[...] = p

            @pl.when(k > 0)
            def _():
                acc_ref[...] += p

            @pl.when(k == nk - 1)
            def _():
                finish(acc_ref[...])

        if comm:
            @pl.when((ids[0] == grid[0] - 1) & (ids[1] == grid[1] - 1) & (ids[2] == grid[2] - 1))
            def _():
                if ex_middle is not None:
                    ex_middle()
                ex_finish()

    ins = [a, b]
    in_specs = [a_spec, b_spec]
    if has_bias:
        ins.append(bias)
        in_specs.append(bias_spec)
    if has_add:
        ins.append(add)
        in_specs.append(add_spec)
    scratch = []
    if nk > 1:
        blk = [d for d in out_spec.block_shape if d is not None]
        scratch.append(pltpu.VMEM(tuple(blk), F32))
    if not comm:
        return pl.pallas_call(
            body, out_shape=out_shape, grid=grid, in_specs=in_specs, out_specs=out_spec,
            scratch_shapes=scratch, name=name,
            compiler_params=_cparams(("parallel", "parallel", "arbitrary")),
        )(*ins)
    n_main = len(ins)
    scratch += [pltpu.SemaphoreType.DMA((comm["n_sems"],)), pltpu.SemaphoreType.DMA((comm["n_sems"],))]
    res = pl.pallas_call(
        body, out_shape=[out_shape] + list(comm["out_shapes"]), grid=grid,
        in_specs=in_specs + [ANY] * n_ci, out_specs=[out_spec] + [ANY] * n_co, scratch_shapes=scratch,
        input_output_aliases={n_main + i: 1 + i for i in range(n_ci)} if comm["alias"] else {},
        name=name, compiler_params=_cparams(("arbitrary", "arbitrary", "arbitrary")),
    )(*ins, *comm["ins"])
    return res[0], list(res[1:])


def _mm_nn(a, w, kind, *, bias=None, out_dtype=F32, name, comm=None):
    M, K = a.shape
    tm = _tile(M, 1024, SUBLANES)
    tk = K if K <= 2048 else _tile(K, 1408)
    if kind == "col":
        Nc = w.shape[2]
        N = N_CHIPS * Nc
        tn = _tile(Nc, 1408)
        npc = Nc // tn
        b_spec = pl.BlockSpec((None, tk, tn), lambda i, j, k: (j // npc, k, j % npc))
    else:
        w = w.reshape(-1, w.shape[-1])
        N = w.shape[1]
        tn = _tile(N, 2048)
        if K > 2048 and a.dtype == BF16:
            tk, tn = K, _tile(N, 512)
        b_spec = pl.BlockSpec((tk, tn), lambda i, j, k: (k, j))
    grid = (M // tm, N // tn, K // tk)
    bias_spec = pl.BlockSpec((1, tn), lambda i, j, k: (0, j)) if bias is not None else None
    return _mm_call(
        a, w, dims=(((1,), (0,)), ((), ())), grid=grid,
        a_spec=pl.BlockSpec((tm, tk), lambda i, j, k: (i, k)), b_spec=b_spec,
        out_spec=pl.BlockSpec((tm, tn), lambda i, j, k: (i, j)),
        out_shape=jax.ShapeDtypeStruct((M, N), out_dtype), name=name,
        bias=bias, bias_spec=bias_spec, comm=comm)


def _mm_nt(dy, w, kind, *, add=None, add_scale=1.0, out_dtype=F32, name, comm=None):
    M, N = dy.shape
    tm = _tile(M, 512, SUBLANES)
    if kind == "col" and dy.dtype == BF16:
        K, Nc = w.shape[1], w.shape[2]
        cps = 2
        tm = _tile(M, 1024, SUBLANES)
        tko = _tile(K, 512)

        def prod(a_ref, b_ref):
            acc = None
            for c in range(cps):
                t = _dot(a_ref[:, c * Nc:(c + 1) * Nc], b_ref[c], NT)
                acc = t if acc is None else acc + t
            return acc

        add_spec = pl.BlockSpec((tm, tko), lambda i, j, r: (i, j)) if add is not None else None
        return _mm_call(
            dy, w, dims=NT, grid=(M // tm, K // tko, N_CHIPS // cps), prod=prod,
            a_spec=pl.BlockSpec((tm, cps * Nc), lambda i, j, r: (i, r)),
            b_spec=pl.BlockSpec((cps, tko, Nc), lambda i, j, r: (r, j, 0)),
            out_spec=pl.BlockSpec((tm, tko), lambda i, j, r: (i, j)),
            out_shape=jax.ShapeDtypeStruct((M, K), out_dtype), name=name,
            add=add, add_spec=add_spec, add_scale=add_scale, comm=comm)
    if kind == "col":
        K, Nc = w.shape[1], w.shape[2]
        tn = _tile(Nc, 1408)
        npc = Nc // tn
        tko = _tile(K, 2048)
        b_spec = pl.BlockSpec((None, tko, tn), lambda i, j, r: (r // npc, j, r % npc))
    else:
        w = w.reshape(-1, w.shape[-1])
        K = w.shape[0]
        tn = _tile(N, 2048)
        tko = _tile(K, 2048)
        b_spec = pl.BlockSpec((tko, tn), lambda i, j, r: (j, r))
    grid = (M // tm, K // tko, N // tn)
    add_spec = pl.BlockSpec((tm, tko), lambda i, j, r: (i, j)) if add is not None else None
    return _mm_call(
        dy, w, dims=NT, grid=grid,
        a_spec=pl.BlockSpec((tm, tn), lambda i, j, r: (i, r)), b_spec=b_spec,
        out_spec=pl.BlockSpec((tm, tko), lambda i, j, r: (i, j)),
        out_shape=jax.ShapeDtypeStruct((M, K), out_dtype), name=name,
        add=add, add_spec=add_spec, add_scale=add_scale, comm=comm)


def _mm_tn(a, dy, kind, *, name, comm=None):
    M, K = a.shape
    N = dy.shape[1]
    tm = M if a.dtype == BF16 and dy.dtype == BF16 else _tile(M, 2048, SUBLANES)
    tkw = _tile(K, 512)
    tn = _tile(N // N_CHIPS, 1408) if kind == "col" else _tile(N, 1024)
    a_inner = tkw * a.dtype.itemsize <= tn * dy.dtype.itemsize
    ij = (lambda o, n: (n, o)) if a_inner else (lambda o, n: (o, n))
    if kind == "col":
        Nc = N // N_CHIPS
        npc = Nc // tn

        def out_map(o, n, m):
            i, j = ij(o, n)
            return (j // npc, i, j % npc)

        out_spec = pl.BlockSpec((None, tkw, tn), out_map)
        out_shape = jax.ShapeDtypeStruct((N_CHIPS, K, Nc), F32)
    else:
        out_spec = pl.BlockSpec((tkw, tn), lambda o, n, m: ij(o, n))
        out_shape = jax.ShapeDtypeStruct((K, N), F32)
    ni, nj = K // tkw, N // tn
    grid = (nj, ni, M // tm) if a_inner else (ni, nj, M // tm)
    out = _mm_call(
        a, dy, dims=TN, grid=grid,
        a_spec=pl.BlockSpec((tm, tkw), lambda o, n, m: (m, ij(o, n)[0])),
        b_spec=pl.BlockSpec((tm, tn), lambda o, n, m: (m, ij(o, n)[1])),
        out_spec=out_spec, out_shape=out_shape, name=name, comm=comm)
    extra = None
    if comm:
        out, extra = out
    if kind == "row":
        out = out.reshape(N_CHIPS, K // N_CHIPS, N)
    return (out, extra) if comm else out


def _gelu_grad(x):
    k = math.sqrt(2.0 / math.pi)
    inner = k * (x + 0.044715 * x * x * x)
    t = jnp.tanh(inner)
    return 0.5 * (1.0 + t) + 0.5 * x * (1.0 - t * t) * k * (1.0 + 3.0 * 0.044715 * x * x)


def _ln_fwd(xin, delta, g, b, alpha):
    S, D = xin.shape
    ts = _tile(S, 256, SUBLANES)

    def body(x_ref, d_ref, g_ref, b_ref, y_ref, yb_ref, xh_ref, rs_ref):
        r = alpha * x_ref[...] + d_ref[...]
        mu = jnp.mean(r, axis=-1, keepdims=True)
        rc = r - mu
        var = jnp.mean(rc * rc, axis=-1, keepdims=True)
        rstd = lax.rsqrt(var + LN_EPS)
        xh = rc * rstd
        y = xh * g_ref[...] + b_ref[...]
        y_ref[...] = y
        yb_ref[...] = y.astype(BF16)
        xh_ref[...] = xh
        rs_ref[...] = rstd

    row = pl.BlockSpec((ts, D), lambda i: (i, 0))
    vec = pl.BlockSpec((1, D), lambda i: (0, 0))
    return pl.pallas_call(
        body, grid=(S // ts,), in_specs=[row, row, vec, vec],
        out_specs=[row, row, row, pl.BlockSpec((ts, 1), lambda i: (i, 0))],
        out_shape=[jax.ShapeDtypeStruct((S, D), F32), jax.ShapeDtypeStruct((S, D), BF16),
                   jax.ShapeDtypeStruct((S, D), F32), jax.ShapeDtypeStruct((S, 1), F32)],
        name="ln_fwd", compiler_params=_cparams(("parallel",)),
    )(xin, delta, g, b)


def _ln_bwd(dy, xh, rstd, g):
    S, D = dy.shape
    ts = _tile(S, 256, SUBLANES)

    def body(dy_ref, xh_ref, rs_ref, g_ref, dr_ref, drb_ref, dg_ref, db_ref):
        @pl.when(pl.program_id(0) == 0)
        def _():
            dg_ref[...] = jnp.zeros_like(dg_ref)
            db_ref[...] = jnp.zeros_like(db_ref)

        dyv = dy_ref[...]
        xhv = xh_ref[...]
        dyg = dyv * g_ref[...]
        m1 = jnp.mean(dyg, axis=-1, keepdims=True)
        m2 = jnp.mean(dyg * xhv, axis=-1, keepdims=True)
        dr = rs_ref[...] * (dyg - m1 - xhv * m2)
        dr_ref[...] = dr
        drb_ref[...] = dr.astype(BF16)
        dg_ref[...] += jnp.sum(dyv * xhv, axis=0, keepdims=True)
        db_ref[...] += jnp.sum(dyv, axis=0, keepdims=True)

    row = pl.BlockSpec((ts, D), lambda i: (i, 0))
    vec = pl.BlockSpec((1, D), lambda i: (0, 0))
    return pl.pallas_call(
        body, grid=(S // ts,),
        in_specs=[row, row, pl.BlockSpec((ts, 1), lambda i: (i, 0)), vec],
        out_specs=[row, row, vec, vec],
        out_shape=[jax.ShapeDtypeStruct((S, D), F32), jax.ShapeDtypeStruct((S, D), BF16),
                   jax.ShapeDtypeStruct((1, D), F32), jax.ShapeDtypeStruct((1, D), F32)],
        name="ln_bwd", compiler_params=_cparams(("arbitrary",)),
    )(dy, xh, rstd, g)


def _merge_fwd(proj, gate_off, p_sb, p_ssm, p_mem):
    S, D = p_sb.shape
    ts = _tile(S, 256, SUBLANES)
    gb = gate_off // D

    def body(g0, g1, g2, a0, a1, a2, o_ref):
        o_ref[...] = (jax.nn.sigmoid(g0[...]) * a0[...] + jax.nn.sigmoid(g1[...]) * a1[...]
                      + jax.nn.sigmoid(g2[...]) * a2[...]).astype(o_ref.dtype)

    row = pl.BlockSpec((ts, D), lambda i: (i, 0))
    gates = [pl.BlockSpec((ts, D), functools.partial(lambda i, n: (i, gb + n), n=n)) for n in range(3)]
    return pl.pallas_call(
        body, grid=(S // ts,), in_specs=gates + [row, row, row], out_specs=row,
        out_shape=jax.ShapeDtypeStruct((S, D), BF16), name="merge_fwd",
        compiler_params=_cparams(("parallel",)),
    )(proj, proj, proj, p_sb, p_ssm, p_mem)


def _merge_bwd(dmerged, proj, gate_off, p_sb, p_ssm, p_mem):
    S, D = p_sb.shape
    ts = _tile(S, 256, SUBLANES)
    gb = gate_off // D

    def body(dm_ref, g0, g1, g2, a0, a1, a2, d0, d1, d2, l0, l1, l2):
        dm = dm_ref[...]
        for g_ref, a_ref, d_ref, l_ref in ((g0, a0, d0, l0), (g1, a1, d1, l1), (g2, a2, d2, l2)):
            s = jax.nn.sigmoid(g_ref[...])
            d_ref[...] = (dm * s).astype(d_ref.dtype)
            l_ref[...] = dm * a_ref[...] * s * (1.0 - s)

    row = pl.BlockSpec((ts, D), lambda i: (i, 0))
    gates = [pl.BlockSpec((ts, D), functools.partial(lambda i, n: (i, gb + n), n=n)) for n in range(3)]
    sd = jax.ShapeDtypeStruct((S, D), F32)
    return pl.pallas_call(
        body, grid=(S // ts,), in_specs=[row] + gates + [row, row, row], out_specs=[row] * 6,
        out_shape=[jax.ShapeDtypeStruct((S, D), BF16)] * 3 + [sd] * 3, name="merge_bwd", compiler_params=_cparams(("parallel",)),
    )(dmerged, proj, proj, proj, p_sb, p_ssm, p_mem)


def _glu_fwd(glu):
    S, W2 = glu.shape
    W = W2 // 2
    ts = _tile(S, 512, SUBLANES)

    def body(x_ref, o_ref):
        o_ref[...] = (x_ref[:, :W] * jax.nn.sigmoid(x_ref[:, W:])).astype(o_ref.dtype)

    return pl.pallas_call(
        body, grid=(S // ts,), in_specs=[pl.BlockSpec((ts, W2), lambda i: (i, 0))],
        out_specs=pl.BlockSpec((ts, W), lambda i: (i, 0)),
        out_shape=jax.ShapeDtypeStruct((S, W), BF16), name="glu_fwd",
        compiler_params=_cparams(("parallel",)),
    )(glu)


def _glu_bwd(dzz, glu):
    S, W2 = glu.shape
    W = W2 // 2
    ts = _tile(S, 512, SUBLANES)

    def body(d_ref, x_ref, o_ref):
        d = d_ref[...]
        a = x_ref[:, :W]
        s = jax.nn.sigmoid(x_ref[:, W:])
        o_ref[:, :W] = (d * s).astype(o_ref.dtype)
        o_ref[:, W:] = (d * a * s * (1.0 - s)).astype(o_ref.dtype)

    return pl.pallas_call(
        body, grid=(S // ts,),
        in_specs=[pl.BlockSpec((ts, W), lambda i: (i, 0)), pl.BlockSpec((ts, W2), lambda i: (i, 0))],
        out_specs=pl.BlockSpec((ts, W2), lambda i: (i, 0)),
        out_shape=jax.ShapeDtypeStruct((S, W2), BF16), name="glu_bwd",
        compiler_params=_cparams(("parallel",)),
    )(dzz, glu)


def _swiglu_fwd(gu):
    S, F2 = gu.shape
    Fh = F2 // 2
    ts = _tile(S, 128, SUBLANES)

    def body(x_ref, o_ref):
        fg = x_ref[:, :Fh]
        o_ref[...] = (fg * jax.nn.sigmoid(fg) * x_ref[:, Fh:]).astype(o_ref.dtype)

    return pl.pallas_call(
        body, grid=(S // ts,), in_specs=[pl.BlockSpec((ts, F2), lambda i: (i, 0))],
        out_specs=pl.BlockSpec((ts, Fh), lambda i: (i, 0)),
        out_shape=jax.ShapeDtypeStruct((S, Fh), BF16), name="swiglu_fwd",
        compiler_params=_cparams(("parallel",)),
    )(gu)


def _swiglu_bwd(dhid, gu):
    S, F2 = gu.shape
    Fh = F2 // 2
    ts = _tile(S, 128, SUBLANES)

    def body(d_ref, x_ref, o_ref):
        d = d_ref[...]
        fg = x_ref[:, :Fh]
        fu = x_ref[:, Fh:]
        s = jax.nn.sigmoid(fg)
        o_ref[:, :Fh] = (d * fu * s * (1.0 + fg * (1.0 - s))).astype(o_ref.dtype)
        o_ref[:, Fh:] = (d * fg * s).astype(o_ref.dtype)

    return pl.pallas_call(
        body, grid=(S // ts,),
        in_specs=[pl.BlockSpec((ts, Fh), lambda i: (i, 0)), pl.BlockSpec((ts, F2), lambda i: (i, 0))],
        out_specs=pl.BlockSpec((ts, F2), lambda i: (i, 0)),
        out_shape=jax.ShapeDtypeStruct((S, F2), BF16), name="swiglu_bwd",
        compiler_params=_cparams(("parallel",)),
    )(dhid, gu)


def _assemble_dproj(pieces):
    S = pieces[0].shape[0]
    widths = [p.shape[1] for p in pieces]
    total = sum(widths)
    ts = _tile(S, 128, SUBLANES)
    n = len(pieces)

    def body(*refs):
        o_ref, b_ref = refs[n], refs[n + 1]

        @pl.when(pl.program_id(0) == 0)
        def _():
            b_ref[...] = jnp.zeros_like(b_ref)

        off = 0
        for r, w in zip(refs[:n], widths):
            v = r[...].astype(F32)
            o_ref[:, off:off + w] = v.astype(o_ref.dtype)
            b_ref[:, off:off + w] += jnp.sum(v, axis=0, keepdims=True)
            off += w

    return pl.pallas_call(
        body, grid=(S // ts,),
        in_specs=[pl.BlockSpec((ts, w), lambda i: (i, 0)) for w in widths],
        out_specs=[pl.BlockSpec((ts, total), lambda i: (i, 0)), pl.BlockSpec((1, total), lambda i: (0, 0))],
        out_shape=[jax.ShapeDtypeStruct((S, total), BF16), jax.ShapeDtypeStruct((1, total), F32)],
        name="assemble_dproj", compiler_params=_cparams(("arbitrary",)),
    )(*pieces)


def _loss_head(y, target):
    S, D = y.shape
    ts = _tile(S, 256, SUBLANES)

    def body(y_ref, t_ref, dy_ref, l_ref):
        @pl.when(pl.program_id(0) == 0)
        def _():
            l_ref[...] = jnp.zeros_like(l_ref)

        e = y_ref[...] - t_ref[...]
        dy_ref[...] = e * (1.0 / D)
        part = jnp.sum(jnp.sum(e * e, axis=1, keepdims=True), axis=0, keepdims=True) * (0.5 / D)
        l_ref[...] += jnp.broadcast_to(part, l_ref.shape)

    row = pl.BlockSpec((ts, D), lambda i: (i, 0))
    return pl.pallas_call(
        body, grid=(S // ts,), in_specs=[row, row],
        out_specs=[row, pl.BlockSpec((1, LANES), lambda i: (0, 0))],
        out_shape=[jax.ShapeDtypeStruct((S, D), F32), jax.ShapeDtypeStruct((1, LANES), F32)],
        name="loss_head", compiler_params=_cparams(("arbitrary",)),
    )(y, target)


SB_TQ = 512
SB_TK = 512


def _sb_tile_terms(q, kb, scale, causal):
    z = _dot(q, kb, NT) * scale
    soft = jnp.log(1.0 + jnp.exp(-jnp.abs(z)))
    ls = jnp.minimum(z, 0.0) - soft
    l1m = jnp.minimum(-z, 0.0) - soft
    if causal is not None:
        l1m = jnp.where(causal, l1m, 0.0)
    return ls, l1m


def _sb_causal(qi, kj, TQ, TK):
    t_idx = qi * TQ + lax.broadcasted_iota(jnp.int32, (TQ, TK), 0)
    s_idx = kj * TK + lax.broadcasted_iota(jnp.int32, (TQ, TK), 1)
    return s_idx < t_idx


def _exchange_begin(heads, nq, start, middle=None):
    h, qi = pl.program_id(0), pl.program_id(1)

    @pl.when((h == 0) & (qi == 0))
    def _():
        start()

    if middle is not None:
        @pl.when((h == heads - 1) & (qi == nq - 1))
        def _():
            middle()


def _exchange_end(heads, nq, finish):
    @pl.when((pl.program_id(0) == heads - 1) & (pl.program_id(1) == nq - 1))
    def _():
        finish()


def _sb_fwd(proj, q_off, k_off, v_off, heads, ag=None):
    S = proj.shape[0]
    Dh = HEAD_DIM
    TQ = min(SB_TQ, S)
    TK = min(SB_TK, TQ)
    nq = S // TQ
    scale = Dh ** -0.5
    qb, kb0, vb0 = q_off // Dh, k_off // Dh, v_off // Dh
    n_ag = 0 if ag is None else len(ag)
    assert ag is None or heads >= 2

    def body(q_ref, k_ref, v_ref, *rest):
        o_ref, c_ref = rest[n_ag:n_ag + 2]
        if n_ag:
            ag_start, ag_middle, ag_finish = _ag_hooks(rest[n_ag + 2:2 * n_ag + 2], *rest[2 * n_ag + 2:])
            _exchange_begin(heads, nq, ag_start, ag_middle)
        qi = pl.program_id(1)
        q = q_ref[...].astype(BF16)
        upper = _tri2(TK, lambda j, s: j > s)
        nfull = (qi * TQ) // TK

        def block(kj, carry, masked):
            c, acc = carry
            off = pl.multiple_of(kj * TK, TK)
            kblk = k_ref[pl.ds(off, TK), :].astype(BF16)
            vblk = v_ref[pl.ds(off, TK), :].astype(BF16)
            causal = _sb_causal(qi, kj, TQ, TK) if masked else None
            ls, l1m = _sb_tile_terms(q, kblk, scale, causal)
            w = jnp.exp(ls + _dot_mask(l1m, upper) + c)
            if masked:
                w = jnp.where(causal, w, 0.0)
            acc = acc + _dot(w.astype(BF16), vblk)
            c = c + jnp.sum(l1m, axis=1, keepdims=True)
            return c, acc

        carry = (jnp.zeros((TQ, 1), F32), jnp.zeros((TQ, Dh), F32))
        for d in range(TQ // TK - 1, -1, -1):
            carry = block(nfull + d, carry, True)

        c, acc = lax.fori_loop(0, nfull, lambda jj, carry: block(nfull - 1 - jj, carry, False), carry)
        o_ref[...] = acc.astype(o_ref.dtype)
        c_ref[...] = c
        if n_ag:
            _exchange_end(heads, nq, ag_finish)

    ag = [] if ag is None else list(ag)
    res = pl.pallas_call(
        body, grid=(heads, nq),
        in_specs=[pl.BlockSpec((TQ, Dh), lambda h, i: (i, qb + h)),
                  pl.BlockSpec((S, Dh), lambda h, i: (0, kb0 + h)),
                  pl.BlockSpec((S, Dh), lambda h, i: (0, vb0 + h))] + [ANY] * n_ag,
        out_specs=[pl.BlockSpec((TQ, Dh), lambda h, i: (i, h)),
                   pl.BlockSpec((None, TQ, 1), lambda h, i: (h, i, 0))] + [ANY] * n_ag,
        out_shape=[jax.ShapeDtypeStruct((S, heads * Dh), BF16), jax.ShapeDtypeStruct((heads, S, 1), F32)]
        + [jax.ShapeDtypeStruct(b.shape, b.dtype) for b in ag],
        input_output_aliases={3 + i: 2 + i for i in range(n_ag)},
        scratch_shapes=[pltpu.SemaphoreType.DMA((n_ag * 6,)), pltpu.SemaphoreType.DMA((n_ag * 6,))] if n_ag else [],
        name="sb_fwd_ag" if n_ag else "sb_fwd",
        compiler_params=_cparams(("arbitrary", "arbitrary") if n_ag else ("parallel", "arbitrary")),
    )(proj, proj, proj, *ag)
    return res[0], res[1], list(res[2:])


def _sb_bwd(proj, q_off, k_off, v_off, heads, dout, ctot, rs=None):
    S = proj.shape[0]
    Dh = HEAD_DIM
    TQ = min(SB_TQ, S)
    TK = min(SB_TK, TQ)
    nq = S // TQ
    scale = Dh ** -0.5
    qb, kb0, vb0 = q_off // Dh, k_off // Dh, v_off // Dh
    n_rs = 0 if rs is None else len(rs)

    def body(q_ref, k_ref, v_ref, do_ref, c_ref, *rest):
        dq_ref, dk_ref, dv_ref = rest[n_rs:n_rs + 3]
        if n_rs:
            rs_start, rs_finish = _rs_chips_hooks(rest[:n_rs], rest[n_rs + 3:2 * n_rs + 3], *rest[2 * n_rs + 3:])
            _exchange_begin(heads, nq, rs_start)
        qi = pl.program_id(1)

        @pl.when(qi == 0)
        def _():
            dk_ref[...] = jnp.zeros_like(dk_ref)
            dv_ref[...] = jnp.zeros_like(dv_ref)

        q = q_ref[...].astype(BF16)
        do = do_ref[...].astype(BF16)
        ctot = c_ref[...]
        lower_incl = _tri2(TK, lambda j, s: j <= s)
        lower = _tri2(TK, lambda j, s: j < s)
        nfull = (qi * TQ) // TK

        def block(kj, carry, masked):
            cl, ce, dq = carry
            off = pl.multiple_of(kj * TK, TK)
            kblk = k_ref[pl.ds(off, TK), :].astype(BF16)
            vblk = v_ref[pl.ds(off, TK), :].astype(BF16)
            causal = _sb_causal(qi, kj, TQ, TK) if masked else None
            ls, l1m = _sb_tile_terms(q, kblk, scale, causal)
            w = jnp.exp(ls + (ctot - cl - _dot_mask(l1m, lower_incl)))
            if masked:
                w = jnp.where(causal, w, 0.0)
            e = w * _dot(do, vblk, NT)
            before = ce + _dot_mask(e, lower)
            beta = jnp.exp(ls)
            dz = (e * (1.0 - beta) - beta * before) * scale
            if masked:
                dz = jnp.where(causal, dz, 0.0)
            dzb = dz.astype(BF16)
            dq = dq + _dot(dzb, kblk)
            dk_ref[pl.ds(off, TK), :] += _dot(dzb, q, TN)
            dv_ref[pl.ds(off, TK), :] += _dot(w.astype(BF16), do, TN)
            cl = cl + jnp.sum(l1m, axis=1, keepdims=True)
            ce = ce + jnp.sum(e, axis=1, keepdims=True)
            return cl, ce, dq

        zero = jnp.zeros((TQ, 1), F32)
        carry = lax.fori_loop(0, nfull, lambda kj, carry: block(kj, carry, False),
                              (zero, zero, jnp.zeros((TQ, Dh), F32)))
        for d in range(TQ // TK):
            carry = block(nfull + d, carry, True)
        dq_ref[...] = carry[2]
        if n_rs:
            _exchange_end(heads, nq, rs_finish)

    rs = [] if rs is None else list(rs)
    blk = pl.BlockSpec((TQ, Dh), lambda h, i: (i, h))
    col = pl.BlockSpec((S, Dh), lambda h, i: (0, h))
    sd = jax.ShapeDtypeStruct((S, heads * Dh), F32)
    res = pl.pallas_call(
        body, grid=(heads, nq),
        in_specs=[pl.BlockSpec((TQ, Dh), lambda h, i: (i, qb + h)),
                  pl.BlockSpec((S, Dh), lambda h, i: (0, kb0 + h)),
                  pl.BlockSpec((S, Dh), lambda h, i: (0, vb0 + h)), blk,
                  pl.BlockSpec((None, TQ, 1), lambda h, i: (h, i, 0))] + [ANY] * n_rs,
        out_specs=[blk, col, col] + [ANY] * n_rs, out_shape=[sd, sd, sd] + _rs_chips_shapes(rs),
        scratch_shapes=[pltpu.SemaphoreType.DMA((n_rs * 3,)), pltpu.SemaphoreType.DMA((n_rs * 3,))] if n_rs else [],
        name="sb_bwd_rs" if n_rs else "sb_bwd",
        compiler_params=_cparams(("arbitrary", "arbitrary") if n_rs else ("parallel", "arbitrary")),
    )(proj, proj, proj, dout, ctot, *rs)
    return res[0], res[1], res[2], list(res[3:])


def _mem_probs(qh, kh, scale):
    s = _dot(qh, kh, NT) * scale
    m = jnp.max(s, axis=-1, keepdims=True)
    p = jnp.exp(s - m)
    return p / jnp.sum(p, axis=-1, keepdims=True)


def _mem_fwd(proj, q_off, width, kv):
    S = proj.shape[0]
    Dh = HEAD_DIM
    heads = width // Dh
    ts = _tile(S, 512, SUBLANES)
    scale = Dh ** -0.5
    M = kv.shape[0]

    def body(q_ref, kv_ref, o_ref):
        for h in range(heads):
            qh = q_ref[:, h * Dh:(h + 1) * Dh].astype(BF16)
            kh = kv_ref[:, h * Dh:(h + 1) * Dh].astype(BF16)
            vh = kv_ref[:, width + h * Dh:width + (h + 1) * Dh].astype(BF16)
            p = _mem_probs(qh, kh, scale)
            o_ref[:, h * Dh:(h + 1) * Dh] = _dot(p.astype(BF16), vh).astype(o_ref.dtype)

    return pl.pallas_call(
        body, grid=(S // ts,),
        in_specs=[pl.BlockSpec((ts, width), lambda i: (i, q_off // width)),
                  pl.BlockSpec((M, 2 * width), lambda i: (0, 0))],
        out_specs=pl.BlockSpec((ts, width), lambda i: (i, 0)),
        out_shape=jax.ShapeDtypeStruct((S, width), BF16), name="mem_fwd",
        compiler_params=_cparams(("parallel",)),
    )(proj, kv)


def _mem_bwd(proj, q_off, width, kv, dmm):
    S = proj.shape[0]
    Dh = HEAD_DIM
    heads = width // Dh
    ts = _tile(S, 512, SUBLANES)
    scale = Dh ** -0.5
    M = kv.shape[0]

    def body(q_ref, kv_ref, d_ref, dq_ref, dkv_ref):
        @pl.when(pl.program_id(0) == 0)
        def _():
            dkv_ref[...] = jnp.zeros_like(dkv_ref)

        for h in range(heads):
            qh = q_ref[:, h * Dh:(h + 1) * Dh].astype(BF16)
            kh = kv_ref[:, h * Dh:(h + 1) * Dh].astype(BF16)
            vh = kv_ref[:, width + h * Dh:width + (h + 1) * Dh].astype(BF16)
            dh = d_ref[:, h * Dh:(h + 1) * Dh].astype(BF16)
            p = _mem_probs(qh, kh, scale)
            dp = _dot(dh, vh, NT)
            ds = p * (dp - jnp.sum(dp * p, axis=-1, keepdims=True)) * scale
            dsb = ds.astype(BF16)
            dq_ref[:, h * Dh:(h + 1) * Dh] = _dot(dsb, kh)
            dkv_ref[:, h * Dh:(h + 1) * Dh] += _dot(dsb, qh, TN)
            dkv_ref[:, width + h * Dh:width + (h + 1) * Dh] += _dot(p.astype(BF16), dh, TN)

    row = pl.BlockSpec((ts, width), lambda i: (i, 0))
    full = pl.BlockSpec((M, 2 * width), lambda i: (0, 0))
    return pl.pallas_call(
        body, grid=(S // ts,),
        in_specs=[pl.BlockSpec((ts, width), lambda i: (i, q_off // width)), full, row],
        out_specs=[row, full],
        out_shape=[jax.ShapeDtypeStruct((S, width), F32), jax.ShapeDtypeStruct((M, 2 * width), F32)],
        name="mem_bwd", compiler_params=_cparams(("arbitrary",)),
    )(proj, kv, dmm)


def _disc_math(lre, lim, logdt, bre_t, bim_t):
    dt = jnp.exp(logdt)
    mag = jnp.exp(lre * dt)
    ang = lim * dt
    a = mag * jnp.cos(ang)
    b = mag * jnp.sin(ang)
    den = lre * lre + lim * lim
    nr = a - 1.0
    fre = (nr * lre + b * lim) / den
    fim = (b * lre - nr * lim) / den
    bbre = fre * bre_t - fim * bim_t
    bbim = fre * bim_t + fim * bre_t
    return a, b, bbre, bbim


def _s5_disc(lre, lim, logdt, bre_t, bim_t):
    G, _, P = lre.shape
    C = bre_t.shape[1]

    def body(lre_ref, lim_ref, dt_ref, br_ref, bi_ref, a_ref, b_ref, bbre_ref, bbim_ref):
        a, b, bbre, bbim = _disc_math(lre_ref[...], lim_ref[...], dt_ref[...], br_ref[...], bi_ref[...])
        a_ref[...] = a
        b_ref[...] = b
        bbre_ref[...] = bbre
        bbim_ref[...] = bbim

    gp = jax.ShapeDtypeStruct((G, 1, P), F32)
    gcp = jax.ShapeDtypeStruct((G, C, P), F32)
    return pl.pallas_call(
        body, in_specs=[VMEM] * 5, out_specs=[VMEM] * 4, out_shape=[gp, gp, gcp, gcp], name="s5_disc",
    )(lre, lim, logdt, bre_t, bim_t)


def _s5_disc_bwd(lre, lim, logdt, bre_t, bim_t, da, db, dbbre, dbbim):
    G, _, P = lre.shape
    C = bre_t.shape[1]

    def body(lre_ref, lim_ref, dt_ref, br_ref, bi_ref, da_ref, db_ref, dbr_ref, dbi_ref,
             o_lre, o_lim, o_dt, o_br, o_bi):
        _, vjp = jax.vjp(_disc_math, lre_ref[...], lim_ref[...], dt_ref[...], br_ref[...], bi_ref[...])
        g = vjp((da_ref[...], db_ref[...], dbr_ref[...], dbi_ref[...]))
        o_lre[...] = g[0]
        o_lim[...] = g[1]
        o_dt[...] = g[2]
        o_br[...] = g[3]
        o_bi[...] = g[4]

    gp = jax.ShapeDtypeStruct((G, 1, P), F32)
    gcp = jax.ShapeDtypeStruct((G, C, P), F32)
    return pl.pallas_call(
        body, in_specs=[VMEM] * 9, out_specs=[VMEM] * 5,
        out_shape=[gp, gp, jax.ShapeDtypeStruct((G, 1, 1), F32), gcp, gcp], name="s5_disc_bwd",
    )(lre, lim, logdt, bre_t, bim_t, da, db, dbbre, dbbim)


S5_CHUNK = 256


def _load_once(pairs):
    @pl.when(pl.program_id(0) == 0)
    def _():
        for src, dst in pairs:
            pltpu.sync_copy(src, dst)


def _s5_fwd(proj, u_off, width, a_row, b_row, bmre, bmim, cmre, cmimn, d_row):
    S = proj.shape[0]
    GP = a_row.shape[1]
    T = min(S5_CHUNK, S)

    def body(u_ref, a_ref, b_ref, d_ref, bre_hbm, bim_hbm, cre_hbm, cim_hbm,
             uo_ref, y_ref, gy_ref, hre_ref, him_ref, st_ref, bure_s, buim_s, bre_ref, bim_ref, cre_ref, cim_ref):
        @pl.when(pl.program_id(0) == 0)
        def _():
            st_ref[...] = jnp.zeros_like(st_ref)

        _load_once([(bre_hbm, bre_ref), (bim_hbm, bim_ref), (cre_hbm, cre_ref), (cim_hbm, cim_ref)])
        u = u_ref[...]
        uo_ref[...] = u
        for ws, gs in _bd_blocks(width, GP):
            bure_s[:, gs] = _dot_a2(u[:, ws], bre_ref[ws, gs])
            buim_s[:, gs] = _dot_a2(u[:, ws], bim_ref[ws, gs])
        a = a_ref[...]
        b = b_ref[...]

        def step(ii, carry):
            hre, him = carry
            base = pl.multiple_of(ii * SUBLANES, SUBLANES)
            br = bure_s[pl.ds(base, SUBLANES), :]
            bi = buim_s[pl.ds(base, SUBLANES), :]
            rows_re, rows_im = [], []
            for j in range(SUBLANES):
                nre = a * hre - b * him + br[j:j + 1, :]
                nim = a * him + b * hre + bi[j:j + 1, :]
                hre, him = nre, nim
                rows_re.append(nre)
                rows_im.append(nim)
            hre_ref[pl.ds(base, SUBLANES), :] = jnp.concatenate(rows_re, axis=0)
            him_ref[pl.ds(base, SUBLANES), :] = jnp.concatenate(rows_im, axis=0)
            return hre, him

        hre, him = lax.fori_loop(0, T // SUBLANES, step, (st_ref[0:1, :], st_ref[1:2, :]))
        st_ref[0:1, :] = hre
        st_ref[1:2, :] = him
        for ws, gs in _bd_blocks(width, GP):
            y = (_dot_a2(hre_ref[:, gs], cre_ref[gs, ws]) + _dot_a2(him_ref[:, gs], cim_ref[gs, ws])
                 + d_ref[:, ws] * u[:, ws])
            y_ref[:, ws] = y
            gy_ref[:, ws] = jax.nn.gelu(y).astype(gy_ref.dtype)

    c0 = lambda i: (0, 0)
    urow = pl.BlockSpec((T, width), lambda i: (i, u_off // width))
    row = pl.BlockSpec((T, width), lambda i: (i, 0))
    hrow = pl.BlockSpec((T, GP), lambda i: (i, 0))
    sw = jax.ShapeDtypeStruct((S, width), F32)
    sg = jax.ShapeDtypeStruct((S, GP), F32)
    return pl.pallas_call(
        body, grid=(S // T,),
        in_specs=[urow, pl.BlockSpec((1, GP), c0), pl.BlockSpec((1, GP), c0), pl.BlockSpec((1, width), c0),
                  ANY, ANY, ANY, ANY],
        out_specs=[row, row, row, hrow, hrow],
        out_shape=[sw, sw, jax.ShapeDtypeStruct((S, width), BF16), sg, sg],
        scratch_shapes=[pltpu.VMEM((SUBLANES, GP), F32), pltpu.VMEM((T, GP), F32), pltpu.VMEM((T, GP), F32),
                        pltpu.VMEM((width, GP), F32), pltpu.VMEM((width, GP), F32),
                        pltpu.VMEM((GP, width), F32), pltpu.VMEM((GP, width), F32)],
        name="s5_fwd", compiler_params=_cparams(("arbitrary",)),
    )(proj, a_row, b_row, d_row, bmre, bmim, cmre, cmimn)


def _s5_bwd(u, dgy, y, hre, him, a_row, b_row, bmre, bmim, cmre, cmimn, d_row):
    S, width = u.shape
    GP = a_row.shape[1]
    T = min(S5_CHUNK, S)
    nchunk = S // T

    def body(u_ref, dgy_ref, y_ref, hre_ref, him_ref, a_ref, b_ref, d_ref, bre_hbm, bim_hbm, cre_hbm, cim_hbm,
             du_ref, dy_ref, gre_s, gim_s, dd_ref, da_ref, db_ref,
             st_ref, bre_ref, bim_ref, cre_ref, cim_ref):
        @pl.when(pl.program_id(0) == 0)
        def _():
            st_ref[...] = jnp.zeros_like(st_ref)
            for r in (dd_ref, da_ref, db_ref):
                r[...] = jnp.zeros_like(r)

        _load_once([(bre_hbm, bre_ref), (bim_hbm, bim_ref), (cre_hbm, cre_ref), (cim_hbm, cim_ref)])
        u = u_ref[...]
        dy = dgy_ref[...] * _gelu_grad(y_ref[...])
        dy_ref[...] = dy
        for ws, gs in _bd_blocks(width, GP):
            gre_s[:, gs] = _dot_a2(dy[:, ws], cre_ref[gs, ws], NT)
            gim_s[:, gs] = _dot_a2(dy[:, ws], cim_ref[gs, ws], NT)
        a = a_ref[...]
        b = b_ref[...]
        g_in_re = st_ref[0:1, :]
        g_in_im = st_ref[1:2, :]

        def step(ii, carry):
            gre, gim = carry
            base = pl.multiple_of((T // SUBLANES - 1 - ii) * SUBLANES, SUBLANES)
            dr = gre_s[pl.ds(base, SUBLANES), :]
            di = gim_s[pl.ds(base, SUBLANES), :]
            rows_re = [None] * SUBLANES
            rows_im = [None] * SUBLANES
            for j in range(SUBLANES - 1, -1, -1):
                nre = dr[j:j + 1, :] + a * gre + b * gim
                nim = di[j:j + 1, :] - b * gre + a * gim
                gre, gim = nre, nim
                rows_re[j] = nre
                rows_im[j] = nim
            gre_s[pl.ds(base, SUBLANES), :] = jnp.concatenate(rows_re, axis=0)
            gim_s[pl.ds(base, SUBLANES), :] = jnp.concatenate(rows_im, axis=0)
            return gre, gim

        gre, gim = lax.fori_loop(0, T // SUBLANES, step, (g_in_re, g_in_im))
        st_ref[0:1, :] = gre
        st_ref[1:2, :] = gim
        last = lax.broadcasted_iota(jnp.int32, (T, 1), 0) == T - 1
        nxt_re = jnp.where(last, g_in_re, pltpu.roll(gre_s[...], T - 1, 0))
        nxt_im = jnp.where(last, g_in_im, pltpu.roll(gim_s[...], T - 1, 0))
        hre = hre_ref[...]
        him = him_ref[...]
        da_ref[...] += jnp.sum(nxt_re * hre + nxt_im * him, axis=0, keepdims=True)
        db_ref[...] += jnp.sum(nxt_im * hre - nxt_re * him, axis=0, keepdims=True)
        for ws, gs in _bd_blocks(width, GP):
            du_ref[:, ws] = (_dot_a2(gre_s[:, gs], bre_ref[ws, gs], NT) + _dot_a2(gim_s[:, gs], bim_ref[ws, gs], NT)
                             + d_ref[:, ws] * dy[:, ws])
        dd_ref[...] += jnp.sum(dy * u, axis=0, keepdims=True)

    c0 = lambda i: (0, 0)
    rev = lambda i: (nchunk - 1 - i, 0)
    row = pl.BlockSpec((T, width), rev)
    hrow = pl.BlockSpec((T, GP), rev)
    v_gp = pl.BlockSpec((1, GP), c0)
    v_w = pl.BlockSpec((1, width), c0)
    sw = jax.ShapeDtypeStruct((S, width), F32)
    sg = jax.ShapeDtypeStruct((S, GP), F32)
    return pl.pallas_call(
        body, grid=(nchunk,),
        in_specs=[row, row, row, hrow, hrow, v_gp, v_gp, v_w, ANY, ANY, ANY, ANY],
        out_specs=[row, row, hrow, hrow, v_w, v_gp, v_gp],
        out_shape=[sw, sw, sg, sg, jax.ShapeDtypeStruct((1, width), F32),
                   jax.ShapeDtypeStruct((1, GP), F32), jax.ShapeDtypeStruct((1, GP), F32)],
        scratch_shapes=[pltpu.VMEM((SUBLANES, GP), F32),
                        pltpu.VMEM((width, GP), F32), pltpu.VMEM((width, GP), F32),
                        pltpu.VMEM((GP, width), F32), pltpu.VMEM((GP, width), F32)],
        name="s5_bwd", compiler_params=_cparams(("arbitrary",)),
    )(u, dgy, y, hre, him, a_row, b_row, d_row, bmre, bmim, cmre, cmimn)


def _block_diag(x):
    G, A, B = x.shape
    eye = jnp.eye(G, dtype=x.dtype)
    return (eye[:, None, :, None] * x[:, :, None, :]).reshape(G * A, G * B)


def _block_diag_take(m, G):
    A, B = m.shape[0] // G, m.shape[1] // G
    return jnp.einsum("gagb->gab", m.reshape(G, A, G, B))


def _adamw(w, g, m, v, with_grad=False):
    shape = w.shape
    C = shape[-1]
    w2, g2, m2, v2 = (t.reshape(-1, C) for t in (w, g, m, v))
    R = w2.shape[0]
    rb = _tile(R, max(SUBLANES, (1 << 19) // C), SUBLANES)
    c1 = 1.0 - ADAM_B1 ** ADAM_STEP
    c2 = 1.0 - ADAM_B2 ** ADAM_STEP
    n_out = 4 if with_grad else 3

    def body(w_ref, g_ref, m_ref, v_ref, d_ref, nm_ref, nv_ref, *g_out):
        gv = g_ref[...]
        nm = ADAM_B1 * m_ref[...] + (1.0 - ADAM_B1) * gv
        nv = ADAM_B2 * v_ref[...] + (1.0 - ADAM_B2) * (gv * gv)
        d_ref[...] = -ADAM_LR * ((nm / c1) / (jnp.sqrt(nv / c2) + ADAM_EPS) + ADAM_WD * w_ref[...])
        nm_ref[...] = nm
        nv_ref[...] = nv
        if with_grad:
            g_out[0][...] = gv

    blk = pl.BlockSpec((rb, C), lambda i: (i, 0))
    sd = jax.ShapeDtypeStruct((R, C), F32)
    outs = pl.pallas_call(
        body, grid=(R // rb,), in_specs=[blk] * 4, out_specs=[blk] * n_out, out_shape=[sd] * n_out,
        name="adamw", compiler_params=_cparams(("parallel",)),
    )(w2, g2, m2, v2)
    return tuple(o.reshape(shape) for o in outs)


def _coords():
    x, y, c = lax.axis_index("x"), lax.axis_index("y"), lax.axis_index("c")
    return x, y, c


def _place_shard(w, l, place):
    _, R, C = w.shape
    rb = _tile(R, max(16, (1 << 19) // C), 16)

    def body(place_ref, w_ref, o_ref):
        o_ref[...] = w_ref[...].astype(BF16)

    grid_spec = pltpu.PrefetchScalarGridSpec(
        num_scalar_prefetch=1, grid=(R // rb,),
        in_specs=[pl.BlockSpec((None, rb, C), lambda i, p: (l, i, 0))],
        out_specs=pl.BlockSpec((None, rb, C), lambda i, p: (p[1], i, 0)))
    return pl.pallas_call(
        body, grid_spec=grid_spec, out_shape=jax.ShapeDtypeStruct((N_CHIPS, R, C), BF16),
        name="place_shard", compiler_params=_cparams(("arbitrary",)),
    )(place, w)


def _ag_hooks(outs, send_sems, recv_sems):
    n = len(outs)

    def rcopy(a, k, block, to):
        cx, cy, cc = block
        hr = outs[a].shape[1] // 2
        blk = outs[a].at[2 * cx + cy, pl.ds(cc * hr, hr)]
        return pltpu.make_async_remote_copy(
            src_ref=blk, dst_ref=blk, send_sem=send_sems.at[a * 6 + k], recv_sem=recv_sems.at[a * 6 + k],
            device_id=to, device_id_type=MESH)

    def places():
        x, y, c = _coords()
        return (x, y, c), (x, y, 1 - c), [(1 - x, y), (x, 1 - y), (1 - x, 1 - y)]

    def start():
        me, _, others = places()
        for a in range(n):
            for j, ch in enumerate(others):
                rcopy(a, j, me, (*ch, me[2])).start()

    def middle():
        me, sibling, others = places()
        for a in range(n):
            for j, ch in enumerate(others):
                rcopy(a, j, (*ch, me[2]), me).wait_recv()
                rcopy(a, 3 + j, (*ch, me[2]), sibling).start()

    def finish():
        me, sibling, others = places()
        for a in range(n):
            for j, ch in enumerate(others):
                rcopy(a, 3 + j, (*ch, sibling[2]), me).wait_recv()
        for a in range(n):
            for j, ch in enumerate(others):
                rcopy(a, j, me, (*ch, me[2])).wait_send()
                rcopy(a, 3 + j, (*ch, me[2]), sibling).wait_send()

    return start, middle, finish


def _ag_weights(bufs):
    n = len(bufs)

    def body(*refs):
        start, middle, finish = _ag_hooks(refs[n:2 * n], *refs[2 * n:])
        start()
        middle()
        finish()

    return pl.pallas_call(
        body, out_shape=[jax.ShapeDtypeStruct(b.shape, b.dtype) for b in bufs],
        in_specs=[ANY] * n, out_specs=[ANY] * n, input_output_aliases={i: i for i in range(n)},
        scratch_shapes=[pltpu.SemaphoreType.DMA((n * 6,)), pltpu.SemaphoreType.DMA((n * 6,))],
        name="ag_weights",
    )(*bufs)


def _rs_pair_hooks(gs, outs, send_sems, recv_sems):
    def copies():
        x, y, c = _coords()
        cps = []
        for i in range(len(gs)):
            hr = gs[i].shape[1] // 2
            cps.append(pltpu.make_async_remote_copy(
                src_ref=gs[i].at[:, pl.ds((1 - c) * hr, hr)], dst_ref=outs[i],
                send_sem=send_sems.at[i], recv_sem=recv_sems.at[i],
                device_id=(x, y, 1 - c), device_id_type=MESH))
        return cps

    def start():
        for cp in copies():
            cp.start()

    def finish():
        for cp in copies():
            cp.wait()

    return start, None, finish


def _rs_pair_comm(grads):
    return dict(ins=list(grads), alias=False, n_sems=len(grads), hooks=_rs_pair_hooks,
                out_shapes=[jax.ShapeDtypeStruct((N_CHIPS, g.shape[1] // 2, g.shape[2]), F32) for g in grads])


def _ag_comm(bufs):
    return dict(ins=list(bufs), alias=True, n_sems=6 * len(bufs),
                hooks=lambda ins, outs, send_sems, recv_sems: _ag_hooks(outs, send_sems, recv_sems),
                out_shapes=[jax.ShapeDtypeStruct(b.shape, b.dtype) for b in bufs])


def _rs_pair(grads):
    n = len(grads)
    comm = _rs_pair_comm(grads)

    def body(*refs):
        start, _, finish = _rs_pair_hooks(refs[:n], refs[n:2 * n], *refs[2 * n:])
        start()
        finish()

    return pl.pallas_call(
        body, out_shape=comm["out_shapes"], in_specs=[ANY] * n, out_specs=[ANY] * n,
        scratch_shapes=[pltpu.SemaphoreType.DMA((n,)), pltpu.SemaphoreType.DMA((n,))],
        name="rs_pair",
    )(*grads)


def _pair_add(g, r, place):
    _, R, C = g.shape
    hr = R // 2
    rb = _tile(hr, max(16, (1 << 19) // C), 16)
    nb = hr // rb

    def body(place_ref, g_ref, r_ref, p16_ref, own_ref):
        s = g_ref[...] + r_ref[...]
        p16_ref[...] = s.astype(BF16)

        @pl.when(pl.program_id(1) == place_ref[1])
        def _():
            own_ref[...] = s

    grid_spec = pltpu.PrefetchScalarGridSpec(
        num_scalar_prefetch=1, grid=(nb, N_CHIPS),
        in_specs=[pl.BlockSpec((None, rb, C), lambda i, k, p: (k, p[0] * nb + i, 0)),
                  pl.BlockSpec((None, rb, C), lambda i, k, p: (k, i, 0))],
        out_specs=[pl.BlockSpec((None, rb, C), lambda i, k, p: (k, i, 0)),
                   pl.BlockSpec((rb, C), lambda i, k, p: (i, 0))])
    return pl.pallas_call(
        body, grid_spec=grid_spec,
        out_shape=[jax.ShapeDtypeStruct((N_CHIPS, hr, C), BF16), jax.ShapeDtypeStruct((hr, C), F32)],
        name="pair_add", compiler_params=_cparams(("arbitrary", "arbitrary")),
    )(place, g, r)


def _rs_chips_hooks(ps, outs, send_sems, recv_sems):
    def copies():
        x, y, c = _coords()
        cps = []
        for a in range(len(ps)):
            for r in (1, 2, 3):
                kx = 1 - x if (r >> 1) else x
                ky = 1 - y if (r & 1) else y
                cps.append(pltpu.make_async_remote_copy(
                    src_ref=ps[a].at[2 * kx + ky], dst_ref=outs[a].at[r - 1],
                    send_sem=send_sems.at[a * 3 + r - 1], recv_sem=recv_sems.at[a * 3 + r - 1],
                    device_id=(kx, ky, c), device_id_type=MESH))
        return cps

    def start():
        for cp in copies():
            cp.start()

    def finish():
        for cp in copies():
            cp.wait()

    return start, finish


def _rs_chips_shapes(p16):
    return [jax.ShapeDtypeStruct((3,) + p.shape[1:], BF16) for p in p16]


def _rs_chips_comm(p16):
    def hooks(ins, outs, send_sems, recv_sems):
        start, finish = _rs_chips_hooks(ins, outs, send_sems, recv_sems)
        return start, None, finish

    return dict(ins=list(p16), alias=False, n_sems=3 * len(p16), hooks=hooks, out_shapes=_rs_chips_shapes(p16))


def _rs_chips(p16):
    n = len(p16)

    def body(*refs):
        start, finish = _rs_chips_hooks(refs[:n], refs[n:2 * n], *refs[2 * n:])
        start()
        finish()

    return pl.pallas_call(
        body, out_shape=_rs_chips_shapes(p16), in_specs=[ANY] * n, out_specs=[ANY] * n,
        scratch_shapes=[pltpu.SemaphoreType.DMA((n * 3,)), pltpu.SemaphoreType.DMA((n * 3,))],
        name="rs_chips",
    )(*p16)


def _chip_sum(own, recv, full, l, place):
    hr, C = own.shape
    rb = _tile(hr, max(16, (1 << 19) // C), 16)
    nb = hr // rb

    def body(place_ref, o_ref, r_ref, full_ref, s_ref):
        s = o_ref[...] + r_ref[0].astype(F32)
        s = s + r_ref[1].astype(F32)
        s_ref[...] = s + r_ref[2].astype(F32)

    grid_spec = pltpu.PrefetchScalarGridSpec(
        num_scalar_prefetch=1, grid=(nb,),
        in_specs=[pl.BlockSpec((rb, C), lambda i, p: (i, 0)), pl.BlockSpec((3, rb, C), lambda i, p: (0, i, 0)), ANY],
        out_specs=pl.BlockSpec((None, rb, C), lambda i, p: (l, p[0] * nb + i, 0)))
    return pl.pallas_call(
        body, grid_spec=grid_spec, out_shape=jax.ShapeDtypeStruct(full.shape, F32),
        input_output_aliases={3: 0}, name="chip_sum", compiler_params=_cparams(("arbitrary",)),
    )(place, own, recv, full)


def _share_pair(fulls):
    n = len(fulls)

    def body(*refs):
        outs = refs[n:2 * n]
        send_sems, recv_sems = refs[2 * n:]
        x, y, c = _coords()
        copies = []
        for a in range(n):
            hr = outs[a].shape[1] // 2
            mine = outs[a].at[:, pl.ds(c * hr, hr)]
            cp = pltpu.make_async_remote_copy(
                src_ref=mine, dst_ref=mine, send_sem=send_sems.at[a], recv_sem=recv_sems.at[a],
                device_id=(x, y, 1 - c), device_id_type=MESH)
            cp.start()
            copies.append(cp)
        for cp in copies:
            cp.wait_recv()
        for cp in copies:
            cp.wait_send()

    return pl.pallas_call(
        body, out_shape=[jax.ShapeDtypeStruct(f.shape, f.dtype) for f in fulls],
        in_specs=[ANY] * n, out_specs=[ANY] * n, input_output_aliases={i: i for i in range(n)},
        scratch_shapes=[pltpu.SemaphoreType.DMA((n,)), pltpu.SemaphoreType.DMA((n,))],
        name="share_pair",
    )(*fulls)


def _small_allreduce(packed):
    m_per, ncol = packed.shape

    def body(x_ref, out_ref, tot_ref, send_sems, recv_sems, local_sem):
        x, y, c = _coords()
        me, sibling = (x, y, c), (x, y, 1 - c)
        chips = [(1 - x, y), (x, 1 - y), (1 - x, 1 - y)]

        def rows(px, py, pc):
            return out_ref.at[pl.ds((4 * px + 2 * py + pc) * m_per, m_per), :]

        def copy(k, block, to, src=None):
            return pltpu.make_async_remote_copy(
                src_ref=rows(*block) if src is None else src, dst_ref=rows(*block),
                send_sem=send_sems.at[k], recv_sem=recv_sems.at[k], device_id=to, device_id_type=MESH)

        mine = pltpu.make_async_copy(x_ref, rows(*me), local_sem)
        mine.start()
        first = [copy(0, me, sibling, src=x_ref)]
        first += [copy(1 + j, me, (*chip, c), src=x_ref) for j, chip in enumerate(chips)]
        for cp in first:
            cp.start()
        passed = [copy(4 + j, (*chip, c), sibling) for j, chip in enumerate(chips)]
        for j, chip in enumerate(chips):
            copy(1 + j, (*chip, c), me).wait_recv()
            passed[j].start()
        copy(0, sibling, me).wait_recv()
        for j, chip in enumerate(chips):
            copy(4 + j, (*chip, 1 - c), me).wait_recv()
        for cp in first + passed:
            cp.wait_send()
        mine.wait()
        tot = out_ref[pl.ds(0, m_per), :]
        for d in range(1, N_DEV):
            tot = tot + out_ref[pl.ds(d * m_per, m_per), :]
        tot_ref[...] = tot

    _, tot = pl.pallas_call(
        body,
        out_shape=[jax.ShapeDtypeStruct((N_DEV * m_per, ncol), F32), jax.ShapeDtypeStruct((m_per, ncol), F32)],
        in_specs=[VMEM], out_specs=[VMEM, VMEM],
        scratch_shapes=[pltpu.SemaphoreType.DMA((7,)), pltpu.SemaphoreType.DMA((7,)), pltpu.SemaphoreType.DMA],
        name="small_allreduce",
        compiler_params=pltpu.CompilerParams(vmem_limit_bytes=VMEM_LIMIT_MB * 1024 * 1024),
    )(packed)
    return tot


BIG = ["w_in", "sb_w_out", "ssm_w_glu", "ssm_w_out", "mem_w_kv", "mem_w_out", "w_o", "ffn_w_gate_up", "ffn_w_down"]
KIND = {"w_in": "col", "sb_w_out": "col", "ssm_w_glu": "col", "ssm_w_out": "col", "mem_w_kv": "row",
        "mem_w_out": "col", "w_o": "row", "ffn_w_gate_up": "col", "ffn_w_down": "row"}
AG_EARLY = ["w_in", "sb_w_out", "ssm_w_glu", "ssm_w_out", "mem_w_kv", "mem_w_out", "w_o"]
FFN_WEIGHTS = ["ffn_w_down", "ffn_w_gate_up"]
SMALL = ["b_in", "ssm_lambda_re", "ssm_lambda_im", "ssm_log_dt", "ssm_b_re", "ssm_b_im", "ssm_c_re", "ssm_c_im",
         "ssm_d", "ln1_g", "ln1_b", "ln2_g", "ln2_b"]
WEIGHTS = ["w_in", "b_in", "sb_w_out", "ssm_lambda_re", "ssm_lambda_im", "ssm_log_dt", "ssm_b_re", "ssm_b_im",
           "ssm_c_re", "ssm_c_im", "ssm_d", "ssm_w_glu", "ssm_w_out", "mem_w_kv", "mem_w_out", "w_o", "ln1_g",
           "ln1_b", "ffn_w_gate_up", "ffn_w_down", "ln2_g", "ln2_b"]


def _pack(arrs):
    flat = jnp.concatenate([a.reshape(-1).astype(F32) for a in arrs])
    n = flat.shape[0]
    rows = -(-n // LANES)
    rows = -(-rows // SUBLANES) * SUBLANES
    return jnp.pad(flat, (0, rows * LANES - n)).reshape(rows, LANES)


def _unpack(packed, like):
    flat = packed.reshape(-1)
    out, off = [], 0
    for a in like:
        out.append(flat[off:off + a.size].reshape(a.shape))
        off += a.size
    return out


def _step(x, mem, target, W, M1, V1):
    S, D = x.shape[1], x.shape[2]
    L = W["w_in"].shape[0]
    x0 = x.reshape(S, D)
    mem2 = mem.reshape(mem.shape[1], D)
    tgt = target.reshape(S, D)
    alpha = (2 * L) ** 0.25
    sbw = W["sb_w_out"].shape[1]
    ssw = W["ssm_d"].shape[1]
    mw = W["mem_w_out"].shape[1]
    heads = sbw // HEAD_DIM
    G, P = W["ssm_lambda_re"].shape[1], W["ssm_lambda_re"].shape[2]
    q_off, k_off, v_off = 0, sbw, 2 * sbw
    u_off = 3 * sbw
    qm_off = u_off + ssw
    gate_off = qm_off + mw

    x_i, y_i, c_i = _coords()
    place = jnp.stack([c_i, 2 * x_i + y_i]).astype(jnp.int32)
    placed = [[_place_shard(W[n], l, place) for n in BIG] for l in range(L)]
    Wg = {n: [None] * L for n in BIG}
    placed = [dict(zip(BIG, row)) for row in placed]
    for n, buf in zip(AG_EARLY, _ag_weights([placed[0][n] for n in AG_EARLY])):
        Wg[n][0] = buf

    def hosted(plan):
        return _ag_comm([placed[ll][n] for n, ll in plan]) if plan else None

    def keep(plan, bufs):
        for (n, ll), buf in zip(plan, bufs):
            Wg[n][ll] = buf

    saved = []
    xl = x0
    xlb = x0.astype(BF16)
    for l in range(L):
        sv = {"x": xlb}
        more = l + 1 < L
        plan_proj = [("ffn_w_gate_up", 0)] if l == 0 else []
        plan_sb = ([("ffn_w_down", 0)] if l == 0 else []) + ([(n, l + 1) for n in AG_EARLY] if more else [])
        plan_gate_up = [("ffn_w_gate_up", l + 1)] if more else []
        plan_down = [("ffn_w_down", l + 1)] if more else []

        proj = _mm_nn(xlb, Wg["w_in"][l], "col", bias=W["b_in"][l][None, :], comm=hosted(plan_proj),
                      name="mm_proj_ag" if plan_proj else "mm_proj")
        if plan_proj:
            proj, got = proj
            keep(plan_proj, got)
        sb, sb_ctot, got = _sb_fwd(proj, q_off, k_off, v_off, heads,
                                   ag=[placed[ll][n] for n, ll in plan_sb] if plan_sb else None)
        keep(plan_sb, got)
        p_sb = _mm_nn(sb, Wg["sb_w_out"][l], "col", name="mm_sb_out")

        bre_t = W["ssm_b_re"][l].transpose(0, 2, 1)
        bim_t = W["ssm_b_im"][l].transpose(0, 2, 1)
        logdt = W["ssm_log_dt"][l][:, None, None]
        lre3 = W["ssm_lambda_re"][l][:, None, :]
        lim3 = W["ssm_lambda_im"][l][:, None, :]
        a_gp, b_gp, bbre, bbim = _s5_disc(lre3, lim3, logdt, bre_t, bim_t)
        a_row, b_row = a_gp.reshape(1, G * P), b_gp.reshape(1, G * P)
        bmre, bmim = _block_diag(bbre), _block_diag(bbim)
        cmre = _block_diag(W["ssm_c_re"][l].transpose(0, 2, 1))
        cmimn = _block_diag(-W["ssm_c_im"][l].transpose(0, 2, 1))
        d_row = W["ssm_d"][l][None, :]
        u_ssm, y, gy, hre, him = _s5_fwd(proj, u_off, ssw, a_row, b_row, bmre, bmim, cmre, cmimn, d_row)
        glu = _mm_nn(gy, Wg["ssm_w_glu"][l], "col", name="mm_glu")
        zz = _glu_fwd(glu)
        p_ssm = _mm_nn(zz, Wg["ssm_w_out"][l], "col", name="mm_ssm_out")

        kv = _mm_nn(mem2, Wg["mem_w_kv"][l], "row", name="mm_kv")
        mm_o = _mem_fwd(proj, qm_off, mw, kv)
        p_mem = _mm_nn(mm_o, Wg["mem_w_out"][l], "col", name="mm_mem_out")

        merged = _merge_fwd(proj, gate_off, p_sb, p_ssm, p_mem)
        mix = _mm_nn(merged, Wg["w_o"][l], "row", name="mm_wo")
        x1, x1b, xh1, rs1 = _ln_fwd(xl, mix, W["ln1_g"][l][None, :], W["ln1_b"][l][None, :], alpha)
        gu = _mm_nn(x1b, Wg["ffn_w_gate_up"][l], "col", comm=hosted(plan_gate_up),
                    name="mm_gate_up_ag" if plan_gate_up else "mm_gate_up")
        if plan_gate_up:
            gu, got = gu
            keep(plan_gate_up, got)
        hid = _swiglu_fwd(gu)
        ffn = _mm_nn(hid, Wg["ffn_w_down"][l], "row", comm=hosted(plan_down),
                     name="mm_down_ag" if plan_down else "mm_down")
        if plan_down:
            ffn, got = ffn
            keep(plan_down, got)
        x2, x2b, xh2, rs2 = _ln_fwd(x1, ffn, W["ln2_g"][l][None, :], W["ln2_b"][l][None, :], alpha)
        sv.update(proj=proj, sb=sb, sb_ctot=sb_ctot, p_sb=p_sb, y=y, gy=gy, hre=hre, him=him, glu=glu, zz=zz,
                  p_ssm=p_ssm, kv=kv, mm_o=mm_o, p_mem=p_mem, merged=merged, x1=x1b, xh1=xh1, rs1=rs1, gu=gu,
                  hid=hid, xh2=xh2, rs2=rs2, u=u_ssm,
                  disc=(lre3, lim3, logdt, bre_t, bim_t, a_row, b_row, bmre, bmim, cmre, cmimn, d_row))
        saved.append(sv)
        xl, xlb = x2, x2b

    dxl, loss_part = _loss_head(xl, tgt)

    gbig = {n: [None] * L for n in BIG}
    gsmall = {n: [None] * L for n in SMALL}
    own = [None] * L
    from_chips = [None] * L
    above = None
    pending = None
    for l in range(L - 1, -1, -1):
        sv = saved[l]
        proj = sv["proj"]
        dr2, dr2b, dg2, db2 = _ln_bwd(dxl, sv["xh2"], sv["rs2"], W["ln2_g"][l][None, :])
        gsmall["ln2_g"][l], gsmall["ln2_b"][l] = dg2[0], db2[0]
        dhid = _mm_nt(dr2b, Wg["ffn_w_down"][l], "row", name="mm_d_hid")
        gbig["ffn_w_down"][l] = _mm_tn(sv["hid"], dr2b, "row", name="mm_g_down")
        dgu = _swiglu_bwd(dhid, sv["gu"])
        if above is None:
            dx1 = _mm_nt(dgu, Wg["ffn_w_gate_up"][l], "col", add=dr2, add_scale=alpha, name="mm_d_x1")
        else:
            dx1, from_sibling = _mm_nt(dgu, Wg["ffn_w_gate_up"][l], "col", add=dr2, add_scale=alpha,
                                       name="mm_d_x1_rs", comm=_rs_pair_comm(above))
            sums = [_pair_add(g, r, place) for g, r in zip(above, from_sibling)]
            pending = [s[0] for s in sums]
            own[l + 1] = [s[1] for s in sums]
        gbig["ffn_w_gate_up"][l] = _mm_tn(sv["x1"], dgu, "col", name="mm_g_gate_up")

        dr1, dr1b, dg1, db1 = _ln_bwd(dx1, sv["xh1"], sv["rs1"], W["ln1_g"][l][None, :])
        gsmall["ln1_g"][l], gsmall["ln1_b"][l] = dg1[0], db1[0]
        dmerged = _mm_nt(dr1b, Wg["w_o"][l], "row", name="mm_d_merged")
        gbig["w_o"][l] = _mm_tn(sv["merged"], dr1b, "row", name="mm_g_wo")
        dp_sb, dp_ssm, dp_mem, dgl0, dgl1, dgl2 = _merge_bwd(
            dmerged, proj, gate_off, sv["p_sb"], sv["p_ssm"], sv["p_mem"])

        dsb = _mm_nt(dp_sb, Wg["sb_w_out"][l], "col", out_dtype=BF16, name="mm_d_sb")
        gbig["sb_w_out"][l] = _mm_tn(sv["sb"], dp_sb, "col", name="mm_g_sb_out")

        dzz = _mm_nt(dp_ssm, Wg["ssm_w_out"][l], "col", name="mm_d_zz")
        gbig["ssm_w_out"][l] = _mm_tn(sv["zz"], dp_ssm, "col", name="mm_g_ssm_out")
        dglu = _glu_bwd(dzz, sv["glu"])
        dgy = _mm_nt(dglu, Wg["ssm_w_glu"][l], "col", name="mm_d_gy")
        gbig["ssm_w_glu"][l] = _mm_tn(sv["gy"], dglu, "col", name="mm_g_glu")
        lre3, lim3, logdt, bre_t, bim_t, a_row, b_row, bmre, bmim, cmre, cmimn, d_row = sv["disc"]
        du, dy_ssm, g_re, g_im, dd, da, db = _s5_bwd(
            sv["u"], dgy, sv["y"], sv["hre"], sv["him"], a_row, b_row, bmre, bmim, cmre, cmimn, d_row)
        dbmre = _mm_tn(sv["u"], g_re, "plain", name="mm_g_ssm_bre")
        dbmim = _mm_tn(sv["u"], g_im, "plain", name="mm_g_ssm_bim")
        dcmre = _mm_tn(sv["hre"], dy_ssm, "plain", name="mm_g_ssm_cre")
        dcmimn = _mm_tn(sv["him"], dy_ssm, "plain", name="mm_g_ssm_cim")
        dlre, dlim, dlogdt, dbre_t, dbim_t = _s5_disc_bwd(
            lre3, lim3, logdt, bre_t, bim_t, da.reshape(G, 1, P), db.reshape(G, 1, P),
            _block_diag_take(dbmre, G), _block_diag_take(dbmim, G))
        gsmall["ssm_lambda_re"][l], gsmall["ssm_lambda_im"][l] = dlre.reshape(G, P), dlim.reshape(G, P)
        gsmall["ssm_log_dt"][l] = dlogdt.reshape(G)
        gsmall["ssm_b_re"][l] = dbre_t.transpose(0, 2, 1)
        gsmall["ssm_b_im"][l] = dbim_t.transpose(0, 2, 1)
        gsmall["ssm_c_re"][l] = _block_diag_take(dcmre, G).transpose(0, 2, 1)
        gsmall["ssm_c_im"][l] = -_block_diag_take(dcmimn, G).transpose(0, 2, 1)
        gsmall["ssm_d"][l] = dd[0]

        dmm = _mm_nt(dp_mem, Wg["mem_w_out"][l], "col", out_dtype=BF16, name="mm_d_mm")
        gbig["mem_w_out"][l] = _mm_tn(sv["mm_o"], dp_mem, "col", name="mm_g_mem_out")
        dqm, dkv = _mem_bwd(proj, qm_off, mw, sv["kv"], dmm)
        gbig["mem_w_kv"][l] = _mm_tn(mem2, dkv, "row", name="mm_g_kv")

        dq, dk, dv, arrived = _sb_bwd(proj, q_off, k_off, v_off, heads, dsb, sv["sb_ctot"], rs=pending)
        if pending is not None:
            from_chips[l + 1] = arrived
        dproj, dbin = _assemble_dproj([dq, dk, dv, du, dqm, dgl0, dgl1, dgl2])
        gsmall["b_in"][l] = dbin[0]
        if l > 0:
            dxl = _mm_nt(dproj, Wg["w_in"][l], "col", add=dr1, add_scale=alpha, name="mm_d_x")
            gbig["w_in"][l] = _mm_tn(sv["x"], dproj, "col", name="mm_g_win")
            above = [gbig[n][l] for n in BIG]
        else:
            ffn = [gbig[n][0] for n in FFN_WEIGHTS]
            ffn_sums = [_pair_add(g, r, place) for g, r in zip(ffn, _rs_pair(ffn))]
            dxl, got_down = _mm_nt(dproj, Wg["w_in"][0], "col", add=dr1, add_scale=alpha, name="mm_d_x_rs",
                                   comm=_rs_chips_comm([ffn_sums[0][0]]))
            gbig["w_in"][0], got_gate_up = _mm_tn(sv["x"], dproj, "col", name="mm_g_win_rs",
                                                  comm=_rs_chips_comm([ffn_sums[1][0]]))

    rest_names = [n for n in BIG if n not in FFN_WEIGHTS]
    rest = [gbig[n][0] for n in rest_names]
    rest_sums = [_pair_add(g, r, place) for g, r in zip(rest, _rs_pair(rest))]
    rest_got = _rs_chips([s[0] for s in rest_sums])
    by_name = {n: (s[1], r) for n, s, r in zip(rest_names, rest_sums, rest_got)}
    by_name[FFN_WEIGHTS[0]] = (ffn_sums[0][1], got_down[0])
    by_name[FFN_WEIGHTS[1]] = (ffn_sums[1][1], got_gate_up[0])
    own[0] = [by_name[n][0] for n in BIG]
    from_chips[0] = [by_name[n][1] for n in BIG]
    grad_x = dxl.reshape(x.shape)

    fulls = []
    for a, n in enumerate(BIG):
        full = lax.empty(W[n].shape, F32)
        for l in range(L - 1, -1, -1):
            full = _chip_sum(own[l][a], from_chips[l][a], full, l, place)
        fulls.append(full)
    reduced = _share_pair(fulls)
    grads = {n: reduced[i] for i, n in enumerate(BIG)}

    small_local = [jnp.stack(gsmall[n]) for n in SMALL]
    packed = _pack(small_local + [loss_part[0, :1]])
    total = _small_allreduce(packed)
    unpacked = _unpack(total, small_local + [loss_part[0, :1]])
    for n, g in zip(SMALL, unpacked[:-1]):
        grads[n] = g
    loss = unpacked[-1][0]

    delta, new_m, new_v = {}, {}, {}
    for n in BIG:
        delta[n], new_m[n], new_v[n], grads[n] = _adamw(W[n], grads[n], M1[n], V1[n], with_grad=True)
    sm = _adamw(_pack([W[n] for n in SMALL]), _pack([grads[n] for n in SMALL]),
                _pack([M1[n] for n in SMALL]), _pack([V1[n] for n in SMALL]))
    like = [W[n] for n in SMALL]
    for n, d, m_, v_ in zip(SMALL, _unpack(sm[0], like), _unpack(sm[1], like), _unpack(sm[2], like)):
        delta[n], new_m[n], new_v[n] = d, m_, v_

    return (loss, grad_x, *[grads[n] for n in WEIGHTS], *[delta[n] for n in WEIGHTS],
            *[new_m[n] for n in WEIGHTS], *[new_v[n] for n in WEIGHTS])


def kernel(x, mem, w_in, b_in, sb_w_out, ssm_lambda_re, ssm_lambda_im, ssm_log_dt, ssm_b_re, ssm_b_im, ssm_c_re, ssm_c_im, ssm_d, ssm_w_glu, ssm_w_out, mem_w_kv, mem_w_out, w_o, ln1_g, ln1_b, ffn_w_gate_up, ffn_w_down, ln2_g, ln2_b, loss_target, m_w_in, m_b_in, m_sb_w_out, m_ssm_lambda_re, m_ssm_lambda_im, m_ssm_log_dt, m_ssm_b_re, m_ssm_b_im, m_ssm_c_re, m_ssm_c_im, m_ssm_d, m_ssm_w_glu, m_ssm_w_out, m_mem_w_kv, m_mem_w_out, m_w_o, m_ln1_g, m_ln1_b, m_ffn_w_gate_up, m_ffn_w_down, m_ln2_g, m_ln2_b, v_w_in, v_b_in, v_sb_w_out, v_ssm_lambda_re, v_ssm_lambda_im, v_ssm_log_dt, v_ssm_b_re, v_ssm_b_im, v_ssm_c_re, v_ssm_c_im, v_ssm_d, v_ssm_w_glu, v_ssm_w_out, v_mem_w_kv, v_mem_w_out, v_w_o, v_ln1_g, v_ln1_b, v_ffn_w_gate_up, v_ffn_w_down, v_ln2_g, v_ln2_b):
    W = dict(w_in=w_in, b_in=b_in, sb_w_out=sb_w_out, ssm_lambda_re=ssm_lambda_re, ssm_lambda_im=ssm_lambda_im,
             ssm_log_dt=ssm_log_dt, ssm_b_re=ssm_b_re, ssm_b_im=ssm_b_im, ssm_c_re=ssm_c_re, ssm_c_im=ssm_c_im,
             ssm_d=ssm_d, ssm_w_glu=ssm_w_glu, ssm_w_out=ssm_w_out, mem_w_kv=mem_w_kv, mem_w_out=mem_w_out,
             w_o=w_o, ln1_g=ln1_g, ln1_b=ln1_b, ffn_w_gate_up=ffn_w_gate_up, ffn_w_down=ffn_w_down,
             ln2_g=ln2_g, ln2_b=ln2_b)
    M1 = dict(w_in=m_w_in, b_in=m_b_in, sb_w_out=m_sb_w_out, ssm_lambda_re=m_ssm_lambda_re,
              ssm_lambda_im=m_ssm_lambda_im, ssm_log_dt=m_ssm_log_dt, ssm_b_re=m_ssm_b_re, ssm_b_im=m_ssm_b_im,
              ssm_c_re=m_ssm_c_re, ssm_c_im=m_ssm_c_im, ssm_d=m_ssm_d, ssm_w_glu=m_ssm_w_glu,
              ssm_w_out=m_ssm_w_out, mem_w_kv=m_mem_w_kv, mem_w_out=m_mem_w_out, w_o=m_w_o, ln1_g=m_ln1_g,
              ln1_b=m_ln1_b, ffn_w_gate_up=m_ffn_w_gate_up, ffn_w_down=m_ffn_w_down, ln2_g=m_ln2_g, ln2_b=m_ln2_b)
    V1 = dict(w_in=v_w_in, b_in=v_b_in, sb_w_out=v_sb_w_out, ssm_lambda_re=v_ssm_lambda_re,
              ssm_lambda_im=v_ssm_lambda_im, ssm_log_dt=v_ssm_log_dt, ssm_b_re=v_ssm_b_re, ssm_b_im=v_ssm_b_im,
              ssm_c_re=v_ssm_c_re, ssm_c_im=v_ssm_c_im, ssm_d=v_ssm_d, ssm_w_glu=v_ssm_w_glu,
              ssm_w_out=v_ssm_w_out, mem_w_kv=v_mem_w_kv, mem_w_out=v_mem_w_out, w_o=v_w_o, ln1_g=v_ln1_g,
              ln1_b=v_ln1_b, ffn_w_gate_up=v_ffn_w_gate_up, ffn_w_down=v_ffn_w_down, ln2_g=v_ln2_g, ln2_b=v_ln2_b)
    return _step(x, mem, loss_target, W, M1, V1)
```

```python
import functools
import math

import jax
import jax.numpy as jnp
from jax import lax
from jax.experimental import pallas as pl
from jax.experimental.pallas import tpu as pltpu

F32 = jnp.float32
BF16 = jnp.bfloat16
MESH = pl.DeviceIdType.MESH
ANY = pl.BlockSpec(memory_space=pl.ANY)
VMEM = pl.BlockSpec(memory_space=pltpu.VMEM)

HEAD_DIM = 128
SSM_GROUP = 16
N_CHIPS = 4
N_DEV = 8
LN_EPS = 1e-5
ADAM_LR = 0.001
ADAM_B1 = 0.9
ADAM_B2 = 0.999
ADAM_EPS = 1e-08
ADAM_WD = 0.01
ADAM_STEP = 10
LANES = 128
SUBLANES = 8
VMEM_LIMIT_MB = 56


def _cparams(sem, mb=VMEM_LIMIT_MB):
    return pltpu.CompilerParams(dimension_semantics=sem, vmem_limit_bytes=mb * 1024 * 1024)


def _tile(n, pref, mult=LANES):
    best = None
    t = mult
    while t <= min(n, pref):
        if n % t == 0:
            best = t
        t += mult
    return n if best is None else best


def _dot(a, b, dims=(((1,), (0,)), ((), ()))):
    return lax.dot_general(a, b, dims, preferred_element_type=F32)


NT = (((1,), (1,)), ((), ()))
TN = (((0,), (0,)), ((), ()))


def _split2(x):
    hi = x.astype(BF16)
    lo = (x - hi.astype(F32)).astype(BF16)
    return hi, lo


def _dot_a2(a, b, dims=(((1,), (0,)), ((), ()))):
    ah, al = _split2(a)
    bb = b.astype(BF16)
    return _dot(ah, bb, dims) + _dot(al, bb, dims)


def _bd_blocks(n_in, n_out):
    if (n_in // 2) % LANES == 0 and (n_out // 2) % LANES == 0:
        return [(slice(0, n_in // 2), slice(0, n_out // 2)), (slice(n_in // 2, n_in), slice(n_out // 2, n_out))]
    return [(slice(0, n_in), slice(0, n_out))]


def _dot_mask(x, u2):
    hi, lo = _split2(x)
    return _dot(jnp.concatenate([hi, lo], axis=1), u2)


def _tri2(n, rel):
    ri = lax.broadcasted_iota(jnp.int32, (2 * n, n), 0)
    ci = lax.broadcasted_iota(jnp.int32, (2 * n, n), 1)
    ri = jnp.where(ri >= n, ri - n, ri)
    return rel(ri, ci).astype(BF16)


def _mm_call(a, b, *, dims, grid, a_spec, b_spec, out_spec, out_shape, name,
             bias=None, bias_spec=None, add=None, add_spec=None, add_scale=1.0, comm=None, prod=None):
    nk = grid[2]
    has_bias = bias is not None
    has_add = add is not None
    n_ci = len(comm["ins"]) if comm else 0
    n_co = len(comm["out_shapes"]) if comm else 0

    def body(*refs):
        a_ref, b_ref = refs[0], refs[1]
        pos = 2
        bias_ref = refs[pos] if has_bias else None
        pos += int(has_bias)
        add_ref = refs[pos] if has_add else None
        pos += int(has_add)
        comm_in = refs[pos:pos + n_ci]
        pos += n_ci
        o_ref = refs[pos]
        comm_out = refs[pos + 1:pos + 1 + n_co]
        pos += 1 + n_co
        acc_ref = refs[pos] if nk > 1 else None
        pos += int(nk > 1)
        if comm:
            ex_start, ex_middle, ex_finish = comm["hooks"](comm_in, comm_out, refs[pos], refs[pos + 1])
            ids = [pl.program_id(d) for d in range(3)]

            @pl.when((ids[0] == 0) & (ids[1] == 0) & (ids[2] == 0))
            def _():
                ex_start()

        if prod is None:
            p = _dot(a_ref[...].astype(BF16), b_ref[...].astype(BF16), dims)
        else:
            p = prod(a_ref, b_ref)

        def finish(acc):
            if has_bias:
                acc = acc + bias_ref[...]
            if has_add:
                acc = acc + add_scale * add_ref[...].astype(F32)
            o_ref[...] = acc.astype(o_ref.dtype)

        if nk == 1:
            finish(p)
        else:
            k = pl.program_id(2)

            @pl.when(k == 0)
            def _():
                acc_ref[...] = p

            @pl.when(k > 0)
            def _():
                acc_ref[...] += p

            @pl.when(k == nk - 1)
            def _():
                finish(acc_ref[...])

        if comm:
            @pl.when((ids[0] == grid[0] - 1) & (ids[1] == grid[1] - 1) & (ids[2] == grid[2] - 1))
            def _():
                if ex_middle is not None:
                    ex_middle()
                ex_finish()

    ins = [a, b]
    in_specs = [a_spec, b_spec]
    if has_bias:
        ins.append(bias)
        in_specs.append(bias_spec)
    if has_add:
        ins.append(add)
        in_specs.append(add_spec)
    scratch = []
    if nk > 1:
        blk = [d for d in out_spec.block_shape if d is not None]
        scratch.append(pltpu.VMEM(tuple(blk), F32))
    if not comm:
        return pl.pallas_call(
            body, out_shape=out_shape, grid=grid, in_specs=in_specs, out_specs=out_spec,
            scratch_shapes=scratch, name=name,
            compiler_params=_cparams(("parallel", "parallel", "arbitrary")),
        )(*ins)
    n_main = len(ins)
    scratch += [pltpu.SemaphoreType.DMA((comm["n_sems"],)), pltpu.SemaphoreType.DMA((comm["n_sems"],))]
    res = pl.pallas_call(
        body, out_shape=[out_shape] + list(comm["out_shapes"]), grid=grid,
        in_specs=in_specs + [ANY] * n_ci, out_specs=[out_spec] + [ANY] * n_co, scratch_shapes=scratch,
        input_output_aliases={n_main + i: 1 + i for i in range(n_ci)} if comm["alias"] else {},
        name=name, compiler_params=_cparams(("arbitrary", "arbitrary", "arbitrary")),
    )(*ins, *comm["ins"])
    return res[0], list(res[1:])


def _mm_nn(a, w, kind, *, bias=None, out_dtype=F32, name, comm=None):
    M, K = a.shape
    tm = _tile(M, 1024, SUBLANES)
    tk = K if K <= 2048 else _tile(K, 1408)
    if kind == "col":
        Nc = w.shape[2]
        N = N_CHIPS * Nc
        tn = _tile(Nc, 1408)
        npc = Nc // tn
        b_spec = pl.BlockSpec((None, tk, tn), lambda i, j, k: (j // npc, k, j % npc))
    else:
        w = w.reshape(-1, w.shape[-1])
        N = w.shape[1]
        tn = _tile(N, 2048)
        if K > 2048 and a.dtype == BF16:
            tk, tn = K, _tile(N, 512)
        b_spec = pl.BlockSpec((tk, tn), lambda i, j, k: (k, j))
    grid = (M // tm, N // tn, K // tk)
    bias_spec = pl.BlockSpec((1, tn), lambda i, j, k: (0, j)) if bias is not None else None
    return _mm_call(
        a, w, dims=(((1,), (0,)), ((), ())), grid=grid,
        a_spec=pl.BlockSpec((tm, tk), lambda i, j, k: (i, k)), b_spec=b_spec,
        out_spec=pl.BlockSpec((tm, tn), lambda i, j, k: (i, j)),
        out_shape=jax.ShapeDtypeStruct((M, N), out_dtype), name=name,
        bias=bias, bias_spec=bias_spec, comm=comm)


def _mm_nt(dy, w, kind, *, add=None, add_scale=1.0, out_dtype=F32, name, comm=None):
    M, N = dy.shape
    tm = _tile(M, 1024 if dy.dtype == BF16 else 512, SUBLANES)
    if kind == "col" and dy.dtype == BF16:
        K, Nc = w.shape[1], w.shape[2]
        cps = 2
        tm = _tile(M, 1024, SUBLANES)
        tko = _tile(K, 512)

        def prod(a_ref, b_ref):
            acc = None
            for c in range(cps):
                t = _dot(a_ref[:, c * Nc:(c + 1) * Nc], b_ref[c], NT)
                acc = t if acc is None else acc + t
            return acc

        add_spec = pl.BlockSpec((tm, tko), lambda i, j, r: (i, j)) if add is not None else None
        return _mm_call(
            dy, w, dims=NT, grid=(M // tm, K // tko, N_CHIPS // cps), prod=prod,
            a_spec=pl.BlockSpec((tm, cps * Nc), lambda i, j, r: (i, r)),
            b_spec=pl.BlockSpec((cps, tko, Nc), lambda i, j, r: (r, j, 0)),
            out_spec=pl.BlockSpec((tm, tko), lambda i, j, r: (i, j)),
            out_shape=jax.ShapeDtypeStruct((M, K), out_dtype), name=name,
            add=add, add_spec=add_spec, add_scale=add_scale, comm=comm)
    if kind == "col":
        K, Nc = w.shape[1], w.shape[2]
        tn = _tile(Nc, 1408)
        npc = Nc // tn
        tko = _tile(K, 2048)
        b_spec = pl.BlockSpec((None, tko, tn), lambda i, j, r: (r // npc, j, r % npc))
    else:
        w = w.reshape(-1, w.shape[-1])
        K = w.shape[0]
        tn = _tile(N, 2048)
        tko = _tile(K, 2048)
        b_spec = pl.BlockSpec((tko, tn), lambda i, j, r: (j, r))
    grid = (M // tm, K // tko, N // tn)
    add_spec = pl.BlockSpec((tm, tko), lambda i, j, r: (i, j)) if add is not None else None
    return _mm_call(
        dy, w, dims=NT, grid=grid,
        a_spec=pl.BlockSpec((tm, tn), lambda i, j, r: (i, r)), b_spec=b_spec,
        out_spec=pl.BlockSpec((tm, tko), lambda i, j, r: (i, j)),
        out_shape=jax.ShapeDtypeStruct((M, K), out_dtype), name=name,
        add=add, add_spec=add_spec, add_scale=add_scale, comm=comm)


def _mm_tn(a, dy, kind, *, name, comm=None):
    M, K = a.shape
    N = dy.shape[1]
    tm = M if a.dtype == BF16 and dy.dtype == BF16 else _tile(M, 2048, SUBLANES)
    tkw = _tile(K, 512)
    tn = _tile(N // N_CHIPS, 1408) if kind == "col" else _tile(N, 1024)
    a_inner = tkw * a.dtype.itemsize <= tn * dy.dtype.itemsize
    ij = (lambda o, n: (n, o)) if a_inner else (lambda o, n: (o, n))
    if kind == "col":
        Nc = N // N_CHIPS
        npc = Nc // tn

        def out_map(o, n, m):
            i, j = ij(o, n)
            return (j // npc, i, j % npc)

        out_spec = pl.BlockSpec((None, tkw, tn), out_map)
        out_shape = jax.ShapeDtypeStruct((N_CHIPS, K, Nc), F32)
    else:
        out_spec = pl.BlockSpec((tkw, tn), lambda o, n, m: ij(o, n))
        out_shape = jax.ShapeDtypeStruct((K, N), F32)
    ni, nj = K // tkw, N // tn
    grid = (nj, ni, M // tm) if a_inner else (ni, nj, M // tm)
    out = _mm_call(
        a, dy, dims=TN, grid=grid,
        a_spec=pl.BlockSpec((tm, tkw), lambda o, n, m: (m, ij(o, n)[0])),
        b_spec=pl.BlockSpec((tm, tn), lambda o, n, m: (m, ij(o, n)[1])),
        out_spec=out_spec, out_shape=out_shape, name=name, comm=comm)
    extra = None
    if comm:
        out, extra = out
    if kind == "row":
        out = out.reshape(N_CHIPS, K // N_CHIPS, N)
    return (out, extra) if comm else out


def _gelu_grad(x):
    k = math.sqrt(2.0 / math.pi)
    inner = k * (x + 0.044715 * x * x * x)
    t = jnp.tanh(inner)
    return 0.5 * (1.0 + t) + 0.5 * x * (1.0 - t * t) * k * (1.0 + 3.0 * 0.044715 * x * x)


def _ln_fwd(xin, delta, g, b, alpha):
    S, D = xin.shape
    ts = _tile(S, 256, SUBLANES)

    def body(x_ref, d_ref, g_ref, b_ref, y_ref, yb_ref, xh_ref, rs_ref):
        r = alpha * x_ref[...] + d_ref[...]
        mu = jnp.mean(r, axis=-1, keepdims=True)
        rc = r - mu
        var = jnp.mean(rc * rc, axis=-1, keepdims=True)
        rstd = lax.rsqrt(var + LN_EPS)
        xh = rc * rstd
        y = xh * g_ref[...] + b_ref[...]
        y_ref[...] = y
        yb_ref[...] = y.astype(BF16)
        xh_ref[...] = xh
        rs_ref[...] = rstd

    row = pl.BlockSpec((ts, D), lambda i: (i, 0))
    vec = pl.BlockSpec((1, D), lambda i: (0, 0))
    return pl.pallas_call(
        body, grid=(S // ts,), in_specs=[row, row, vec, vec],
        out_specs=[row, row, row, pl.BlockSpec((ts, 1), lambda i: (i, 0))],
        out_shape=[jax.ShapeDtypeStruct((S, D), F32), jax.ShapeDtypeStruct((S, D), BF16),
                   jax.ShapeDtypeStruct((S, D), F32), jax.ShapeDtypeStruct((S, 1), F32)],
        name="ln_fwd", compiler_params=_cparams(("parallel",)),
    )(xin, delta, g, b)


def _ln_bwd(dy, xh, rstd, g):
    S, D = dy.shape
    ts = _tile(S, 256, SUBLANES)

    def body(dy_ref, xh_ref, rs_ref, g_ref, dr_ref, drb_ref, dg_ref, db_ref):
        @pl.when(pl.program_id(0) == 0)
        def _():
            dg_ref[...] = jnp.zeros_like(dg_ref)
            db_ref[...] = jnp.zeros_like(db_ref)

        dyv = dy_ref[...]
        xhv = xh_ref[...]
        dyg = dyv * g_ref[...]
        m1 = jnp.mean(dyg, axis=-1, keepdims=True)
        m2 = jnp.mean(dyg * xhv, axis=-1, keepdims=True)
        dr = rs_ref[...] * (dyg - m1 - xhv * m2)
        dr_ref[...] = dr
        drb_ref[...] = dr.astype(BF16)
        dg_ref[...] += jnp.sum(dyv * xhv, axis=0, keepdims=True)
        db_ref[...] += jnp.sum(dyv, axis=0, keepdims=True)

    row = pl.BlockSpec((ts, D), lambda i: (i, 0))
    vec = pl.BlockSpec((1, D), lambda i: (0, 0))
    return pl.pallas_call(
        body, grid=(S // ts,),
        in_specs=[row, row, pl.BlockSpec((ts, 1), lambda i: (i, 0)), vec],
        out_specs=[row, row, vec, vec],
        out_shape=[jax.ShapeDtypeStruct((S, D), F32), jax.ShapeDtypeStruct((S, D), BF16),
                   jax.ShapeDtypeStruct((1, D), F32), jax.ShapeDtypeStruct((1, D), F32)],
        name="ln_bwd", compiler_params=_cparams(("arbitrary",)),
    )(dy, xh, rstd, g)


def _merge_fwd(proj, gate_off, p_sb, p_ssm, p_mem):
    S, D = p_sb.shape
    ts = _tile(S, 256, SUBLANES)
    gb = gate_off // D

    def body(g0, g1, g2, a0, a1, a2, o_ref):
        o_ref[...] = (jax.nn.sigmoid(g0[...]) * a0[...] + jax.nn.sigmoid(g1[...]) * a1[...]
                      + jax.nn.sigmoid(g2[...]) * a2[...]).astype(o_ref.dtype)

    row = pl.BlockSpec((ts, D), lambda i: (i, 0))
    gates = [pl.BlockSpec((ts, D), functools.partial(lambda i, n: (i, gb + n), n=n)) for n in range(3)]
    return pl.pallas_call(
        body, grid=(S // ts,), in_specs=gates + [row, row, row], out_specs=row,
        out_shape=jax.ShapeDtypeStruct((S, D), BF16), name="merge_fwd",
        compiler_params=_cparams(("parallel",)),
    )(proj, proj, proj, p_sb, p_ssm, p_mem)


def _merge_bwd(dmerged, proj, gate_off, p_sb, p_ssm, p_mem):
    S, D = p_sb.shape
    ts = _tile(S, 256, SUBLANES)
    gb = gate_off // D

    def body(dm_ref, g0, g1, g2, a0, a1, a2, d0, d1, d2, l0, l1, l2):
        dm = dm_ref[...]
        for g_ref, a_ref, d_ref, l_ref in ((g0, a0, d0, l0), (g1, a1, d1, l1), (g2, a2, d2, l2)):
            s = jax.nn.sigmoid(g_ref[...])
            d_ref[...] = (dm * s).astype(d_ref.dtype)
            l_ref[...] = dm * a_ref[...] * s * (1.0 - s)

    row = pl.BlockSpec((ts, D), lambda i: (i, 0))
    gates = [pl.BlockSpec((ts, D), functools.partial(lambda i, n: (i, gb + n), n=n)) for n in range(3)]
    sd = jax.ShapeDtypeStruct((S, D), F32)
    return pl.pallas_call(
        body, grid=(S // ts,), in_specs=[row] + gates + [row, row, row], out_specs=[row] * 6,
        out_shape=[jax.ShapeDtypeStruct((S, D), BF16)] * 3 + [sd] * 3, name="merge_bwd", compiler_params=_cparams(("parallel",)),
    )(dmerged, proj, proj, proj, p_sb, p_ssm, p_mem)


def _glu_fwd(glu):
    S, W2 = glu.shape
    W = W2 // 2
    ts = _tile(S, 512, SUBLANES)

    def body(x_ref, o_ref):
        o_ref[...] = (x_ref[:, :W] * jax.nn.sigmoid(x_ref[:, W:])).astype(o_ref.dtype)

    return pl.pallas_call(
        body, grid=(S // ts,), in_specs=[pl.BlockSpec((ts, W2), lambda i: (i, 0))],
        out_specs=pl.BlockSpec((ts, W), lambda i: (i, 0)),
        out_shape=jax.ShapeDtypeStruct((S, W), BF16), name="glu_fwd",
        compiler_params=_cparams(("parallel",)),
    )(glu)


def _glu_bwd(dzz, glu):
    S, W2 = glu.shape
    W = W2 // 2
    ts = _tile(S, 512, SUBLANES)

    def body(d_ref, x_ref, o_ref):
        d = d_ref[...]
        a = x_ref[:, :W]
        s = jax.nn.sigmoid(x_ref[:, W:])
        o_ref[:, :W] = (d * s).astype(o_ref.dtype)
        o_ref[:, W:] = (d * a * s * (1.0 - s)).astype(o_ref.dtype)

    return pl.pallas_call(
        body, grid=(S // ts,),
        in_specs=[pl.BlockSpec((ts, W), lambda i: (i, 0)), pl.BlockSpec((ts, W2), lambda i: (i, 0))],
        out_specs=pl.BlockSpec((ts, W2), lambda i: (i, 0)),
        out_shape=jax.ShapeDtypeStruct((S, W2), BF16), name="glu_bwd",
        compiler_params=_cparams(("parallel",)),
    )(dzz, glu)


def _swiglu_fwd(gu):
    S, F2 = gu.shape
    Fh = F2 // 2
    ts = _tile(S, 128, SUBLANES)

    def body(x_ref, o_ref):
        fg = x_ref[:, :Fh]
        o_ref[...] = (fg * jax.nn.sigmoid(fg) * x_ref[:, Fh:]).astype(o_ref.dtype)

    return pl.pallas_call(
        body, grid=(S // ts,), in_specs=[pl.BlockSpec((ts, F2), lambda i: (i, 0))],
        out_specs=pl.BlockSpec((ts, Fh), lambda i: (i, 0)),
        out_shape=jax.ShapeDtypeStruct((S, Fh), BF16), name="swiglu_fwd",
        compiler_params=_cparams(("parallel",)),
    )(gu)


def _swiglu_bwd(dhid, gu):
    S, F2 = gu.shape
    Fh = F2 // 2
    ts = _tile(S, 128, SUBLANES)

    def body(d_ref, x_ref, o_ref):
        d = d_ref[...]
        fg = x_ref[:, :Fh]
        fu = x_ref[:, Fh:]
        s = jax.nn.sigmoid(fg)
        o_ref[:, :Fh] = (d * fu * s * (1.0 + fg * (1.0 - s))).astype(o_ref.dtype)
        o_ref[:, Fh:] = (d * fg * s).astype(o_ref.dtype)

    return pl.pallas_call(
        body, grid=(S // ts,),
        in_specs=[pl.BlockSpec((ts, Fh), lambda i: (i, 0)), pl.BlockSpec((ts, F2), lambda i: (i, 0))],
        out_specs=pl.BlockSpec((ts, F2), lambda i: (i, 0)),
        out_shape=jax.ShapeDtypeStruct((S, F2), BF16), name="swiglu_bwd",
        compiler_params=_cparams(("parallel",)),
    )(dhid, gu)


def _assemble_dproj(pieces):
    S = pieces[0].shape[0]
    widths = [p.shape[1] for p in pieces]
    total = sum(widths)
    ts = _tile(S, 128, SUBLANES)
    n = len(pieces)

    def body(*refs):
        o_ref, b_ref = refs[n], refs[n + 1]

        @pl.when(pl.program_id(0) == 0)
        def _():
            b_ref[...] = jnp.zeros_like(b_ref)

        off = 0
        for r, w in zip(refs[:n], widths):
            v = r[...].astype(F32)
            o_ref[:, off:off + w] = v.astype(o_ref.dtype)
            b_ref[:, off:off + w] += jnp.sum(v, axis=0, keepdims=True)
            off += w

    return pl.pallas_call(
        body, grid=(S // ts,),
        in_specs=[pl.BlockSpec((ts, w), lambda i: (i, 0)) for w in widths],
        out_specs=[pl.BlockSpec((ts, total), lambda i: (i, 0)), pl.BlockSpec((1, total), lambda i: (0, 0))],
        out_shape=[jax.ShapeDtypeStruct((S, total), BF16), jax.ShapeDtypeStruct((1, total), F32)],
        name="assemble_dproj", compiler_params=_cparams(("arbitrary",)),
    )(*pieces)


def _loss_head(y, target):
    S, D = y.shape
    ts = _tile(S, 256, SUBLANES)

    def body(y_ref, t_ref, dy_ref, l_ref):
        @pl.when(pl.program_id(0) == 0)
        def _():
            l_ref[...] = jnp.zeros_like(l_ref)

        e = y_ref[...] - t_ref[...]
        dy_ref[...] = e * (1.0 / D)
        part = jnp.sum(jnp.sum(e * e, axis=1, keepdims=True), axis=0, keepdims=True) * (0.5 / D)
        l_ref[...] += jnp.broadcast_to(part, l_ref.shape)

    row = pl.BlockSpec((ts, D), lambda i: (i, 0))
    return pl.pallas_call(
        body, grid=(S // ts,), in_specs=[row, row],
        out_specs=[row, pl.BlockSpec((1, LANES), lambda i: (0, 0))],
        out_shape=[jax.ShapeDtypeStruct((S, D), F32), jax.ShapeDtypeStruct((1, LANES), F32)],
        name="loss_head", compiler_params=_cparams(("arbitrary",)),
    )(y, target)


SB_TQ = 512
SB_TK = 512


def _sb_tile_terms(q, kb, scale, causal):
    z = _dot(q, kb, NT) * scale
    soft = jnp.log(1.0 + jnp.exp(-jnp.abs(z)))
    ls = jnp.minimum(z, 0.0) - soft
    l1m = jnp.minimum(-z, 0.0) - soft
    if causal is not None:
        l1m = jnp.where(causal, l1m, 0.0)
    return ls, l1m


def _sb_causal(qi, kj, TQ, TK):
    t_idx = qi * TQ + lax.broadcasted_iota(jnp.int32, (TQ, TK), 0)
    s_idx = kj * TK + lax.broadcasted_iota(jnp.int32, (TQ, TK), 1)
    return s_idx < t_idx


def _exchange_begin(heads, nq, start, middle=None):
    h, qi = pl.program_id(0), pl.program_id(1)

    @pl.when((h == 0) & (qi == 0))
    def _():
        start()

    if middle is not None:
        @pl.when((h == heads - 1) & (qi == nq - 1))
        def _():
            middle()


def _exchange_end(heads, nq, finish):
    @pl.when((pl.program_id(0) == heads - 1) & (pl.program_id(1) == nq - 1))
    def _():
        finish()


def _sb_fwd(proj, q_off, k_off, v_off, heads, ag=None):
    S = proj.shape[0]
    Dh = HEAD_DIM
    TQ = min(SB_TQ, S)
    TK = min(SB_TK, TQ)
    nq = S // TQ
    scale = Dh ** -0.5
    qb, kb0, vb0 = q_off // Dh, k_off // Dh, v_off // Dh
    n_ag = 0 if ag is None else len(ag)
    assert ag is None or heads >= 2

    def body(q_ref, k_ref, v_ref, *rest):
        o_ref, c_ref = rest[n_ag:n_ag + 2]
        if n_ag:
            ag_start, ag_middle, ag_finish = _ag_hooks(rest[n_ag + 2:2 * n_ag + 2], *rest[2 * n_ag + 2:])
            _exchange_begin(heads, nq, ag_start, ag_middle)
        qi = pl.program_id(1)
        q = q_ref[...].astype(BF16)
        upper = _tri2(TK, lambda j, s: j > s)
        nfull = (qi * TQ) // TK

        def block(kj, carry, masked):
            c, acc = carry
            off = pl.multiple_of(kj * TK, TK)
            kblk = k_ref[pl.ds(off, TK), :].astype(BF16)
            vblk = v_ref[pl.ds(off, TK), :].astype(BF16)
            causal = _sb_causal(qi, kj, TQ, TK) if masked else None
            ls, l1m = _sb_tile_terms(q, kblk, scale, causal)
            w = jnp.exp(ls + _dot_mask(l1m, upper) + c)
            if masked:
                w = jnp.where(causal, w, 0.0)
            acc = acc + _dot(w.astype(BF16), vblk)
            c = c + jnp.sum(l1m, axis=1, keepdims=True)
            return c, acc

        carry = (jnp.zeros((TQ, 1), F32), jnp.zeros((TQ, Dh), F32))
        for d in range(TQ // TK - 1, -1, -1):
            carry = block(nfull + d, carry, True)

        c, acc = lax.fori_loop(0, nfull, lambda jj, carry: block(nfull - 1 - jj, carry, False), carry)
        o_ref[...] = acc.astype(o_ref.dtype)
        c_ref[...] = c
        if n_ag:
            _exchange_end(heads, nq, ag_finish)

    ag = [] if ag is None else list(ag)
    res = pl.pallas_call(
        body, grid=(heads, nq),
        in_specs=[pl.BlockSpec((TQ, Dh), lambda h, i: (i, qb + h)),
                  pl.BlockSpec((S, Dh), lambda h, i: (0, kb0 + h)),
                  pl.BlockSpec((S, Dh), lambda h, i: (0, vb0 + h))] + [ANY] * n_ag,
        out_specs=[pl.BlockSpec((TQ, Dh), lambda h, i: (i, h)),
                   pl.BlockSpec((None, TQ, 1), lambda h, i: (h, i, 0))] + [ANY] * n_ag,
        out_shape=[jax.ShapeDtypeStruct((S, heads * Dh), BF16), jax.ShapeDtypeStruct((heads, S, 1), F32)]
        + [jax.ShapeDtypeStruct(b.shape, b.dtype) for b in ag],
        input_output_aliases={3 + i: 2 + i for i in range(n_ag)},
        scratch_shapes=[pltpu.SemaphoreType.DMA((n_ag * 6,)), pltpu.SemaphoreType.DMA((n_ag * 6,))] if n_ag else [],
        name="sb_fwd_ag" if n_ag else "sb_fwd",
        compiler_params=_cparams(("arbitrary", "arbitrary") if n_ag else ("parallel", "arbitrary")),
    )(proj, proj, proj, *ag)
    return res[0], res[1], list(res[2:])


def _sb_bwd(proj, q_off, k_off, v_off, heads, dout, ctot, rs=None):
    S = proj.shape[0]
    Dh = HEAD_DIM
    TQ = min(SB_TQ, S)
    TK = min(SB_TK, TQ)
    nq = S // TQ
    scale = Dh ** -0.5
    qb, kb0, vb0 = q_off // Dh, k_off // Dh, v_off // Dh
    n_rs = 0 if rs is None else len(rs)

    def body(q_ref, k_ref, v_ref, do_ref, c_ref, *rest):
        dq_ref, dk_ref, dv_ref = rest[n_rs:n_rs + 3]
        if n_rs:
            rs_start, rs_finish = _rs_chips_hooks(rest[:n_rs], rest[n_rs + 3:2 * n_rs + 3], *rest[2 * n_rs + 3:])
            _exchange_begin(heads, nq, rs_start)
        qi = pl.program_id(1)

        @pl.when(qi == 0)
        def _():
            dk_ref[...] = jnp.zeros_like(dk_ref)
            dv_ref[...] = jnp.zeros_like(dv_ref)

        q = q_ref[...].astype(BF16)
        do = do_ref[...].astype(BF16)
        ctot = c_ref[...]
        lower_incl = _tri2(TK, lambda j, s: j <= s)
        lower = _tri2(TK, lambda j, s: j < s)
        nfull = (qi * TQ) // TK

        def block(kj, carry, masked):
            cl, ce, dq = carry
            off = pl.multiple_of(kj * TK, TK)
            kblk = k_ref[pl.ds(off, TK), :].astype(BF16)
            vblk = v_ref[pl.ds(off, TK), :].astype(BF16)
            causal = _sb_causal(qi, kj, TQ, TK) if masked else None
            ls, l1m = _sb_tile_terms(q, kblk, scale, causal)
            w = jnp.exp(ls + (ctot - cl - _dot_mask(l1m, lower_incl)))
            if masked:
                w = jnp.where(causal, w, 0.0)
            e = w * _dot(do, vblk, NT)
            before = ce + _dot_mask(e, lower)
            beta = jnp.exp(ls)
            dz = (e * (1.0 - beta) - beta * before) * scale
            if masked:
                dz = jnp.where(causal, dz, 0.0)
            dzb = dz.astype(BF16)
            dq = dq + _dot(dzb, kblk)
            dk_ref[pl.ds(off, TK), :] += _dot(dzb, q, TN)
            dv_ref[pl.ds(off, TK), :] += _dot(w.astype(BF16), do, TN)
            cl = cl + jnp.sum(l1m, axis=1, keepdims=True)
            ce = ce + jnp.sum(e, axis=1, keepdims=True)
            return cl, ce, dq

        zero = jnp.zeros((TQ, 1), F32)
        carry = lax.fori_loop(0, nfull, lambda kj, carry: block(kj, carry, False),
                              (zero, zero, jnp.zeros((TQ, Dh), F32)))
        for d in range(TQ // TK):
            carry = block(nfull + d, carry, True)
        dq_ref[...] = carry[2]
        if n_rs:
            _exchange_end(heads, nq, rs_finish)

    rs = [] if rs is None else list(rs)
    blk = pl.BlockSpec((TQ, Dh), lambda h, i: (i, h))
    col = pl.BlockSpec((S, Dh), lambda h, i: (0, h))
    sd = jax.ShapeDtypeStruct((S, heads * Dh), F32)
    res = pl.pallas_call(
        body, grid=(heads, nq),
        in_specs=[pl.BlockSpec((TQ, Dh), lambda h, i: (i, qb + h)),
                  pl.BlockSpec((S, Dh), lambda h, i: (0, kb0 + h)),
                  pl.BlockSpec((S, Dh), lambda h, i: (0, vb0 + h)), blk,
                  pl.BlockSpec((None, TQ, 1), lambda h, i: (h, i, 0))] + [ANY] * n_rs,
        out_specs=[blk, col, col] + [ANY] * n_rs, out_shape=[sd, sd, sd] + _rs_chips_shapes(rs),
        scratch_shapes=[pltpu.SemaphoreType.DMA((n_rs * 3,)), pltpu.SemaphoreType.DMA((n_rs * 3,))] if n_rs else [],
        name="sb_bwd_rs" if n_rs else "sb_bwd",
        compiler_params=_cparams(("arbitrary", "arbitrary") if n_rs else ("parallel", "arbitrary")),
    )(proj, proj, proj, dout, ctot, *rs)
    return res[0], res[1], res[2], list(res[3:])


def _mem_probs(qh, kh, scale):
    s = _dot(qh, kh, NT) * scale
    m = jnp.max(s, axis=-1, keepdims=True)
    p = jnp.exp(s - m)
    return p / jnp.sum(p, axis=-1, keepdims=True)


def _mem_fwd(proj, q_off, width, kv):
    S = proj.shape[0]
    Dh = HEAD_DIM
    heads = width // Dh
    ts = _tile(S, 512, SUBLANES)
    scale = Dh ** -0.5
    M = kv.shape[0]

    def body(q_ref, kv_ref, o_ref):
        for h in range(heads):
            qh = q_ref[:, h * Dh:(h + 1) * Dh].astype(BF16)
            kh = kv_ref[:, h * Dh:(h + 1) * Dh].astype(BF16)
            vh = kv_ref[:, width + h * Dh:width + (h + 1) * Dh].astype(BF16)
            p = _mem_probs(qh, kh, scale)
            o_ref[:, h * Dh:(h + 1) * Dh] = _dot(p.astype(BF16), vh).astype(o_ref.dtype)

    return pl.pallas_call(
        body, grid=(S // ts,),
        in_specs=[pl.BlockSpec((ts, width), lambda i: (i, q_off // width)),
                  pl.BlockSpec((M, 2 * width), lambda i: (0, 0))],
        out_specs=pl.BlockSpec((ts, width), lambda i: (i, 0)),
        out_shape=jax.ShapeDtypeStruct((S, width), BF16), name="mem_fwd",
        compiler_params=_cparams(("parallel",)),
    )(proj, kv)


def _mem_bwd(proj, q_off, width, kv, dmm):
    S = proj.shape[0]
    Dh = HEAD_DIM
    heads = width // Dh
    ts = _tile(S, 512, SUBLANES)
    scale = Dh ** -0.5
    M = kv.shape[0]

    def body(q_ref, kv_ref, d_ref, dq_ref, dkv_ref):
        @pl.when(pl.program_id(0) == 0)
        def _():
            dkv_ref[...] = jnp.zeros_like(dkv_ref)

        for h in range(heads):
            qh = q_ref[:, h * Dh:(h + 1) * Dh].astype(BF16)
            kh = kv_ref[:, h * Dh:(h + 1) * Dh].astype(BF16)
            vh = kv_ref[:, width + h * Dh:width + (h + 1) * Dh].astype(BF16)
            dh = d_ref[:, h * Dh:(h + 1) * Dh].astype(BF16)
            p = _mem_probs(qh, kh, scale)
            dp = _dot(dh, vh, NT)
            ds = p * (dp - jnp.sum(dp * p, axis=-1, keepdims=True)) * scale
            dsb = ds.astype(BF16)
            dq_ref[:, h * Dh:(h + 1) * Dh] = _dot(dsb, kh)
            dkv_ref[:, h * Dh:(h + 1) * Dh] += _dot(dsb, qh, TN)
            dkv_ref[:, width + h * Dh:width + (h + 1) * Dh] += _dot(p.astype(BF16), dh, TN)

    row = pl.BlockSpec((ts, width), lambda i: (i, 0))
    full = pl.BlockSpec((M, 2 * width), lambda i: (0, 0))
    return pl.pallas_call(
        body, grid=(S // ts,),
        in_specs=[pl.BlockSpec((ts, width), lambda i: (i, q_off // width)), full, row],
        out_specs=[row, full],
        out_shape=[jax.ShapeDtypeStruct((S, width), F32), jax.ShapeDtypeStruct((M, 2 * width), F32)],
        name="mem_bwd", compiler_params=_cparams(("arbitrary",)),
    )(proj, kv, dmm)


def _disc_math(lre, lim, logdt, bre_t, bim_t):
    dt = jnp.exp(logdt)
    mag = jnp.exp(lre * dt)
    ang = lim * dt
    a = mag * jnp.cos(ang)
    b = mag * jnp.sin(ang)
    den = lre * lre + lim * lim
    nr = a - 1.0
    fre = (nr * lre + b * lim) / den
    fim = (b * lre - nr * lim) / den
    bbre = fre * bre_t - fim * bim_t
    bbim = fre * bim_t + fim * bre_t
    return a, b, bbre, bbim


def _s5_disc(lre, lim, logdt, bre_t, bim_t):
    G, _, P = lre.shape
    C = bre_t.shape[1]

    def body(lre_ref, lim_ref, dt_ref, br_ref, bi_ref, a_ref, b_ref, bbre_ref, bbim_ref):
        a, b, bbre, bbim = _disc_math(lre_ref[...], lim_ref[...], dt_ref[...], br_ref[...], bi_ref[...])
        a_ref[...] = a
        b_ref[...] = b
        bbre_ref[...] = bbre
        bbim_ref[...] = bbim

    gp = jax.ShapeDtypeStruct((G, 1, P), F32)
    gcp = jax.ShapeDtypeStruct((G, C, P), F32)
    return pl.pallas_call(
        body, in_specs=[VMEM] * 5, out_specs=[VMEM] * 4, out_shape=[gp, gp, gcp, gcp], name="s5_disc",
    )(lre, lim, logdt, bre_t, bim_t)


def _s5_disc_bwd(lre, lim, logdt, bre_t, bim_t, da, db, dbbre, dbbim):
    G, _, P = lre.shape
    C = bre_t.shape[1]

    def body(lre_ref, lim_ref, dt_ref, br_ref, bi_ref, da_ref, db_ref, dbr_ref, dbi_ref,
             o_lre, o_lim, o_dt, o_br, o_bi):
        _, vjp = jax.vjp(_disc_math, lre_ref[...], lim_ref[...], dt_ref[...], br_ref[...], bi_ref[...])
        g = vjp((da_ref[...], db_ref[...], dbr_ref[...], dbi_ref[...]))
        o_lre[...] = g[0]
        o_lim[...] = g[1]
        o_dt[...] = g[2]
        o_br[...] = g[3]
        o_bi[...] = g[4]

    gp = jax.ShapeDtypeStruct((G, 1, P), F32)
    gcp = jax.ShapeDtypeStruct((G, C, P), F32)
    return pl.pallas_call(
        body, in_specs=[VMEM] * 9, out_specs=[VMEM] * 5,
        out_shape=[gp, gp, jax.ShapeDtypeStruct((G, 1, 1), F32), gcp, gcp], name="s5_disc_bwd",
    )(lre, lim, logdt, bre_t, bim_t, da, db, dbbre, dbbim)


S5_CHUNK = 256


def _load_once(pairs):
    @pl.when(pl.program_id(0) == 0)
    def _():
        for src, dst in pairs:
            pltpu.sync_copy(src, dst)


def _s5_fwd(proj, u_off, width, a_row, b_row, bmre, bmim, cmre, cmimn, d_row):
    S = proj.shape[0]
    GP = a_row.shape[1]
    T = min(S5_CHUNK, S)

    def body(u_ref, a_ref, b_ref, d_ref, bre_hbm, bim_hbm, cre_hbm, cim_hbm,
             uo_ref, y_ref, gy_ref, hre_ref, him_ref, st_ref, bure_s, buim_s, bre_ref, bim_ref, cre_ref, cim_ref):
        @pl.when(pl.program_id(0) == 0)
        def _():
            st_ref[...] = jnp.zeros_like(st_ref)

        _load_once([(bre_hbm, bre_ref), (bim_hbm, bim_ref), (cre_hbm, cre_ref), (cim_hbm, cim_ref)])
        u = u_ref[...]
        uo_ref[...] = u
        for ws, gs in _bd_blocks(width, GP):
            bure_s[:, gs] = _dot_a2(u[:, ws], bre_ref[ws, gs])
            buim_s[:, gs] = _dot_a2(u[:, ws], bim_ref[ws, gs])
        a = a_ref[...]
        b = b_ref[...]

        def step(ii, carry):
            hre, him = carry
            base = pl.multiple_of(ii * SUBLANES, SUBLANES)
            br = bure_s[pl.ds(base, SUBLANES), :]
            bi = buim_s[pl.ds(base, SUBLANES), :]
            rows_re, rows_im = [], []
            for j in range(SUBLANES):
                nre = a * hre - b * him + br[j:j + 1, :]
                nim = a * him + b * hre + bi[j:j + 1, :]
                hre, him = nre, nim
                rows_re.append(nre)
                rows_im.append(nim)
            hre_ref[pl.ds(base, SUBLANES), :] = jnp.concatenate(rows_re, axis=0)
            him_ref[pl.ds(base, SUBLANES), :] = jnp.concatenate(rows_im, axis=0)
            return hre, him

        hre, him = lax.fori_loop(0, T // SUBLANES, step, (st_ref[0:1, :], st_ref[1:2, :]))
        st_ref[0:1, :] = hre
        st_ref[1:2, :] = him
        for ws, gs in _bd_blocks(width, GP):
            y = (_dot_a2(hre_ref[:, gs], cre_ref[gs, ws]) + _dot_a2(him_ref[:, gs], cim_ref[gs, ws])
                 + d_ref[:, ws] * u[:, ws])
            y_ref[:, ws] = y
            gy_ref[:, ws] = jax.nn.gelu(y).astype(gy_ref.dtype)

    c0 = lambda i: (0, 0)
    urow = pl.BlockSpec((T, width), lambda i: (i, u_off // width))
    row = pl.BlockSpec((T, width), lambda i: (i, 0))
    hrow = pl.BlockSpec((T, GP), lambda i: (i, 0))
    sw = jax.ShapeDtypeStruct((S, width), F32)
    sg = jax.ShapeDtypeStruct((S, GP), F32)
    return pl.pallas_call(
        body, grid=(S // T,),
        in_specs=[urow, pl.BlockSpec((1, GP), c0), pl.BlockSpec((1, GP), c0), pl.BlockSpec((1, width), c0),
                  ANY, ANY, ANY, ANY],
        out_specs=[row, row, row, hrow, hrow],
        out_shape=[sw, sw, jax.ShapeDtypeStruct((S, width), BF16), sg, sg],
        scratch_shapes=[pltpu.VMEM((SUBLANES, GP), F32), pltpu.VMEM((T, GP), F32), pltpu.VMEM((T, GP), F32),
                        pltpu.VMEM((width, GP), F32), pltpu.VMEM((width, GP), F32),
                        pltpu.VMEM((GP, width), F32), pltpu.VMEM((GP, width), F32)],
        name="s5_fwd", compiler_params=_cparams(("arbitrary",)),
    )(proj, a_row, b_row, d_row, bmre, bmim, cmre, cmimn)


def _s5_bwd(u, dgy, y, hre, him, a_row, b_row, bmre, bmim, cmre, cmimn, d_row):
    S, width = u.shape
    GP = a_row.shape[1]
    T = min(S5_CHUNK, S)
    nchunk = S // T

    def body(u_ref, dgy_ref, y_ref, hre_ref, him_ref, a_ref, b_ref, d_ref, bre_hbm, bim_hbm, cre_hbm, cim_hbm,
             du_ref, dy_ref, gre_s, gim_s, dd_ref, da_ref, db_ref,
             st_ref, bre_ref, bim_ref, cre_ref, cim_ref):
        @pl.when(pl.program_id(0) == 0)
        def _():
            st_ref[...] = jnp.zeros_like(st_ref)
            for r in (dd_ref, da_ref, db_ref):
                r[...] = jnp.zeros_like(r)

        _load_once([(bre_hbm, bre_ref), (bim_hbm, bim_ref), (cre_hbm, cre_ref), (cim_hbm, cim_ref)])
        u = u_ref[...]
        dy = dgy_ref[...] * _gelu_grad(y_ref[...])
        dy_ref[...] = dy
        for ws, gs in _bd_blocks(width, GP):
            gre_s[:, gs] = _dot_a2(dy[:, ws], cre_ref[gs, ws], NT)
            gim_s[:, gs] = _dot_a2(dy[:, ws], cim_ref[gs, ws], NT)
        a = a_ref[...]
        b = b_ref[...]
        g_in_re = st_ref[0:1, :]
        g_in_im = st_ref[1:2, :]

        def step(ii, carry):
            gre, gim = carry
            base = pl.multiple_of((T // SUBLANES - 1 - ii) * SUBLANES, SUBLANES)
            dr = gre_s[pl.ds(base, SUBLANES), :]
            di = gim_s[pl.ds(base, SUBLANES), :]
            rows_re = [None] * SUBLANES
            rows_im = [None] * SUBLANES
            for j in range(SUBLANES - 1, -1, -1):
                nre = dr[j:j + 1, :] + a * gre + b * gim
                nim = di[j:j + 1, :] - b * gre + a * gim
                gre, gim = nre, nim
                rows_re[j] = nre
                rows_im[j] = nim
            gre_s[pl.ds(base, SUBLANES), :] = jnp.concatenate(rows_re, axis=0)
            gim_s[pl.ds(base, SUBLANES), :] = jnp.concatenate(rows_im, axis=0)
            return gre, gim

        gre, gim = lax.fori_loop(0, T // SUBLANES, step, (g_in_re, g_in_im))
        st_ref[0:1, :] = gre
        st_ref[1:2, :] = gim
        last = lax.broadcasted_iota(jnp.int32, (T, 1), 0) == T - 1
        nxt_re = jnp.where(last, g_in_re, pltpu.roll(gre_s[...], T - 1, 0))
        nxt_im = jnp.where(last, g_in_im, pltpu.roll(gim_s[...], T - 1, 0))
        hre = hre_ref[...]
        him = him_ref[...]
        da_ref[...] += jnp.sum(nxt_re * hre + nxt_im * him, axis=0, keepdims=True)
        db_ref[...] += jnp.sum(nxt_im * hre - nxt_re * him, axis=0, keepdims=True)
        for ws, gs in _bd_blocks(width, GP):
            du_ref[:, ws] = (_dot_a2(gre_s[:, gs], bre_ref[ws, gs], NT) + _dot_a2(gim_s[:, gs], bim_ref[ws, gs], NT)
                             + d_ref[:, ws] * dy[:, ws])
        dd_ref[...] += jnp.sum(dy * u, axis=0, keepdims=True)

    c0 = lambda i: (0, 0)
    rev = lambda i: (nchunk - 1 - i, 0)
    row = pl.BlockSpec((T, width), rev)
    hrow = pl.BlockSpec((T, GP), rev)
    v_gp = pl.BlockSpec((1, GP), c0)
    v_w = pl.BlockSpec((1, width), c0)
    sw = jax.ShapeDtypeStruct((S, width), F32)
    sg = jax.ShapeDtypeStruct((S, GP), F32)
    return pl.pallas_call(
        body, grid=(nchunk,),
        in_specs=[row, row, row, hrow, hrow, v_gp, v_gp, v_w, ANY, ANY, ANY, ANY],
        out_specs=[row, row, hrow, hrow, v_w, v_gp, v_gp],
        out_shape=[sw, sw, sg, sg, jax.ShapeDtypeStruct((1, width), F32),
                   jax.ShapeDtypeStruct((1, GP), F32), jax.ShapeDtypeStruct((1, GP), F32)],
        scratch_shapes=[pltpu.VMEM((SUBLANES, GP), F32),
                        pltpu.VMEM((width, GP), F32), pltpu.VMEM((width, GP), F32),
                        pltpu.VMEM((GP, width), F32), pltpu.VMEM((GP, width), F32)],
        name="s5_bwd", compiler_params=_cparams(("arbitrary",)),
    )(u, dgy, y, hre, him, a_row, b_row, d_row, bmre, bmim, cmre, cmimn)


def _block_diag(x):
    G, A, B = x.shape
    eye = jnp.eye(G, dtype=x.dtype)
    return (eye[:, None, :, None] * x[:, :, None, :]).reshape(G * A, G * B)


def _block_diag_take(m, G):
    A, B = m.shape[0] // G, m.shape[1] // G
    return jnp.einsum("gagb->gab", m.reshape(G, A, G, B))


def _adamw(w, g, m, v, with_grad=False):
    shape = w.shape
    C = shape[-1]
    w2, g2, m2, v2 = (t.reshape(-1, C) for t in (w, g, m, v))
    R = w2.shape[0]
    rb = _tile(R, max(SUBLANES, (1 << 19) // C), SUBLANES)
    c1 = 1.0 - ADAM_B1 ** ADAM_STEP
    c2 = 1.0 - ADAM_B2 ** ADAM_STEP
    n_out = 4 if with_grad else 3

    def body(w_ref, g_ref, m_ref, v_ref, d_ref, nm_ref, nv_ref, *g_out):
        gv = g_ref[...]
        nm = ADAM_B1 * m_ref[...] + (1.0 - ADAM_B1) * gv
        nv = ADAM_B2 * v_ref[...] + (1.0 - ADAM_B2) * (gv * gv)
        d_ref[...] = -ADAM_LR * ((nm / c1) / (jnp.sqrt(nv / c2) + ADAM_EPS) + ADAM_WD * w_ref[...])
        nm_ref[...] = nm
        nv_ref[...] = nv
        if with_grad:
            g_out[0][...] = gv

    blk = pl.BlockSpec((rb, C), lambda i: (i, 0))
    sd = jax.ShapeDtypeStruct((R, C), F32)
    outs = pl.pallas_call(
        body, grid=(R // rb,), in_specs=[blk] * 4, out_specs=[blk] * n_out, out_shape=[sd] * n_out,
        name="adamw", compiler_params=_cparams(("parallel",)),
    )(w2, g2, m2, v2)
    return tuple(o.reshape(shape) for o in outs)


def _coords():
    x, y, c = lax.axis_index("x"), lax.axis_index("y"), lax.axis_index("c")
    return x, y, c


def _place_shard(w, l, place):
    _, R, C = w.shape
    rb = _tile(R, max(16, (1 << 19) // C), 16)

    def body(place_ref, w_ref, o_ref):
        o_ref[...] = w_ref[...].astype(BF16)

    grid_spec = pltpu.PrefetchScalarGridSpec(
        num_scalar_prefetch=1, grid=(R // rb,),
        in_specs=[pl.BlockSpec((None, rb, C), lambda i, p: (l, i, 0))],
        out_specs=pl.BlockSpec((None, rb, C), lambda i, p: (p[1], i, 0)))
    return pl.pallas_call(
        body, grid_spec=grid_spec, out_shape=jax.ShapeDtypeStruct((N_CHIPS, R, C), BF16),
        name="place_shard", compiler_params=_cparams(("arbitrary",)),
    )(place, w)


def _ag_hooks(outs, send_sems, recv_sems):
    n = len(outs)

    def rcopy(a, k, block, to):
        cx, cy, cc = block
        hr = outs[a].shape[1] // 2
        blk = outs[a].at[2 * cx + cy, pl.ds(cc * hr, hr)]
        return pltpu.make_async_remote_copy(
            src_ref=blk, dst_ref=blk, send_sem=send_sems.at[a * 6 + k], recv_sem=recv_sems.at[a * 6 + k],
            device_id=to, device_id_type=MESH)

    def places():
        x, y, c = _coords()
        return (x, y, c), (x, y, 1 - c), [(1 - x, y), (x, 1 - y), (1 - x, 1 - y)]

    def start():
        me, _, others = places()
        for a in range(n):
            for j, ch in enumerate(others):
                rcopy(a, j, me, (*ch, me[2])).start()

    def middle():
        me, sibling, others = places()
        for a in range(n):
            for j, ch in enumerate(others):
                rcopy(a, j, (*ch, me[2]), me).wait_recv()
                rcopy(a, 3 + j, (*ch, me[2]), sibling).start()

    def finish():
        me, sibling, others = places()
        for a in range(n):
            for j, ch in enumerate(others):
                rcopy(a, 3 + j, (*ch, sibling[2]), me).wait_recv()
        for a in range(n):
            for j, ch in enumerate(others):
                rcopy(a, j, me, (*ch, me[2])).wait_send()
                rcopy(a, 3 + j, (*ch, me[2]), sibling).wait_send()

    return start, middle, finish


def _ag_weights(bufs):
    n = len(bufs)

    def body(*refs):
        start, middle, finish = _ag_hooks(refs[n:2 * n], *refs[2 * n:])
        start()
        middle()
        finish()

    return pl.pallas_call(
        body, out_shape=[jax.ShapeDtypeStruct(b.shape, b.dtype) for b in bufs],
        in_specs=[ANY] * n, out_specs=[ANY] * n, input_output_aliases={i: i for i in range(n)},
        scratch_shapes=[pltpu.SemaphoreType.DMA((n * 6,)), pltpu.SemaphoreType.DMA((n * 6,))],
        name="ag_weights",
    )(*bufs)


def _rs_pair_hooks(gs, outs, send_sems, recv_sems):
    def copies():
        x, y, c = _coords()
        cps = []
        for i in range(len(gs)):
            hr = gs[i].shape[1] // 2
            cps.append(pltpu.make_async_remote_copy(
                src_ref=gs[i].at[:, pl.ds((1 - c) * hr, hr)], dst_ref=outs[i],
                send_sem=send_sems.at[i], recv_sem=recv_sems.at[i],
                device_id=(x, y, 1 - c), device_id_type=MESH))
        return cps

    def start():
        for cp in copies():
            cp.start()

    def finish():
        for cp in copies():
            cp.wait()

    return start, None, finish


def _rs_pair_comm(grads):
    return dict(ins=list(grads), alias=False, n_sems=len(grads), hooks=_rs_pair_hooks,
                out_shapes=[jax.ShapeDtypeStruct((N_CHIPS, g.shape[1] // 2, g.shape[2]), F32) for g in grads])


def _ag_comm(bufs):
    return dict(ins=list(bufs), alias=True, n_sems=6 * len(bufs),
                hooks=lambda ins, outs, send_sems, recv_sems: _ag_hooks(outs, send_sems, recv_sems),
                out_shapes=[jax.ShapeDtypeStruct(b.shape, b.dtype) for b in bufs])


def _rs_pair(grads):
    n = len(grads)
    comm = _rs_pair_comm(grads)

    def body(*refs):
        start, _, finish = _rs_pair_hooks(refs[:n], refs[n:2 * n], *refs[2 * n:])
        start()
        finish()

    return pl.pallas_call(
        body, out_shape=comm["out_shapes"], in_specs=[ANY] * n, out_specs=[ANY] * n,
        scratch_shapes=[pltpu.SemaphoreType.DMA((n,)), pltpu.SemaphoreType.DMA((n,))],
        name="rs_pair",
    )(*grads)


def _pair_add(g, r, place):
    _, R, C = g.shape
    hr = R // 2
    rb = _tile(hr, max(16, (1 << 19) // C), 16)
    nb = hr // rb

    def body(place_ref, g_ref, r_ref, p16_ref, own_ref):
        s = g_ref[...] + r_ref[...]
        p16_ref[...] = s.astype(BF16)

        @pl.when(pl.program_id(1) == place_ref[1])
        def _():
            own_ref[...] = s

    grid_spec = pltpu.PrefetchScalarGridSpec(
        num_scalar_prefetch=1, grid=(nb, N_CHIPS),
        in_specs=[pl.BlockSpec((None, rb, C), lambda i, k, p: (k, p[0] * nb + i, 0)),
                  pl.BlockSpec((None, rb, C), lambda i, k, p: (k, i, 0))],
        out_specs=[pl.BlockSpec((None, rb, C), lambda i, k, p: (k, i, 0)),
                   pl.BlockSpec((rb, C), lambda i, k, p: (i, 0))])
    return pl.pallas_call(
        body, grid_spec=grid_spec,
        out_shape=[jax.ShapeDtypeStruct((N_CHIPS, hr, C), BF16), jax.ShapeDtypeStruct((hr, C), F32)],
        name="pair_add", compiler_params=_cparams(("arbitrary", "arbitrary")),
    )(place, g, r)


def _rs_chips_hooks(ps, outs, send_sems, recv_sems):
    def copies():
        x, y, c = _coords()
        cps = []
        for a in range(len(ps)):
            for r in (1, 2, 3):
                kx = 1 - x if (r >> 1) else x
                ky = 1 - y if (r & 1) else y
                cps.append(pltpu.make_async_remote_copy(
                    src_ref=ps[a].at[2 * kx + ky], dst_ref=outs[a].at[r - 1],
                    send_sem=send_sems.at[a * 3 + r - 1], recv_sem=recv_sems.at[a * 3 + r - 1],
                    device_id=(kx, ky, c), device_id_type=MESH))
        return cps

    def start():
        for cp in copies():
            cp.start()

    def finish():
        for cp in copies():
            cp.wait()

    return start, finish


def _rs_chips_shapes(p16):
    return [jax.ShapeDtypeStruct((3,) + p.shape[1:], BF16) for p in p16]


def _rs_chips_comm(p16):
    def hooks(ins, outs, send_sems, recv_sems):
        start, finish = _rs_chips_hooks(ins, outs, send_sems, recv_sems)
        return start, None, finish

    return dict(ins=list(p16), alias=False, n_sems=3 * len(p16), hooks=hooks, out_shapes=_rs_chips_shapes(p16))


def _rs_chips(p16):
    n = len(p16)

    def body(*refs):
        start, finish = _rs_chips_hooks(refs[:n], refs[n:2 * n], *refs[2 * n:])
        start()
        finish()

    return pl.pallas_call(
        body, out_shape=_rs_chips_shapes(p16), in_specs=[ANY] * n, out_specs=[ANY] * n,
        scratch_shapes=[pltpu.SemaphoreType.DMA((n * 3,)), pltpu.SemaphoreType.DMA((n * 3,))],
        name="rs_chips",
    )(*p16)


def _chip_sum(own, recv, full, l, place):
    hr, C = own.shape
    rb = _tile(hr, max(16, (1 << 19) // C), 16)
    nb = hr // rb

    def body(place_ref, o_ref, r_ref, full_ref, s_ref):
        s = o_ref[...] + r_ref[0].astype(F32)
        s = s + r_ref[1].astype(F32)
        s_ref[...] = s + r_ref[2].astype(F32)

    grid_spec = pltpu.PrefetchScalarGridSpec(
        num_scalar_prefetch=1, grid=(nb,),
        in_specs=[pl.BlockSpec((rb, C), lambda i, p: (i, 0)), pl.BlockSpec((3, rb, C), lambda i, p: (0, i, 0)), ANY],
        out_specs=pl.BlockSpec((None, rb, C), lambda i, p: (l, p[0] * nb + i, 0)))
    return pl.pallas_call(
        body, grid_spec=grid_spec, out_shape=jax.ShapeDtypeStruct(full.shape, F32),
        input_output_aliases={3: 0}, name="chip_sum", compiler_params=_cparams(("arbitrary",)),
    )(place, own, recv, full)


def _share_pair(fulls):
    n = len(fulls)

    def body(*refs):
        outs = refs[n:2 * n]
        send_sems, recv_sems = refs[2 * n:]
        x, y, c = _coords()
        copies = []
        for a in range(n):
            hr = outs[a].shape[1] // 2
            mine = outs[a].at[:, pl.ds(c * hr, hr)]
            cp = pltpu.make_async_remote_copy(
                src_ref=mine, dst_ref=mine, send_sem=send_sems.at[a], recv_sem=recv_sems.at[a],
                device_id=(x, y, 1 - c), device_id_type=MESH)
            cp.start()
            copies.append(cp)
        for cp in copies:
            cp.wait_recv()
        for cp in copies:
            cp.wait_send()

    return pl.pallas_call(
        body, out_shape=[jax.ShapeDtypeStruct(f.shape, f.dtype) for f in fulls],
        in_specs=[ANY] * n, out_specs=[ANY] * n, input_output_aliases={i: i for i in range(n)},
        scratch_shapes=[pltpu.SemaphoreType.DMA((n,)), pltpu.SemaphoreType.DMA((n,))],
        name="share_pair",
    )(*fulls)


def _small_allreduce(packed):
    m_per, ncol = packed.shape

    def body(x_ref, out_ref, tot_ref, send_sems, recv_sems, local_sem):
        x, y, c = _coords()
        me, sibling = (x, y, c), (x, y, 1 - c)
        chips = [(1 - x, y), (x, 1 - y), (1 - x, 1 - y)]

        def rows(px, py, pc):
            return out_ref.at[pl.ds((4 * px + 2 * py + pc) * m_per, m_per), :]

        def copy(k, block, to, src=None):
            return pltpu.make_async_remote_copy(
                src_ref=rows(*block) if src is None else src, dst_ref=rows(*block),
                send_sem=send_sems.at[k], recv_sem=recv_sems.at[k], device_id=to, device_id_type=MESH)

        mine = pltpu.make_async_copy(x_ref, rows(*me), local_sem)
        mine.start()
        first = [copy(0, me, sibling, src=x_ref)]
        first += [copy(1 + j, me, (*chip, c), src=x_ref) for j, chip in enumerate(chips)]
        for cp in first:
            cp.start()
        passed = [copy(4 + j, (*chip, c), sibling) for j, chip in enumerate(chips)]
        for j, chip in enumerate(chips):
            copy(1 + j, (*chip, c), me).wait_recv()
            passed[j].start()
        copy(0, sibling, me).wait_recv()
        for j, chip in enumerate(chips):
            copy(4 + j, (*chip, 1 - c), me).wait_recv()
        for cp in first + passed:
            cp.wait_send()
        mine.wait()
        tot = out_ref[pl.ds(0, m_per), :]
        for d in range(1, N_DEV):
            tot = tot + out_ref[pl.ds(d * m_per, m_per), :]
        tot_ref[...] = tot

    _, tot = pl.pallas_call(
        body,
        out_shape=[jax.ShapeDtypeStruct((N_DEV * m_per, ncol), F32), jax.ShapeDtypeStruct((m_per, ncol), F32)],
        in_specs=[VMEM], out_specs=[VMEM, VMEM],
        scratch_shapes=[pltpu.SemaphoreType.DMA((7,)), pltpu.SemaphoreType.DMA((7,)), pltpu.SemaphoreType.DMA],
        name="small_allreduce",
        compiler_params=pltpu.CompilerParams(vmem_limit_bytes=VMEM_LIMIT_MB * 1024 * 1024),
    )(packed)
    return tot


BIG = ["w_in", "sb_w_out", "ssm_w_glu", "ssm_w_out", "mem_w_kv", "mem_w_out", "w_o", "ffn_w_gate_up", "ffn_w_down"]
KIND = {"w_in": "col", "sb_w_out": "col", "ssm_w_glu": "col", "ssm_w_out": "col", "mem_w_kv": "row",
        "mem_w_out": "col", "w_o": "row", "ffn_w_gate_up": "col", "ffn_w_down": "row"}
AG_EARLY = ["w_in", "sb_w_out", "ssm_w_glu", "ssm_w_out", "mem_w_kv", "mem_w_out", "w_o"]
FFN_WEIGHTS = ["ffn_w_down", "ffn_w_gate_up"]
SMALL = ["b_in", "ssm_lambda_re", "ssm_lambda_im", "ssm_log_dt", "ssm_b_re", "ssm_b_im", "ssm_c_re", "ssm_c_im",
         "ssm_d", "ln1_g", "ln1_b", "ln2_g", "ln2_b"]
WEIGHTS = ["w_in", "b_in", "sb_w_out", "ssm_lambda_re", "ssm_lambda_im", "ssm_log_dt", "ssm_b_re", "ssm_b_im",
           "ssm_c_re", "ssm_c_im", "ssm_d", "ssm_w_glu", "ssm_w_out", "mem_w_kv", "mem_w_out", "w_o", "ln1_g",
           "ln1_b", "ffn_w_gate_up", "ffn_w_down", "ln2_g", "ln2_b"]


def _pack(arrs):
    flat = jnp.concatenate([a.reshape(-1).astype(F32) for a in arrs])
    n = flat.shape[0]
    rows = -(-n // LANES)
    rows = -(-rows // SUBLANES) * SUBLANES
    return jnp.pad(flat, (0, rows * LANES - n)).reshape(rows, LANES)


def _unpack(packed, like):
    flat = packed.reshape(-1)
    out, off = [], 0
    for a in like:
        out.append(flat[off:off + a.size].reshape(a.shape))
        off += a.size
    return out


def _step(x, mem, target, W, M1, V1):
    S, D = x.shape[1], x.shape[2]
    L = W["w_in"].shape[0]
    x0 = x.reshape(S, D)
    mem2 = mem.reshape(mem.shape[1], D)
    tgt = target.reshape(S, D)
    alpha = (2 * L) ** 0.25
    sbw = W["sb_w_out"].shape[1]
    ssw = W["ssm_d"].shape[1]
    mw = W["mem_w_out"].shape[1]
    heads = sbw // HEAD_DIM
    G, P = W["ssm_lambda_re"].shape[1], W["ssm_lambda_re"].shape[2]
    q_off, k_off, v_off = 0, sbw, 2 * sbw
    u_off = 3 * sbw
    qm_off = u_off + ssw
    gate_off = qm_off + mw

    x_i, y_i, c_i = _coords()
    place = jnp.stack([c_i, 2 * x_i + y_i]).astype(jnp.int32)
    placed = [[_place_shard(W[n], l, place) for n in BIG] for l in range(L)]
    Wg = {n: [None] * L for n in BIG}
    placed = [dict(zip(BIG, row)) for row in placed]
    for n, buf in zip(AG_EARLY, _ag_weights([placed[0][n] for n in AG_EARLY])):
        Wg[n][0] = buf

    def hosted(plan):
        return _ag_comm([placed[ll][n] for n, ll in plan]) if plan else None

    def keep(plan, bufs):
        for (n, ll), buf in zip(plan, bufs):
            Wg[n][ll] = buf

    saved = []
    xl = x0
    xlb = x0.astype(BF16)
    for l in range(L):
        sv = {"x": xlb}
        more = l + 1 < L
        plan_proj = [("ffn_w_gate_up", 0)] if l == 0 else []
        plan_sb = ([("ffn_w_down", 0)] if l == 0 else []) + ([(n, l + 1) for n in AG_EARLY] if more else [])
        plan_gate_up = [("ffn_w_gate_up", l + 1)] if more else []
        plan_down = [("ffn_w_down", l + 1)] if more else []

        proj = _mm_nn(xlb, Wg["w_in"][l], "col", bias=W["b_in"][l][None, :], comm=hosted(plan_proj),
                      name="mm_proj_ag" if plan_proj else "mm_proj")
        if plan_proj:
            proj, got = proj
            keep(plan_proj, got)
        sb, sb_ctot, got = _sb_fwd(proj, q_off, k_off, v_off, heads,
                                   ag=[placed[ll][n] for n, ll in plan_sb] if plan_sb else None)
        keep(plan_sb, got)
        p_sb = _mm_nn(sb, Wg["sb_w_out"][l], "col", name="mm_sb_out")

        bre_t = W["ssm_b_re"][l].transpose(0, 2, 1)
        bim_t = W["ssm_b_im"][l].transpose(0, 2, 1)
        logdt = W["ssm_log_dt"][l][:, None, None]
        lre3 = W["ssm_lambda_re"][l][:, None, :]
        lim3 = W["ssm_lambda_im"][l][:, None, :]
        a_gp, b_gp, bbre, bbim = _s5_disc(lre3, lim3, logdt, bre_t, bim_t)
        a_row, b_row = a_gp.reshape(1, G * P), b_gp.reshape(1, G * P)
        bmre, bmim = _block_diag(bbre), _block_diag(bbim)
        cmre = _block_diag(W["ssm_c_re"][l].transpose(0, 2, 1))
        cmimn = _block_diag(-W["ssm_c_im"][l].transpose(0, 2, 1))
        d_row = W["ssm_d"][l][None, :]
        u_ssm, y, gy, hre, him = _s5_fwd(proj, u_off, ssw, a_row, b_row, bmre, bmim, cmre, cmimn, d_row)
        glu = _mm_nn(gy, Wg["ssm_w_glu"][l], "col", name="mm_glu")
        zz = _glu_fwd(glu)
        p_ssm = _mm_nn(zz, Wg["ssm_w_out"][l], "col", name="mm_ssm_out")

        kv = _mm_nn(mem2, Wg["mem_w_kv"][l], "row", name="mm_kv")
        mm_o = _mem_fwd(proj, qm_off, mw, kv)
        p_mem = _mm_nn(mm_o, Wg["mem_w_out"][l], "col", name="mm_mem_out")

        merged = _merge_fwd(proj, gate_off, p_sb, p_ssm, p_mem)
        mix = _mm_nn(merged, Wg["w_o"][l], "row", name="mm_wo")
        x1, x1b, xh1, rs1 = _ln_fwd(xl, mix, W["ln1_g"][l][None, :], W["ln1_b"][l][None, :], alpha)
        gu = _mm_nn(x1b, Wg["ffn_w_gate_up"][l], "col", comm=hosted(plan_gate_up),
                    name="mm_gate_up_ag" if plan_gate_up else "mm_gate_up")
        if plan_gate_up:
            gu, got = gu
            keep(plan_gate_up, got)
        hid = _swiglu_fwd(gu)
        ffn = _mm_nn(hid, Wg["ffn_w_down"][l], "row", comm=hosted(plan_down),
                     name="mm_down_ag" if plan_down else "mm_down")
        if plan_down:
            ffn, got = ffn
            keep(plan_down, got)
        x2, x2b, xh2, rs2 = _ln_fwd(x1, ffn, W["ln2_g"][l][None, :], W["ln2_b"][l][None, :], alpha)
        sv.update(proj=proj, sb=sb, sb_ctot=sb_ctot, p_sb=p_sb, y=y, gy=gy, hre=hre, him=him, glu=glu, zz=zz,
                  p_ssm=p_ssm, kv=kv, mm_o=mm_o, p_mem=p_mem, merged=merged, x1=x1b, xh1=xh1, rs1=rs1, gu=gu,
                  hid=hid, xh2=xh2, rs2=rs2, u=u_ssm,
                  disc=(lre3, lim3, logdt, bre_t, bim_t, a_row, b_row, bmre, bmim, cmre, cmimn, d_row))
        saved.append(sv)
        xl, xlb = x2, x2b

    dxl, loss_part = _loss_head(xl, tgt)

    gbig = {n: [None] * L for n in BIG}
    gsmall = {n: [None] * L for n in SMALL}
    own = [None] * L
    from_chips = [None] * L
    above = None
    pending = None
    for l in range(L - 1, -1, -1):
        sv = saved[l]
        proj = sv["proj"]
        dr2, dr2b, dg2, db2 = _ln_bwd(dxl, sv["xh2"], sv["rs2"], W["ln2_g"][l][None, :])
        gsmall["ln2_g"][l], gsmall["ln2_b"][l] = dg2[0], db2[0]
        dhid = _mm_nt(dr2b, Wg["ffn_w_down"][l], "row", name="mm_d_hid")
        gbig["ffn_w_down"][l] = _mm_tn(sv["hid"], dr2b, "row", name="mm_g_down")
        dgu = _swiglu_bwd(dhid, sv["gu"])
        if above is None:
            dx1 = _mm_nt(dgu, Wg["ffn_w_gate_up"][l], "col", add=dr2, add_scale=alpha, name="mm_d_x1")
        else:
            dx1, from_sibling = _mm_nt(dgu, Wg["ffn_w_gate_up"][l], "col", add=dr2, add_scale=alpha,
                                       name="mm_d_x1_rs", comm=_rs_pair_comm(above))
            sums = [_pair_add(g, r, place) for g, r in zip(above, from_sibling)]
            pending = [s[0] for s in sums]
            own[l + 1] = [s[1] for s in sums]
        gbig["ffn_w_gate_up"][l] = _mm_tn(sv["x1"], dgu, "col", name="mm_g_gate_up")

        dr1, dr1b, dg1, db1 = _ln_bwd(dx1, sv["xh1"], sv["rs1"], W["ln1_g"][l][None, :])
        gsmall["ln1_g"][l], gsmall["ln1_b"][l] = dg1[0], db1[0]
        dmerged = _mm_nt(dr1b, Wg["w_o"][l], "row", name="mm_d_merged")
        gbig["w_o"][l] = _mm_tn(sv["merged"], dr1b, "row", name="mm_g_wo")
        dp_sb, dp_ssm, dp_mem, dgl0, dgl1, dgl2 = _merge_bwd(
            dmerged, proj, gate_off, sv["p_sb"], sv["p_ssm"], sv["p_mem"])

        dsb = _mm_nt(dp_sb, Wg["sb_w_out"][l], "col", out_dtype=BF16, name="mm_d_sb")
        gbig["sb_w_out"][l] = _mm_tn(sv["sb"], dp_sb, "col", name="mm_g_sb_out")

        dzz = _mm_nt(dp_ssm, Wg["ssm_w_out"][l], "col", name="mm_d_zz")
        gbig["ssm_w_out"][l] = _mm_tn(sv["zz"], dp_ssm, "col", name="mm_g_ssm_out")
        dglu = _glu_bwd(dzz, sv["glu"])
        dgy = _mm_nt(dglu, Wg["ssm_w_glu"][l], "col", name="mm_d_gy")
        gbig["ssm_w_glu"][l] = _mm_tn(sv["gy"], dglu, "col", name="mm_g_glu")
        lre3, lim3, logdt, bre_t, bim_t, a_row, b_row, bmre, bmim, cmre, cmimn, d_row = sv["disc"]
        du, dy_ssm, g_re, g_im, dd, da, db = _s5_bwd(
            sv["u"], dgy, sv["y"], sv["hre"], sv["him"], a_row, b_row, bmre, bmim, cmre, cmimn, d_row)
        dbmre = _mm_tn(sv["u"], g_re, "plain", name="mm_g_ssm_bre")
        dbmim = _mm_tn(sv["u"], g_im, "plain", name="mm_g_ssm_bim")
        dcmre = _mm_tn(sv["hre"], dy_ssm, "plain", name="mm_g_ssm_cre")
        dcmimn = _mm_tn(sv["him"], dy_ssm, "plain", name="mm_g_ssm_cim")
        dlre, dlim, dlogdt, dbre_t, dbim_t = _s5_disc_bwd(
            lre3, lim3, logdt, bre_t, bim_t, da.reshape(G, 1, P), db.reshape(G, 1, P),
            _block_diag_take(dbmre, G), _block_diag_take(dbmim, G))
        gsmall["ssm_lambda_re"][l], gsmall["ssm_lambda_im"][l] = dlre.reshape(G, P), dlim.reshape(G, P)
        gsmall["ssm_log_dt"][l] = dlogdt.reshape(G)
        gsmall["ssm_b_re"][l] = dbre_t.transpose(0, 2, 1)
        gsmall["ssm_b_im"][l] = dbim_t.transpose(0, 2, 1)
        gsmall["ssm_c_re"][l] = _block_diag_take(dcmre, G).transpose(0, 2, 1)
        gsmall["ssm_c_im"][l] = -_block_diag_take(dcmimn, G).transpose(0, 2, 1)
        gsmall["ssm_d"][l] = dd[0]

        dmm = _mm_nt(dp_mem, Wg["mem_w_out"][l], "col", out_dtype=BF16, name="mm_d_mm")
        gbig["mem_w_out"][l] = _mm_tn(sv["mm_o"], dp_mem, "col", name="mm_g_mem_out")
        dqm, dkv = _mem_bwd(proj, qm_off, mw, sv["kv"], dmm)
        gbig["mem_w_kv"][l] = _mm_tn(mem2, dkv, "row", name="mm_g_kv")

        dq, dk, dv, arrived = _sb_bwd(proj, q_off, k_off, v_off, heads, dsb, sv["sb_ctot"], rs=pending)
        if pending is not None:
            from_chips[l + 1] = arrived
        dproj, dbin = _assemble_dproj([dq, dk, dv, du, dqm, dgl0, dgl1, dgl2])
        gsmall["b_in"][l] = dbin[0]
        if l > 0:
            dxl = _mm_nt(dproj, Wg["w_in"][l], "col", add=dr1, add_scale=alpha, name="mm_d_x")
            gbig["w_in"][l] = _mm_tn(sv["x"], dproj, "col", name="mm_g_win")
            above = [gbig[n][l] for n in BIG]
        else:
            ffn = [gbig[n][0] for n in FFN_WEIGHTS]
            ffn_sums = [_pair_add(g, r, place) for g, r in zip(ffn, _rs_pair(ffn))]
            dxl, got_down = _mm_nt(dproj, Wg["w_in"][0], "col", add=dr1, add_scale=alpha, name="mm_d_x_rs",
                                   comm=_rs_chips_comm([ffn_sums[0][0]]))
            gbig["w_in"][0], got_gate_up = _mm_tn(sv["x"], dproj, "col", name="mm_g_win_rs",
                                                  comm=_rs_chips_comm([ffn_sums[1][0]]))

    rest_names = [n for n in BIG if n not in FFN_WEIGHTS]
    rest = [gbig[n][0] for n in rest_names]
    rest_sums = [_pair_add(g, r, place) for g, r in zip(rest, _rs_pair(rest))]
    rest_got = _rs_chips([s[0] for s in rest_sums])
    by_name = {n: (s[1], r) for n, s, r in zip(rest_names, rest_sums, rest_got)}
    by_name[FFN_WEIGHTS[0]] = (ffn_sums[0][1], got_down[0])
    by_name[FFN_WEIGHTS[1]] = (ffn_sums[1][1], got_gate_up[0])
    own[0] = [by_name[n][0] for n in BIG]
    from_chips[0] = [by_name[n][1] for n in BIG]
    grad_x = dxl.reshape(x.shape)

    fulls = []
    for a, n in enumerate(BIG):
        full = lax.empty(W[n].shape, F32)
        for l in range(L - 1, -1, -1):
            full = _chip_sum(own[l][a], from_chips[l][a], full, l, place)
        fulls.append(full)
    reduced = _share_pair(fulls)
    grads = {n: reduced[i] for i, n in enumerate(BIG)}

    small_local = [jnp.stack(gsmall[n]) for n in SMALL]
    packed = _pack(small_local + [loss_part[0, :1]])
    total = _small_allreduce(packed)
    unpacked = _unpack(total, small_local + [loss_part[0, :1]])
    for n, g in zip(SMALL, unpacked[:-1]):
        grads[n] = g
    loss = unpacked[-1][0]

    delta, new_m, new_v = {}, {}, {}
    for n in BIG:
        delta[n], new_m[n], new_v[n], grads[n] = _adamw(W[n], grads[n], M1[n], V1[n], with_grad=True)
    sm = _adamw(_pack([W[n] for n in SMALL]), _pack([grads[n] for n in SMALL]),
                _pack([M1[n] for n in SMALL]), _pack([V1[n] for n in SMALL]))
    like = [W[n] for n in SMALL]
    for n, d, m_, v_ in zip(SMALL, _unpack(sm[0], like), _unpack(sm[1], like), _unpack(sm[2], like)):
        delta[n], new_m[n], new_v[n] = d, m_, v_

    return (loss, grad_x, *[grads[n] for n in WEIGHTS], *[delta[n] for n in WEIGHTS],
            *[new_m[n] for n in WEIGHTS], *[new_v[n] for n in WEIGHTS])


def kernel(x, mem, w_in, b_in, sb_w_out, ssm_lambda_re, ssm_lambda_im, ssm_log_dt, ssm_b_re, ssm_b_im, ssm_c_re, ssm_c_im, ssm_d, ssm_w_glu, ssm_w_out, mem_w_kv, mem_w_out, w_o, ln1_g, ln1_b, ffn_w_gate_up, ffn_w_down, ln2_g, ln2_b, loss_target, m_w_in, m_b_in, m_sb_w_out, m_ssm_lambda_re, m_ssm_lambda_im, m_ssm_log_dt, m_ssm_b_re, m_ssm_b_im, m_ssm_c_re, m_ssm_c_im, m_ssm_d, m_ssm_w_glu, m_ssm_w_out, m_mem_w_kv, m_mem_w_out, m_w_o, m_ln1_g, m_ln1_b, m_ffn_w_gate_up, m_ffn_w_down, m_ln2_g, m_ln2_b, v_w_in, v_b_in, v_sb_w_out, v_ssm_lambda_re, v_ssm_lambda_im, v_ssm_log_dt, v_ssm_b_re, v_ssm_b_im, v_ssm_c_re, v_ssm_c_im, v_ssm_d, v_ssm_w_glu, v_ssm_w_out, v_mem_w_kv, v_mem_w_out, v_w_o, v_ln1_g, v_ln1_b, v_ffn_w_gate_up, v_ffn_w_down, v_ln2_g, v_ln2_b):
    W = dict(w_in=w_in, b_in=b_in, sb_w_out=sb_w_out, ssm_lambda_re=ssm_lambda_re, ssm_lambda_im=ssm_lambda_im,
             ssm_log_dt=ssm_log_dt, ssm_b_re=ssm_b_re, ssm_b_im=ssm_b_im, ssm_c_re=ssm_c_re, ssm_c_im=ssm_c_im,
             ssm_d=ssm_d, ssm_w_glu=ssm_w_glu, ssm_w_out=ssm_w_out, mem_w_kv=mem_w_kv, mem_w_out=mem_w_out,
             w_o=w_o, ln1_g=ln1_g, ln1_b=ln1_b, ffn_w_gate_up=ffn_w_gate_up, ffn_w_down=ffn_w_down,
             ln2_g=ln2_g, ln2_b=ln2_b)
    M1 = dict(w_in=m_w_in, b_in=m_b_in, sb_w_out=m_sb_w_out, ssm_lambda_re=m_ssm_lambda_re,
              ssm_lambda_im=m_ssm_lambda_im, ssm_log_dt=m_ssm_log_dt, ssm_b_re=m_ssm_b_re, ssm_b_im=m_ssm_b_im,
              ssm_c_re=m_ssm_c_re, ssm_c_im=m_ssm_c_im, ssm_d=m_ssm_d, ssm_w_glu=m_ssm_w_glu,
              ssm_w_out=m_ssm_w_out, mem_w_kv=m_mem_w_kv, mem_w_out=m_mem_w_out, w_o=m_w_o, ln1_g=m_ln1_g,
              ln1_b=m_ln1_b, ffn_w_gate_up=m_ffn_w_gate_up, ffn_w_down=m_ffn_w_down, ln2_g=m_ln2_g, ln2_b=m_ln2_b)
    V1 = dict(w_in=v_w_in, b_in=v_b_in, sb_w_out=v_sb_w_out, ssm_lambda_re=v_ssm_lambda_re,
              ssm_lambda_im=v_ssm_lambda_im, ssm_log_dt=v_ssm_log_dt, ssm_b_re=v_ssm_b_re, ssm_b_im=v_ssm_b_im,
              ssm_c_re=v_ssm_c_re, ssm_c_im=v_ssm_c_im, ssm_d=v_ssm_d, ssm_w_glu=v_ssm_w_glu,
              ssm_w_out=v_ssm_w_out, mem_w_kv=v_mem_w_kv, mem_w_out=v_mem_w_out, w_o=v_w_o, ln1_g=v_ln1_g,
              ln1_b=v_ln1_b, ffn_w_gate_up=v_ffn_w_gate_up, ffn_w_down=v_ffn_w_down, ln2_g=v_ln2_g, ln2_b=v_ln2_b)
    return _step(x, mem, loss_target, W, M1, V1)
```

```python
import functools
import math

import jax
import jax.numpy as jnp
from jax import lax
from jax.experimental import pallas as pl
from jax.experimental.pallas import tpu as pltpu

F32 = jnp.float32
BF16 = jnp.bfloat16
MESH = pl.DeviceIdType.MESH
ANY = pl.BlockSpec(memory_space=pl.ANY)
VMEM = pl.BlockSpec(memory_space=pltpu.VMEM)

HEAD_DIM = 128
SSM_GROUP = 16
N_CHIPS = 4
N_DEV = 8
LN_EPS = 1e-5
ADAM_LR = 0.001
ADAM_B1 = 0.9
ADAM_B2 = 0.999
ADAM_EPS = 1e-08
ADAM_WD = 0.01
ADAM_STEP = 10
LANES = 128
SUBLANES = 8
VMEM_LIMIT_MB = 56


def _cparams(sem, mb=VMEM_LIMIT_MB):
    return pltpu.CompilerParams(dimension_semantics=sem, vmem_limit_bytes=mb * 1024 * 1024)


def _tile(n, pref, mult=LANES):
    best = None
    t = mult
    while t <= min(n, pref):
        if n % t == 0:
            best = t
        t += mult
    return n if best is None else best


def _dot(a, b, dims=(((1,), (0,)), ((), ()))):
    return lax.dot_general(a, b, dims, preferred_element_type=F32)


NT = (((1,), (1,)), ((), ()))
TN = (((0,), (0,)), ((), ()))


def _split2(x):
    hi = x.astype(BF16)
    lo = (x - hi.astype(F32)).astype(BF16)
    return hi, lo


def _dot_a2(a, b, dims=(((1,), (0,)), ((), ()))):
    ah, al = _split2(a)
    bb = b.astype(BF16)
    return _dot(ah, bb, dims) + _dot(al, bb, dims)


def _bd_blocks(n_in, n_out):
    if (n_in // 2) % LANES == 0 and (n_out // 2) % LANES == 0:
        return [(slice(0, n_in // 2), slice(0, n_out // 2)), (slice(n_in // 2, n_in), slice(n_out // 2, n_out))]
    return [(slice(0, n_in), slice(0, n_out))]


def _dot_mask(x, u2):
    hi, lo = _split2(x)
    return _dot(jnp.concatenate([hi, lo], axis=1), u2)


def _prefix_lanes(x, u2):
    groups = x.shape[1] // LANES
    out, run = [], None
    for k in range(groups):
        xk = x[:, k * LANES:(k + 1) * LANES]
        r = _dot_mask(xk, u2)
        out.append(r if run is None else r + run)
        t = jnp.sum(xk, axis=1, keepdims=True)
        run = t if run is None else run + t
    return jnp.concatenate(out, axis=1)


def _tri2(n, rel):
    ri = lax.broadcasted_iota(jnp.int32, (2 * n, n), 0)
    ci = lax.broadcasted_iota(jnp.int32, (2 * n, n), 1)
    ri = jnp.where(ri >= n, ri - n, ri)
    return rel(ri, ci).astype(BF16)


def _mm_call(a, b, *, dims, grid, a_spec, b_spec, out_spec, out_shape, name,
             bias=None, bias_spec=None, add=None, add_spec=None, add_scale=1.0, comm=None, prod=None):
    nk = grid[2]
    has_bias = bias is not None
    has_add = add is not None
    n_ci = len(comm["ins"]) if comm else 0
    n_co = len(comm["out_shapes"]) if comm else 0

    def body(*refs):
        a_ref, b_ref = refs[0], refs[1]
        pos = 2
        bias_ref = refs[pos] if has_bias else None
        pos += int(has_bias)
        add_ref = refs[pos] if has_add else None
        pos += int(has_add)
        comm_in = refs[pos:pos + n_ci]
        pos += n_ci
        o_ref = refs[pos]
        comm_out = refs[pos + 1:pos + 1 + n_co]
        pos += 1 + n_co
        acc_ref = refs[pos] if nk > 1 else None
        pos += int(nk > 1)
        if comm:
            ex_start, ex_middle, ex_finish = comm["hooks"](comm_in, comm_out, refs[pos], refs[pos + 1])
            ids = [pl.program_id(d) for d in range(3)]

            @pl.when((ids[0] == 0) & (ids[1] == 0) & (ids[2] == 0))
            def _():
                ex_start()

        if prod is None:
            p = _dot(a_ref[...].astype(BF16), b_ref[...].astype(BF16), dims)
        else:
            p = prod(a_ref, b_ref)

        def finish(acc):
            if has_bias:
                acc = acc + bias_ref[...]
            if has_add:
                acc = acc + add_scale * add_ref[...].astype(F32)
            o_ref[...] = acc.astype(o_ref.dtype)

        if nk == 1:
            finish(p)
        else:
            k = pl.program_id(2)

            @pl.when(k == 0)
            def _():
                acc_ref[...] = p

            @pl.when(k > 0)
            def _():
                acc_ref[...] += p

            @pl.when(k == nk - 1)
            def _():
                finish(acc_ref[...])

        if comm:
            @pl.when((ids[0] == grid[0] - 1) & (ids[1] == grid[1] - 1) & (ids[2] == grid[2] - 1))
            def _():
                if ex_middle is not None:
                    ex_middle()
                ex_finish()

    ins = [a, b]
    in_specs = [a_spec, b_spec]
    if has_bias:
        ins.append(bias)
        in_specs.append(bias_spec)
    if has_add:
        ins.append(add)
        in_specs.append(add_spec)
    scratch = []
    if nk > 1:
        blk = [d for d in out_spec.block_shape if d is not None]
        scratch.append(pltpu.VMEM(tuple(blk), F32))
    if not comm:
        return pl.pallas_call(
            body, out_shape=out_shape, grid=grid, in_specs=in_specs, out_specs=out_spec,
            scratch_shapes=scratch, name=name,
            compiler_params=_cparams(("parallel", "parallel", "arbitrary")),
        )(*ins)
    n_main = len(ins)
    scratch += [pltpu.SemaphoreType.DMA((comm["n_sems"],)), pltpu.SemaphoreType.DMA((comm["n_sems"],))]
    res = pl.pallas_call(
        body, out_shape=[out_shape] + list(comm["out_shapes"]), grid=grid,
        in_specs=in_specs + [ANY] * n_ci, out_specs=[out_spec] + [ANY] * n_co, scratch_shapes=scratch,
        input_output_aliases={n_main + i: 1 + i for i in range(n_ci)} if comm["alias"] else {},
        name=name, compiler_params=_cparams(("arbitrary", "arbitrary", "arbitrary")),
    )(*ins, *comm["ins"])
    return res[0], list(res[1:])


def _mm_nn(a, w, kind, *, bias=None, out_dtype=F32, name, comm=None):
    M, K = a.shape
    tm = _tile(M, 1024, SUBLANES)
    tk = K if K <= 2048 else _tile(K, 1408)
    if kind == "col":
        Nc = w.shape[2]
        N = N_CHIPS * Nc
        tn = _tile(Nc, 1408)
        npc = Nc // tn
        b_spec = pl.BlockSpec((None, tk, tn), lambda i, j, k: (j // npc, k, j % npc))
    else:
        w = w.reshape(-1, w.shape[-1])
        N = w.shape[1]
        tn = _tile(N, 2048)
        if K > 2048 and a.dtype == BF16:
            tk, tn = K, _tile(N, 512)
        b_spec = pl.BlockSpec((tk, tn), lambda i, j, k: (k, j))
    grid = (M // tm, N // tn, K // tk)
    bias_spec = pl.BlockSpec((1, tn), lambda i, j, k: (0, j)) if bias is not None else None
    return _mm_call(
        a, w, dims=(((1,), (0,)), ((), ())), grid=grid,
        a_spec=pl.BlockSpec((tm, tk), lambda i, j, k: (i, k)), b_spec=b_spec,
        out_spec=pl.BlockSpec((tm, tn), lambda i, j, k: (i, j)),
        out_shape=jax.ShapeDtypeStruct((M, N), out_dtype), name=name,
        bias=bias, bias_spec=bias_spec, comm=comm)


def _mm_nt(dy, w, kind, *, add=None, add_scale=1.0, out_dtype=F32, name, comm=None):
    M, N = dy.shape
    tm = _tile(M, 1024 if dy.dtype == BF16 else 512, SUBLANES)
    if kind == "col" and dy.dtype == BF16:
        K, Nc = w.shape[1], w.shape[2]
        cps = 2
        tm = _tile(M, 1024, SUBLANES)
        tko = _tile(K, 512)

        def prod(a_ref, b_ref):
            acc = None
            for c in range(cps):
                t = _dot(a_ref[:, c * Nc:(c + 1) * Nc], b_ref[c], NT)
                acc = t if acc is None else acc + t
            return acc

        add_spec = pl.BlockSpec((tm, tko), lambda i, j, r: (i, j)) if add is not None else None
        return _mm_call(
            dy, w, dims=NT, grid=(M // tm, K // tko, N_CHIPS // cps), prod=prod,
            a_spec=pl.BlockSpec((tm, cps * Nc), lambda i, j, r: (i, r)),
            b_spec=pl.BlockSpec((cps, tko, Nc), lambda i, j, r: (r, j, 0)),
            out_spec=pl.BlockSpec((tm, tko), lambda i, j, r: (i, j)),
            out_shape=jax.ShapeDtypeStruct((M, K), out_dtype), name=name,
            add=add, add_spec=add_spec, add_scale=add_scale, comm=comm)
    if kind == "col":
        K, Nc = w.shape[1], w.shape[2]
        tn = _tile(Nc, 1408)
        npc = Nc // tn
        tko = _tile(K, 2048)
        b_spec = pl.BlockSpec((None, tko, tn), lambda i, j, r: (r // npc, j, r % npc))
    else:
        w = w.reshape(-1, w.shape[-1])
        K = w.shape[0]
        tn = _tile(N, 2048)
        tko = _tile(K, 2048)
        b_spec = pl.BlockSpec((tko, tn), lambda i, j, r: (j, r))
    grid = (M // tm, K // tko, N // tn)
    add_spec = pl.BlockSpec((tm, tko), lambda i, j, r: (i, j)) if add is not None else None
    return _mm_call(
        dy, w, dims=NT, grid=grid,
        a_spec=pl.BlockSpec((tm, tn), lambda i, j, r: (i, r)), b_spec=b_spec,
        out_spec=pl.BlockSpec((tm, tko), lambda i, j, r: (i, j)),
        out_shape=jax.ShapeDtypeStruct((M, K), out_dtype), name=name,
        add=add, add_spec=add_spec, add_scale=add_scale, comm=comm)


def _mm_tn(a, dy, kind, *, name, comm=None):
    M, K = a.shape
    N = dy.shape[1]
    tm = M if a.dtype == BF16 and dy.dtype == BF16 else _tile(M, 2048, SUBLANES)
    tkw = _tile(K, 512)
    tn = _tile(N // N_CHIPS, 1408) if kind == "col" else _tile(N, 1024)
    a_inner = tkw * a.dtype.itemsize <= tn * dy.dtype.itemsize
    ij = (lambda o, n: (n, o)) if a_inner else (lambda o, n: (o, n))
    if kind == "col":
        Nc = N // N_CHIPS
        npc = Nc // tn

        def out_map(o, n, m):
            i, j = ij(o, n)
            return (j // npc, i, j % npc)

        out_spec = pl.BlockSpec((None, tkw, tn), out_map)
        out_shape = jax.ShapeDtypeStruct((N_CHIPS, K, Nc), F32)
    else:
        out_spec = pl.BlockSpec((tkw, tn), lambda o, n, m: ij(o, n))
        out_shape = jax.ShapeDtypeStruct((K, N), F32)
    ni, nj = K // tkw, N // tn
    grid = (nj, ni, M // tm) if a_inner else (ni, nj, M // tm)
    out = _mm_call(
        a, dy, dims=TN, grid=grid,
        a_spec=pl.BlockSpec((tm, tkw), lambda o, n, m: (m, ij(o, n)[0])),
        b_spec=pl.BlockSpec((tm, tn), lambda o, n, m: (m, ij(o, n)[1])),
        out_spec=out_spec, out_shape=out_shape, name=name, comm=comm)
    extra = None
    if comm:
        out, extra = out
    if kind == "row":
        out = out.reshape(N_CHIPS, K // N_CHIPS, N)
    return (out, extra) if comm else out


def _gelu_grad(x):
    k = math.sqrt(2.0 / math.pi)
    inner = k * (x + 0.044715 * x * x * x)
    t = jnp.tanh(inner)
    return 0.5 * (1.0 + t) + 0.5 * x * (1.0 - t * t) * k * (1.0 + 3.0 * 0.044715 * x * x)


def _ln_fwd(xin, delta, g, b, alpha):
    S, D = xin.shape
    ts = _tile(S, 256, SUBLANES)

    def body(x_ref, d_ref, g_ref, b_ref, y_ref, yb_ref, xh_ref, rs_ref):
        r = alpha * x_ref[...] + d_ref[...]
        mu = jnp.mean(r, axis=-1, keepdims=True)
        rc = r - mu
        var = jnp.mean(rc * rc, axis=-1, keepdims=True)
        rstd = lax.rsqrt(var + LN_EPS)
        xh = rc * rstd
        y = xh * g_ref[...] + b_ref[...]
        y_ref[...] = y
        yb_ref[...] = y.astype(BF16)
        xh_ref[...] = xh
        rs_ref[...] = rstd

    row = pl.BlockSpec((ts, D), lambda i: (i, 0))
    vec = pl.BlockSpec((1, D), lambda i: (0, 0))
    return pl.pallas_call(
        body, grid=(S // ts,), in_specs=[row, row, vec, vec],
        out_specs=[row, row, row, pl.BlockSpec((ts, 1), lambda i: (i, 0))],
        out_shape=[jax.ShapeDtypeStruct((S, D), F32), jax.ShapeDtypeStruct((S, D), BF16),
                   jax.ShapeDtypeStruct((S, D), F32), jax.ShapeDtypeStruct((S, 1), F32)],
        name="ln_fwd", compiler_params=_cparams(("parallel",)),
    )(xin, delta, g, b)


def _ln_bwd(dy, xh, rstd, g):
    S, D = dy.shape
    ts = _tile(S, 256, SUBLANES)

    def body(dy_ref, xh_ref, rs_ref, g_ref, dr_ref, drb_ref, dg_ref, db_ref):
        @pl.when(pl.program_id(0) == 0)
        def _():
            dg_ref[...] = jnp.zeros_like(dg_ref)
            db_ref[...] = jnp.zeros_like(db_ref)

        dyv = dy_ref[...]
        xhv = xh_ref[...]
        dyg = dyv * g_ref[...]
        m1 = jnp.mean(dyg, axis=-1, keepdims=True)
        m2 = jnp.mean(dyg * xhv, axis=-1, keepdims=True)
        dr = rs_ref[...] * (dyg - m1 - xhv * m2)
        dr_ref[...] = dr
        drb_ref[...] = dr.astype(BF16)
        dg_ref[...] += jnp.sum(dyv * xhv, axis=0, keepdims=True)
        db_ref[...] += jnp.sum(dyv, axis=0, keepdims=True)

    row = pl.BlockSpec((ts, D), lambda i: (i, 0))
    vec = pl.BlockSpec((1, D), lambda i: (0, 0))
    return pl.pallas_call(
        body, grid=(S // ts,),
        in_specs=[row, row, pl.BlockSpec((ts, 1), lambda i: (i, 0)), vec],
        out_specs=[row, row, vec, vec],
        out_shape=[jax.ShapeDtypeStruct((S, D), F32), jax.ShapeDtypeStruct((S, D), BF16),
                   jax.ShapeDtypeStruct((1, D), F32), jax.ShapeDtypeStruct((1, D), F32)],
        name="ln_bwd", compiler_params=_cparams(("arbitrary",)),
    )(dy, xh, rstd, g)


def _merge_fwd(proj, gate_off, p_sb, p_ssm, p_mem):
    S, D = p_sb.shape
    ts = _tile(S, 256, SUBLANES)
    gb = gate_off // D

    def body(g0, g1, g2, a0, a1, a2, o_ref):
        o_ref[...] = (jax.nn.sigmoid(g0[...]) * a0[...] + jax.nn.sigmoid(g1[...]) * a1[...]
                      + jax.nn.sigmoid(g2[...]) * a2[...]).astype(o_ref.dtype)

    row = pl.BlockSpec((ts, D), lambda i: (i, 0))
    gates = [pl.BlockSpec((ts, D), functools.partial(lambda i, n: (i, gb + n), n=n)) for n in range(3)]
    return pl.pallas_call(
        body, grid=(S // ts,), in_specs=gates + [row, row, row], out_specs=row,
        out_shape=jax.ShapeDtypeStruct((S, D), BF16), name="merge_fwd",
        compiler_params=_cparams(("parallel",)),
    )(proj, proj, proj, p_sb, p_ssm, p_mem)


def _merge_bwd(dmerged, proj, gate_off, p_sb, p_ssm, p_mem):
    S, D = p_sb.shape
    ts = _tile(S, 256, SUBLANES)
    gb = gate_off // D

    def body(dm_ref, g0, g1, g2, a0, a1, a2, d0, d1, d2, l0, l1, l2):
        dm = dm_ref[...]
        for g_ref, a_ref, d_ref, l_ref in ((g0, a0, d0, l0), (g1, a1, d1, l1), (g2, a2, d2, l2)):
            s = jax.nn.sigmoid(g_ref[...])
            d_ref[...] = (dm * s).astype(d_ref.dtype)
            l_ref[...] = dm * a_ref[...] * s * (1.0 - s)

    row = pl.BlockSpec((ts, D), lambda i: (i, 0))
    gates = [pl.BlockSpec((ts, D), functools.partial(lambda i, n: (i, gb + n), n=n)) for n in range(3)]
    sd = jax.ShapeDtypeStruct((S, D), F32)
    return pl.pallas_call(
        body, grid=(S // ts,), in_specs=[row] + gates + [row, row, row], out_specs=[row] * 6,
        out_shape=[jax.ShapeDtypeStruct((S, D), BF16)] * 3 + [sd] * 3, name="merge_bwd", compiler_params=_cparams(("parallel",)),
    )(dmerged, proj, proj, proj, p_sb, p_ssm, p_mem)


def _glu_fwd(glu):
    S, W2 = glu.shape
    W = W2 // 2
    ts = _tile(S, 512, SUBLANES)

    def body(x_ref, o_ref):
        o_ref[...] = (x_ref[:, :W] * jax.nn.sigmoid(x_ref[:, W:])).astype(o_ref.dtype)

    return pl.pallas_call(
        body, grid=(S // ts,), in_specs=[pl.BlockSpec((ts, W2), lambda i: (i, 0))],
        out_specs=pl.BlockSpec((ts, W), lambda i: (i, 0)),
        out_shape=jax.ShapeDtypeStruct((S, W), BF16), name="glu_fwd",
        compiler_params=_cparams(("parallel",)),
    )(glu)


def _glu_bwd(dzz, glu):
    S, W2 = glu.shape
    W = W2 // 2
    ts = _tile(S, 512, SUBLANES)

    def body(d_ref, x_ref, o_ref):
        d = d_ref[...]
        a = x_ref[:, :W]
        s = jax.nn.sigmoid(x_ref[:, W:])
        o_ref[:, :W] = (d * s).astype(o_ref.dtype)
        o_ref[:, W:] = (d * a * s * (1.0 - s)).astype(o_ref.dtype)

    return pl.pallas_call(
        body, grid=(S // ts,),
        in_specs=[pl.BlockSpec((ts, W), lambda i: (i, 0)), pl.BlockSpec((ts, W2), lambda i: (i, 0))],
        out_specs=pl.BlockSpec((ts, W2), lambda i: (i, 0)),
        out_shape=jax.ShapeDtypeStruct((S, W2), BF16), name="glu_bwd",
        compiler_params=_cparams(("parallel",)),
    )(dzz, glu)


def _swiglu_fwd(gu):
    S, F2 = gu.shape
    Fh = F2 // 2
    ts = _tile(S, 128, SUBLANES)

    def body(x_ref, o_ref):
        fg = x_ref[:, :Fh]
        o_ref[...] = (fg * jax.nn.sigmoid(fg) * x_ref[:, Fh:]).astype(o_ref.dtype)

    return pl.pallas_call(
        body, grid=(S // ts,), in_specs=[pl.BlockSpec((ts, F2), lambda i: (i, 0))],
        out_specs=pl.BlockSpec((ts, Fh), lambda i: (i, 0)),
        out_shape=jax.ShapeDtypeStruct((S, Fh), BF16), name="swiglu_fwd",
        compiler_params=_cparams(("parallel",)),
    )(gu)


def _swiglu_bwd(dhid, gu):
    S, F2 = gu.shape
    Fh = F2 // 2
    ts = _tile(S, 128, SUBLANES)

    def body(d_ref, x_ref, o_ref):
        d = d_ref[...]
        fg = x_ref[:, :Fh]
        fu = x_ref[:, Fh:]
        s = jax.nn.sigmoid(fg)
        o_ref[:, :Fh] = (d * fu * s * (1.0 + fg * (1.0 - s))).astype(o_ref.dtype)
        o_ref[:, Fh:] = (d * fg * s).astype(o_ref.dtype)

    return pl.pallas_call(
        body, grid=(S // ts,),
        in_specs=[pl.BlockSpec((ts, Fh), lambda i: (i, 0)), pl.BlockSpec((ts, F2), lambda i: (i, 0))],
        out_specs=pl.BlockSpec((ts, F2), lambda i: (i, 0)),
        out_shape=jax.ShapeDtypeStruct((S, F2), BF16), name="swiglu_bwd",
        compiler_params=_cparams(("parallel",)),
    )(dhid, gu)


def _assemble_dproj(pieces):
    S = pieces[0].shape[0]
    widths = [p.shape[1] for p in pieces]
    total = sum(widths)
    ts = _tile(S, 128, SUBLANES)
    n = len(pieces)

    def body(*refs):
        o_ref, b_ref = refs[n], refs[n + 1]

        @pl.when(pl.program_id(0) == 0)
        def _():
            b_ref[...] = jnp.zeros_like(b_ref)

        off = 0
        for r, w in zip(refs[:n], widths):
            v = r[...].astype(F32)
            o_ref[:, off:off + w] = v.astype(o_ref.dtype)
            b_ref[:, off:off + w] += jnp.sum(v, axis=0, keepdims=True)
            off += w

    return pl.pallas_call(
        body, grid=(S // ts,),
        in_specs=[pl.BlockSpec((ts, w), lambda i: (i, 0)) for w in widths],
        out_specs=[pl.BlockSpec((ts, total), lambda i: (i, 0)), pl.BlockSpec((1, total), lambda i: (0, 0))],
        out_shape=[jax.ShapeDtypeStruct((S, total), BF16), jax.ShapeDtypeStruct((1, total), F32)],
        name="assemble_dproj", compiler_params=_cparams(("arbitrary",)),
    )(*pieces)


def _loss_head(y, target):
    S, D = y.shape
    ts = _tile(S, 256, SUBLANES)

    def body(y_ref, t_ref, dy_ref, l_ref):
        @pl.when(pl.program_id(0) == 0)
        def _():
            l_ref[...] = jnp.zeros_like(l_ref)

        e = y_ref[...] - t_ref[...]
        dy_ref[...] = e * (1.0 / D)
        part = jnp.sum(jnp.sum(e * e, axis=1, keepdims=True), axis=0, keepdims=True) * (0.5 / D)
        l_ref[...] += jnp.broadcast_to(part, l_ref.shape)

    row = pl.BlockSpec((ts, D), lambda i: (i, 0))
    return pl.pallas_call(
        body, grid=(S // ts,), in_specs=[row, row],
        out_specs=[row, pl.BlockSpec((1, LANES), lambda i: (0, 0))],
        out_shape=[jax.ShapeDtypeStruct((S, D), F32), jax.ShapeDtypeStruct((1, LANES), F32)],
        name="loss_head", compiler_params=_cparams(("arbitrary",)),
    )(y, target)


SB_TQ = 512
SB_TK = 512


def _sb_tile_terms(q, kb, scale, causal):
    z = _dot(q, kb, NT) * scale
    soft = jnp.log(1.0 + jnp.exp(-jnp.abs(z)))
    ls = jnp.minimum(z, 0.0) - soft
    l1m = jnp.minimum(-z, 0.0) - soft
    if causal is not None:
        l1m = jnp.where(causal, l1m, 0.0)
    return ls, l1m


def _sb_causal(qi, kj, TQ, TK):
    t_idx = qi * TQ + lax.broadcasted_iota(jnp.int32, (TQ, TK), 0)
    s_idx = kj * TK + lax.broadcasted_iota(jnp.int32, (TQ, TK), 1)
    return s_idx < t_idx


def _exchange_begin(heads, nq, start, middle=None):
    h, qi = pl.program_id(0), pl.program_id(1)

    @pl.when((h == 0) & (qi == 0))
    def _():
        start()

    if middle is not None:
        @pl.when((h == heads - 1) & (qi == nq - 1))
        def _():
            middle()


def _exchange_end(heads, nq, finish):
    @pl.when((pl.program_id(0) == heads - 1) & (pl.program_id(1) == nq - 1))
    def _():
        finish()


def _sb_fwd(proj, q_off, k_off, v_off, heads, ag=None):
    S = proj.shape[0]
    Dh = HEAD_DIM
    TQ = min(SB_TQ, S)
    TK = min(SB_TK, TQ)
    nq = S // TQ
    scale = Dh ** -0.5
    qb, kb0, vb0 = q_off // Dh, k_off // Dh, v_off // Dh
    n_ag = 0 if ag is None else len(ag)
    assert ag is None or heads >= 2

    def body(q_ref, k_ref, v_ref, *rest):
        o_ref, c_ref = rest[n_ag:n_ag + 2]
        if n_ag:
            ag_start, ag_middle, ag_finish = _ag_hooks(rest[n_ag + 2:2 * n_ag + 2], *rest[2 * n_ag + 2:])
            _exchange_begin(heads, nq, ag_start, ag_middle)
        qi = pl.program_id(1)
        q = q_ref[...].astype(BF16)
        upper = _tri2(TK, lambda j, s: j > s)
        nfull = (qi * TQ) // TK

        def block(kj, carry, masked):
            c, acc = carry
            off = pl.multiple_of(kj * TK, TK)
            kblk = k_ref[pl.ds(off, TK), :].astype(BF16)
            vblk = v_ref[pl.ds(off, TK), :].astype(BF16)
            causal = _sb_causal(qi, kj, TQ, TK) if masked else None
            ls, l1m = _sb_tile_terms(q, kblk, scale, causal)
            w = jnp.exp(ls + _dot_mask(l1m, upper) + c)
            if masked:
                w = jnp.where(causal, w, 0.0)
            acc = acc + _dot(w.astype(BF16), vblk)
            c = c + jnp.sum(l1m, axis=1, keepdims=True)
            return c, acc

        carry = (jnp.zeros((TQ, 1), F32), jnp.zeros((TQ, Dh), F32))
        for d in range(TQ // TK - 1, -1, -1):
            carry = block(nfull + d, carry, True)

        c, acc = lax.fori_loop(0, nfull, lambda jj, carry: block(nfull - 1 - jj, carry, False), carry)
        o_ref[...] = acc.astype(o_ref.dtype)
        c_ref[...] = c
        if n_ag:
            _exchange_end(heads, nq, ag_finish)

    ag = [] if ag is None else list(ag)
    res = pl.pallas_call(
        body, grid=(heads, nq),
        in_specs=[pl.BlockSpec((TQ, Dh), lambda h, i: (i, qb + h)),
                  pl.BlockSpec((S, Dh), lambda h, i: (0, kb0 + h)),
                  pl.BlockSpec((S, Dh), lambda h, i: (0, vb0 + h))] + [ANY] * n_ag,
        out_specs=[pl.BlockSpec((TQ, Dh), lambda h, i: (i, h)),
                   pl.BlockSpec((None, TQ, 1), lambda h, i: (h, i, 0))] + [ANY] * n_ag,
        out_shape=[jax.ShapeDtypeStruct((S, heads * Dh), BF16), jax.ShapeDtypeStruct((heads, S, 1), F32)]
        + [jax.ShapeDtypeStruct(b.shape, b.dtype) for b in ag],
        input_output_aliases={3 + i: 2 + i for i in range(n_ag)},
        scratch_shapes=[pltpu.SemaphoreType.DMA((n_ag * 6,)), pltpu.SemaphoreType.DMA((n_ag * 6,))] if n_ag else [],
        name="sb_fwd_ag" if n_ag else "sb_fwd",
        compiler_params=_cparams(("arbitrary", "arbitrary") if n_ag else ("parallel", "arbitrary")),
    )(proj, proj, proj, *ag)
    return res[0], res[1], list(res[2:])


def _sb_bwd(proj, q_off, k_off, v_off, heads, dout, ctot, rs=None):
    S = proj.shape[0]
    Dh = HEAD_DIM
    TQ = min(SB_TQ, S)
    TK = min(SB_TK, TQ)
    nq = S // TQ
    scale = Dh ** -0.5
    qb, kb0, vb0 = q_off // Dh, k_off // Dh, v_off // Dh
    n_rs = 0 if rs is None else len(rs)

    def body(q_ref, k_ref, v_ref, do_ref, c_ref, *rest):
        dq_ref, dk_ref, dv_ref = rest[n_rs:n_rs + 3]
        if n_rs:
            rs_start, rs_finish = _rs_chips_hooks(rest[:n_rs], rest[n_rs + 3:2 * n_rs + 3], *rest[2 * n_rs + 3:])
            _exchange_begin(heads, nq, rs_start)
        qi = pl.program_id(1)

        @pl.when(qi == 0)
        def _():
            dk_ref[...] = jnp.zeros_like(dk_ref)
            dv_ref[...] = jnp.zeros_like(dv_ref)

        q = q_ref[...].astype(BF16)
        do = do_ref[...].astype(BF16)
        ctot = c_ref[...]
        lower_incl = _tri2(LANES, lambda j, s: j <= s)
        lower = _tri2(LANES, lambda j, s: j < s)
        nfull = (qi * TQ) // TK

        def block(kj, carry, masked):
            cl, ce, dq = carry
            off = pl.multiple_of(kj * TK, TK)
            kblk = k_ref[pl.ds(off, TK), :].astype(BF16)
            vblk = v_ref[pl.ds(off, TK), :].astype(BF16)
            causal = _sb_causal(qi, kj, TQ, TK) if masked else None
            ls, l1m = _sb_tile_terms(q, kblk, scale, causal)
            w = jnp.exp(ls + (ctot - cl - _prefix_lanes(l1m, lower_incl)))
            if masked:
                w = jnp.where(causal, w, 0.0)
            e = w * _dot(do, vblk, NT)
            before = ce + _prefix_lanes(e, lower)
            beta = jnp.exp(ls)
            dz = (e * (1.0 - beta) - beta * before) * scale
            if masked:
                dz = jnp.where(causal, dz, 0.0)
            dzb = dz.astype(BF16)
            dq = dq + _dot(dzb, kblk)
            dk_ref[pl.ds(off, TK), :] += _dot(dzb, q, TN)
            dv_ref[pl.ds(off, TK), :] += _dot(w.astype(BF16), do, TN)
            cl = cl + jnp.sum(l1m, axis=1, keepdims=True)
            ce = ce + jnp.sum(e, axis=1, keepdims=True)
            return cl, ce, dq

        zero = jnp.zeros((TQ, 1), F32)
        carry = lax.fori_loop(0, nfull, lambda kj, carry: block(kj, carry, False),
                              (zero, zero, jnp.zeros((TQ, Dh), F32)))
        for d in range(TQ // TK):
            carry = block(nfull + d, carry, True)
        dq_ref[...] = carry[2]
        if n_rs:
            _exchange_end(heads, nq, rs_finish)

    rs = [] if rs is None else list(rs)
    blk = pl.BlockSpec((TQ, Dh), lambda h, i: (i, h))
    col = pl.BlockSpec((S, Dh), lambda h, i: (0, h))
    sd = jax.ShapeDtypeStruct((S, heads * Dh), F32)
    res = pl.pallas_call(
        body, grid=(heads, nq),
        in_specs=[pl.BlockSpec((TQ, Dh), lambda h, i: (i, qb + h)),
                  pl.BlockSpec((S, Dh), lambda h, i: (0, kb0 + h)),
                  pl.BlockSpec((S, Dh), lambda h, i: (0, vb0 + h)), blk,
                  pl.BlockSpec((None, TQ, 1), lambda h, i: (h, i, 0))] + [ANY] * n_rs,
        out_specs=[blk, col, col] + [ANY] * n_rs, out_shape=[sd, sd, sd] + _rs_chips_shapes(rs),
        scratch_shapes=[pltpu.SemaphoreType.DMA((n_rs * 3,)), pltpu.SemaphoreType.DMA((n_rs * 3,))] if n_rs else [],
        name="sb_bwd_rs" if n_rs else "sb_bwd",
        compiler_params=_cparams(("arbitrary", "arbitrary") if n_rs else ("parallel", "arbitrary")),
    )(proj, proj, proj, dout, ctot, *rs)
    return res[0], res[1], res[2], list(res[3:])


def _mem_probs(qh, kh, scale):
    s = _dot(qh, kh, NT) * scale
    m = jnp.max(s, axis=-1, keepdims=True)
    p = jnp.exp(s - m)
    return p / jnp.sum(p, axis=-1, keepdims=True)


def _mem_fwd(proj, q_off, width, kv):
    S = proj.shape[0]
    Dh = HEAD_DIM
    heads = width // Dh
    ts = _tile(S, 512, SUBLANES)
    scale = Dh ** -0.5
    M = kv.shape[0]

    def body(q_ref, kv_ref, o_ref):
        for h in range(heads):
            qh = q_ref[:, h * Dh:(h + 1) * Dh].astype(BF16)
            kh = kv_ref[:, h * Dh:(h + 1) * Dh].astype(BF16)
            vh = kv_ref[:, width + h * Dh:width + (h + 1) * Dh].astype(BF16)
            p = _mem_probs(qh, kh, scale)
            o_ref[:, h * Dh:(h + 1) * Dh] = _dot(p.astype(BF16), vh).astype(o_ref.dtype)

    return pl.pallas_call(
        body, grid=(S // ts,),
        in_specs=[pl.BlockSpec((ts, width), lambda i: (i, q_off // width)),
                  pl.BlockSpec((M, 2 * width), lambda i: (0, 0))],
        out_specs=pl.BlockSpec((ts, width), lambda i: (i, 0)),
        out_shape=jax.ShapeDtypeStruct((S, width), BF16), name="mem_fwd",
        compiler_params=_cparams(("parallel",)),
    )(proj, kv)


def _mem_bwd(proj, q_off, width, kv, dmm):
    S = proj.shape[0]
    Dh = HEAD_DIM
    heads = width // Dh
    ts = _tile(S, 512, SUBLANES)
    scale = Dh ** -0.5
    M = kv.shape[0]

    def body(q_ref, kv_ref, d_ref, dq_ref, dkv_ref):
        @pl.when(pl.program_id(0) == 0)
        def _():
            dkv_ref[...] = jnp.zeros_like(dkv_ref)

        for h in range(heads):
            qh = q_ref[:, h * Dh:(h + 1) * Dh].astype(BF16)
            kh = kv_ref[:, h * Dh:(h + 1) * Dh].astype(BF16)
            vh = kv_ref[:, width + h * Dh:width + (h + 1) * Dh].astype(BF16)
            dh = d_ref[:, h * Dh:(h + 1) * Dh].astype(BF16)
            p = _mem_probs(qh, kh, scale)
            dp = _dot(dh, vh, NT)
            ds = p * (dp - jnp.sum(dp * p, axis=-1, keepdims=True)) * scale
            dsb = ds.astype(BF16)
            dq_ref[:, h * Dh:(h + 1) * Dh] = _dot(dsb, kh)
            dkv_ref[:, h * Dh:(h + 1) * Dh] += _dot(dsb, qh, TN)
            dkv_ref[:, width + h * Dh:width + (h + 1) * Dh] += _dot(p.astype(BF16), dh, TN)

    row = pl.BlockSpec((ts, width), lambda i: (i, 0))
    full = pl.BlockSpec((M, 2 * width), lambda i: (0, 0))
    return pl.pallas_call(
        body, grid=(S // ts,),
        in_specs=[pl.BlockSpec((ts, width), lambda i: (i, q_off // width)), full, row],
        out_specs=[row, full],
        out_shape=[jax.ShapeDtypeStruct((S, width), F32), jax.ShapeDtypeStruct((M, 2 * width), F32)],
        name="mem_bwd", compiler_params=_cparams(("arbitrary",)),
    )(proj, kv, dmm)


def _disc_math(lre, lim, logdt, bre_t, bim_t):
    dt = jnp.exp(logdt)
    mag = jnp.exp(lre * dt)
    ang = lim * dt
    a = mag * jnp.cos(ang)
    b = mag * jnp.sin(ang)
    den = lre * lre + lim * lim
    nr = a - 1.0
    fre = (nr * lre + b * lim) / den
    fim = (b * lre - nr * lim) / den
    bbre = fre * bre_t - fim * bim_t
    bbim = fre * bim_t + fim * bre_t
    return a, b, bbre, bbim


def _s5_disc(lre, lim, logdt, bre_t, bim_t):
    G, _, P = lre.shape
    C = bre_t.shape[1]

    def body(lre_ref, lim_ref, dt_ref, br_ref, bi_ref, a_ref, b_ref, bbre_ref, bbim_ref):
        a, b, bbre, bbim = _disc_math(lre_ref[...], lim_ref[...], dt_ref[...], br_ref[...], bi_ref[...])
        a_ref[...] = a
        b_ref[...] = b
        bbre_ref[...] = bbre
        bbim_ref[...] = bbim

    gp = jax.ShapeDtypeStruct((G, 1, P), F32)
    gcp = jax.ShapeDtypeStruct((G, C, P), F32)
    return pl.pallas_call(
        body, in_specs=[VMEM] * 5, out_specs=[VMEM] * 4, out_shape=[gp, gp, gcp, gcp], name="s5_disc",
    )(lre, lim, logdt, bre_t, bim_t)


def _s5_disc_bwd(lre, lim, logdt, bre_t, bim_t, da, db, dbbre, dbbim):
    G, _, P = lre.shape
    C = bre_t.shape[1]

    def body(lre_ref, lim_ref, dt_ref, br_ref, bi_ref, da_ref, db_ref, dbr_ref, dbi_ref,
             o_lre, o_lim, o_dt, o_br, o_bi):
        _, vjp = jax.vjp(_disc_math, lre_ref[...], lim_ref[...], dt_ref[...], br_ref[...], bi_ref[...])
        g = vjp((da_ref[...], db_ref[...], dbr_ref[...], dbi_ref[...]))
        o_lre[...] = g[0]
        o_lim[...] = g[1]
        o_dt[...] = g[2]
        o_br[...] = g[3]
        o_bi[...] = g[4]

    gp = jax.ShapeDtypeStruct((G, 1, P), F32)
    gcp = jax.ShapeDtypeStruct((G, C, P), F32)
    return pl.pallas_call(
        body, in_specs=[VMEM] * 9, out_specs=[VMEM] * 5,
        out_shape=[gp, gp, jax.ShapeDtypeStruct((G, 1, 1), F32), gcp, gcp], name="s5_disc_bwd",
    )(lre, lim, logdt, bre_t, bim_t, da, db, dbbre, dbbim)


S5_CHUNK = 256


def _load_once(pairs):
    @pl.when(pl.program_id(0) == 0)
    def _():
        for src, dst in pairs:
            pltpu.sync_copy(src, dst)


def _s5_fwd(proj, u_off, width, a_row, b_row, bmre, bmim, cmre, cmimn, d_row):
    S = proj.shape[0]
    GP = a_row.shape[1]
    T = min(S5_CHUNK, S)

    def body(u_ref, a_ref, b_ref, d_ref, bre_hbm, bim_hbm, cre_hbm, cim_hbm,
             uo_ref, y_ref, gy_ref, hre_ref, him_ref, st_ref, bure_s, buim_s, bre_ref, bim_ref, cre_ref, cim_ref):
        @pl.when(pl.program_id(0) == 0)
        def _():
            st_ref[...] = jnp.zeros_like(st_ref)

        _load_once([(bre_hbm, bre_ref), (bim_hbm, bim_ref), (cre_hbm, cre_ref), (cim_hbm, cim_ref)])
        u = u_ref[...]
        uo_ref[...] = u
        for ws, gs in _bd_blocks(width, GP):
            bure_s[:, gs] = _dot_a2(u[:, ws], bre_ref[ws, gs])
            buim_s[:, gs] = _dot_a2(u[:, ws], bim_ref[ws, gs])
        a = a_ref[...]
        b = b_ref[...]

        def step(ii, carry):
            hre, him = carry
            base = pl.multiple_of(ii * SUBLANES, SUBLANES)
            br = bure_s[pl.ds(base, SUBLANES), :]
            bi = buim_s[pl.ds(base, SUBLANES), :]
            rows_re, rows_im = [], []
            for j in range(SUBLANES):
                nre = a * hre - b * him + br[j:j + 1, :]
                nim = a * him + b * hre + bi[j:j + 1, :]
                hre, him = nre, nim
                rows_re.append(nre)
                rows_im.append(nim)
            hre_ref[pl.ds(base, SUBLANES), :] = jnp.concatenate(rows_re, axis=0)
            him_ref[pl.ds(base, SUBLANES), :] = jnp.concatenate(rows_im, axis=0)
            return hre, him

        hre, him = lax.fori_loop(0, T // SUBLANES, step, (st_ref[0:1, :], st_ref[1:2, :]))
        st_ref[0:1, :] = hre
        st_ref[1:2, :] = him
        for ws, gs in _bd_blocks(width, GP):
            y = (_dot_a2(hre_ref[:, gs], cre_ref[gs, ws]) + _dot_a2(him_ref[:, gs], cim_ref[gs, ws])
                 + d_ref[:, ws] * u[:, ws])
            y_ref[:, ws] = y
            gy_ref[:, ws] = jax.nn.gelu(y).astype(gy_ref.dtype)

    c0 = lambda i: (0, 0)
    urow = pl.BlockSpec((T, width), lambda i: (i, u_off // width))
    row = pl.BlockSpec((T, width), lambda i: (i, 0))
    hrow = pl.BlockSpec((T, GP), lambda i: (i, 0))
    sw = jax.ShapeDtypeStruct((S, width), F32)
    sg = jax.ShapeDtypeStruct((S, GP), F32)
    return pl.pallas_call(
        body, grid=(S // T,),
        in_specs=[urow, pl.BlockSpec((1, GP), c0), pl.BlockSpec((1, GP), c0), pl.BlockSpec((1, width), c0),
                  ANY, ANY, ANY, ANY],
        out_specs=[row, row, row, hrow, hrow],
        out_shape=[sw, sw, jax.ShapeDtypeStruct((S, width), BF16), sg, sg],
        scratch_shapes=[pltpu.VMEM((SUBLANES, GP), F32), pltpu.VMEM((T, GP), F32), pltpu.VMEM((T, GP), F32),
                        pltpu.VMEM((width, GP), F32), pltpu.VMEM((width, GP), F32),
                        pltpu.VMEM((GP, width), F32), pltpu.VMEM((GP, width), F32)],
        name="s5_fwd", compiler_params=_cparams(("arbitrary",)),
    )(proj, a_row, b_row, d_row, bmre, bmim, cmre, cmimn)


def _s5_bwd(u, dgy, y, hre, him, a_row, b_row, bmre, bmim, cmre, cmimn, d_row):
    S, width = u.shape
    GP = a_row.shape[1]
    T = min(S5_CHUNK, S)
    nchunk = S // T

    def body(u_ref, dgy_ref, y_ref, hre_ref, him_ref, a_ref, b_ref, d_ref, bre_hbm, bim_hbm, cre_hbm, cim_hbm,
             du_ref, dy_ref, gre_s, gim_s, dd_ref, da_ref, db_ref,
             st_ref, bre_ref, bim_ref, cre_ref, cim_ref):
        @pl.when(pl.program_id(0) == 0)
        def _():
            st_ref[...] = jnp.zeros_like(st_ref)
            for r in (dd_ref, da_ref, db_ref):
                r[...] = jnp.zeros_like(r)

        _load_once([(bre_hbm, bre_ref), (bim_hbm, bim_ref), (cre_hbm, cre_ref), (cim_hbm, cim_ref)])
        u = u_ref[...]
        dy = dgy_ref[...] * _gelu_grad(y_ref[...])
        dy_ref[...] = dy
        for ws, gs in _bd_blocks(width, GP):
            gre_s[:, gs] = _dot_a2(dy[:, ws], cre_ref[gs, ws], NT)
            gim_s[:, gs] = _dot_a2(dy[:, ws], cim_ref[gs, ws], NT)
        a = a_ref[...]
        b = b_ref[...]
        g_in_re = st_ref[0:1, :]
        g_in_im = st_ref[1:2, :]

        def step(ii, carry):
            gre, gim = carry
            base = pl.multiple_of((T // SUBLANES - 1 - ii) * SUBLANES, SUBLANES)
            dr = gre_s[pl.ds(base, SUBLANES), :]
            di = gim_s[pl.ds(base, SUBLANES), :]
            rows_re = [None] * SUBLANES
            rows_im = [None] * SUBLANES
            for j in range(SUBLANES - 1, -1, -1):
                nre = dr[j:j + 1, :] + a * gre + b * gim
                nim = di[j:j + 1, :] - b * gre + a * gim
                gre, gim = nre, nim
                rows_re[j] = nre
                rows_im[j] = nim
            gre_s[pl.ds(base, SUBLANES), :] = jnp.concatenate(rows_re, axis=0)
            gim_s[pl.ds(base, SUBLANES), :] = jnp.concatenate(rows_im, axis=0)
            return gre, gim

        gre, gim = lax.fori_loop(0, T // SUBLANES, step, (g_in_re, g_in_im))
        st_ref[0:1, :] = gre
        st_ref[1:2, :] = gim
        last = lax.broadcasted_iota(jnp.int32, (T, 1), 0) == T - 1
        nxt_re = jnp.where(last, g_in_re, pltpu.roll(gre_s[...], T - 1, 0))
        nxt_im = jnp.where(last, g_in_im, pltpu.roll(gim_s[...], T - 1, 0))
        hre = hre_ref[...]
        him = him_ref[...]
        da_ref[...] += jnp.sum(nxt_re * hre + nxt_im * him, axis=0, keepdims=True)
        db_ref[...] += jnp.sum(nxt_im * hre - nxt_re * him, axis=0, keepdims=True)
        for ws, gs in _bd_blocks(width, GP):
            du_ref[:, ws] = (_dot_a2(gre_s[:, gs], bre_ref[ws, gs], NT) + _dot_a2(gim_s[:, gs], bim_ref[ws, gs], NT)
                             + d_ref[:, ws] * dy[:, ws])
        dd_ref[...] += jnp.sum(dy * u, axis=0, keepdims=True)

    c0 = lambda i: (0, 0)
    rev = lambda i: (nchunk - 1 - i, 0)
    row = pl.BlockSpec((T, width), rev)
    hrow = pl.BlockSpec((T, GP), rev)
    v_gp = pl.BlockSpec((1, GP), c0)
    v_w = pl.BlockSpec((1, width), c0)
    sw = jax.ShapeDtypeStruct((S, width), F32)
    sg = jax.ShapeDtypeStruct((S, GP), F32)
    return pl.pallas_call(
        body, grid=(nchunk,),
        in_specs=[row, row, row, hrow, hrow, v_gp, v_gp, v_w, ANY, ANY, ANY, ANY],
        out_specs=[row, row, hrow, hrow, v_w, v_gp, v_gp],
        out_shape=[sw, sw, sg, sg, jax.ShapeDtypeStruct((1, width), F32),
                   jax.ShapeDtypeStruct((1, GP), F32), jax.ShapeDtypeStruct((1, GP), F32)],
        scratch_shapes=[pltpu.VMEM((SUBLANES, GP), F32),
                        pltpu.VMEM((width, GP), F32), pltpu.VMEM((width, GP), F32),
                        pltpu.VMEM((GP, width), F32), pltpu.VMEM((GP, width), F32)],
        name="s5_bwd", compiler_params=_cparams(("arbitrary",)),
    )(u, dgy, y, hre, him, a_row, b_row, d_row, bmre, bmim, cmre, cmimn)


def _block_diag(x):
    G, A, B = x.shape
    eye = jnp.eye(G, dtype=x.dtype)
    return (eye[:, None, :, None] * x[:, :, None, :]).reshape(G * A, G * B)


def _block_diag_take(m, G):
    A, B = m.shape[0] // G, m.shape[1] // G
    return jnp.einsum("gagb->gab", m.reshape(G, A, G, B))


def _adamw(w, g, m, v, with_grad=False):
    shape = w.shape
    C = shape[-1]
    w2, g2, m2, v2 = (t.reshape(-1, C) for t in (w, g, m, v))
    R = w2.shape[0]
    rb = _tile(R, max(SUBLANES, (1 << 19) // C), SUBLANES)
    c1 = 1.0 - ADAM_B1 ** ADAM_STEP
    c2 = 1.0 - ADAM_B2 ** ADAM_STEP
    n_out = 4 if with_grad else 3

    def body(w_ref, g_ref, m_ref, v_ref, d_ref, nm_ref, nv_ref, *g_out):
        gv = g_ref[...]
        nm = ADAM_B1 * m_ref[...] + (1.0 - ADAM_B1) * gv
        nv = ADAM_B2 * v_ref[...] + (1.0 - ADAM_B2) * (gv * gv)
        d_ref[...] = -ADAM_LR * ((nm / c1) / (jnp.sqrt(nv / c2) + ADAM_EPS) + ADAM_WD * w_ref[...])
        nm_ref[...] = nm
        nv_ref[...] = nv
        if with_grad:
            g_out[0][...] = gv

    blk = pl.BlockSpec((rb, C), lambda i: (i, 0))
    sd = jax.ShapeDtypeStruct((R, C), F32)
    outs = pl.pallas_call(
        body, grid=(R // rb,), in_specs=[blk] * 4, out_specs=[blk] * n_out, out_shape=[sd] * n_out,
        name="adamw", compiler_params=_cparams(("parallel",)),
    )(w2, g2, m2, v2)
    return tuple(o.reshape(shape) for o in outs)


def _coords():
    x, y, c = lax.axis_index("x"), lax.axis_index("y"), lax.axis_index("c")
    return x, y, c


def _place_shard(w, l, place):
    _, R, C = w.shape
    rb = _tile(R, max(16, (1 << 19) // C), 16)

    def body(place_ref, w_ref, o_ref):
        o_ref[...] = w_ref[...].astype(BF16)

    grid_spec = pltpu.PrefetchScalarGridSpec(
        num_scalar_prefetch=1, grid=(R // rb,),
        in_specs=[pl.BlockSpec((None, rb, C), lambda i, p: (l, i, 0))],
        out_specs=pl.BlockSpec((None, rb, C), lambda i, p: (p[1], i, 0)))
    return pl.pallas_call(
        body, grid_spec=grid_spec, out_shape=jax.ShapeDtypeStruct((N_CHIPS, R, C), BF16),
        name="place_shard", compiler_params=_cparams(("arbitrary",)),
    )(place, w)


def _ag_hooks(outs, send_sems, recv_sems):
    n = len(outs)

    def rcopy(a, k, block, to):
        cx, cy, cc = block
        hr = outs[a].shape[1] // 2
        blk = outs[a].at[2 * cx + cy, pl.ds(cc * hr, hr)]
        return pltpu.make_async_remote_copy(
            src_ref=blk, dst_ref=blk, send_sem=send_sems.at[a * 6 + k], recv_sem=recv_sems.at[a * 6 + k],
            device_id=to, device_id_type=MESH)

    def places():
        x, y, c = _coords()
        return (x, y, c), (x, y, 1 - c), [(1 - x, y), (x, 1 - y), (1 - x, 1 - y)]

    def start():
        me, _, others = places()
        for a in range(n):
            for j, ch in enumerate(others):
                rcopy(a, j, me, (*ch, me[2])).start()

    def middle():
        me, sibling, others = places()
        for a in range(n):
            for j, ch in enumerate(others):
                rcopy(a, j, (*ch, me[2]), me).wait_recv()
                rcopy(a, 3 + j, (*ch, me[2]), sibling).start()

    def finish():
        me, sibling, others = places()
        for a in range(n):
            for j, ch in enumerate(others):
                rcopy(a, 3 + j, (*ch, sibling[2]), me).wait_recv()
        for a in range(n):
            for j, ch in enumerate(others):
                rcopy(a, j, me, (*ch, me[2])).wait_send()
                rcopy(a, 3 + j, (*ch, me[2]), sibling).wait_send()

    return start, middle, finish


def _ag_weights(bufs):
    n = len(bufs)

    def body(*refs):
        start, middle, finish = _ag_hooks(refs[n:2 * n], *refs[2 * n:])
        start()
        middle()
        finish()

    return pl.pallas_call(
        body, out_shape=[jax.ShapeDtypeStruct(b.shape, b.dtype) for b in bufs],
        in_specs=[ANY] * n, out_specs=[ANY] * n, input_output_aliases={i: i for i in range(n)},
        scratch_shapes=[pltpu.SemaphoreType.DMA((n * 6,)), pltpu.SemaphoreType.DMA((n * 6,))],
        name="ag_weights",
    )(*bufs)


def _rs_pair_hooks(gs, outs, send_sems, recv_sems):
    def copies():
        x, y, c = _coords()
        cps = []
        for i in range(len(gs)):
            hr = gs[i].shape[1] // 2
            cps.append(pltpu.make_async_remote_copy(
                src_ref=gs[i].at[:, pl.ds((1 - c) * hr, hr)], dst_ref=outs[i],
                send_sem=send_sems.at[i], recv_sem=recv_sems.at[i],
                device_id=(x, y, 1 - c), device_id_type=MESH))
        return cps

    def start():
        for cp in copies():
            cp.start()

    def finish():
        for cp in copies():
            cp.wait()

    return start, None, finish


def _rs_pair_comm(grads):
    return dict(ins=list(grads), alias=False, n_sems=len(grads), hooks=_rs_pair_hooks,
                out_shapes=[jax.ShapeDtypeStruct((N_CHIPS, g.shape[1] // 2, g.shape[2]), F32) for g in grads])


def _ag_comm(bufs):
    return dict(ins=list(bufs), alias=True, n_sems=6 * len(bufs),
                hooks=lambda ins, outs, send_sems, recv_sems: _ag_hooks(outs, send_sems, recv_sems),
                out_shapes=[jax.ShapeDtypeStruct(b.shape, b.dtype) for b in bufs])


def _rs_pair(grads):
    n = len(grads)
    comm = _rs_pair_comm(grads)

    def body(*refs):
        start, _, finish = _rs_pair_hooks(refs[:n], refs[n:2 * n], *refs[2 * n:])
        start()
        finish()

    return pl.pallas_call(
        body, out_shape=comm["out_shapes"], in_specs=[ANY] * n, out_specs=[ANY] * n,
        scratch_shapes=[pltpu.SemaphoreType.DMA((n,)), pltpu.SemaphoreType.DMA((n,))],
        name="rs_pair",
    )(*grads)


def _pair_add(g, r, place):
    _, R, C = g.shape
    hr = R // 2
    rb = _tile(hr, max(16, (1 << 19) // C), 16)
    nb = hr // rb

    def body(place_ref, g_ref, r_ref, p16_ref, own_ref):
        s = g_ref[...] + r_ref[...]
        p16_ref[...] = s.astype(BF16)

        @pl.when(pl.program_id(1) == place_ref[1])
        def _():
            own_ref[...] = s

    grid_spec = pltpu.PrefetchScalarGridSpec(
        num_scalar_prefetch=1, grid=(nb, N_CHIPS),
        in_specs=[pl.BlockSpec((None, rb, C), lambda i, k, p: (k, p[0] * nb + i, 0)),
                  pl.BlockSpec((None, rb, C), lambda i, k, p: (k, i, 0))],
        out_specs=[pl.BlockSpec((None, rb, C), lambda i, k, p: (k, i, 0)),
                   pl.BlockSpec((rb, C), lambda i, k, p: (i, 0))])
    return pl.pallas_call(
        body, grid_spec=grid_spec,
        out_shape=[jax.ShapeDtypeStruct((N_CHIPS, hr, C), BF16), jax.ShapeDtypeStruct((hr, C), F32)],
        name="pair_add", compiler_params=_cparams(("arbitrary", "arbitrary")),
    )(place, g, r)


def _rs_chips_hooks(ps, outs, send_sems, recv_sems):
    def copies():
        x, y, c = _coords()
        cps = []
        for a in range(len(ps)):
            for r in (1, 2, 3):
                kx = 1 - x if (r >> 1) else x
                ky = 1 - y if (r & 1) else y
                cps.append(pltpu.make_async_remote_copy(
                    src_ref=ps[a].at[2 * kx + ky], dst_ref=outs[a].at[r - 1],
                    send_sem=send_sems.at[a * 3 + r - 1], recv_sem=recv_sems.at[a * 3 + r - 1],
                    device_id=(kx, ky, c), device_id_type=MESH))
        return cps

    def start():
        for cp in copies():
            cp.start()

    def finish():
        for cp in copies():
            cp.wait()

    return start, finish


def _rs_chips_shapes(p16):
    return [jax.ShapeDtypeStruct((3,) + p.shape[1:], BF16) for p in p16]


def _rs_chips_comm(p16):
    def hooks(ins, outs, send_sems, recv_sems):
        start, finish = _rs_chips_hooks(ins, outs, send_sems, recv_sems)
        return start, None, finish

    return dict(ins=list(p16), alias=False, n_sems=3 * len(p16), hooks=hooks, out_shapes=_rs_chips_shapes(p16))


def _rs_chips(p16):
    n = len(p16)

    def body(*refs):
        start, finish = _rs_chips_hooks(refs[:n], refs[n:2 * n], *refs[2 * n:])
        start()
        finish()

    return pl.pallas_call(
        body, out_shape=_rs_chips_shapes(p16), in_specs=[ANY] * n, out_specs=[ANY] * n,
        scratch_shapes=[pltpu.SemaphoreType.DMA((n * 3,)), pltpu.SemaphoreType.DMA((n * 3,))],
        name="rs_chips",
    )(*p16)


def _chip_sum(own, recv, full, l, place):
    hr, C = own.shape
    rb = _tile(hr, max(16, (1 << 19) // C), 16)
    nb = hr // rb

    def body(place_ref, o_ref, r_ref, full_ref, s_ref):
        s = o_ref[...] + r_ref[0].astype(F32)
        s = s + r_ref[1].astype(F32)
        s_ref[...] = s + r_ref[2].astype(F32)

    grid_spec = pltpu.PrefetchScalarGridSpec(
        num_scalar_prefetch=1, grid=(nb,),
        in_specs=[pl.BlockSpec((rb, C), lambda i, p: (i, 0)), pl.BlockSpec((3, rb, C), lambda i, p: (0, i, 0)), ANY],
        out_specs=pl.BlockSpec((None, rb, C), lambda i, p: (l, p[0] * nb + i, 0)))
    return pl.pallas_call(
        body, grid_spec=grid_spec, out_shape=jax.ShapeDtypeStruct(full.shape, F32),
        input_output_aliases={3: 0}, name="chip_sum", compiler_params=_cparams(("arbitrary",)),
    )(place, own, recv, full)


def _share_pair(fulls):
    n = len(fulls)

    def body(*refs):
        outs = refs[n:2 * n]
        send_sems, recv_sems = refs[2 * n:]
        x, y, c = _coords()
        copies = []
        for a in range(n):
            hr = outs[a].shape[1] // 2
            mine = outs[a].at[:, pl.ds(c * hr, hr)]
            cp = pltpu.make_async_remote_copy(
                src_ref=mine, dst_ref=mine, send_sem=send_sems.at[a], recv_sem=recv_sems.at[a],
                device_id=(x, y, 1 - c), device_id_type=MESH)
            cp.start()
            copies.append(cp)
        for cp in copies:
            cp.wait_recv()
        for cp in copies:
            cp.wait_send()

    return pl.pallas_call(
        body, out_shape=[jax.ShapeDtypeStruct(f.shape, f.dtype) for f in fulls],
        in_specs=[ANY] * n, out_specs=[ANY] * n, input_output_aliases={i: i for i in range(n)},
        scratch_shapes=[pltpu.SemaphoreType.DMA((n,)), pltpu.SemaphoreType.DMA((n,))],
        name="share_pair",
    )(*fulls)


def _small_allreduce(packed):
    m_per, ncol = packed.shape

    def body(x_ref, out_ref, tot_ref, send_sems, recv_sems, local_sem):
        x, y, c = _coords()
        me, sibling = (x, y, c), (x, y, 1 - c)
        chips = [(1 - x, y), (x, 1 - y), (1 - x, 1 - y)]

        def rows(px, py, pc):
            return out_ref.at[pl.ds((4 * px + 2 * py + pc) * m_per, m_per), :]

        def copy(k, block, to, src=None):
            return pltpu.make_async_remote_copy(
                src_ref=rows(*block) if src is None else src, dst_ref=rows(*block),
                send_sem=send_sems.at[k], recv_sem=recv_sems.at[k], device_id=to, device_id_type=MESH)

        mine = pltpu.make_async_copy(x_ref, rows(*me), local_sem)
        mine.start()
        first = [copy(0, me, sibling, src=x_ref)]
        first += [copy(1 + j, me, (*chip, c), src=x_ref) for j, chip in enumerate(chips)]
        for cp in first:
            cp.start()
        passed = [copy(4 + j, (*chip, c), sibling) for j, chip in enumerate(chips)]
        for j, chip in enumerate(chips):
            copy(1 + j, (*chip, c), me).wait_recv()
            passed[j].start()
        copy(0, sibling, me).wait_recv()
        for j, chip in enumerate(chips):
            copy(4 + j, (*chip, 1 - c), me).wait_recv()
        for cp in first + passed:
            cp.wait_send()
        mine.wait()
        tot = out_ref[pl.ds(0, m_per), :]
        for d in range(1, N_DEV):
            tot = tot + out_ref[pl.ds(d * m_per, m_per), :]
        tot_ref[...] = tot

    _, tot = pl.pallas_call(
        body,
        out_shape=[jax.ShapeDtypeStruct((N_DEV * m_per, ncol), F32), jax.ShapeDtypeStruct((m_per, ncol), F32)],
        in_specs=[VMEM], out_specs=[VMEM, VMEM],
        scratch_shapes=[pltpu.SemaphoreType.DMA((7,)), pltpu.SemaphoreType.DMA((7,)), pltpu.SemaphoreType.DMA],
        name="small_allreduce",
        compiler_params=pltpu.CompilerParams(vmem_limit_bytes=VMEM_LIMIT_MB * 1024 * 1024),
    )(packed)
    return tot


BIG = ["w_in", "sb_w_out", "ssm_w_glu", "ssm_w_out", "mem_w_kv", "mem_w_out", "w_o", "ffn_w_gate_up", "ffn_w_down"]
KIND = {"w_in": "col", "sb_w_out": "col", "ssm_w_glu": "col", "ssm_w_out": "col", "mem_w_kv": "row",
        "mem_w_out": "col", "w_o": "row", "ffn_w_gate_up": "col", "ffn_w_down": "row"}
AG_EARLY = ["w_in", "sb_w_out", "ssm_w_glu", "ssm_w_out", "mem_w_kv", "mem_w_out", "w_o"]
FFN_WEIGHTS = ["ffn_w_down", "ffn_w_gate_up"]
SMALL = ["b_in", "ssm_lambda_re", "ssm_lambda_im", "ssm_log_dt", "ssm_b_re", "ssm_b_im", "ssm_c_re", "ssm_c_im",
         "ssm_d", "ln1_g", "ln1_b", "ln2_g", "ln2_b"]
WEIGHTS = ["w_in", "b_in", "sb_w_out", "ssm_lambda_re", "ssm_lambda_im", "ssm_log_dt", "ssm_b_re", "ssm_b_im",
           "ssm_c_re", "ssm_c_im", "ssm_d", "ssm_w_glu", "ssm_w_out", "mem_w_kv", "mem_w_out", "w_o", "ln1_g",
           "ln1_b", "ffn_w_gate_up", "ffn_w_down", "ln2_g", "ln2_b"]


def _pack(arrs):
    flat = jnp.concatenate([a.reshape(-1).astype(F32) for a in arrs])
    n = flat.shape[0]
    rows = -(-n // LANES)
    rows = -(-rows // SUBLANES) * SUBLANES
    return jnp.pad(flat, (0, rows * LANES - n)).reshape(rows, LANES)


def _unpack(packed, like):
    flat = packed.reshape(-1)
    out, off = [], 0
    for a in like:
        out.append(flat[off:off + a.size].reshape(a.shape))
        off += a.size
    return out


def _step(x, mem, target, W, M1, V1):
    S, D = x.shape[1], x.shape[2]
    L = W["w_in"].shape[0]
    x0 = x.reshape(S, D)
    mem2 = mem.reshape(mem.shape[1], D)
    tgt = target.reshape(S, D)
    alpha = (2 * L) ** 0.25
    sbw = W["sb_w_out"].shape[1]
    ssw = W["ssm_d"].shape[1]
    mw = W["mem_w_out"].shape[1]
    heads = sbw // HEAD_DIM
    G, P = W["ssm_lambda_re"].shape[1], W["ssm_lambda_re"].shape[2]
    q_off, k_off, v_off = 0, sbw, 2 * sbw
    u_off = 3 * sbw
    qm_off = u_off + ssw
    gate_off = qm_off + mw

    x_i, y_i, c_i = _coords()
    place = jnp.stack([c_i, 2 * x_i + y_i]).astype(jnp.int32)
    placed = [[_place_shard(W[n], l, place) for n in BIG] for l in range(L)]
    Wg = {n: [None] * L for n in BIG}
    placed = [dict(zip(BIG, row)) for row in placed]
    for n, buf in zip(AG_EARLY, _ag_weights([placed[0][n] for n in AG_EARLY])):
        Wg[n][0] = buf

    def hosted(plan):
        return _ag_comm([placed[ll][n] for n, ll in plan]) if plan else None

    def keep(plan, bufs):
        for (n, ll), buf in zip(plan, bufs):
            Wg[n][ll] = buf

    saved = []
    xl = x0
    xlb = x0.astype(BF16)
    for l in range(L):
        sv = {"x": xlb}
        more = l + 1 < L
        plan_proj = [("ffn_w_gate_up", 0)] if l == 0 else []
        plan_sb = ([("ffn_w_down", 0)] if l == 0 else []) + ([(n, l + 1) for n in AG_EARLY] if more else [])
        plan_gate_up = [("ffn_w_gate_up", l + 1)] if more else []
        plan_down = [("ffn_w_down", l + 1)] if more else []

        proj = _mm_nn(xlb, Wg["w_in"][l], "col", bias=W["b_in"][l][None, :], comm=hosted(plan_proj),
                      name="mm_proj_ag" if plan_proj else "mm_proj")
        if plan_proj:
            proj, got = proj
            keep(plan_proj, got)
        sb, sb_ctot, got = _sb_fwd(proj, q_off, k_off, v_off, heads,
                                   ag=[placed[ll][n] for n, ll in plan_sb] if plan_sb else None)
        keep(plan_sb, got)
        p_sb = _mm_nn(sb, Wg["sb_w_out"][l], "col", name="mm_sb_out")

        bre_t = W["ssm_b_re"][l].transpose(0, 2, 1)
        bim_t = W["ssm_b_im"][l].transpose(0, 2, 1)
        logdt = W["ssm_log_dt"][l][:, None, None]
        lre3 = W["ssm_lambda_re"][l][:, None, :]
        lim3 = W["ssm_lambda_im"][l][:, None, :]
        a_gp, b_gp, bbre, bbim = _s5_disc(lre3, lim3, logdt, bre_t, bim_t)
        a_row, b_row = a_gp.reshape(1, G * P), b_gp.reshape(1, G * P)
        bmre, bmim = _block_diag(bbre), _block_diag(bbim)
        cmre = _block_diag(W["ssm_c_re"][l].transpose(0, 2, 1))
        cmimn = _block_diag(-W["ssm_c_im"][l].transpose(0, 2, 1))
        d_row = W["ssm_d"][l][None, :]
        u_ssm, y, gy, hre, him = _s5_fwd(proj, u_off, ssw, a_row, b_row, bmre, bmim, cmre, cmimn, d_row)
        glu = _mm_nn(gy, Wg["ssm_w_glu"][l], "col", name="mm_glu")
        zz = _glu_fwd(glu)
        p_ssm = _mm_nn(zz, Wg["ssm_w_out"][l], "col", name="mm_ssm_out")

        kv = _mm_nn(mem2, Wg["mem_w_kv"][l], "row", name="mm_kv")
        mm_o = _mem_fwd(proj, qm_off, mw, kv)
        p_mem = _mm_nn(mm_o, Wg["mem_w_out"][l], "col", name="mm_mem_out")

        merged = _merge_fwd(proj, gate_off, p_sb, p_ssm, p_mem)
        mix = _mm_nn(merged, Wg["w_o"][l], "row", name="mm_wo")
        x1, x1b, xh1, rs1 = _ln_fwd(xl, mix, W["ln1_g"][l][None, :], W["ln1_b"][l][None, :], alpha)
        gu = _mm_nn(x1b, Wg["ffn_w_gate_up"][l], "col", comm=hosted(plan_gate_up),
                    name="mm_gate_up_ag" if plan_gate_up else "mm_gate_up")
        if plan_gate_up:
            gu, got = gu
            keep(plan_gate_up, got)
        hid = _swiglu_fwd(gu)
        ffn = _mm_nn(hid, Wg["ffn_w_down"][l], "row", comm=hosted(plan_down),
                     name="mm_down_ag" if plan_down else "mm_down")
        if plan_down:
            ffn, got = ffn
            keep(plan_down, got)
        x2, x2b, xh2, rs2 = _ln_fwd(x1, ffn, W["ln2_g"][l][None, :], W["ln2_b"][l][None, :], alpha)
        sv.update(proj=proj, sb=sb, sb_ctot=sb_ctot, p_sb=p_sb, y=y, gy=gy, hre=hre, him=him, glu=glu, zz=zz,
                  p_ssm=p_ssm, kv=kv, mm_o=mm_o, p_mem=p_mem, merged=merged, x1=x1b, xh1=xh1, rs1=rs1, gu=gu,
                  hid=hid, xh2=xh2, rs2=rs2, u=u_ssm,
                  disc=(lre3, lim3, logdt, bre_t, bim_t, a_row, b_row, bmre, bmim, cmre, cmimn, d_row))
        saved.append(sv)
        xl, xlb = x2, x2b

    dxl, loss_part = _loss_head(xl, tgt)

    gbig = {n: [None] * L for n in BIG}
    gsmall = {n: [None] * L for n in SMALL}
    own = [None] * L
    from_chips = [None] * L
    above = None
    pending = None
    for l in range(L - 1, -1, -1):
        sv = saved[l]
        proj = sv["proj"]
        dr2, dr2b, dg2, db2 = _ln_bwd(dxl, sv["xh2"], sv["rs2"], W["ln2_g"][l][None, :])
        gsmall["ln2_g"][l], gsmall["ln2_b"][l] = dg2[0], db2[0]
        dhid = _mm_nt(dr2b, Wg["ffn_w_down"][l], "row", name="mm_d_hid")
        gbig["ffn_w_down"][l] = _mm_tn(sv["hid"], dr2b, "row", name="mm_g_down")
        dgu = _swiglu_bwd(dhid, sv["gu"])
        if above is None:
            dx1 = _mm_nt(dgu, Wg["ffn_w_gate_up"][l], "col", add=dr2, add_scale=alpha, name="mm_d_x1")
        else:
            dx1, from_sibling = _mm_nt(dgu, Wg["ffn_w_gate_up"][l], "col", add=dr2, add_scale=alpha,
                                       name="mm_d_x1_rs", comm=_rs_pair_comm(above))
            sums = [_pair_add(g, r, place) for g, r in zip(above, from_sibling)]
            pending = [s[0] for s in sums]
            own[l + 1] = [s[1] for s in sums]
        gbig["ffn_w_gate_up"][l] = _mm_tn(sv["x1"], dgu, "col", name="mm_g_gate_up")

        dr1, dr1b, dg1, db1 = _ln_bwd(dx1, sv["xh1"], sv["rs1"], W["ln1_g"][l][None, :])
        gsmall["ln1_g"][l], gsmall["ln1_b"][l] = dg1[0], db1[0]
        dmerged = _mm_nt(dr1b, Wg["w_o"][l], "row", name="mm_d_merged")
        gbig["w_o"][l] = _mm_tn(sv["merged"], dr1b, "row", name="mm_g_wo")
        dp_sb, dp_ssm, dp_mem, dgl0, dgl1, dgl2 = _merge_bwd(
            dmerged, proj, gate_off, sv["p_sb"], sv["p_ssm"], sv["p_mem"])

        dsb = _mm_nt(dp_sb, Wg["sb_w_out"][l], "col", out_dtype=BF16, name="mm_d_sb")
        gbig["sb_w_out"][l] = _mm_tn(sv["sb"], dp_sb, "col", name="mm_g_sb_out")

        dzz = _mm_nt(dp_ssm, Wg["ssm_w_out"][l], "col", name="mm_d_zz")
        gbig["ssm_w_out"][l] = _mm_tn(sv["zz"], dp_ssm, "col", name="mm_g_ssm_out")
        dglu = _glu_bwd(dzz, sv["glu"])
        dgy = _mm_nt(dglu, Wg["ssm_w_glu"][l], "col", name="mm_d_gy")
        gbig["ssm_w_glu"][l] = _mm_tn(sv["gy"], dglu, "col", name="mm_g_glu")
        lre3, lim3, logdt, bre_t, bim_t, a_row, b_row, bmre, bmim, cmre, cmimn, d_row = sv["disc"]
        du, dy_ssm, g_re, g_im, dd, da, db = _s5_bwd(
            sv["u"], dgy, sv["y"], sv["hre"], sv["him"], a_row, b_row, bmre, bmim, cmre, cmimn, d_row)
        dbmre = _mm_tn(sv["u"], g_re, "plain", name="mm_g_ssm_bre")
        dbmim = _mm_tn(sv["u"], g_im, "plain", name="mm_g_ssm_bim")
        dcmre = _mm_tn(sv["hre"], dy_ssm, "plain", name="mm_g_ssm_cre")
        dcmimn = _mm_tn(sv["him"], dy_ssm, "plain", name="mm_g_ssm_cim")
        dlre, dlim, dlogdt, dbre_t, dbim_t = _s5_disc_bwd(
            lre3, lim3, logdt, bre_t, bim_t, da.reshape(G, 1, P), db.reshape(G, 1, P),
            _block_diag_take(dbmre, G), _block_diag_take(dbmim, G))
        gsmall["ssm_lambda_re"][l], gsmall["ssm_lambda_im"][l] = dlre.reshape(G, P), dlim.reshape(G, P)
        gsmall["ssm_log_dt"][l] = dlogdt.reshape(G)
        gsmall["ssm_b_re"][l] = dbre_t.transpose(0, 2, 1)
        gsmall["ssm_b_im"][l] = dbim_t.transpose(0, 2, 1)
        gsmall["ssm_c_re"][l] = _block_diag_take(dcmre, G).transpose(0, 2, 1)
        gsmall["ssm_c_im"][l] = -_block_diag_take(dcmimn, G).transpose(0, 2, 1)
        gsmall["ssm_d"][l] = dd[0]

        dmm = _mm_nt(dp_mem, Wg["mem_w_out"][l], "col", out_dtype=BF16, name="mm_d_mm")
        gbig["mem_w_out"][l] = _mm_tn(sv["mm_o"], dp_mem, "col", name="mm_g_mem_out")
        dqm, dkv = _mem_bwd(proj, qm_off, mw, sv["kv"], dmm)
        gbig["mem_w_kv"][l] = _mm_tn(mem2, dkv, "row", name="mm_g_kv")

        dq, dk, dv, arrived = _sb_bwd(proj, q_off, k_off, v_off, heads, dsb, sv["sb_ctot"], rs=pending)
        if pending is not None:
            from_chips[l + 1] = arrived
        dproj, dbin = _assemble_dproj([dq, dk, dv, du, dqm, dgl0, dgl1, dgl2])
        gsmall["b_in"][l] = dbin[0]
        if l > 0:
            dxl = _mm_nt(dproj, Wg["w_in"][l], "col", add=dr1, add_scale=alpha, name="mm_d_x")
            gbig["w_in"][l] = _mm_tn(sv["x"], dproj, "col", name="mm_g_win")
            above = [gbig[n][l] for n in BIG]
        else:
            ffn = [gbig[n][0] for n in FFN_WEIGHTS]
            ffn_sums = [_pair_add(g, r, place) for g, r in zip(ffn, _rs_pair(ffn))]
            dxl, got_down = _mm_nt(dproj, Wg["w_in"][0], "col", add=dr1, add_scale=alpha, name="mm_d_x_rs",
                                   comm=_rs_chips_comm([ffn_sums[0][0]]))
            gbig["w_in"][0], got_gate_up = _mm_tn(sv["x"], dproj, "col", name="mm_g_win_rs",
                                                  comm=_rs_chips_comm([ffn_sums[1][0]]))

    rest_names = [n for n in BIG if n not in FFN_WEIGHTS]
    rest = [gbig[n][0] for n in rest_names]
    rest_sums = [_pair_add(g, r, place) for g, r in zip(rest, _rs_pair(rest))]
    rest_got = _rs_chips([s[0] for s in rest_sums])
    by_name = {n: (s[1], r) for n, s, r in zip(rest_names, rest_sums, rest_got)}
    by_name[FFN_WEIGHTS[0]] = (ffn_sums[0][1], got_down[0])
    by_name[FFN_WEIGHTS[1]] = (ffn_sums[1][1], got_gate_up[0])
    own[0] = [by_name[n][0] for n in BIG]
    from_chips[0] = [by_name[n][1] for n in BIG]
    grad_x = dxl.reshape(x.shape)

    fulls = []
    for a, n in enumerate(BIG):
        full = lax.empty(W[n].shape, F32)
        for l in range(L - 1, -1, -1):
            full = _chip_sum(own[l][a], from_chips[l][a], full, l, place)
        fulls.append(full)
    reduced = _share_pair(fulls)
    grads = {n: reduced[i] for i, n in enumerate(BIG)}

    small_local = [jnp.stack(gsmall[n]) for n in SMALL]
    packed = _pack(small_local + [loss_part[0, :1]])
    total = _small_allreduce(packed)
    unpacked = _unpack(total, small_local + [loss_part[0, :1]])
    for n, g in zip(SMALL, unpacked[:-1]):
        grads[n] = g
    loss = unpacked[-1][0]

    delta, new_m, new_v = {}, {}, {}
    for n in BIG:
        delta[n], new_m[n], new_v[n], grads[n] = _adamw(W[n], grads[n], M1[n], V1[n], with_grad=True)
    sm = _adamw(_pack([W[n] for n in SMALL]), _pack([grads[n] for n in SMALL]),
                _pack([M1[n] for n in SMALL]), _pack([V1[n] for n in SMALL]))
    like = [W[n] for n in SMALL]
    for n, d, m_, v_ in zip(SMALL, _unpack(sm[0], like), _unpack(sm[1], like), _unpack(sm[2], like)):
        delta[n], new_m[n], new_v[n] = d, m_, v_

    return (loss, grad_x, *[grads[n] for n in WEIGHTS], *[delta[n] for n in WEIGHTS],
            *[new_m[n] for n in WEIGHTS], *[new_v[n] for n in WEIGHTS])


def kernel(x, mem, w_in, b_in, sb_w_out, ssm_lambda_re, ssm_lambda_im, ssm_log_dt, ssm_b_re, ssm_b_im, ssm_c_re, ssm_c_im, ssm_d, ssm_w_glu, ssm_w_out, mem_w_kv, mem_w_out, w_o, ln1_g, ln1_b, ffn_w_gate_up, ffn_w_down, ln2_g, ln2_b, loss_target, m_w_in, m_b_in, m_sb_w_out, m_ssm_lambda_re, m_ssm_lambda_im, m_ssm_log_dt, m_ssm_b_re, m_ssm_b_im, m_ssm_c_re, m_ssm_c_im, m_ssm_d, m_ssm_w_glu, m_ssm_w_out, m_mem_w_kv, m_mem_w_out, m_w_o, m_ln1_g, m_ln1_b, m_ffn_w_gate_up, m_ffn_w_down, m_ln2_g, m_ln2_b, v_w_in, v_b_in, v_sb_w_out, v_ssm_lambda_re, v_ssm_lambda_im, v_ssm_log_dt, v_ssm_b_re, v_ssm_b_im, v_ssm_c_re, v_ssm_c_im, v_ssm_d, v_ssm_w_glu, v_ssm_w_out, v_mem_w_kv, v_mem_w_out, v_w_o, v_ln1_g, v_ln1_b, v_ffn_w_gate_up, v_ffn_w_down, v_ln2_g, v_ln2_b):
    W = dict(w_in=w_in, b_in=b_in, sb_w_out=sb_w_out, ssm_lambda_re=ssm_lambda_re, ssm_lambda_im=ssm_lambda_im,
             ssm_log_dt=ssm_log_dt, ssm_b_re=ssm_b_re, ssm_b_im=ssm_b_im, ssm_c_re=ssm_c_re, ssm_c_im=ssm_c_im,
             ssm_d=ssm_d, ssm_w_glu=ssm_w_glu, ssm_w_out=ssm_w_out, mem_w_kv=mem_w_kv, mem_w_out=mem_w_out,
             w_o=w_o, ln1_g=ln1_g, ln1_b=ln1_b, ffn_w_gate_up=ffn_w_gate_up, ffn_w_down=ffn_w_down,
             ln2_g=ln2_g, ln2_b=ln2_b)
    M1 = dict(w_in=m_w_in, b_in=m_b_in, sb_w_out=m_sb_w_out, ssm_lambda_re=m_ssm_lambda_re,
              ssm_lambda_im=m_ssm_lambda_im, ssm_log_dt=m_ssm_log_dt, ssm_b_re=m_ssm_b_re, ssm_b_im=m_ssm_b_im,
              ssm_c_re=m_ssm_c_re, ssm_c_im=m_ssm_c_im, ssm_d=m_ssm_d, ssm_w_glu=m_ssm_w_glu,
              ssm_w_out=m_ssm_w_out, mem_w_kv=m_mem_w_kv, mem_w_out=m_mem_w_out, w_o=m_w_o, ln1_g=m_ln1_g,
              ln1_b=m_ln1_b, ffn_w_gate_up=m_ffn_w_gate_up, ffn_w_down=m_ffn_w_down, ln2_g=m_ln2_g, ln2_b=m_ln2_b)
    V1 = dict(w_in=v_w_in, b_in=v_b_in, sb_w_out=v_sb_w_out, ssm_lambda_re=v_ssm_lambda_re,
              ssm_lambda_im=v_ssm_lambda_im, ssm_log_dt=v_ssm_log_dt, ssm_b_re=v_ssm_b_re, ssm_b_im=v_ssm_b_im,
              ssm_c_re=v_ssm_c_re, ssm_c_im=v_ssm_c_im, ssm_d=v_ssm_d, ssm_w_glu=v_ssm_w_glu,
              ssm_w_out=v_ssm_w_out, mem_w_kv=v_mem_w_kv, mem_w_out=v_mem_w_out, w_o=v_w_o, ln1_g=v_ln1_g,
              ln1_b=v_ln1_b, ffn_w_gate_up=v_ffn_w_gate_up, ffn_w_down=v_ffn_w_down, ln2_g=v_ln2_g, ln2_b=v_ln2_b)
    return _step(x, mem, loss_target, W, M1, V1)
```
